```python
import jax
import jax.numpy as jnp
from jax import lax
import numpy as np

D_MODEL = 1024
BATCH = 2
SEQ = 8192
DEPTH = 2
DEC_BATCH = 32
DEC_SEQ = 16
PAST_LEN = 4096

CHUNK = 64
N_EVEN = (DEPTH + 1) // 2
N_ODD = DEPTH // 2
HEAD_DIM = 64
N_Q_HEADS = 8
N_KV_HEADS = 2
Q_GROUP = N_Q_HEADS // N_KV_HEADS
WINDOW = 128
WIN_CHUNKS = -(-WINDOW // CHUNK)
SWA_ROWS = WIN_CHUNKS * CHUNK
BAND = (WIN_CHUNKS + 1) * CHUNK
ROPE_THETA = 10000.0
NEG = -1e30
D_LRU = 512
LRU_BLOCKS = 8
LRU_BW = D_LRU // LRU_BLOCKS
CONV_W = 4
LRU_C = 8.0
Q_W = N_Q_HEADS * HEAD_DIM
KV_W = N_KV_HEADS * HEAD_DIM
EVEN_IN = Q_W + 2 * KV_W + 2 * D_LRU
EVEN_MIX = Q_W + D_LRU
CHUNK_MLP = 128
D_C = D_MODEL
C_GROUPS = 8
C_GW = D_C // C_GROUPS
D_FF = 4 * D_MODEL
EPS = 1e-6

kernel_name = 'hybrid_swa_rglru_gmlp_stream_step'


def rms_norm(x, g):
    xf = x.astype(jnp.float32)
    y = xf * lax.rsqrt(jnp.mean(xf * xf, axis=-1, keepdims=True) + EPS)
    return (y * g.astype(jnp.float32)).astype(x.dtype)


def layer_norm(x, g):
    xf = x.astype(jnp.float32)
    xc = xf - jnp.mean(xf, axis=-1, keepdims=True)
    y = xc * lax.rsqrt(jnp.mean(xc * xc, axis=-1, keepdims=True) + EPS)
    return (y * g.astype(jnp.float32)).astype(x.dtype)


def rope(x, pos):
    half = HEAD_DIM // 2
    inv = ROPE_THETA ** (-jnp.arange(half, dtype=jnp.float32) / half)
    ang = pos.astype(jnp.float32)[:, None] * inv[None, :]
    cos = jnp.cos(ang)[None, :, None, :]
    sin = jnp.sin(ang)[None, :, None, :]
    xf = x.astype(jnp.float32)
    x1, x2 = xf[..., :half], xf[..., half:]
    return jnp.concatenate([x1 * cos - x2 * sin, x2 * cos + x1 * sin], axis=-1).astype(x.dtype)


def softmax_with_sink(s, sink):
    sk = sink.astype(jnp.float32)[:, :, None, None]
    m = jnp.maximum(jnp.max(s, axis=-1, keepdims=True), sk)
    e = jnp.exp(s - m)
    return e / (jnp.sum(e, axis=-1, keepdims=True) + jnp.exp(sk - m))


def swa_banded(q, k, v, sinks):
    bsz, s_len = q.shape[:2]
    nc = s_len // CHUNK
    pad = WIN_CHUNKS * CHUNK
    kp = jnp.pad(k, ((0, 0), (pad, 0), (0, 0), (0, 0)))
    vp = jnp.pad(v, ((0, 0), (pad, 0), (0, 0), (0, 0)))

    def band(t):
        return jnp.concatenate(
            [t[:, j * CHUNK:j * CHUNK + s_len].reshape(bsz, nc, CHUNK, N_KV_HEADS, HEAD_DIM)
             for j in range(WIN_CHUNKS + 1)], axis=2)

    kb, vb = band(kp), band(vp)
    qb = q.reshape(bsz, nc, CHUNK, N_KV_HEADS, Q_GROUP, HEAD_DIM)
    s = jnp.einsum('bcqkgd,bcskd->bckgqs', qb, kb, preferred_element_type=jnp.float32) * (HEAD_DIM ** -0.5)
    key_chunk = jnp.arange(nc)[:, None] - WIN_CHUNKS + jnp.arange(BAND)[None, :] // CHUNK
    s = jnp.where((key_chunk >= 0)[None, :, None, None, None, :], s, NEG)
    p = softmax_with_sink(s, sinks.reshape(N_KV_HEADS, Q_GROUP))
    o = jnp.einsum('bckgqs,bcskd->bcqkgd', p.astype(v.dtype), vb)
    return o.reshape(bsz, s_len, Q_W)


def swa_step(q, kk, vv, sinks):
    bsz, t = q.shape[:2]
    qg = q.reshape(bsz, t, N_KV_HEADS, Q_GROUP, HEAD_DIM)
    s = jnp.einsum('btkgd,bskd->bkgts', qg, kk, preferred_element_type=jnp.float32) * (HEAD_DIM ** -0.5)
    p = softmax_with_sink(s, sinks.reshape(N_KV_HEADS, Q_GROUP))
    o = jnp.einsum('bkgts,bskd->btkgd', p.astype(vv.dtype), vv)
    return o.reshape(bsz, t, Q_W)


def causal_conv(xr, buf, w, b):
    t = xr.shape[1]
    xp = jnp.concatenate([buf.astype(xr.dtype), xr], axis=1)
    y = xp[:, 0:t] * w[0] + b
    for i in range(1, CONV_W):
        y = y + xp[:, i:i + t] * w[i]
    return y, xp[:, -(CONV_W - 1):]


def rg_lru(xc, h0, wa, ba, wx, bx, lam):
    bsz, t, _ = xc.shape
    xb = xc.reshape(bsz, t, LRU_BLOCKS, LRU_BW)
    r = jax.nn.sigmoid(jnp.einsum('btnc,ncd->btnd', xb, wa).reshape(bsz, t, D_LRU) + ba)
    gi = jax.nn.sigmoid(jnp.einsum('btnc,ncd->btnd', xb, wx).reshape(bsz, t, D_LRU) + bx)
    log_a = -LRU_C * r.astype(jnp.float32) * jax.nn.softplus(-lam.astype(jnp.float32))
    a = jnp.exp(log_a)
    bterm = jnp.sqrt(-jnp.expm1(2.0 * log_a)) * (gi * xc).astype(jnp.float32)
    bterm = bterm.at[:, 0].add(a[:, 0] * h0.astype(jnp.float32))

    def combine(lhs, rhs):
        a1, b1 = lhs
        a2, b2 = rhs
        return a1 * a2, a2 * b1 + b2

    _, h = lax.associative_scan(combine, (a, bterm), axis=1)
    return h.astype(xc.dtype), h[:, -1].astype(h0.dtype)


def even_mixer(x, pos, k_cache, v_cache, h0, conv_buf, w_in, q_g, k_g, sinks,
               conv_w, conv_b, wa, ba, wx, bx, lam, w_out):
    bsz, t, _ = x.shape
    h = x @ w_in
    o1 = Q_W
    o2 = o1 + KV_W
    o3 = o2 + KV_W
    o4 = o3 + D_LRU
    q = h[..., :o1].reshape(bsz, t, N_Q_HEADS, HEAD_DIM)
    k = h[..., o1:o2].reshape(bsz, t, N_KV_HEADS, HEAD_DIM)
    v = h[..., o2:o3].reshape(bsz, t, N_KV_HEADS, HEAD_DIM)
    xr = h[..., o3:o4]
    gr = h[..., o4:]
    q = rope(rms_norm(q, q_g), pos)
    k = rope(rms_norm(k, k_g), pos)
    if k_cache is None:
        att = swa_banded(q, k, v, sinks)
        new_k, new_v = k[:, -SWA_ROWS:], v[:, -SWA_ROWS:]
        h0 = jnp.zeros((bsz, D_LRU), x.dtype)
        conv_buf = jnp.zeros((bsz, CONV_W - 1, D_LRU), x.dtype)
    else:
        rows = k_cache.shape[1]
        kk = jnp.concatenate([k_cache.astype(k.dtype), k], axis=1)
        vv = jnp.concatenate([v_cache.astype(v.dtype), v], axis=1)
        att = swa_step(q, kk, vv, sinks)
        new_k, new_v = kk[:, -rows:], vv[:, -rows:]
    xc, new_buf = causal_conv(xr, conv_buf, conv_w, conv_b)
    hs, new_h = rg_lru(xc, h0, wa, ba, wx, bx, lam)
    rec = hs * jax.nn.gelu(gr)
    out = jnp.concatenate([att, rec], axis=-1) @ w_out
    return out, new_k, new_v, new_h, new_buf


def gmlp_mixer(x, w_in, v_g, ws, bs, w_out):
    bsz, t, _ = x.shape
    z = jax.nn.gelu(x @ w_in)
    u = z[..., :D_C]
    v = layer_norm(z[..., D_C:], v_g)
    rows = min(t, CHUNK_MLP)
    nc = t // rows
    mask = jnp.tril(jnp.ones((rows, rows), dtype=bool))
    w = jnp.where(mask[None], ws[:, :rows, :rows], 0.0)
    vb = v.reshape(bsz, nc, rows, C_GROUPS, C_GW)
    sv = jnp.einsum('gts,bcsgw->bctgw', w.astype(v.dtype), vb) + bs[:, :rows].T[None, None, :, :, None]
    out = (u * sv.reshape(bsz, t, D_C)) @ w_out
    return out, v


def channel_mlp(x, w1, w2):
    return jnp.square(jax.nn.relu(x @ w1)) @ w2


def setup_inputs(seed: int = 0) -> dict:
    key = jax.random.key(seed)
    ks = iter(jax.random.split(key, 40))

    def nrm(shape, scale):
        return scale * jax.random.normal(next(ks), shape, jnp.float32)

    e, o = N_EVEN, N_ODD
    u = jax.random.uniform(next(ks), (e, D_LRU), jnp.float32, 0.9, 0.999)
    sg = u ** (1.0 / LRU_C)
    lam = jnp.log(sg) - jnp.log1p(-sg)
    return {
        'x_prompt': nrm((BATCH, SEQ, D_MODEL), 1.0),
        'x_sample': nrm((DEC_BATCH, DEC_SEQ, D_MODEL), 1.0),
        'cache_swa_k': nrm((e, DEC_BATCH, SWA_ROWS, N_KV_HEADS, HEAD_DIM), 1.0),
        'cache_swa_v': nrm((e, DEC_BATCH, SWA_ROWS, N_KV_HEADS, HEAD_DIM), 1.0),
        'state_lru_h': nrm((e, DEC_BATCH, D_LRU), 0.5),
        'state_lru_conv': nrm((e, DEC_BATCH, CONV_W - 1, D_LRU), 1.0),
        'e_norm_g': 1.0 + nrm((e, D_MODEL), 0.02),
        'e_w_in': nrm((e, D_MODEL, EVEN_IN), D_MODEL ** -0.5),
        'e_q_norm_g': 1.0 + nrm((e, HEAD_DIM), 0.02),
        'e_k_norm_g': 1.0 + nrm((e, HEAD_DIM), 0.02),
        'e_sinks': nrm((e, N_Q_HEADS), 0.5),
        'e_conv_w': nrm((e, CONV_W, D_LRU), CONV_W ** -0.5),
        'e_conv_b': nrm((e, D_LRU), 0.02),
        'e_gate_a_w': nrm((e, LRU_BLOCKS, LRU_BW, LRU_BW), LRU_BW ** -0.5),
        'e_gate_a_b': nrm((e, D_LRU), 0.02),
        'e_gate_x_w': nrm((e, LRU_BLOCKS, LRU_BW, LRU_BW), LRU_BW ** -0.5),
        'e_gate_x_b': nrm((e, D_LRU), 0.02),
        'e_lru_lambda': lam,
        'e_w_out': nrm((e, EVEN_MIX, D_MODEL), EVEN_MIX ** -0.5),
        'o_norm_g': 1.0 + nrm((o, D_MODEL), 0.02),
        'o_w_in': nrm((o, D_MODEL, 2 * D_C), D_MODEL ** -0.5),
        'o_v_norm_g': 1.0 + nrm((o, D_C), 0.02),
        'o_spatial_w': nrm((o, C_GROUPS, CHUNK_MLP, CHUNK_MLP), CHUNK_MLP ** -0.5),
        'o_spatial_b': 1.0 + nrm((o, C_GROUPS, CHUNK_MLP), 0.02),
        'o_w_out': nrm((o, D_C, D_MODEL), D_C ** -0.5),
        'ffn_norm_g': 1.0 + nrm((DEPTH, D_MODEL), 0.02),
        'ffn_w1': nrm((DEPTH, D_MODEL, D_FF), D_MODEL ** -0.5),
        'ffn_w2': nrm((DEPTH, D_FF, D_MODEL), D_FF ** -0.5),
    }


def reference(x_prompt, x_sample, cache_swa_k, cache_swa_v, state_lru_h, state_lru_conv,
              e_norm_g, e_w_in, e_q_norm_g, e_k_norm_g, e_sinks, e_conv_w, e_conv_b,
              e_gate_a_w, e_gate_a_b, e_gate_x_w, e_gate_x_b, e_lru_lambda, e_w_out,
              o_norm_g, o_w_in, o_v_norm_g, o_spatial_w, o_spatial_b, o_w_out,
              ffn_norm_g, ffn_w1, ffn_w2):
    pos_p = jnp.arange(x_prompt.shape[1])
    pos_s = PAST_LEN + jnp.arange(x_sample.shape[1])
    yp, ys = x_prompt, x_sample
    kp_l, vp_l, hp_l, cp_l = [], [], [], []
    ks_l, vs_l, hs_l, cs_l = [], [], [], []
    gv_l = []
    for layer in range(DEPTH):
        if layer % 2 == 0:
            e = layer // 2
            ew = (e_w_in[e], e_q_norm_g[e], e_k_norm_g[e], e_sinks[e], e_conv_w[e], e_conv_b[e],
                  e_gate_a_w[e], e_gate_a_b[e], e_gate_x_w[e], e_gate_x_b[e], e_lru_lambda[e], e_w_out[e])
            mp, kpn, vpn, hpn, cpn = even_mixer(rms_norm(yp, e_norm_g[e]), pos_p, None, None, None, None, *ew)
            ms, ksn, vsn, hsn, csn = even_mixer(rms_norm(ys, e_norm_g[e]), pos_s, cache_swa_k[e], cache_swa_v[e],
                                                state_lru_h[e], state_lru_conv[e], *ew)
            kp_l.append(kpn)
            vp_l.append(vpn)
            hp_l.append(hpn)
            cp_l.append(cpn)
            ks_l.append(ksn)
            vs_l.append(vsn)
            hs_l.append(hsn)
            cs_l.append(csn)
        else:
            o = layer // 2
            ow = (o_w_in[o], o_v_norm_g[o], o_spatial_w[o], o_spatial_b[o], o_w_out[o])
            mp, _ = gmlp_mixer(rms_norm(yp, o_norm_g[o]), *ow)
            ms, vsn = gmlp_mixer(rms_norm(ys, o_norm_g[o]), *ow)
            gv_l.append(vsn)
        yp = yp + mp
        ys = ys + ms
        yp = yp + channel_mlp(rms_norm(yp, ffn_norm_g[layer]), ffn_w1[layer], ffn_w2[layer])
        ys = ys + channel_mlp(rms_norm(ys, ffn_norm_g[layer]), ffn_w1[layer], ffn_w2[layer])
    return (yp, ys,
            jnp.stack(kp_l), jnp.stack(vp_l), jnp.stack(hp_l), jnp.stack(cp_l),
            jnp.stack(ks_l), jnp.stack(vs_l), jnp.stack(hs_l), jnp.stack(cs_l),
            jnp.stack(gv_l))
```

```python
import functools

import jax
import jax.numpy as jnp
from jax import lax
from jax.experimental import pallas as pl
from jax.experimental.pallas import tpu as pltpu

F32 = jnp.float32
BF16 = jnp.bfloat16

D_MODEL = 1024
CHUNK = 64
HEAD_DIM = 64
N_Q_HEADS = 8
N_KV_HEADS = 2
SWA_ROWS = 128
PAST_LEN = 4096
ROPE_THETA = 10000.0
NEG = -1e30
D_LRU = 512
LRU_BLOCKS = 8
LRU_BW = D_LRU // LRU_BLOCKS
CONV_W = 4
LRU_C = 8.0
Q_W = N_Q_HEADS * HEAD_DIM
KV_W = N_KV_HEADS * HEAD_DIM
CHUNK_MLP = 128
D_C = D_MODEL
C_GROUPS = 8
D_FF = 4 * D_MODEL
EPS = 1e-6

LANES = 128
KEY_BLOCK = 2 * CHUNK
BAND_BLOCKS = 3
VMEM_LIMIT = 56 * 1024 * 1024

PROMPT_TILE = 256
ROW_TILE = 512


def _rms_rows(x, g):
    return x * lax.rsqrt(jnp.mean(x * x, axis=-1, keepdims=True) + EPS) * g


def _lane_iota(shape):
    return lax.broadcasted_iota(jnp.int32, shape, 1)


def _row_iota(shape):
    return lax.broadcasted_iota(jnp.int32, shape, 0)


def _head_ones():
    return ((_row_iota((LANES, LANES)) // HEAD_DIM) == (_lane_iota((LANES, LANES)) // HEAD_DIM)).astype(BF16)


def _head_norm(xt, gain, ones_bd):
    sq = xt * xt
    hi = sq.astype(BF16)
    lo = (sq - hi.astype(F32)).astype(BF16)
    ss = jnp.dot(hi, ones_bd, preferred_element_type=F32) + jnp.dot(lo, ones_bd, preferred_element_type=F32)
    return xt * lax.rsqrt(ss * (1.0 / HEAD_DIM) + EPS) * gain


def _rope(xt, cos_t, sin_t):
    first_half = (_lane_iota(xt.shape) % HEAD_DIM) < (HEAD_DIM // 2)
    partner = jnp.where(first_half, pltpu.roll(xt, LANES - HEAD_DIM // 2, 1), pltpu.roll(xt, HEAD_DIM // 2, 1))
    return xt * cos_t + partner * sin_t


def _dup_head(t, j):
    lo = _lane_iota(t.shape) < HEAD_DIM
    swapped = pltpu.roll(t, HEAD_DIM, 1)
    return jnp.where(lo, t, swapped) if j == 0 else jnp.where(lo, swapped, t)


def _split_diag(t):
    lo = _lane_iota(t.shape) < HEAD_DIM
    return jnp.where(lo, t, 0.0), jnp.where(lo, 0.0, t)


def _ones_diag(rows):
    r = _row_iota((rows, LANES))
    return (((r % KEY_BLOCK) // CHUNK) == (_lane_iota((rows, LANES)) // HEAD_DIM)).astype(BF16)


def _attend(qs, kb, vb, sink_rows, col_mask):
    s = lax.dot_general(qs, kb, (((1,), (1,)), ((), ())), preferred_element_type=F32)
    if col_mask is not None:
        s = jnp.where(col_mask, s, NEG)
    mx = jnp.maximum(jnp.maximum(s[:, :LANES], s[:, LANES:2 * LANES]), s[:, 2 * LANES:])
    lo = _lane_iota(mx.shape) < HEAD_DIM
    m0 = jnp.max(jnp.where(lo, mx, -jnp.inf), axis=1, keepdims=True)
    m1 = jnp.max(jnp.where(lo, -jnp.inf, mx), axis=1, keepdims=True)
    m = jnp.maximum(jnp.where(lo, m0, m1), sink_rows)
    e = jnp.exp(s - jnp.concatenate([m, m, m], axis=1))
    od = jnp.dot(e.astype(BF16), vb, preferred_element_type=F32)
    den = od[:, LANES:] + jnp.exp(sink_rows - m)
    return od[:, :LANES] / den


def _lru_scan(a, b, seg, tmod):
    d = 1
    while d < seg:
        ok = tmod >= d
        b = jnp.where(ok, a * pltpu.roll(b, d, 0), 0.0) + b
        if 2 * d < seg:
            a = jnp.where(ok, a * pltpu.roll(a, d, 0), a)
        d *= 2
    return b


def _lru_branch(xs, gr, h_in, seg, tmod, convw_ref, convb_ref, wg_ref, ba_ref, bx_ref, lam_ref):
    xc = xs[0] * convw_ref[0:1, :] + convb_ref[...]
    for i in range(1, CONV_W):
        xc = xc + xs[i] * convw_ref[i:i + 1, :]
    xcb = xc.astype(BF16)
    half = D_LRU // 2
    rec_parts, h_parts = [], []
    for hh in range(2):
        g = jnp.dot(xcb[:, hh * half:(hh + 1) * half], wg_ref[hh], preferred_element_type=F32)
        for tt in range(2):
            c0 = hh * half + tt * LANES
            sl = slice(c0, c0 + LANES)
            r = jax.nn.sigmoid(g[:, tt * LANES:(tt + 1) * LANES] + ba_ref[:, sl])
            gi = jax.nn.sigmoid(g[:, half + tt * LANES:half + (tt + 1) * LANES] + bx_ref[:, sl])
            log_a = -LRU_C * r * jax.nn.softplus(-lam_ref[:, sl])
            a = jnp.exp(log_a)
            b = jnp.sqrt(-jnp.tanh(log_a) * (a * a + 1.0)) * (gi * xc[:, sl])
            b = b + jnp.where(tmod == 0, a * h_in[:, sl], 0.0)
            h = _lru_scan(a, b, seg, tmod)
            h_parts.append(h)
            rec_parts.append(h * jax.nn.gelu(gr[:, sl]))
    return jnp.concatenate(rec_parts, axis=1), jnp.concatenate(h_parts, axis=1)


def _even_prompt_kernel(x_ref, cos_ref, sin_ref, g_ref, win_ref, qg_ref, kg_ref, sink_ref, convw_ref, convb_ref,
                        wg_ref, ba_ref, bx_ref, lam_ref, wout_ref,
                        y_ref, nk_ref, nv_ref, nh_ref, nc_ref,
                        kb_scr, vb_scr, xr_scr, h_scr, mix_scr):
    t = pl.program_id(1)
    nt = pl.num_programs(1)
    tile = x_ref.shape[1]
    n_chunks = tile // CHUNK
    hist = 2 * KEY_BLOCK
    slot = t % 2

    @pl.when(t == 0)
    def _():
        ones = _ones_diag(vb_scr.shape[2])
        for j in range(N_KV_HEADS):
            kb_scr[0, j, 0:hist, :] = jnp.zeros((hist, LANES), BF16)
            vb_scr[0, j, 0:hist, 0:LANES] = jnp.zeros((hist, LANES), BF16)
            for s in range(2):
                vb_scr[s, j, :, LANES:2 * LANES] = ones
        xr_scr[0:8, :] = jnp.zeros((8, D_LRU), F32)
        h_scr[...] = jnp.zeros(h_scr.shape, F32)

    x = x_ref[0]
    xn = _rms_rows(x, g_ref[...]).astype(BF16)
    cos_t = cos_ref[...]
    sin_t = sin_ref[...]
    ones_bd = _head_ones()

    hq = jnp.dot(xn, win_ref[:, 0:Q_W], preferred_element_type=F32)
    q_tiles = []
    for n in range(Q_W // LANES):
        qt = _head_norm(hq[:, n * LANES:(n + 1) * LANES], qg_ref[...], ones_bd)
        q_tiles.append((_rope(qt, cos_t, sin_t) * (HEAD_DIM ** -0.5)).astype(BF16))
    hkv = jnp.dot(xn, win_ref[:, Q_W:Q_W + 2 * KV_W], preferred_element_type=F32)
    kr = _rope(_head_norm(hkv[:, 0:KV_W], kg_ref[...], ones_bd), cos_t, sin_t)
    vv = hkv[:, KV_W:2 * KV_W]
    for j in range(N_KV_HEADS):
        ktop, kbot = _split_diag(_dup_head(kr, j))
        vtop, vbot = _split_diag(_dup_head(vv, j))
        for i in range(n_chunks):
            r0 = hist + i * KEY_BLOCK
            rows = slice(i * CHUNK, (i + 1) * CHUNK)
            kb_scr[slot, j, r0:r0 + CHUNK, :] = ktop[rows].astype(BF16)
            kb_scr[slot, j, r0 + CHUNK:r0 + KEY_BLOCK, :] = kbot[rows].astype(BF16)
            vb_scr[slot, j, r0:r0 + CHUNK, 0:LANES] = vtop[rows].astype(BF16)
            vb_scr[slot, j, r0 + CHUNK:r0 + KEY_BLOCK, 0:LANES] = vbot[rows].astype(BF16)

    band = BAND_BLOCKS * KEY_BLOCK
    col = _lane_iota((2 * CHUNK, band))
    for i in range(n_chunks):
        rows = slice(i * CHUNK, (i + 1) * CHUNK)
        col_mask = None
        if i < BAND_BLOCKS - 1:
            col_mask = col >= (1 - jnp.minimum(t, 1)) * ((BAND_BLOCKS - 1 - i) * KEY_BLOCK)
        for j in range(N_KV_HEADS):
            qs = jnp.concatenate([q_tiles[2 * j][rows], q_tiles[2 * j + 1][rows]], axis=0)
            sink_rows = jnp.concatenate(
                [jnp.broadcast_to(sink_ref[2 * j:2 * j + 1, :], (CHUNK, LANES)),
                 jnp.broadcast_to(sink_ref[2 * j + 1:2 * j + 2, :], (CHUNK, LANES))], axis=0)
            kb = kb_scr[slot, j, i * KEY_BLOCK:i * KEY_BLOCK + band, :]
            vb = vb_scr[slot, j, i * KEY_BLOCK:i * KEY_BLOCK + band, :]
            o = _attend(qs, kb, vb, sink_rows, col_mask)
            c0 = 2 * j * LANES
            mix_scr[rows, c0:c0 + LANES] = o[0:CHUNK].astype(BF16)
            mix_scr[rows, c0 + LANES:c0 + 2 * LANES] = o[CHUNK:2 * CHUNK].astype(BF16)

    last = n_chunks * KEY_BLOCK
    for j in range(N_KV_HEADS):
        kb_scr[1 - slot, j, 0:hist, :] = kb_scr[slot, j, last:last + hist, :]
        vb_scr[1 - slot, j, 0:hist, 0:LANES] = vb_scr[slot, j, last:last + hist, 0:LANES]

    o3 = Q_W + 2 * KV_W
    xr = jnp.dot(xn, win_ref[:, o3:o3 + D_LRU], preferred_element_type=F32)
    gr = jnp.dot(xn, win_ref[:, o3 + D_LRU:o3 + 2 * D_LRU], preferred_element_type=F32)
    xr_scr[8:8 + tile, :] = xr
    xs = [xr_scr[8 - (CONV_W - 1 - i):8 - (CONV_W - 1 - i) + tile, :] for i in range(CONV_W - 1)] + [xr]
    tmod = _row_iota((tile, LANES))
    h_in = jnp.broadcast_to(h_scr[0:1, :], (tile, D_LRU))
    rec, h = _lru_branch(xs, gr, h_in, tile, tmod, convw_ref, convb_ref, wg_ref, ba_ref, bx_ref, lam_ref)
    mix_scr[:, Q_W:Q_W + D_LRU] = rec.astype(BF16)
    h_scr[0:1, :] = h[tile - 1:tile, :]
    xr_scr[0:8, :] = xr[tile - 8:tile, :]

    y_ref[0] = x + jnp.dot(mix_scr[...], wout_ref[...], preferred_element_type=F32)

    @pl.when(t == nt - 1)
    def _():
        nk_ref[0] = kr[tile - SWA_ROWS:tile, :]
        nv_ref[0] = vv[tile - SWA_ROWS:tile, :]
        nh_ref[0] = h[tile - 1:tile, :]
        nc_ref[0] = xr[tile - (CONV_W - 1):tile, :]


def _const_spec(shape):
    zeros = (0,) * len(shape)
    return pl.BlockSpec(shape, lambda *_: zeros)


def _even_prompt(x, cos_t, sin_t, p):
    bsz, s_len, _ = x.shape
    tile = PROMPT_TILE
    nt = s_len // tile
    key_rows = (tile // CHUNK + 2) * KEY_BLOCK
    consts = [p['g'], p['win'], p['qg'], p['kg'], p['sink'], p['convw'], p['convb'], p['wg'], p['ba'], p['bx'],
              p['lam'], p['wout']]
    out_shape = [
        jax.ShapeDtypeStruct((bsz, s_len, D_MODEL), F32),
        jax.ShapeDtypeStruct((bsz, SWA_ROWS, KV_W), F32),
        jax.ShapeDtypeStruct((bsz, SWA_ROWS, KV_W), F32),
        jax.ShapeDtypeStruct((bsz, 1, D_LRU), F32),
        jax.ShapeDtypeStruct((bsz, CONV_W - 1, D_LRU), F32),
    ]
    return pl.pallas_call(
        _even_prompt_kernel,
        out_shape=out_shape,
        grid=(bsz, nt),
        in_specs=[pl.BlockSpec((1, tile, D_MODEL), lambda b, t: (b, t, 0)),
                  pl.BlockSpec((tile, LANES), lambda b, t: (t, 0)),
                  pl.BlockSpec((tile, LANES), lambda b, t: (t, 0))] + [_const_spec(c.shape) for c in consts],
        out_specs=[pl.BlockSpec((1, tile, D_MODEL), lambda b, t: (b, t, 0)),
                   pl.BlockSpec((1, SWA_ROWS, KV_W), lambda b, t: (b, 0, 0)),
                   pl.BlockSpec((1, SWA_ROWS, KV_W), lambda b, t: (b, 0, 0)),
                   pl.BlockSpec((1, 1, D_LRU), lambda b, t: (b, 0, 0)),
                   pl.BlockSpec((1, CONV_W - 1, D_LRU), lambda b, t: (b, 0, 0))],
        scratch_shapes=[pltpu.VMEM((2, N_KV_HEADS, key_rows, LANES), BF16),
                        pltpu.VMEM((2, N_KV_HEADS, key_rows, 2 * LANES), BF16),
                        pltpu.VMEM((tile + 8, D_LRU), F32),
                        pltpu.VMEM((8, D_LRU), F32),
                        pltpu.VMEM((tile, Q_W + D_LRU), BF16)],
        compiler_params=pltpu.CompilerParams(dimension_semantics=("arbitrary", "arbitrary"),
                                             vmem_limit_bytes=VMEM_LIMIT),
        name="even_mixer_prompt",
    )(x, cos_t, sin_t, *consts)


def _even_sample_kernel(x_ref, cos_ref, sin_ref, kc_ref, vc_ref, h0_ref, xprev_ref,
                        g_ref, win_ref, qg_ref, kg_ref, sink_ref, convw_ref, convb_ref,
                        wg_ref, ba_ref, bx_ref, lam_ref, wout_ref,
                        y_ref, nk_ref, nv_ref, h_ref, xr_ref,
                        q_scr, k_scr, v_scr, mix_scr):
    rows_all = x_ref.shape[0]
    nseq = kc_ref.shape[0]
    t_len = rows_all // nseq
    x = x_ref[...]
    xn = _rms_rows(x, g_ref[...]).astype(BF16)
    cos_t = cos_ref[...]
    sin_t = sin_ref[...]
    ones_bd = _head_ones()

    hq = jnp.dot(xn, win_ref[:, 0:Q_W], preferred_element_type=F32)
    for n in range(Q_W // LANES):
        qt = _head_norm(hq[:, n * LANES:(n + 1) * LANES], qg_ref[...], ones_bd)
        q_scr[:, n * LANES:(n + 1) * LANES] = _rope(qt, cos_t, sin_t) * (HEAD_DIM ** -0.5)
    hkv = jnp.dot(xn, win_ref[:, Q_W:Q_W + 2 * KV_W], preferred_element_type=F32)
    k_scr[...] = _rope(_head_norm(hkv[:, 0:KV_W], kg_ref[...], ones_bd), cos_t, sin_t)
    v_scr[...] = hkv[:, KV_W:2 * KV_W]

    band = BAND_BLOCKS * KEY_BLOCK
    col = _lane_iota((2 * t_len, band))
    col_mask = jnp.logical_or(col < 2 * KEY_BLOCK, (col % CHUNK) < t_len)
    ones_band = _ones_diag(band)
    pad = jnp.zeros((CHUNK - t_len, LANES), F32)

    def seq_body(b, carry):
        r0 = pl.multiple_of(b * t_len, t_len)
        kc = kc_ref[b]
        vc = vc_ref[b]
        knew = k_scr[pl.ds(r0, t_len), :]
        vnew = v_scr[pl.ds(r0, t_len), :]
        nk_ref[b, 0:SWA_ROWS - t_len, :] = kc[t_len:SWA_ROWS]
        nk_ref[b, SWA_ROWS - t_len:SWA_ROWS, :] = knew
        nv_ref[b, 0:SWA_ROWS - t_len, :] = vc[t_len:SWA_ROWS]
        nv_ref[b, SWA_ROWS - t_len:SWA_ROWS, :] = vnew
        kfull = jnp.concatenate([kc, knew, pad], axis=0)
        vfull = jnp.concatenate([vc, vnew, pad], axis=0)
        for j in range(N_KV_HEADS):
            ktop, kbot = _split_diag(_dup_head(kfull, j))
            vtop, vbot = _split_diag(_dup_head(vfull, j))
            kparts, vparts = [], []
            for c in range(BAND_BLOCKS):
                rs = slice(c * CHUNK, (c + 1) * CHUNK)
                kparts += [ktop[rs], kbot[rs]]
                vparts += [vtop[rs], vbot[rs]]
            kb = jnp.concatenate(kparts, axis=0).astype(BF16)
            vb = jnp.concatenate([jnp.concatenate(vparts, axis=0).astype(BF16), ones_band], axis=1)
            c0 = 2 * j * LANES
            qs = jnp.concatenate([q_scr[pl.ds(r0, t_len), c0:c0 + LANES],
                                  q_scr[pl.ds(r0, t_len), c0 + LANES:c0 + 2 * LANES]], axis=0).astype(BF16)
            sink_rows = jnp.concatenate(
                [jnp.broadcast_to(sink_ref[2 * j:2 * j + 1, :], (t_len, LANES)),
                 jnp.broadcast_to(sink_ref[2 * j + 1:2 * j + 2, :], (t_len, LANES))], axis=0)
            o = _attend(qs, kb, vb, sink_rows, col_mask)
            mix_scr[pl.ds(r0, t_len), c0:c0 + LANES] = o[0:t_len]
            mix_scr[pl.ds(r0, t_len), c0 + LANES:c0 + 2 * LANES] = o[t_len:2 * t_len]
        return carry

    lax.fori_loop(0, nseq, seq_body, 0)

    o3 = Q_W + 2 * KV_W
    xr = jnp.dot(xn, win_ref[:, o3:o3 + D_LRU], preferred_element_type=F32)
    gr = jnp.dot(xn, win_ref[:, o3 + D_LRU:o3 + 2 * D_LRU], preferred_element_type=F32)
    tmod = _row_iota((rows_all, LANES)) % t_len
    tmod_w = _row_iota((rows_all, D_LRU)) % t_len
    xprev = xprev_ref[...]
    xs = []
    for i in range(CONV_W - 1):
        k = CONV_W - 1 - i
        own = pltpu.roll(xr, k, 0)
        cached = xprev if k == CONV_W - 1 else pltpu.roll(xprev, rows_all - (CONV_W - 1 - k), 0)
        xs.append(jnp.where(tmod_w >= k, own, cached))
    xs.append(xr)
    rec, h = _lru_branch(xs, gr, h0_ref[...], t_len, tmod, convw_ref, convb_ref, wg_ref, ba_ref, bx_ref, lam_ref)
    mix_scr[:, Q_W:Q_W + D_LRU] = rec
    xr_ref[...] = xr
    h_ref[...] = h
    y_ref[...] = x + jnp.dot(mix_scr[...].astype(BF16), wout_ref[...], preferred_element_type=F32)


def _even_sample(x2, cos_t, sin_t, kc, vc, h0_rows, xprev, p):
    rows_all = x2.shape[0]
    nseq = kc.shape[0]
    consts = [p['g'], p['win'], p['qg'], p['kg'], p['sink'], p['convw'], p['convb'], p['wg'], p['ba'], p['bx'],
              p['lam'], p['wout']]
    ins = [x2, cos_t, sin_t, kc, vc, h0_rows, xprev] + consts
    out_shape = [
        jax.ShapeDtypeStruct((rows_all, D_MODEL), F32),
        jax.ShapeDtypeStruct((nseq, SWA_ROWS, KV_W), F32),
        jax.ShapeDtypeStruct((nseq, SWA_ROWS, KV_W), F32),
        jax.ShapeDtypeStruct((rows_all, D_LRU), F32),
        jax.ShapeDtypeStruct((rows_all, D_LRU), F32),
    ]
    return pl.pallas_call(
        _even_sample_kernel,
        out_shape=out_shape,
        grid=(1,),
        in_specs=[_const_spec(a.shape) for a in ins],
        out_specs=[_const_spec(s.shape) for s in out_shape],
        scratch_shapes=[pltpu.VMEM((rows_all, Q_W), F32),
                        pltpu.VMEM((rows_all, KV_W), F32),
                        pltpu.VMEM((rows_all, KV_W), F32),
                        pltpu.VMEM((rows_all, Q_W + D_LRU), F32)],
        compiler_params=pltpu.CompilerParams(dimension_semantics=("arbitrary",), vmem_limit_bytes=VMEM_LIMIT),
        name="even_mixer_sample",
    )(*ins)


def _ffn_kernel(x_ref, g_ref, w1_ref, w2_ref, y_ref):
    x = x_ref[...]
    xn = _rms_rows(x, g_ref[...]).astype(BF16)
    acc = x
    step = D_MODEL
    for c in range(D_FF // step):
        hcol = jnp.dot(xn, w1_ref[:, c * step:(c + 1) * step], preferred_element_type=F32)
        hcol = jnp.square(jnp.maximum(hcol, 0.0)).astype(BF16)
        acc = acc + jnp.dot(hcol, w2_ref[c * step:(c + 1) * step, :], preferred_element_type=F32)
    y_ref[...] = acc


def _ffn(x2, g, w1, w2):
    rows = x2.shape[0]
    tile = min(ROW_TILE, rows)
    return pl.pallas_call(
        _ffn_kernel,
        out_shape=jax.ShapeDtypeStruct(x2.shape, F32),
        grid=(rows // tile,),
        in_specs=[pl.BlockSpec((tile, D_MODEL), lambda i: (i, 0)), _const_spec(g.shape), _const_spec(w1.shape),
                  _const_spec(w2.shape)],
        out_specs=pl.BlockSpec((tile, D_MODEL), lambda i: (i, 0)),
        compiler_params=pltpu.CompilerParams(dimension_semantics=("parallel",), vmem_limit_bytes=VMEM_LIMIT),
        name="channel_mlp",
    )(x2, g, w1, w2)


def _gmlp_kernel(seg, x_ref, g_ref, win_ref, vg_ref, ws_ref, bias_ref, wout_ref, y_ref, v_ref, gate_scr):
    tile = x_ref.shape[0]
    x = x_ref[...]
    xn = _rms_rows(x, g_ref[...]).astype(BF16)
    u = jax.nn.gelu(jnp.dot(xn, win_ref[:, 0:D_C], preferred_element_type=F32))
    zv = jax.nn.gelu(jnp.dot(xn, win_ref[:, D_C:2 * D_C], preferred_element_type=F32))
    zc = zv - jnp.mean(zv, axis=-1, keepdims=True)
    v = zc * lax.rsqrt(jnp.mean(zc * zc, axis=-1, keepdims=True) + EPS) * vg_ref[...]
    v_ref[...] = v
    vb = v.astype(BF16)
    r = _row_iota((CHUNK_MLP, CHUNK_MLP))
    c = _lane_iota((CHUNK_MLP, CHUNK_MLP))
    keep = jnp.logical_and((r // seg) == (c // seg), c <= r)
    bias = bias_ref[...]
    for gi in range(C_GROUPS):
        w = jnp.where(keep, ws_ref[gi], 0.0).astype(BF16)
        lanes = slice(gi * LANES, (gi + 1) * LANES)
        for ch in range(tile // CHUNK_MLP):
            rows = slice(ch * CHUNK_MLP, (ch + 1) * CHUNK_MLP)
            sv = jnp.dot(w, vb[rows, lanes], preferred_element_type=F32) + bias[:, lanes]
            gate_scr[rows, lanes] = (u[rows, lanes] * sv).astype(BF16)
    y_ref[...] = x + jnp.dot(gate_scr[...], wout_ref[...], preferred_element_type=F32)


def _gmlp(x2, seg, g, win, vg, ws, bias, wout):
    rows = x2.shape[0]
    tile = min(ROW_TILE, rows)
    return pl.pallas_call(
        functools.partial(_gmlp_kernel, seg),
        out_shape=[jax.ShapeDtypeStruct(x2.shape, F32), jax.ShapeDtypeStruct((rows, D_C), F32)],
        grid=(rows // tile,),
        in_specs=[pl.BlockSpec((tile, D_MODEL), lambda i: (i, 0))] +
                 [_const_spec(a.shape) for a in (g, win, vg, ws, bias, wout)],
        out_specs=[pl.BlockSpec((tile, D_MODEL), lambda i: (i, 0)), pl.BlockSpec((tile, D_C), lambda i: (i, 0))],
        scratch_shapes=[pltpu.VMEM((tile, D_C), BF16)],
        compiler_params=pltpu.CompilerParams(dimension_semantics=("parallel",), vmem_limit_bytes=VMEM_LIMIT),
        name="gmlp_mixer",
    )(x2, g, win, vg, ws, bias, wout)


def _rope_tables(pos):
    half = HEAD_DIM // 2
    inv = ROPE_THETA ** (-jnp.arange(half, dtype=F32) / half)
    ang = pos.astype(F32)[:, None] * inv[None, :]
    cos = jnp.cos(ang)
    sin = jnp.sin(ang)
    return jnp.concatenate([cos] * 4, axis=1), jnp.concatenate([-sin, sin, -sin, sin], axis=1)


def _gate_weights(wa, wx):
    def half_diag(w, hh):
        out = jnp.zeros((D_LRU // 2, D_LRU // 2), F32)
        for n in range(LRU_BLOCKS // 2):
            out = out.at[n * LRU_BW:(n + 1) * LRU_BW, n * LRU_BW:(n + 1) * LRU_BW].set(w[hh * 4 + n])
        return out
    return jnp.stack([jnp.concatenate([half_diag(wa, hh), half_diag(wx, hh)], axis=1)
                      for hh in range(2)]).astype(BF16)


def kernel(x_prompt, x_sample, cache_swa_k, cache_swa_v, state_lru_h, state_lru_conv, e_norm_g, e_w_in, e_q_norm_g, e_k_norm_g, e_sinks, e_conv_w, e_conv_b, e_gate_a_w, e_gate_a_b, e_gate_x_w, e_gate_x_b, e_lru_lambda, e_w_out, o_norm_g, o_w_in, o_v_norm_g, o_spatial_w, o_spatial_b, o_w_out, ffn_norm_g, ffn_w1, ffn_w2):
    bsz, s_len, _ = x_prompt.shape
    nseq, t_len, _ = x_sample.shape
    past_len = PAST_LEN
    row = lambda a: a.reshape(1, -1)

    e = 0
    p = {
        'g': row(e_norm_g[e]),
        'win': e_w_in[e].astype(BF16),
        'qg': row(jnp.tile(e_q_norm_g[e], LANES // HEAD_DIM)),
        'kg': row(jnp.tile(e_k_norm_g[e], LANES // HEAD_DIM)),
        'sink': jnp.repeat(e_sinks[e], HEAD_DIM).reshape(N_Q_HEADS // 2, LANES),
        'convw': e_conv_w[e],
        'convb': row(e_conv_b[e]),
        'wg': _gate_weights(e_gate_a_w[e], e_gate_x_w[e]),
        'ba': row(e_gate_a_b[e]),
        'bx': row(e_gate_x_b[e]),
        'lam': row(e_lru_lambda[e]),
        'wout': e_w_out[e].astype(BF16),
    }
    cos_p, sin_p = _rope_tables(jnp.arange(s_len))
    yp, nk_p, nv_p, nh_p, nc_p = _even_prompt(x_prompt, cos_p, sin_p, p)

    cos_s, sin_s = _rope_tables(past_len + jnp.arange(t_len))
    cos_s = jnp.tile(cos_s, (nseq, 1))
    sin_s = jnp.tile(sin_s, (nseq, 1))
    kc = cache_swa_k[e].reshape(nseq, SWA_ROWS, KV_W)
    vc = cache_swa_v[e].reshape(nseq, SWA_ROWS, KV_W)
    h0_rows = jnp.repeat(state_lru_h[e], t_len, axis=0)
    xprev = jnp.pad(state_lru_conv[e], ((0, 0), (0, t_len - (CONV_W - 1)), (0, 0))).reshape(nseq * t_len, D_LRU)
    ys, nk_s, nv_s, h_s, xr_s = _even_sample(x_sample.reshape(nseq * t_len, D_MODEL), cos_s, sin_s, kc, vc,
                                             h0_rows, xprev, p)
    nh_s = h_s.reshape(nseq, t_len, D_LRU)[:, -1]
    nc_s = xr_s.reshape(nseq, t_len, D_LRU)[:, t_len - (CONV_W - 1):]

    w1 = ffn_w1.astype(BF16)
    w2 = ffn_w2.astype(BF16)
    yp = _ffn(yp.reshape(bsz * s_len, D_MODEL), row(ffn_norm_g[0]), w1[0], w2[0])
    ys = _ffn(ys, row(ffn_norm_g[0]), w1[0], w2[0])

    o = 0
    gw = (row(o_norm_g[o]), o_w_in[o].astype(BF16), row(o_v_norm_g[o]))
    wout_o = o_w_out[o].astype(BF16)
    bias_p = jnp.repeat(o_spatial_b[o].T, LANES, axis=1)
    yp, _ = _gmlp(yp, CHUNK_MLP, *gw, o_spatial_w[o], bias_p, wout_o)
    reps = CHUNK_MLP // t_len
    ws_s = jnp.tile(o_spatial_w[o][:, :t_len, :t_len], (1, reps, reps))
    bias_s = jnp.tile(jnp.repeat(o_spatial_b[o][:, :t_len].T, LANES, axis=1), (reps, 1))
    ys, gv_s = _gmlp(ys, t_len, *gw, ws_s, bias_s, wout_o)

    yp = _ffn(yp, row(ffn_norm_g[1]), w1[1], w2[1])
    ys = _ffn(ys, row(ffn_norm_g[1]), w1[1], w2[1])

    kv_shape = (1, -1, SWA_ROWS, N_KV_HEADS, HEAD_DIM)
    return (yp.reshape(bsz, s_len, D_MODEL), ys.reshape(nseq, t_len, D_MODEL),
            nk_p.reshape(kv_shape), nv_p.reshape(kv_shape), nh_p.reshape(1, bsz, D_LRU), nc_p[None],
            nk_s.reshape(kv_shape), nv_s.reshape(kv_shape), nh_s[None], nc_s[None],
            gv_s.reshape(1, nseq, t_len, D_C))
```

```python
import functools

import jax
import jax.numpy as jnp
from jax import lax
from jax.experimental import pallas as pl
from jax.experimental.pallas import tpu as pltpu

F32 = jnp.float32
BF16 = jnp.bfloat16

D_MODEL = 1024
CHUNK = 64
HEAD_DIM = 64
N_Q_HEADS = 8
N_KV_HEADS = 2
SWA_ROWS = 128
PAST_LEN = 4096
ROPE_THETA = 10000.0
NEG = -1e30
D_LRU = 512
LRU_BLOCKS = 8
LRU_BW = D_LRU // LRU_BLOCKS
CONV_W = 4
LRU_C = 8.0
Q_W = N_Q_HEADS * HEAD_DIM
KV_W = N_KV_HEADS * HEAD_DIM
CHUNK_MLP = 128
D_C = D_MODEL
C_GROUPS = 8
D_FF = 4 * D_MODEL
EPS = 1e-6

LANES = 128
SUBLANES = 8
KEY_BLOCK = 2 * CHUNK
BAND_BLOCKS = 3
VMEM_LIMIT = 56 * 1024 * 1024

PROMPT_TILE = 256
ROW_TILE = 512


def _rms_rows(x, g):
    return x * lax.rsqrt(jnp.mean(x * x, axis=-1, keepdims=True) + EPS) * g


def _lane_iota(shape):
    return lax.broadcasted_iota(jnp.int32, shape, 1)


def _row_iota(shape):
    return lax.broadcasted_iota(jnp.int32, shape, 0)


def _head_norm(xt, gain):
    sq = xt * xt
    lo = _lane_iota(xt.shape) < HEAD_DIM
    s0 = jnp.sum(jnp.where(lo, sq, 0.0), axis=1, keepdims=True)
    s1 = jnp.sum(jnp.where(lo, 0.0, sq), axis=1, keepdims=True)
    ss = jnp.where(lo, s0, s1)
    return xt * lax.rsqrt(ss * (1.0 / HEAD_DIM) + EPS) * gain


def _rope(xt, cos_t, sin_t):
    first_half = (_lane_iota(xt.shape) % HEAD_DIM) < (HEAD_DIM // 2)
    partner = jnp.where(first_half, pltpu.roll(xt, LANES - HEAD_DIM // 2, 1), pltpu.roll(xt, HEAD_DIM // 2, 1))
    return xt * cos_t + partner * sin_t


def _dup_head(t, j):
    lo = _lane_iota(t.shape) < HEAD_DIM
    swapped = pltpu.roll(t, HEAD_DIM, 1)
    return jnp.where(lo, t, swapped) if j == 0 else jnp.where(lo, swapped, t)


def _split_diag(t):
    lo = _lane_iota(t.shape) < HEAD_DIM
    return jnp.where(lo, t, 0.0), jnp.where(lo, 0.0, t)


def _ones_diag(rows):
    r = _row_iota((rows, LANES))
    return (((r % KEY_BLOCK) // CHUNK) == (_lane_iota((rows, LANES)) // HEAD_DIM)).astype(BF16)


def _attend(qs, kb, vb, sink_rows, col_mask):
    e, m = _attend_weights(_attend_scores(qs, kb, col_mask), sink_rows)
    return _attend_output(e, m, vb, sink_rows)


def _attend_scores(qs, kb, col_mask):
    s = lax.dot_general(qs, kb, (((1,), (1,)), ((), ())), preferred_element_type=F32)
    return s if col_mask is None else jnp.where(col_mask, s, NEG)


def _attend_weights(s, sink_rows):
    mx = jnp.maximum(jnp.maximum(s[:, :LANES], s[:, LANES:2 * LANES]), s[:, 2 * LANES:])
    lo = _lane_iota(mx.shape) < HEAD_DIM
    m0 = jnp.max(jnp.where(lo, mx, -jnp.inf), axis=1, keepdims=True)
    m1 = jnp.max(jnp.where(lo, -jnp.inf, mx), axis=1, keepdims=True)
    m = jnp.maximum(jnp.where(lo, m0, m1), sink_rows)
    e = jnp.exp(s - jnp.concatenate([m, m, m], axis=1))
    return e.astype(BF16), m


def _attend_output(e, m, vb, sink_rows):
    od = jnp.dot(e, vb, preferred_element_type=F32)
    den = od[:, LANES:] + jnp.exp(sink_rows - m)
    return od[:, :LANES] / den


def _lru_scan(a, b, seg, tmod):
    d = 1
    while d < seg:
        ok = tmod >= d
        b = jnp.where(ok, a * pltpu.roll(b, d, 0), 0.0) + b
        if 2 * d < seg:
            a = jnp.where(ok, a * pltpu.roll(a, d, 0), a)
        d *= 2
    return b


def _lru_scan_tile(a, b, h_prev):
    rows = a.shape[0]
    groups = rows // SUBLANES
    a3 = a.reshape(groups, SUBLANES, LANES)
    b3 = b.reshape(groups, SUBLANES, LANES)
    sub = lax.broadcasted_iota(jnp.int32, a3.shape, 1)
    d = 1
    while d < SUBLANES:
        ok = sub >= d
        b3 = jnp.where(ok, a3 * pltpu.roll(b3, d, 1), 0.0) + b3
        a3 = jnp.where(ok, a3 * pltpu.roll(a3, d, 1), a3)
        d *= 2
    carry = jnp.broadcast_to(h_prev, (SUBLANES, LANES))
    out = []
    for g in range(groups):
        hg = a3[g] * carry + b3[g]
        out.append(hg)
        carry = jnp.broadcast_to(hg[SUBLANES - 1:SUBLANES, :], (SUBLANES, LANES))
    return jnp.concatenate(out, axis=0)


def _lru_branch(xs, gr, h_in, seg, tmod, convw_ref, convb_ref, wg_ref, ba_ref, bx_ref, lam_ref):
    xc = xs[0] * convw_ref[0:1, :] + convb_ref[...]
    for i in range(1, CONV_W):
        xc = xc + xs[i] * convw_ref[i:i + 1, :]
    xcb = xc.astype(BF16)
    half = D_LRU // 2
    rec_parts, h_parts = [], []
    for hh in range(2):
        g = jnp.dot(xcb[:, hh * half:(hh + 1) * half], wg_ref[hh], preferred_element_type=F32)
        for tt in range(2):
            c0 = hh * half + tt * LANES
            sl = slice(c0, c0 + LANES)
            r = jax.nn.sigmoid(g[:, tt * LANES:(tt + 1) * LANES] + ba_ref[:, sl])
            gi = jax.nn.sigmoid(g[:, half + tt * LANES:half + (tt + 1) * LANES] + bx_ref[:, sl])
            log_a = -LRU_C * r * jax.nn.softplus(-lam_ref[:, sl])
            a = jnp.exp(log_a)
            b = jnp.sqrt(-jnp.tanh(log_a) * (a * a + 1.0)) * (gi * xc[:, sl])
            if seg is None:
                h = _lru_scan_tile(a, b, h_in[:, sl])
            else:
                h = _lru_scan(a, b + jnp.where(tmod == 0, a * h_in[:, sl], 0.0), seg, tmod)
            h_parts.append(h)
            rec_parts.append(h * jax.nn.gelu(gr[:, sl]))
    return jnp.concatenate(rec_parts, axis=1), jnp.concatenate(h_parts, axis=1)


def _even_prompt_kernel(x_ref, cos_ref, sin_ref, g_ref, win_ref, qg_ref, kg_ref, sink_ref, convw_ref, convb_ref,
                        wg_ref, ba_ref, bx_ref, lam_ref, wout_ref,
                        y_ref, nk_ref, nv_ref, nh_ref, nc_ref,
                        kb_scr, vb_scr, xr_scr, h_scr, mix_scr):
    t = pl.program_id(1)
    nt = pl.num_programs(1)
    tile = x_ref.shape[1]
    n_chunks = tile // CHUNK
    hist = 2 * KEY_BLOCK
    slot = t % 2

    @pl.when(t == 0)
    def _():
        ones = _ones_diag(vb_scr.shape[2])
        for j in range(N_KV_HEADS):
            kb_scr[0, j, 0:hist, :] = jnp.zeros((hist, LANES), BF16)
            vb_scr[0, j, 0:hist, 0:LANES] = jnp.zeros((hist, LANES), BF16)
            for s in range(2):
                vb_scr[s, j, :, LANES:2 * LANES] = ones
        xr_scr[0:8, :] = jnp.zeros((8, D_LRU), F32)
        h_scr[...] = jnp.zeros(h_scr.shape, F32)

    x = x_ref[0]
    xn = _rms_rows(x, g_ref[...]).astype(BF16)
    cos_t = cos_ref[...]
    sin_t = sin_ref[...]
    o3 = Q_W + 2 * KV_W
    hq = jnp.dot(xn, win_ref[:, 0:Q_W], preferred_element_type=F32)
    hkv = jnp.dot(xn, win_ref[:, Q_W:o3], preferred_element_type=F32)
    xr = jnp.dot(xn, win_ref[:, o3:o3 + D_LRU], preferred_element_type=F32)
    gr = jnp.dot(xn, win_ref[:, o3 + D_LRU:o3 + 2 * D_LRU], preferred_element_type=F32)

    q_tiles = []
    for n in range(Q_W // LANES):
        qt = _head_norm(hq[:, n * LANES:(n + 1) * LANES], qg_ref[...])
        q_tiles.append((_rope(qt, cos_t, sin_t) * (HEAD_DIM ** -0.5)).astype(BF16))
    kr = _rope(_head_norm(hkv[:, 0:KV_W], kg_ref[...]), cos_t, sin_t)
    vv = hkv[:, KV_W:2 * KV_W]
    for j in range(N_KV_HEADS):
        ktop, kbot = _split_diag(_dup_head(kr, j))
        vtop, vbot = _split_diag(_dup_head(vv, j))
        for i in range(n_chunks):
            r0 = hist + i * KEY_BLOCK
            rows = slice(i * CHUNK, (i + 1) * CHUNK)
            kb_scr[slot, j, r0:r0 + CHUNK, :] = ktop[rows].astype(BF16)
            kb_scr[slot, j, r0 + CHUNK:r0 + KEY_BLOCK, :] = kbot[rows].astype(BF16)
            vb_scr[slot, j, r0:r0 + CHUNK, 0:LANES] = vtop[rows].astype(BF16)
            vb_scr[slot, j, r0 + CHUNK:r0 + KEY_BLOCK, 0:LANES] = vbot[rows].astype(BF16)

    band = BAND_BLOCKS * KEY_BLOCK
    col = _lane_iota((2 * CHUNK, band))
    sink_rows = [jnp.concatenate(
        [jnp.broadcast_to(sink_ref[2 * j:2 * j + 1, :], (CHUNK, LANES)),
         jnp.broadcast_to(sink_ref[2 * j + 1:2 * j + 2, :], (CHUNK, LANES))], axis=0) for j in range(N_KV_HEADS)]
    pairs = [(i, j) for i in range(n_chunks) for j in range(N_KV_HEADS)]
    scores = []
    for i, j in pairs:
        rows = slice(i * CHUNK, (i + 1) * CHUNK)
        col_mask = None
        if i < BAND_BLOCKS - 1:
            col_mask = col >= (1 - jnp.minimum(t, 1)) * ((BAND_BLOCKS - 1 - i) * KEY_BLOCK)
        qs = jnp.concatenate([q_tiles[2 * j][rows], q_tiles[2 * j + 1][rows]], axis=0)
        scores.append(_attend_scores(qs, kb_scr[slot, j, i * KEY_BLOCK:i * KEY_BLOCK + band, :], col_mask))
    weights = [_attend_weights(s, sink_rows[j]) for s, (i, j) in zip(scores, pairs)]
    for (e, m), (i, j) in zip(weights, pairs):
        rows = slice(i * CHUNK, (i + 1) * CHUNK)
        o = _attend_output(e, m, vb_scr[slot, j, i * KEY_BLOCK:i * KEY_BLOCK + band, :], sink_rows[j])
        c0 = 2 * j * LANES
        mix_scr[rows, c0:c0 + LANES] = o[0:CHUNK].astype(BF16)
        mix_scr[rows, c0 + LANES:c0 + 2 * LANES] = o[CHUNK:2 * CHUNK].astype(BF16)

    last = n_chunks * KEY_BLOCK
    for j in range(N_KV_HEADS):
        kb_scr[1 - slot, j, 0:hist, :] = kb_scr[slot, j, last:last + hist, :]
        vb_scr[1 - slot, j, 0:hist, 0:LANES] = vb_scr[slot, j, last:last + hist, 0:LANES]

    xr_scr[8:8 + tile, :] = xr
    xs = [xr_scr[8 - (CONV_W - 1 - i):8 - (CONV_W - 1 - i) + tile, :] for i in range(CONV_W - 1)] + [xr]
    rec, h = _lru_branch(xs, gr, h_scr[0:1, :], None, None, convw_ref, convb_ref, wg_ref, ba_ref, bx_ref, lam_ref)
    mix_scr[:, Q_W:Q_W + D_LRU] = rec.astype(BF16)
    h_scr[0:1, :] = h[tile - 1:tile, :]
    xr_scr[0:8, :] = xr[tile - 8:tile, :]

    y_att = jnp.dot(mix_scr[:, 0:Q_W], wout_ref[0:Q_W, :], preferred_element_type=F32)
    y_rec = jnp.dot(mix_scr[:, Q_W:Q_W + D_LRU], wout_ref[Q_W:Q_W + D_LRU, :], preferred_element_type=F32)
    y_ref[0] = x + y_att + y_rec

    @pl.when(t == nt - 1)
    def _():
        nk_ref[0] = kr[tile - SWA_ROWS:tile, :]
        nv_ref[0] = vv[tile - SWA_ROWS:tile, :]
        nh_ref[0] = h[tile - 1:tile, :]
        nc_ref[0] = xr[tile - (CONV_W - 1):tile, :]


def _const_spec(shape):
    zeros = (0,) * len(shape)
    return pl.BlockSpec(shape, lambda *_: zeros)


def _even_prompt(x, cos_t, sin_t, p):
    bsz, s_len, _ = x.shape
    tile = PROMPT_TILE
    nt = s_len // tile
    key_rows = (tile // CHUNK + 2) * KEY_BLOCK
    consts = [p['g'], p['win'], p['qg'], p['kg'], p['sink'], p['convw'], p['convb'], p['wg'], p['ba'], p['bx'],
              p['lam'], p['wout']]
    out_shape = [
        jax.ShapeDtypeStruct((bsz, s_len, D_MODEL), F32),
        jax.ShapeDtypeStruct((bsz, SWA_ROWS, KV_W), F32),
        jax.ShapeDtypeStruct((bsz, SWA_ROWS, KV_W), F32),
        jax.ShapeDtypeStruct((bsz, 1, D_LRU), F32),
        jax.ShapeDtypeStruct((bsz, CONV_W - 1, D_LRU), F32),
    ]
    return pl.pallas_call(
        _even_prompt_kernel,
        out_shape=out_shape,
        grid=(bsz, nt),
        in_specs=[pl.BlockSpec((1, tile, D_MODEL), lambda b, t: (b, t, 0)),
                  pl.BlockSpec((tile, LANES), lambda b, t: (t, 0)),
                  pl.BlockSpec((tile, LANES), lambda b, t: (t, 0))] + [_const_spec(c.shape) for c in consts],
        out_specs=[pl.BlockSpec((1, tile, D_MODEL), lambda b, t: (b, t, 0)),
                   pl.BlockSpec((1, SWA_ROWS, KV_W), lambda b, t: (b, 0, 0)),
                   pl.BlockSpec((1, SWA_ROWS, KV_W), lambda b, t: (b, 0, 0)),
                   pl.BlockSpec((1, 1, D_LRU), lambda b, t: (b, 0, 0)),
                   pl.BlockSpec((1, CONV_W - 1, D_LRU), lambda b, t: (b, 0, 0))],
        scratch_shapes=[pltpu.VMEM((2, N_KV_HEADS, key_rows, LANES), BF16),
                        pltpu.VMEM((2, N_KV_HEADS, key_rows, 2 * LANES), BF16),
                        pltpu.VMEM((tile + 8, D_LRU), F32),
                        pltpu.VMEM((8, D_LRU), F32),
                        pltpu.VMEM((tile, Q_W + D_LRU), BF16)],
        compiler_params=pltpu.CompilerParams(dimension_semantics=("arbitrary", "arbitrary"),
                                             vmem_limit_bytes=VMEM_LIMIT),
        name="even_mixer_prompt",
    )(x, cos_t, sin_t, *consts)


def _even_sample_kernel(x_ref, cos_ref, sin_ref, kc_ref, vc_ref, h0_ref, xprev_ref,
                        g_ref, win_ref, qg_ref, kg_ref, sink_ref, convw_ref, convb_ref,
                        wg_ref, ba_ref, bx_ref, lam_ref, wout_ref,
                        y_ref, nk_ref, nv_ref, h_ref, xr_ref,
                        q_scr, k_scr, v_scr, mix_scr):
    rows_all = x_ref.shape[0]
    nseq = kc_ref.shape[0]
    t_len = rows_all // nseq
    x = x_ref[...]
    xn = _rms_rows(x, g_ref[...]).astype(BF16)
    cos_t = cos_ref[...]
    sin_t = sin_ref[...]

    hq = jnp.dot(xn, win_ref[:, 0:Q_W], preferred_element_type=F32)
    for n in range(Q_W // LANES):
        qt = _head_norm(hq[:, n * LANES:(n + 1) * LANES], qg_ref[...])
        q_scr[:, n * LANES:(n + 1) * LANES] = _rope(qt, cos_t, sin_t) * (HEAD_DIM ** -0.5)
    hkv = jnp.dot(xn, win_ref[:, Q_W:Q_W + 2 * KV_W], preferred_element_type=F32)
    k_scr[...] = _rope(_head_norm(hkv[:, 0:KV_W], kg_ref[...]), cos_t, sin_t)
    v_scr[...] = hkv[:, KV_W:2 * KV_W]

    band = BAND_BLOCKS * KEY_BLOCK
    col = _lane_iota((2 * t_len, band))
    col_mask = jnp.logical_or(col < 2 * KEY_BLOCK, (col % CHUNK) < t_len)
    ones_band = _ones_diag(band)
    pad = jnp.zeros((CHUNK - t_len, LANES), F32)

    def seq_body(b, carry):
        r0 = pl.multiple_of(b * t_len, t_len)
        kc = kc_ref[b]
        vc = vc_ref[b]
        knew = k_scr[pl.ds(r0, t_len), :]
        vnew = v_scr[pl.ds(r0, t_len), :]
        nk_ref[b, 0:SWA_ROWS - t_len, :] = kc[t_len:SWA_ROWS]
        nk_ref[b, SWA_ROWS - t_len:SWA_ROWS, :] = knew
        nv_ref[b, 0:SWA_ROWS - t_len, :] = vc[t_len:SWA_ROWS]
        nv_ref[b, SWA_ROWS - t_len:SWA_ROWS, :] = vnew
        kfull = jnp.concatenate([kc, knew, pad], axis=0)
        vfull = jnp.concatenate([vc, vnew, pad], axis=0)
        for j in range(N_KV_HEADS):
            ktop, kbot = _split_diag(_dup_head(kfull, j))
            vtop, vbot = _split_diag(_dup_head(vfull, j))
            kparts, vparts = [], []
            for c in range(BAND_BLOCKS):
                rs = slice(c * CHUNK, (c + 1) * CHUNK)
                kparts += [ktop[rs], kbot[rs]]
                vparts += [vtop[rs], vbot[rs]]
            kb = jnp.concatenate(kparts, axis=0).astype(BF16)
            vb = jnp.concatenate([jnp.concatenate(vparts, axis=0).astype(BF16), ones_band], axis=1)
            c0 = 2 * j * LANES
            qs = jnp.concatenate([q_scr[pl.ds(r0, t_len), c0:c0 + LANES],
                                  q_scr[pl.ds(r0, t_len), c0 + LANES:c0 + 2 * LANES]], axis=0).astype(BF16)
            sink_rows = jnp.concatenate(
                [jnp.broadcast_to(sink_ref[2 * j:2 * j + 1, :], (t_len, LANES)),
                 jnp.broadcast_to(sink_ref[2 * j + 1:2 * j + 2, :], (t_len, LANES))], axis=0)
            o = _attend(qs, kb, vb, sink_rows, col_mask)
            mix_scr[pl.ds(r0, t_len), c0:c0 + LANES] = o[0:t_len]
            mix_scr[pl.ds(r0, t_len), c0 + LANES:c0 + 2 * LANES] = o[t_len:2 * t_len]
        return carry

    lax.fori_loop(0, nseq, seq_body, 0)

    o3 = Q_W + 2 * KV_W
    xr = jnp.dot(xn, win_ref[:, o3:o3 + D_LRU], preferred_element_type=F32)
    gr = jnp.dot(xn, win_ref[:, o3 + D_LRU:o3 + 2 * D_LRU], preferred_element_type=F32)
    tmod = _row_iota((rows_all, LANES)) % t_len
    tmod_w = _row_iota((rows_all, D_LRU)) % t_len
    xprev = xprev_ref[...]
    xs = []
    for i in range(CONV_W - 1):
        k = CONV_W - 1 - i
        own = pltpu.roll(xr, k, 0)
        cached = xprev if k == CONV_W - 1 else pltpu.roll(xprev, rows_all - (CONV_W - 1 - k), 0)
        xs.append(jnp.where(tmod_w >= k, own, cached))
    xs.append(xr)
    rec, h = _lru_branch(xs, gr, h0_ref[...], t_len, tmod, convw_ref, convb_ref, wg_ref, ba_ref, bx_ref, lam_ref)
    mix_scr[:, Q_W:Q_W + D_LRU] = rec
    xr_ref[...] = xr
    h_ref[...] = h
    y_ref[...] = x + jnp.dot(mix_scr[...].astype(BF16), wout_ref[...], preferred_element_type=F32)


def _even_sample(x2, cos_t, sin_t, kc, vc, h0_rows, xprev, p):
    rows_all = x2.shape[0]
    nseq = kc.shape[0]
    consts = [p['g'], p['win'], p['qg'], p['kg'], p['sink'], p['convw'], p['convb'], p['wg'], p['ba'], p['bx'],
              p['lam'], p['wout']]
    ins = [x2, cos_t, sin_t, kc, vc, h0_rows, xprev] + consts
    out_shape = [
        jax.ShapeDtypeStruct((rows_all, D_MODEL), F32),
        jax.ShapeDtypeStruct((nseq, SWA_ROWS, KV_W), F32),
        jax.ShapeDtypeStruct((nseq, SWA_ROWS, KV_W), F32),
        jax.ShapeDtypeStruct((rows_all, D_LRU), F32),
        jax.ShapeDtypeStruct((rows_all, D_LRU), F32),
    ]
    return pl.pallas_call(
        _even_sample_kernel,
        out_shape=out_shape,
        grid=(1,),
        in_specs=[_const_spec(a.shape) for a in ins],
        out_specs=[_const_spec(s.shape) for s in out_shape],
        scratch_shapes=[pltpu.VMEM((rows_all, Q_W), F32),
                        pltpu.VMEM((rows_all, KV_W), F32),
                        pltpu.VMEM((rows_all, KV_W), F32),
                        pltpu.VMEM((rows_all, Q_W + D_LRU), F32)],
        compiler_params=pltpu.CompilerParams(dimension_semantics=("arbitrary",), vmem_limit_bytes=VMEM_LIMIT),
        name="even_mixer_sample",
    )(*ins)


def _ffn_kernel(x_ref, g_ref, w1_ref, w2_ref, y_ref):
    x = x_ref[...]
    xn = _rms_rows(x, g_ref[...]).astype(BF16)
    acc = x
    step = D_MODEL
    for c in range(D_FF // step):
        hcol = jnp.dot(xn, w1_ref[:, c * step:(c + 1) * step], preferred_element_type=F32)
        hcol = jnp.square(jnp.maximum(hcol, 0.0)).astype(BF16)
        acc = acc + jnp.dot(hcol, w2_ref[c * step:(c + 1) * step, :], preferred_element_type=F32)
    y_ref[...] = acc


def _ffn(x2, g, w1, w2):
    rows = x2.shape[0]
    tile = min(ROW_TILE, rows)
    return pl.pallas_call(
        _ffn_kernel,
        out_shape=jax.ShapeDtypeStruct(x2.shape, F32),
        grid=(rows // tile,),
        in_specs=[pl.BlockSpec((tile, D_MODEL), lambda i: (i, 0)), _const_spec(g.shape), _const_spec(w1.shape),
                  _const_spec(w2.shape)],
        out_specs=pl.BlockSpec((tile, D_MODEL), lambda i: (i, 0)),
        compiler_params=pltpu.CompilerParams(dimension_semantics=("parallel",), vmem_limit_bytes=VMEM_LIMIT),
        name="channel_mlp",
    )(x2, g, w1, w2)


def _gmlp_kernel(seg, x_ref, g_ref, win_ref, vg_ref, ws_ref, bias_ref, wout_ref, y_ref, v_ref, gate_scr):
    tile = x_ref.shape[0]
    x = x_ref[...]
    xn = _rms_rows(x, g_ref[...]).astype(BF16)
    u = jax.nn.gelu(jnp.dot(xn, win_ref[:, 0:D_C], preferred_element_type=F32))
    zv = jax.nn.gelu(jnp.dot(xn, win_ref[:, D_C:2 * D_C], preferred_element_type=F32))
    zc = zv - jnp.mean(zv, axis=-1, keepdims=True)
    v = zc * lax.rsqrt(jnp.mean(zc * zc, axis=-1, keepdims=True) + EPS) * vg_ref[...]
    v_ref[...] = v
    vb = v.astype(BF16)
    r = _row_iota((CHUNK_MLP, CHUNK_MLP))
    c = _lane_iota((CHUNK_MLP, CHUNK_MLP))
    keep = jnp.logical_and((r // seg) == (c // seg), c <= r)
    bias = bias_ref[...]
    for gi in range(C_GROUPS):
        w = jnp.where(keep, ws_ref[gi], 0.0).astype(BF16)
        lanes = slice(gi * LANES, (gi + 1) * LANES)
        for ch in range(tile // CHUNK_MLP):
            rows = slice(ch * CHUNK_MLP, (ch + 1) * CHUNK_MLP)
            sv = jnp.dot(w, vb[rows, lanes], preferred_element_type=F32) + bias[:, lanes]
            gate_scr[rows, lanes] = (u[rows, lanes] * sv).astype(BF16)
    y_ref[...] = x + jnp.dot(gate_scr[...], wout_ref[...], preferred_element_type=F32)


def _gmlp(x2, seg, g, win, vg, ws, bias, wout):
    rows = x2.shape[0]
    tile = min(ROW_TILE, rows)
    return pl.pallas_call(
        functools.partial(_gmlp_kernel, seg),
        out_shape=[jax.ShapeDtypeStruct(x2.shape, F32), jax.ShapeDtypeStruct((rows, D_C), F32)],
        grid=(rows // tile,),
        in_specs=[pl.BlockSpec((tile, D_MODEL), lambda i: (i, 0))] +
                 [_const_spec(a.shape) for a in (g, win, vg, ws, bias, wout)],
        out_specs=[pl.BlockSpec((tile, D_MODEL), lambda i: (i, 0)), pl.BlockSpec((tile, D_C), lambda i: (i, 0))],
        scratch_shapes=[pltpu.VMEM((tile, D_C), BF16)],
        compiler_params=pltpu.CompilerParams(dimension_semantics=("parallel",), vmem_limit_bytes=VMEM_LIMIT),
        name="gmlp_mixer",
    )(x2, g, win, vg, ws, bias, wout)


def _rope_tables(pos):
    half = HEAD_DIM // 2
    inv = ROPE_THETA ** (-jnp.arange(half, dtype=F32) / half)
    ang = pos.astype(F32)[:, None] * inv[None, :]
    cos = jnp.cos(ang)
    sin = jnp.sin(ang)
    return jnp.concatenate([cos] * 4, axis=1), jnp.concatenate([-sin, sin, -sin, sin], axis=1)


def _gate_weights(wa, wx):
    def half_diag(w, hh):
        out = jnp.zeros((D_LRU // 2, D_LRU // 2), F32)
        for n in range(LRU_BLOCKS // 2):
            out = out.at[n * LRU_BW:(n + 1) * LRU_BW, n * LRU_BW:(n + 1) * LRU_BW].set(w[hh * 4 + n])
        return out
    return jnp.stack([jnp.concatenate([half_diag(wa, hh), half_diag(wx, hh)], axis=1)
                      for hh in range(2)]).astype(BF16)


def kernel(x_prompt, x_sample, cache_swa_k, cache_swa_v, state_lru_h, state_lru_conv, e_norm_g, e_w_in, e_q_norm_g, e_k_norm_g, e_sinks, e_conv_w, e_conv_b, e_gate_a_w, e_gate_a_b, e_gate_x_w, e_gate_x_b, e_lru_lambda, e_w_out, o_norm_g, o_w_in, o_v_norm_g, o_spatial_w, o_spatial_b, o_w_out, ffn_norm_g, ffn_w1, ffn_w2):
    bsz, s_len, _ = x_prompt.shape
    nseq, t_len, _ = x_sample.shape
    past_len = PAST_LEN
    row = lambda a: a.reshape(1, -1)

    e = 0
    p = {
        'g': row(e_norm_g[e]),
        'win': e_w_in[e].astype(BF16),
        'qg': row(jnp.tile(e_q_norm_g[e], LANES // HEAD_DIM)),
        'kg': row(jnp.tile(e_k_norm_g[e], LANES // HEAD_DIM)),
        'sink': jnp.repeat(e_sinks[e], HEAD_DIM).reshape(N_Q_HEADS // 2, LANES),
        'convw': e_conv_w[e],
        'convb': row(e_conv_b[e]),
        'wg': _gate_weights(e_gate_a_w[e], e_gate_x_w[e]),
        'ba': row(e_gate_a_b[e]),
        'bx': row(e_gate_x_b[e]),
        'lam': row(e_lru_lambda[e]),
        'wout': e_w_out[e].astype(BF16),
    }
    cos_p, sin_p = _rope_tables(jnp.arange(s_len))
    yp, nk_p, nv_p, nh_p, nc_p = _even_prompt(x_prompt, cos_p, sin_p, p)

    cos_s, sin_s = _rope_tables(past_len + jnp.arange(t_len))
    cos_s = jnp.tile(cos_s, (nseq, 1))
    sin_s = jnp.tile(sin_s, (nseq, 1))
    kc = cache_swa_k[e].reshape(nseq, SWA_ROWS, KV_W)
    vc = cache_swa_v[e].reshape(nseq, SWA_ROWS, KV_W)
    h0_rows = jnp.repeat(state_lru_h[e], t_len, axis=0)
    xprev = jnp.pad(state_lru_conv[e], ((0, 0), (0, t_len - (CONV_W - 1)), (0, 0))).reshape(nseq * t_len, D_LRU)
    ys, nk_s, nv_s, h_s, xr_s = _even_sample(x_sample.reshape(nseq * t_len, D_MODEL), cos_s, sin_s, kc, vc,
                                             h0_rows, xprev, p)
    nh_s = h_s.reshape(nseq, t_len, D_LRU)[:, -1]
    nc_s = xr_s.reshape(nseq, t_len, D_LRU)[:, t_len - (CONV_W - 1):]

    w1 = ffn_w1.astype(BF16)
    w2 = ffn_w2.astype(BF16)
    yp = _ffn(yp.reshape(bsz * s_len, D_MODEL), row(ffn_norm_g[0]), w1[0], w2[0])
    ys = _ffn(ys, row(ffn_norm_g[0]), w1[0], w2[0])

    o = 0
    gw = (row(o_norm_g[o]), o_w_in[o].astype(BF16), row(o_v_norm_g[o]))
    wout_o = o_w_out[o].astype(BF16)
    bias_p = jnp.repeat(o_spatial_b[o].T, LANES, axis=1)
    yp, _ = _gmlp(yp, CHUNK_MLP, *gw, o_spatial_w[o], bias_p, wout_o)
    reps = CHUNK_MLP // t_len
    ws_s = jnp.tile(o_spatial_w[o][:, :t_len, :t_len], (1, reps, reps))
    bias_s = jnp.tile(jnp.repeat(o_spatial_b[o][:, :t_len].T, LANES, axis=1), (reps, 1))
    ys, gv_s = _gmlp(ys, t_len, *gw, ws_s, bias_s, wout_o)

    yp = _ffn(yp, row(ffn_norm_g[1]), w1[1], w2[1])
    ys = _ffn(ys, row(ffn_norm_g[1]), w1[1], w2[1])

    kv_shape = (1, -1, SWA_ROWS, N_KV_HEADS, HEAD_DIM)
    return (yp.reshape(bsz, s_len, D_MODEL), ys.reshape(nseq, t_len, D_MODEL),
            nk_p.reshape(kv_shape), nv_p.reshape(kv_shape), nh_p.reshape(1, bsz, D_LRU), nc_p[None],
            nk_s.reshape(kv_shape), nv_s.reshape(kv_shape), nh_s[None], nc_s[None],
            gv_s.reshape(1, nseq, t_len, D_C))
```

```python
import functools
import math

import jax
import jax.numpy as jnp
from jax import lax
from jax.experimental import pallas as pl
from jax.experimental.pallas import tpu as pltpu

F32 = jnp.float32
BF16 = jnp.bfloat16

D_MODEL = 1024
CHUNK = 64
HEAD_DIM = 64
N_Q_HEADS = 8
N_KV_HEADS = 2
SWA_ROWS = 128
PAST_LEN = 4096
ROPE_THETA = 10000.0
NEG = -1e30
D_LRU = 512
LRU_BLOCKS = 8
LRU_BW = D_LRU // LRU_BLOCKS
CONV_W = 4
LRU_C = 8.0
Q_W = N_Q_HEADS * HEAD_DIM
KV_W = N_KV_HEADS * HEAD_DIM
CHUNK_MLP = 128
D_C = D_MODEL
C_GROUPS = 8
D_FF = 4 * D_MODEL
EPS = 1e-6

LANES = 128
SUBLANES = 8
MXU_WIDTH = 256
KEY_BLOCK = 2 * CHUNK
BAND_BLOCKS = 3
VMEM_LIMIT = 56 * 1024 * 1024

PROMPT_TILE = 256
ROW_TILE = 512


def _gelu(x):
    k1 = -2.0 * math.sqrt(2.0 / math.pi) * math.log2(math.e)
    return x / (1.0 + jnp.exp2(x * (k1 + (k1 * 0.044715) * (x * x))))


def _rms_rows(x, g):
    return x * lax.rsqrt(jnp.mean(x * x, axis=-1, keepdims=True) + EPS) * g


def _lane_iota(shape):
    return lax.broadcasted_iota(jnp.int32, shape, 1)


def _row_iota(shape):
    return lax.broadcasted_iota(jnp.int32, shape, 0)


def _head_norm(xt, gain, scale=1.0):
    sq = xt * xt
    lo = _lane_iota(xt.shape) < HEAD_DIM
    s0 = jnp.sum(jnp.where(lo, sq, 0.0), axis=1, keepdims=True)
    s1 = jnp.sum(jnp.where(lo, 0.0, sq), axis=1, keepdims=True)
    ss = jnp.where(lo, s0, s1)
    return xt * lax.rsqrt(ss + HEAD_DIM * EPS) * (gain * (scale * math.sqrt(HEAD_DIM)))


def _rope(xt, cos_t, sin_t):
    first_half = (_lane_iota(xt.shape) % HEAD_DIM) < (HEAD_DIM // 2)
    partner = jnp.where(first_half, pltpu.roll(xt, LANES - HEAD_DIM // 2, 1), pltpu.roll(xt, HEAD_DIM // 2, 1))
    return xt * cos_t + partner * sin_t


def _dup_head(t, j):
    lo = _lane_iota(t.shape) < HEAD_DIM
    swapped = pltpu.roll(t, HEAD_DIM, 1)
    return jnp.where(lo, t, swapped) if j == 0 else jnp.where(lo, swapped, t)


def _split_diag(t):
    lo = _lane_iota(t.shape) < HEAD_DIM
    return jnp.where(lo, t, 0.0), jnp.where(lo, 0.0, t)


def _ones_diag(rows):
    r = _row_iota((rows, LANES))
    return (((r % KEY_BLOCK) // CHUNK) == (_lane_iota((rows, LANES)) // HEAD_DIM)).astype(BF16)


def _attend(qs, kb, vb, sink_rows, col_mask):
    e, m = _attend_weights(_attend_scores(qs, kb, col_mask), sink_rows)
    return _attend_output(e, m, vb, sink_rows)


def _attend_scores(qs, kb, col_mask):
    s = lax.dot_general(qs, kb, (((1,), (1,)), ((), ())), preferred_element_type=F32)
    return s if col_mask is None else jnp.where(col_mask, s, NEG)


def _attend_weights(s, sink_rows):
    mx = jnp.maximum(jnp.maximum(s[:, :LANES], s[:, LANES:2 * LANES]), s[:, 2 * LANES:])
    lo = _lane_iota(mx.shape) < HEAD_DIM
    m0 = jnp.max(jnp.where(lo, mx, -jnp.inf), axis=1, keepdims=True)
    m1 = jnp.max(jnp.where(lo, -jnp.inf, mx), axis=1, keepdims=True)
    m = jnp.maximum(jnp.where(lo, m0, m1), sink_rows)
    e = jnp.exp(s - jnp.concatenate([m, m, m], axis=1))
    return e.astype(BF16), m


def _attend_output(e, m, vb, sink_rows):
    od = jnp.dot(e, vb, preferred_element_type=F32)
    den = od[:, LANES:] + jnp.exp(sink_rows - m)
    return od[:, :LANES] / den


def _lru_scan(a, b, seg, tmod):
    d = 1
    while d < seg:
        ok = tmod >= d
        b = jnp.where(ok, a * pltpu.roll(b, d, 0), 0.0) + b
        if 2 * d < seg:
            a = jnp.where(ok, a * pltpu.roll(a, d, 0), a)
        d *= 2
    return b


def _lru_scan_tile(a, b, h_prev):
    rows = a.shape[0]
    groups = rows // SUBLANES
    a3 = a.reshape(groups, SUBLANES, LANES)
    b3 = b.reshape(groups, SUBLANES, LANES)
    sub = lax.broadcasted_iota(jnp.int32, a3.shape, 1)
    d = 1
    while d < SUBLANES:
        ok = sub >= d
        b3 = jnp.where(ok, a3 * pltpu.roll(b3, d, 1), 0.0) + b3
        a3 = jnp.where(ok, a3 * pltpu.roll(a3, d, 1), a3)
        d *= 2
    carry = jnp.broadcast_to(h_prev, (SUBLANES, LANES))
    out = []
    for g in range(groups):
        hg = a3[g] * carry + b3[g]
        out.append(hg)
        carry = jnp.broadcast_to(hg[SUBLANES - 1:SUBLANES, :], (SUBLANES, LANES))
    return jnp.concatenate(out, axis=0)


def _lru_branch(xs, gr, h_in, seg, tmod, convw_ref, convb_ref, wg_ref, ba_ref, bx_ref, lam_ref):
    xc = xs[0] * convw_ref[0:1, :] + convb_ref[...]
    for i in range(1, CONV_W):
        xc = xc + xs[i] * convw_ref[i:i + 1, :]
    xcb = xc.astype(BF16)
    half = D_LRU // 2
    rec_parts, h_parts = [], []
    for hh in range(2):
        g = jnp.dot(xcb[:, hh * half:(hh + 1) * half], wg_ref[hh], preferred_element_type=F32)
        for tt in range(2):
            c0 = hh * half + tt * LANES
            sl = slice(c0, c0 + LANES)
            r = jax.nn.sigmoid(g[:, tt * LANES:(tt + 1) * LANES] + ba_ref[:, sl])
            gi = jax.nn.sigmoid(g[:, half + tt * LANES:half + (tt + 1) * LANES] + bx_ref[:, sl])
            log_a = r * (-LRU_C * jax.nn.softplus(-lam_ref[:, sl]))
            a = jnp.exp(log_a)
            b = jnp.sqrt(-jnp.tanh(log_a) * (a * a + 1.0)) * (gi * xc[:, sl])
            if seg is None:
                h = _lru_scan_tile(a, b, h_in[:, sl])
            else:
                h = _lru_scan(a, b + jnp.where(tmod == 0, a * h_in[:, sl], 0.0), seg, tmod)
            h_parts.append(h)
            rec_parts.append(h * _gelu(gr[:, sl]))
    return jnp.concatenate(rec_parts, axis=1), jnp.concatenate(h_parts, axis=1)


def _even_prompt_kernel(x_ref, cos_ref, sin_ref, g_ref, win_ref, qg_ref, kg_ref, sink_ref, convw_ref, convb_ref,
                        wg_ref, ba_ref, bx_ref, lam_ref, wout_ref,
                        y_ref, nk_ref, nv_ref, nh_ref, nc_ref,
                        kb_scr, vb_scr, xr_scr, h_scr, mix_scr):
    t = pl.program_id(1)
    nt = pl.num_programs(1)
    tile = x_ref.shape[1]
    n_chunks = tile // CHUNK
    hist = 2 * KEY_BLOCK
    slot = t % 2

    @pl.when(t == 0)
    def _():
        ones = _ones_diag(vb_scr.shape[2])
        for j in range(N_KV_HEADS):
            kb_scr[0, j, 0:hist, :] = jnp.zeros((hist, LANES), BF16)
            vb_scr[0, j, 0:hist, 0:LANES] = jnp.zeros((hist, LANES), BF16)
            for s in range(2):
                vb_scr[s, j, :, LANES:2 * LANES] = ones
        xr_scr[0:8, :] = jnp.zeros((8, D_LRU), F32)
        h_scr[...] = jnp.zeros(h_scr.shape, F32)

    x = x_ref[0]
    xn = _rms_rows(x, g_ref[...]).astype(BF16)
    cos_t = cos_ref[...]
    sin_t = sin_ref[...]
    o3 = Q_W + 2 * KV_W
    hq = jnp.dot(xn, win_ref[:, 0:Q_W], preferred_element_type=F32)
    hkv = jnp.dot(xn, win_ref[:, Q_W:o3], preferred_element_type=F32)
    xr = jnp.dot(xn, win_ref[:, o3:o3 + D_LRU], preferred_element_type=F32)
    gr = jnp.dot(xn, win_ref[:, o3 + D_LRU:o3 + 2 * D_LRU], preferred_element_type=F32)

    q_tiles = []
    for n in range(Q_W // LANES):
        qt = _head_norm(hq[:, n * LANES:(n + 1) * LANES], qg_ref[...], HEAD_DIM ** -0.5)
        q_tiles.append(_rope(qt, cos_t, sin_t).astype(BF16))
    kr = _rope(_head_norm(hkv[:, 0:KV_W], kg_ref[...]), cos_t, sin_t)
    vv = hkv[:, KV_W:2 * KV_W]
    for j in range(N_KV_HEADS):
        ktop, kbot = _split_diag(_dup_head(kr, j))
        vtop, vbot = _split_diag(_dup_head(vv, j))
        for i in range(n_chunks):
            r0 = hist + i * KEY_BLOCK
            rows = slice(i * CHUNK, (i + 1) * CHUNK)
            kb_scr[slot, j, r0:r0 + CHUNK, :] = ktop[rows].astype(BF16)
            kb_scr[slot, j, r0 + CHUNK:r0 + KEY_BLOCK, :] = kbot[rows].astype(BF16)
            vb_scr[slot, j, r0:r0 + CHUNK, 0:LANES] = vtop[rows].astype(BF16)
            vb_scr[slot, j, r0 + CHUNK:r0 + KEY_BLOCK, 0:LANES] = vbot[rows].astype(BF16)

    band = BAND_BLOCKS * KEY_BLOCK
    col = _lane_iota((2 * CHUNK, band))
    sink_rows = [jnp.concatenate(
        [jnp.broadcast_to(sink_ref[2 * j:2 * j + 1, :], (CHUNK, LANES)),
         jnp.broadcast_to(sink_ref[2 * j + 1:2 * j + 2, :], (CHUNK, LANES))], axis=0) for j in range(N_KV_HEADS)]
    pairs = [(i, j) for i in range(n_chunks) for j in range(N_KV_HEADS)]
    scores = []
    for i, j in pairs:
        rows = slice(i * CHUNK, (i + 1) * CHUNK)
        col_mask = None
        if i < BAND_BLOCKS - 1:
            col_mask = col >= (1 - jnp.minimum(t, 1)) * ((BAND_BLOCKS - 1 - i) * KEY_BLOCK)
        qs = jnp.concatenate([q_tiles[2 * j][rows], q_tiles[2 * j + 1][rows]], axis=0)
        scores.append(_attend_scores(qs, kb_scr[slot, j, i * KEY_BLOCK:i * KEY_BLOCK + band, :], col_mask))
    weights = [_attend_weights(s, sink_rows[j]) for s, (i, j) in zip(scores, pairs)]
    for (e, m), (i, j) in zip(weights, pairs):
        rows = slice(i * CHUNK, (i + 1) * CHUNK)
        o = _attend_output(e, m, vb_scr[slot, j, i * KEY_BLOCK:i * KEY_BLOCK + band, :], sink_rows[j])
        c0 = 2 * j * LANES
        mix_scr[rows, c0:c0 + LANES] = o[0:CHUNK].astype(BF16)
        mix_scr[rows, c0 + LANES:c0 + 2 * LANES] = o[CHUNK:2 * CHUNK].astype(BF16)

    last = n_chunks * KEY_BLOCK
    for j in range(N_KV_HEADS):
        kb_scr[1 - slot, j, 0:hist, :] = kb_scr[slot, j, last:last + hist, :]
        vb_scr[1 - slot, j, 0:hist, 0:LANES] = vb_scr[slot, j, last:last + hist, 0:LANES]

    xr_scr[8:8 + tile, :] = xr
    xs = [xr_scr[8 - (CONV_W - 1 - i):8 - (CONV_W - 1 - i) + tile, :] for i in range(CONV_W - 1)] + [xr]
    rec, h = _lru_branch(xs, gr, h_scr[0:1, :], None, None, convw_ref, convb_ref, wg_ref, ba_ref, bx_ref, lam_ref)
    mix_scr[:, Q_W:Q_W + D_LRU] = rec.astype(BF16)
    h_scr[0:1, :] = h[tile - 1:tile, :]
    xr_scr[0:8, :] = xr[tile - 8:tile, :]

    y_att = jnp.dot(mix_scr[:, 0:Q_W], wout_ref[0:Q_W, :], preferred_element_type=F32)
    y_rec = jnp.dot(mix_scr[:, Q_W:Q_W + D_LRU], wout_ref[Q_W:Q_W + D_LRU, :], preferred_element_type=F32)
    y_ref[0] = x + y_att + y_rec

    @pl.when(t == nt - 1)
    def _():
        nk_ref[0] = kr[tile - SWA_ROWS:tile, :]
        nv_ref[0] = vv[tile - SWA_ROWS:tile, :]
        nh_ref[0] = h[tile - 1:tile, :]
        nc_ref[0] = xr[tile - (CONV_W - 1):tile, :]


def _const_spec(shape):
    zeros = (0,) * len(shape)
    return pl.BlockSpec(shape, lambda *_: zeros, pipeline_mode=pl.Buffered(1))


def _even_prompt(x, cos_t, sin_t, p):
    bsz, s_len, _ = x.shape
    tile = PROMPT_TILE
    nt = s_len // tile
    key_rows = (tile // CHUNK + 2) * KEY_BLOCK
    consts = [p['g'], p['win'], p['qg'], p['kg'], p['sink'], p['convw'], p['convb'], p['wg'], p['ba'], p['bx'],
              p['lam'], p['wout']]
    out_shape = [
        jax.ShapeDtypeStruct((bsz, s_len, D_MODEL), F32),
        jax.ShapeDtypeStruct((bsz, SWA_ROWS, KV_W), F32),
        jax.ShapeDtypeStruct((bsz, SWA_ROWS, KV_W), F32),
        jax.ShapeDtypeStruct((bsz, 1, D_LRU), F32),
        jax.ShapeDtypeStruct((bsz, CONV_W - 1, D_LRU), F32),
    ]
    return pl.pallas_call(
        _even_prompt_kernel,
        out_shape=out_shape,
        grid=(bsz, nt),
        in_specs=[pl.BlockSpec((1, tile, D_MODEL), lambda b, t: (b, t, 0)),
                  pl.BlockSpec((tile, LANES), lambda b, t: (t, 0)),
                  pl.BlockSpec((tile, LANES), lambda b, t: (t, 0))] + [_const_spec(c.shape) for c in consts],
        out_specs=[pl.BlockSpec((1, tile, D_MODEL), lambda b, t: (b, t, 0)),
                   pl.BlockSpec((1, SWA_ROWS, KV_W), lambda b, t: (b, 0, 0)),
                   pl.BlockSpec((1, SWA_ROWS, KV_W), lambda b, t: (b, 0, 0)),
                   pl.BlockSpec((1, 1, D_LRU), lambda b, t: (b, 0, 0)),
                   pl.BlockSpec((1, CONV_W - 1, D_LRU), lambda b, t: (b, 0, 0))],
        scratch_shapes=[pltpu.VMEM((2, N_KV_HEADS, key_rows, LANES), BF16),
                        pltpu.VMEM((2, N_KV_HEADS, key_rows, 2 * LANES), BF16),
                        pltpu.VMEM((tile + 8, D_LRU), F32),
                        pltpu.VMEM((8, D_LRU), F32),
                        pltpu.VMEM((tile, Q_W + D_LRU), BF16)],
        compiler_params=pltpu.CompilerParams(dimension_semantics=("arbitrary", "arbitrary"),
                                             vmem_limit_bytes=VMEM_LIMIT),
        name="even_mixer_prompt",
    )(x, cos_t, sin_t, *consts)


def _even_sample_kernel(x_ref, cos_ref, sin_ref, kc_ref, vc_ref, h0_ref, xprev_ref,
                        g_ref, win_ref, qg_ref, kg_ref, sink_ref, convw_ref, convb_ref,
                        wg_ref, ba_ref, bx_ref, lam_ref, wout_ref,
                        y_ref, nk_ref, nv_ref, h_ref, xr_ref,
                        q_scr, k_scr, v_scr, mix_scr):
    rows_all = x_ref.shape[0]
    nseq = kc_ref.shape[0]
    t_len = rows_all // nseq
    x = x_ref[...]
    xn = _rms_rows(x, g_ref[...]).astype(BF16)
    cos_t = cos_ref[...]
    sin_t = sin_ref[...]

    hq = jnp.dot(xn, win_ref[:, 0:Q_W], preferred_element_type=F32)
    for n in range(Q_W // LANES):
        qt = _head_norm(hq[:, n * LANES:(n + 1) * LANES], qg_ref[...], HEAD_DIM ** -0.5)
        q_scr[:, n * LANES:(n + 1) * LANES] = _rope(qt, cos_t, sin_t)
    hkv = jnp.dot(xn, win_ref[:, Q_W:Q_W + 2 * KV_W], preferred_element_type=F32)
    k_scr[...] = _rope(_head_norm(hkv[:, 0:KV_W], kg_ref[...]), cos_t, sin_t)
    v_scr[...] = hkv[:, KV_W:2 * KV_W]

    band = BAND_BLOCKS * KEY_BLOCK
    col = _lane_iota((2 * t_len, band))
    col_mask = jnp.logical_or(col < 2 * KEY_BLOCK, (col % CHUNK) < t_len)
    ones_band = _ones_diag(band)
    pad = jnp.zeros((CHUNK - t_len, LANES), F32)

    def seq_body(b, carry):
        r0 = pl.multiple_of(b * t_len, t_len)
        kc = kc_ref[b]
        vc = vc_ref[b]
        knew = k_scr[pl.ds(r0, t_len), :]
        vnew = v_scr[pl.ds(r0, t_len), :]
        nk_ref[b, 0:SWA_ROWS - t_len, :] = kc[t_len:SWA_ROWS]
        nk_ref[b, SWA_ROWS - t_len:SWA_ROWS, :] = knew
        nv_ref[b, 0:SWA_ROWS - t_len, :] = vc[t_len:SWA_ROWS]
        nv_ref[b, SWA_ROWS - t_len:SWA_ROWS, :] = vnew
        kfull = jnp.concatenate([kc, knew, pad], axis=0)
        vfull = jnp.concatenate([vc, vnew, pad], axis=0)
        for j in range(N_KV_HEADS):
            ktop, kbot = _split_diag(_dup_head(kfull, j))
            vtop, vbot = _split_diag(_dup_head(vfull, j))
            kparts, vparts = [], []
            for c in range(BAND_BLOCKS):
                rs = slice(c * CHUNK, (c + 1) * CHUNK)
                kparts += [ktop[rs], kbot[rs]]
                vparts += [vtop[rs], vbot[rs]]
            kb = jnp.concatenate(kparts, axis=0).astype(BF16)
            vb = jnp.concatenate([jnp.concatenate(vparts, axis=0).astype(BF16), ones_band], axis=1)
            c0 = 2 * j * LANES
            qs = jnp.concatenate([q_scr[pl.ds(r0, t_len), c0:c0 + LANES],
                                  q_scr[pl.ds(r0, t_len), c0 + LANES:c0 + 2 * LANES]], axis=0).astype(BF16)
            sink_rows = jnp.concatenate(
                [jnp.broadcast_to(sink_ref[2 * j:2 * j + 1, :], (t_len, LANES)),
                 jnp.broadcast_to(sink_ref[2 * j + 1:2 * j + 2, :], (t_len, LANES))], axis=0)
            o = _attend(qs, kb, vb, sink_rows, col_mask)
            mix_scr[pl.ds(r0, t_len), c0:c0 + LANES] = o[0:t_len]
            mix_scr[pl.ds(r0, t_len), c0 + LANES:c0 + 2 * LANES] = o[t_len:2 * t_len]
        return carry

    lax.fori_loop(0, nseq, seq_body, 0)

    o3 = Q_W + 2 * KV_W
    xr = jnp.dot(xn, win_ref[:, o3:o3 + D_LRU], preferred_element_type=F32)
    gr = jnp.dot(xn, win_ref[:, o3 + D_LRU:o3 + 2 * D_LRU], preferred_element_type=F32)
    tmod = _row_iota((rows_all, LANES)) % t_len
    tmod_w = _row_iota((rows_all, D_LRU)) % t_len
    xprev = xprev_ref[...]
    xs = []
    for i in range(CONV_W - 1):
        k = CONV_W - 1 - i
        own = pltpu.roll(xr, k, 0)
        cached = xprev if k == CONV_W - 1 else pltpu.roll(xprev, rows_all - (CONV_W - 1 - k), 0)
        xs.append(jnp.where(tmod_w >= k, own, cached))
    xs.append(xr)
    rec, h = _lru_branch(xs, gr, h0_ref[...], t_len, tmod, convw_ref, convb_ref, wg_ref, ba_ref, bx_ref, lam_ref)
    mix_scr[:, Q_W:Q_W + D_LRU] = rec
    xr_ref[...] = xr
    h_ref[...] = h
    y_ref[...] = x + jnp.dot(mix_scr[...].astype(BF16), wout_ref[...], preferred_element_type=F32)


def _even_sample(x2, cos_t, sin_t, kc, vc, h0_rows, xprev, p):
    rows_all = x2.shape[0]
    nseq = kc.shape[0]
    consts = [p['g'], p['win'], p['qg'], p['kg'], p['sink'], p['convw'], p['convb'], p['wg'], p['ba'], p['bx'],
              p['lam'], p['wout']]
    ins = [x2, cos_t, sin_t, kc, vc, h0_rows, xprev] + consts
    out_shape = [
        jax.ShapeDtypeStruct((rows_all, D_MODEL), F32),
        jax.ShapeDtypeStruct((nseq, SWA_ROWS, KV_W), F32),
        jax.ShapeDtypeStruct((nseq, SWA_ROWS, KV_W), F32),
        jax.ShapeDtypeStruct((rows_all, D_LRU), F32),
        jax.ShapeDtypeStruct((rows_all, D_LRU), F32),
    ]
    return pl.pallas_call(
        _even_sample_kernel,
        out_shape=out_shape,
        grid=(1,),
        in_specs=[_const_spec(a.shape) for a in ins],
        out_specs=[pl.BlockSpec(s.shape, lambda i, n=len(s.shape): (0,) * n) for s in out_shape],
        scratch_shapes=[pltpu.VMEM((rows_all, Q_W), F32),
                        pltpu.VMEM((rows_all, KV_W), F32),
                        pltpu.VMEM((rows_all, KV_W), F32),
                        pltpu.VMEM((rows_all, Q_W + D_LRU), F32)],
        compiler_params=pltpu.CompilerParams(dimension_semantics=("arbitrary",), vmem_limit_bytes=VMEM_LIMIT),
        name="even_mixer_sample",
    )(*ins)


def _ffn_kernel(xp_ref, xs_ref, g_ref, w1_ref, w2_ref, yp_ref, ys_ref):
    i = pl.program_id(0)
    x = jnp.where(i == 0, xs_ref[...], xp_ref[...])
    xn = _rms_rows(x, g_ref[...]).astype(BF16)
    acc = x
    step = D_MODEL
    for c in range(D_FF // step):
        hcol = jnp.dot(xn, w1_ref[:, c * step:(c + 1) * step], preferred_element_type=F32)
        hcol = jnp.square(jnp.maximum(hcol, 0.0)).astype(BF16)
        acc = acc + jnp.dot(hcol, w2_ref[c * step:(c + 1) * step, :], preferred_element_type=F32)

    yp_ref[...] = acc

    @pl.when(i == 0)
    def _():
        ys_ref[...] = yp_ref[...]


def _sample_then_tiles_specs(tile, width):
    prompt = pl.BlockSpec((tile, width), lambda i: (jnp.maximum(i - 1, 0), 0))
    sample = pl.BlockSpec((tile, width), lambda i: (0, 0))
    return prompt, sample


def _ffn(xp, xs, g, w1, w2, layer):
    tile = xs.shape[0]
    n_tiles = xp.shape[0] // tile
    xp_spec, xs_spec = _sample_then_tiles_specs(tile, D_MODEL)
    layer_spec = lambda shape: pl.BlockSpec((None,) + shape[1:], lambda i: (layer, 0, 0),
                                            pipeline_mode=pl.Buffered(1))
    return pl.pallas_call(
        _ffn_kernel,
        out_shape=[jax.ShapeDtypeStruct(xp.shape, F32), jax.ShapeDtypeStruct(xs.shape, F32)],
        grid=(n_tiles + 1,),
        in_specs=[xp_spec, xs_spec, _const_spec(g.shape), layer_spec(w1.shape), layer_spec(w2.shape)],
        out_specs=list(_sample_then_tiles_specs(tile, D_MODEL)),
        compiler_params=pltpu.CompilerParams(dimension_semantics=("arbitrary",), vmem_limit_bytes=VMEM_LIMIT),
        name="channel_mlp",
    )(xp, xs, g, w1, w2)


def _gmlp_kernel(seg_s, xp_ref, xs_ref, g_ref, win_ref, vg_ref, wsp_ref, wss_ref, biasp_ref, biass_ref, wout_ref,
                 yp_ref, ys_ref, v_ref):
    i = pl.program_id(0)
    is_s = i == 0
    tile = xp_ref.shape[0]
    x = jnp.where(is_s, xs_ref[...], xp_ref[...])
    xn = _rms_rows(x, g_ref[...]).astype(BF16)
    zv = jnp.dot(xn, win_ref[:, D_C:2 * D_C], preferred_element_type=F32)
    u_all = jnp.dot(xn, win_ref[:, 0:D_C], preferred_element_type=F32)
    zv = _gelu(zv)
    zc = zv - jnp.mean(zv, axis=-1, keepdims=True)
    v_all = zc * lax.rsqrt(jnp.mean(zc * zc, axis=-1, keepdims=True) + EPS) * vg_ref[...]
    r = _row_iota((CHUNK_MLP, CHUNK_MLP))
    col = _lane_iota((CHUNK_MLP, CHUNK_MLP))
    keep_p = col <= r
    keep_s = jnp.logical_and((r // seg_s) == (col // seg_s), keep_p)
    vb = v_all.astype(BF16)
    sv = []
    for gi in range(C_GROUPS):
        w = jnp.where(is_s, jnp.where(keep_s, wss_ref[gi], 0.0), jnp.where(keep_p, wsp_ref[gi], 0.0)).astype(BF16)
        lanes = slice(gi * LANES, (gi + 1) * LANES)
        bias = jnp.where(is_s, biass_ref[:, lanes], biasp_ref[:, lanes])
        blocks = [jnp.dot(w, vb[ch * CHUNK_MLP:(ch + 1) * CHUNK_MLP, lanes], preferred_element_type=F32) + bias
                  for ch in range(tile // CHUNK_MLP)]
        sv.append(jnp.concatenate(blocks, axis=0))
    acc = x
    step = MXU_WIDTH
    per_step = step // LANES
    for c in range(D_C // step):
        cols = slice(c * step, (c + 1) * step)
        gate = (_gelu(u_all[:, cols]) * jnp.concatenate(sv[c * per_step:(c + 1) * per_step], axis=1)).astype(BF16)
        acc = acc + jnp.dot(gate, wout_ref[cols, :], preferred_element_type=F32)

    yp_ref[...] = acc

    @pl.when(is_s)
    def _():
        ys_ref[...] = yp_ref[...]
        v_ref[...] = v_all


def _gmlp(xp, xs, seg_s, g, win, vg, ws_p, ws_s, bias_p, bias_s, wout):
    tile = xs.shape[0]
    n_tiles = xp.shape[0] // tile
    xp_spec, xs_spec = _sample_then_tiles_specs(tile, D_MODEL)
    consts = (g, win, vg, ws_p, ws_s, bias_p, bias_s, wout)
    return pl.pallas_call(
        functools.partial(_gmlp_kernel, seg_s),
        out_shape=[jax.ShapeDtypeStruct(xp.shape, F32), jax.ShapeDtypeStruct(xs.shape, F32),
                   jax.ShapeDtypeStruct((tile, D_C), F32)],
        grid=(n_tiles + 1,),
        in_specs=[xp_spec, xs_spec] + [_const_spec(a.shape) for a in consts],
        out_specs=list(_sample_then_tiles_specs(tile, D_MODEL)) + [pl.BlockSpec((tile, D_C), lambda i: (0, 0))],
        compiler_params=pltpu.CompilerParams(dimension_semantics=("arbitrary",), vmem_limit_bytes=VMEM_LIMIT),
        name="gmlp_mixer",
    )(xp, xs, *consts)


def _rope_tables(pos):
    half = HEAD_DIM // 2
    inv = ROPE_THETA ** (-jnp.arange(half, dtype=F32) / half)
    ang = pos.astype(F32)[:, None] * inv[None, :]
    cos = jnp.cos(ang)
    sin = jnp.sin(ang)
    return jnp.concatenate([cos] * 4, axis=1), jnp.concatenate([-sin, sin, -sin, sin], axis=1)


def _gate_weights(wa, wx):
    def half_diag(w, hh):
        out = jnp.zeros((D_LRU // 2, D_LRU // 2), F32)
        for n in range(LRU_BLOCKS // 2):
            out = out.at[n * LRU_BW:(n + 1) * LRU_BW, n * LRU_BW:(n + 1) * LRU_BW].set(w[hh * 4 + n])
        return out
    return jnp.stack([jnp.concatenate([half_diag(wa, hh), half_diag(wx, hh)], axis=1)
                      for hh in range(2)]).astype(BF16)


def kernel(x_prompt, x_sample, cache_swa_k, cache_swa_v, state_lru_h, state_lru_conv, e_norm_g, e_w_in, e_q_norm_g, e_k_norm_g, e_sinks, e_conv_w, e_conv_b, e_gate_a_w, e_gate_a_b, e_gate_x_w, e_gate_x_b, e_lru_lambda, e_w_out, o_norm_g, o_w_in, o_v_norm_g, o_spatial_w, o_spatial_b, o_w_out, ffn_norm_g, ffn_w1, ffn_w2):
    bsz, s_len, _ = x_prompt.shape
    nseq, t_len, _ = x_sample.shape
    past_len = PAST_LEN
    row = lambda a: a.reshape(1, -1)

    e = 0
    p = {
        'g': row(e_norm_g[e]),
        'win': e_w_in[e].astype(BF16),
        'qg': row(jnp.tile(e_q_norm_g[e], LANES // HEAD_DIM)),
        'kg': row(jnp.tile(e_k_norm_g[e], LANES // HEAD_DIM)),
        'sink': jnp.repeat(e_sinks[e], HEAD_DIM).reshape(N_Q_HEADS // 2, LANES),
        'convw': e_conv_w[e],
        'convb': row(e_conv_b[e]),
        'wg': _gate_weights(e_gate_a_w[e], e_gate_x_w[e]),
        'ba': row(e_gate_a_b[e]),
        'bx': row(e_gate_x_b[e]),
        'lam': row(e_lru_lambda[e]),
        'wout': e_w_out[e].astype(BF16),
    }
    cos_p, sin_p = _rope_tables(jnp.arange(s_len))
    yp, nk_p, nv_p, nh_p, nc_p = _even_prompt(x_prompt, cos_p, sin_p, p)

    cos_s, sin_s = _rope_tables(past_len + jnp.arange(t_len))
    cos_s = jnp.tile(cos_s, (nseq, 1))
    sin_s = jnp.tile(sin_s, (nseq, 1))
    kc = cache_swa_k[e].reshape(nseq, SWA_ROWS, KV_W)
    vc = cache_swa_v[e].reshape(nseq, SWA_ROWS, KV_W)
    h0_rows = jnp.repeat(state_lru_h[e], t_len, axis=0)
    xprev = jnp.pad(state_lru_conv[e], ((0, 0), (0, t_len - (CONV_W - 1)), (0, 0))).reshape(nseq * t_len, D_LRU)
    ys, nk_s, nv_s, h_s, xr_s = _even_sample(x_sample.reshape(nseq * t_len, D_MODEL), cos_s, sin_s, kc, vc,
                                             h0_rows, xprev, p)
    nh_s = h_s.reshape(nseq, t_len, D_LRU)[:, -1]
    nc_s = xr_s.reshape(nseq, t_len, D_LRU)[:, t_len - (CONV_W - 1):]

    w1 = ffn_w1.astype(BF16)
    w2 = ffn_w2.astype(BF16)
    yp, ys = _ffn(yp.reshape(bsz * s_len, D_MODEL), ys, row(ffn_norm_g[0]), w1, w2, 0)

    o = 0
    bias_p = jnp.repeat(o_spatial_b[o].T, LANES, axis=1)
    reps = CHUNK_MLP // t_len
    ws_s = jnp.tile(o_spatial_w[o][:, :t_len, :t_len], (1, reps, reps))
    bias_s = jnp.tile(jnp.repeat(o_spatial_b[o][:, :t_len].T, LANES, axis=1), (reps, 1))
    yp, ys, gv_s = _gmlp(yp, ys, t_len, row(o_norm_g[o]), o_w_in[o].astype(BF16), row(o_v_norm_g[o]),
                         o_spatial_w[o], ws_s, bias_p, bias_s, o_w_out[o].astype(BF16))

    yp, ys = _ffn(yp, ys, row(ffn_norm_g[1]), w1, w2, 1)

    kv_shape = (1, -1, SWA_ROWS, N_KV_HEADS, HEAD_DIM)
    return (yp.reshape(bsz, s_len, D_MODEL), ys.reshape(nseq, t_len, D_MODEL),
            nk_p.reshape(kv_shape), nv_p.reshape(kv_shape), nh_p.reshape(1, bsz, D_LRU), nc_p[None],
            nk_s.reshape(kv_shape), nv_s.reshape(kv_shape), nh_s[None], nc_s[None],
            gv_s.reshape(1, nseq, t_len, D_C))
```

```python
import functools
import math

import jax
import jax.numpy as jnp
from jax import lax
from jax.experimental import pallas as pl
from jax.experimental.pallas import tpu as pltpu

F32 = jnp.float32
BF16 = jnp.bfloat16

D_MODEL = 1024
CHUNK = 64
HEAD_DIM = 64
N_Q_HEADS = 8
N_KV_HEADS = 2
SWA_ROWS = 128
PAST_LEN = 4096
ROPE_THETA = 10000.0
NEG = -1e30
D_LRU = 512
LRU_BLOCKS = 8
LRU_BW = D_LRU // LRU_BLOCKS
CONV_W = 4
LRU_C = 8.0
Q_W = N_Q_HEADS * HEAD_DIM
KV_W = N_KV_HEADS * HEAD_DIM
CHUNK_MLP = 128
D_C = D_MODEL
C_GROUPS = 8
D_FF = 4 * D_MODEL
EPS = 1e-6

LANES = 128
SUBLANES = 8
MXU_WIDTH = 256
KEY_BLOCK = 2 * CHUNK
BAND_BLOCKS = 3
VMEM_LIMIT = 56 * 1024 * 1024

PROMPT_TILE = 512
PROMPT_SUBTILE = 256
ROW_TILE = 512


def _gelu(x):
    k1 = -2.0 * math.sqrt(2.0 / math.pi) * math.log2(math.e)
    return x / (1.0 + jnp.exp2(x * (k1 + (k1 * 0.044715) * (x * x))))


def _rms_rows(x, g):
    return x * lax.rsqrt(jnp.mean(x * x, axis=-1, keepdims=True) + EPS) * g


def _lane_iota(shape):
    return lax.broadcasted_iota(jnp.int32, shape, 1)


def _row_iota(shape):
    return lax.broadcasted_iota(jnp.int32, shape, 0)


def _head_norm(xt, gain, scale=1.0):
    sq = xt * xt
    lo = _lane_iota(xt.shape) < HEAD_DIM
    s0 = jnp.sum(jnp.where(lo, sq, 0.0), axis=1, keepdims=True)
    s1 = jnp.sum(jnp.where(lo, 0.0, sq), axis=1, keepdims=True)
    ss = jnp.where(lo, s0, s1)
    return xt * lax.rsqrt(ss + HEAD_DIM * EPS) * (gain * (scale * math.sqrt(HEAD_DIM)))


def _head_ones():
    return ((_row_iota((LANES, LANES)) // HEAD_DIM) == (_lane_iota((LANES, LANES)) // HEAD_DIM)).astype(BF16)


def _norm_rope(x, xp, g_cos, g_sin, ones_bd):
    ss = jnp.dot((x * x).astype(BF16), ones_bd, preferred_element_type=F32)
    return (x * g_cos + xp * g_sin) * lax.rsqrt(ss + HEAD_DIM * EPS)


def _rope(xt, cos_t, sin_t):
    first_half = (_lane_iota(xt.shape) % HEAD_DIM) < (HEAD_DIM // 2)
    partner = jnp.where(first_half, pltpu.roll(xt, LANES - HEAD_DIM // 2, 1), pltpu.roll(xt, HEAD_DIM // 2, 1))
    return xt * cos_t + partner * sin_t


def _dup_head(t, j):
    lo = _lane_iota(t.shape) < HEAD_DIM
    swapped = pltpu.roll(t, HEAD_DIM, 1)
    return jnp.where(lo, t, swapped) if j == 0 else jnp.where(lo, swapped, t)


def _split_diag(t):
    lo = _lane_iota(t.shape) < HEAD_DIM
    return jnp.where(lo, t, 0.0), jnp.where(lo, 0.0, t)


def _ones_diag(rows):
    r = _row_iota((rows, LANES))
    return (((r % KEY_BLOCK) // CHUNK) == (_lane_iota((rows, LANES)) // HEAD_DIM)).astype(BF16)


def _attend(qs, kb, vb, sink_rows, col_mask):
    e, m = _attend_weights(_attend_scores(qs, kb, col_mask), sink_rows)
    return _attend_output(e, m, vb, sink_rows)


def _attend_scores(qs, kb, col_mask):
    s = lax.dot_general(qs, kb, (((1,), (1,)), ((), ())), preferred_element_type=F32)
    return s if col_mask is None else jnp.where(col_mask, s, NEG)


def _attend_weights(s, sink_rows):
    mx = jnp.maximum(jnp.maximum(s[:, :LANES], s[:, LANES:2 * LANES]), s[:, 2 * LANES:])
    lo = _lane_iota(mx.shape) < HEAD_DIM
    m0 = jnp.max(jnp.where(lo, mx, -jnp.inf), axis=1, keepdims=True)
    m1 = jnp.max(jnp.where(lo, -jnp.inf, mx), axis=1, keepdims=True)
    m = jnp.maximum(jnp.where(lo, m0, m1), sink_rows)
    e = jnp.exp(s - jnp.concatenate([m, m, m], axis=1))
    return e.astype(BF16), m


def _attend_output(e, m, vb, sink_rows):
    od = jnp.dot(e, vb, preferred_element_type=F32)
    den = od[:, LANES:] + jnp.exp(sink_rows - m)
    return od[:, :LANES] / den


def _lru_scan(a, b, seg, tmod):
    d = 1
    while d < seg:
        ok = tmod >= d
        b = jnp.where(ok, a * pltpu.roll(b, d, 0), 0.0) + b
        if 2 * d < seg:
            a = jnp.where(ok, a * pltpu.roll(a, d, 0), a)
        d *= 2
    return b


def _lru_scan_tile(a, b, h_prev):
    rows = a.shape[0]
    groups = rows // SUBLANES
    a3 = a.reshape(groups, SUBLANES, LANES)
    b3 = b.reshape(groups, SUBLANES, LANES)
    sub = lax.broadcasted_iota(jnp.int32, a3.shape, 1)
    d = 1
    while d < SUBLANES:
        ok = sub >= d
        b3 = jnp.where(ok, a3 * pltpu.roll(b3, d, 1), 0.0) + b3
        a3 = jnp.where(ok, a3 * pltpu.roll(a3, d, 1), a3)
        d *= 2
    carry = jnp.broadcast_to(h_prev, (SUBLANES, LANES))
    out = []
    for g in range(groups):
        hg = a3[g] * carry + b3[g]
        out.append(hg)
        carry = jnp.broadcast_to(hg[SUBLANES - 1:SUBLANES, :], (SUBLANES, LANES))
    return jnp.concatenate(out, axis=0)


def _lru_branch(xs, gr, h_in, seg, tmod, convw_ref, convb_ref, wg_ref, ba_ref, bx_ref, lam_ref):
    xc, gates = _lru_gates(xs, convw_ref, convb_ref, wg_ref)
    parts = [_lru_piece(k, xc, gates, gr, h_in, seg, tmod, ba_ref, bx_ref, lam_ref) for k in range(D_LRU // LANES)]
    return jnp.concatenate([p[0] for p in parts], axis=1), jnp.concatenate([p[1] for p in parts], axis=1)


def _lru_gates(xs, convw_ref, convb_ref, wg_ref):
    xc = xs[0] * convw_ref[0:1, :] + convb_ref[...]
    for i in range(1, CONV_W):
        xc = xc + xs[i] * convw_ref[i:i + 1, :]
    xcb = xc.astype(BF16)
    half = D_LRU // 2
    return xc, [jnp.dot(xcb[:, hh * half:(hh + 1) * half], wg_ref[hh], preferred_element_type=F32)
                for hh in range(2)]


def _lru_piece(k, xc, gates, gr, h_in, seg, tmod, ba_ref, bx_ref, lam_ref):
    half = D_LRU // 2
    hh, tt = divmod(k, half // LANES)
    sl = slice(k * LANES, (k + 1) * LANES)
    g = gates[hh]
    r = jax.nn.sigmoid(g[:, tt * LANES:(tt + 1) * LANES] + ba_ref[:, sl])
    gi = jax.nn.sigmoid(g[:, half + tt * LANES:half + (tt + 1) * LANES] + bx_ref[:, sl])
    log_a = r * (-LRU_C * jax.nn.softplus(-lam_ref[:, sl]))
    a = jnp.exp(log_a)
    b = jnp.sqrt(-jnp.tanh(log_a) * (a * a + 1.0)) * (gi * xc[:, sl])
    if seg is None:
        h = _lru_scan_tile(a, b, h_in[:, sl])
    else:
        h = _lru_scan(a, b + jnp.where(tmod == 0, a * h_in[:, sl], 0.0), seg, tmod)
    return h * _gelu(gr[:, sl]), h


def _even_prompt_kernel(x_ref, cos_ref, sin_ref, inv_ref, sgn_ref, g_ref, win_ref, qg_ref, kg_ref, sink_ref, convw_ref, convb_ref,
                        wg_ref, ba_ref, bx_ref, lam_ref, wout_ref,
                        y_ref, nk_ref, nv_ref, nh_ref, nc_ref,
                        kb_scr, vb_scr, xr_scr, h_scr, mix_scr):
    t = pl.program_id(1)
    nt = pl.num_programs(1)
    tile = x_ref.shape[1]
    n_chunks = tile // CHUNK
    hist = 2 * KEY_BLOCK
    slot = t % 2

    @pl.when(t == 0)
    def _():
        ones = _ones_diag(vb_scr.shape[2])
        for j in range(N_KV_HEADS):
            kb_scr[0, j, 0:hist, :] = jnp.zeros((hist, LANES), BF16)
            vb_scr[0, j, 0:hist, 0:LANES] = jnp.zeros((hist, LANES), BF16)
            for s in range(2):
                vb_scr[s, j, :, LANES:2 * LANES] = ones
        xr_scr[0:8, :] = jnp.zeros((8, D_LRU), F32)
        h_scr[...] = jnp.zeros(h_scr.shape, F32)

    sub = PROMPT_SUBTILE
    n_sub = tile // sub
    sub_chunks = sub // CHUNK
    o3 = Q_W + 2 * KV_W
    xs_in, proj = [], []
    for u in range(n_sub):
        x = x_ref[0, u * sub:(u + 1) * sub, :]
        xn = _rms_rows(x, g_ref[...]).astype(BF16)
        xs_in.append(x)
        o5 = o3 + 2 * D_LRU
        proj.append((jnp.dot(xn, win_ref[:, 0:Q_W], preferred_element_type=F32),
                     jnp.dot(xn, win_ref[:, Q_W:o3], preferred_element_type=F32),
                     jnp.dot(xn, win_ref[:, o3:o3 + D_LRU], preferred_element_type=F32),
                     jnp.dot(xn, win_ref[:, o3 + D_LRU:o5], preferred_element_type=F32),
                     jnp.dot(xn, win_ref[:, o5:o5 + Q_W], preferred_element_type=F32),
                     jnp.dot(xn, win_ref[:, o5 + Q_W:o5 + Q_W + KV_W], preferred_element_type=F32)))
    for u in range(n_sub):
        xr_scr[8 + u * sub:8 + (u + 1) * sub, :] = proj[u][2]

    base = (t * tile).astype(F32) * inv_ref[...]
    cos_b = jnp.cos(base)
    sin_b = jnp.sin(base) * sgn_ref[...]

    band = BAND_BLOCKS * KEY_BLOCK
    col = _lane_iota((2 * CHUNK, band))
    sink_rows = [jnp.concatenate(
        [jnp.broadcast_to(sink_ref[2 * j:2 * j + 1, :], (CHUNK, LANES)),
         jnp.broadcast_to(sink_ref[2 * j + 1:2 * j + 2, :], (CHUNK, LANES))], axis=0) for j in range(N_KV_HEADS)]
    ones_bd = _head_ones()
    norm_c = math.sqrt(HEAD_DIM)
    q_all = []
    for u in range(n_sub):
        hq, hkv, xr, gr, hqp, hkp = proj[u]
        cos_l = cos_ref[u * sub:(u + 1) * sub, :]
        sin_l = sin_ref[u * sub:(u + 1) * sub, :]
        cos_t = cos_b * cos_l - sin_b * sin_l
        sin_t = sin_b * cos_l + cos_b * sin_l
        q_cos = cos_t * (qg_ref[0:1, :] * (norm_c * HEAD_DIM ** -0.5))
        q_sin = sin_t * (qg_ref[1:2, :] * (norm_c * HEAD_DIM ** -0.5))
        q_tiles = [_norm_rope(hq[:, n * LANES:(n + 1) * LANES], hqp[:, n * LANES:(n + 1) * LANES],
                              q_cos, q_sin, ones_bd).astype(BF16) for n in range(Q_W // LANES)]
        kr = _norm_rope(hkv[:, 0:KV_W], hkp, cos_t * (kg_ref[0:1, :] * norm_c), sin_t * (kg_ref[1:2, :] * norm_c),
                        ones_bd)
        vv = hkv[:, KV_W:2 * KV_W]
        for j in range(N_KV_HEADS):
            ktop, kbot = _split_diag(_dup_head(kr, j))
            vtop, vbot = _split_diag(_dup_head(vv, j))
            for i in range(sub_chunks):
                r0 = hist + (u * sub_chunks + i) * KEY_BLOCK
                rows = slice(i * CHUNK, (i + 1) * CHUNK)
                kb_scr[slot, j, r0:r0 + CHUNK, :] = ktop[rows].astype(BF16)
                kb_scr[slot, j, r0 + CHUNK:r0 + KEY_BLOCK, :] = kbot[rows].astype(BF16)
                vb_scr[slot, j, r0:r0 + CHUNK, 0:LANES] = vtop[rows].astype(BF16)
                vb_scr[slot, j, r0 + CHUNK:r0 + KEY_BLOCK, 0:LANES] = vbot[rows].astype(BF16)
        q_all.append(q_tiles)

    pairs = [(i, j) for i in range(sub_chunks) for j in range(N_KV_HEADS)]
    scores_all, gates_all = [], []
    for u in range(n_sub):
        q_tiles = q_all[u]
        scores = []
        for i, j in pairs:
            rows = slice(i * CHUNK, (i + 1) * CHUNK)
            ci = u * sub_chunks + i
            col_mask = None
            if ci < BAND_BLOCKS - 1:
                col_mask = col >= (1 - jnp.minimum(t, 1)) * ((BAND_BLOCKS - 1 - ci) * KEY_BLOCK)
            qs = jnp.concatenate([q_tiles[2 * j][rows], q_tiles[2 * j + 1][rows]], axis=0)
            scores.append(_attend_scores(qs, kb_scr[slot, j, ci * KEY_BLOCK:ci * KEY_BLOCK + band, :], col_mask))
        scores_all.append(scores)
        first = 8 + u * sub
        taps = [xr_scr[first - (CONV_W - 1 - i):first - (CONV_W - 1 - i) + sub, :] for i in range(CONV_W - 1)]
        gates_all.append(_lru_gates(taps + [proj[u][2]], convw_ref, convb_ref, wg_ref))

    h_carry = h_scr[0:1, :]
    for u in range(n_sub):
        xr, gr = proj[u][2], proj[u][3]
        scores = scores_all[u]
        xc, gates = gates_all[u]
        n_pieces = D_LRU // LANES
        per_piece = len(pairs) // n_pieces
        weights, lru = [], []
        for k in range(n_pieces):
            weights += [_attend_weights(scores[p], sink_rows[pairs[p][1]])
                        for p in range(k * per_piece, (k + 1) * per_piece)]
            lru.append(_lru_piece(k, xc, gates, gr, h_carry, None, None, ba_ref, bx_ref, lam_ref))
        for (e, m), (i, j) in zip(weights, pairs):
            ci = u * sub_chunks + i
            rows = slice(u * sub + i * CHUNK, u * sub + (i + 1) * CHUNK)
            o = _attend_output(e, m, vb_scr[slot, j, ci * KEY_BLOCK:ci * KEY_BLOCK + band, :], sink_rows[j])
            c0 = 2 * j * LANES
            mix_scr[rows, c0:c0 + LANES] = o[0:CHUNK].astype(BF16)
            mix_scr[rows, c0 + LANES:c0 + 2 * LANES] = o[CHUNK:2 * CHUNK].astype(BF16)

        h_carry = jnp.concatenate([h[sub - 1:sub, :] for _, h in lru], axis=1)
        rows = slice(u * sub, (u + 1) * sub)
        for k, (rec, _) in enumerate(lru):
            mix_scr[rows, Q_W + k * LANES:Q_W + (k + 1) * LANES] = rec.astype(BF16)

        y_att = jnp.dot(mix_scr[rows, 0:Q_W], wout_ref[0:Q_W, :], preferred_element_type=F32)
        y_rec = jnp.dot(mix_scr[rows, Q_W:Q_W + D_LRU], wout_ref[Q_W:Q_W + D_LRU, :], preferred_element_type=F32)
        y_ref[0, rows, :] = xs_in[u] + y_att + y_rec

    last = n_chunks * KEY_BLOCK
    for j in range(N_KV_HEADS):
        kb_scr[1 - slot, j, 0:hist, :] = kb_scr[slot, j, last:last + hist, :]
        vb_scr[1 - slot, j, 0:hist, 0:LANES] = vb_scr[slot, j, last:last + hist, 0:LANES]
    h_scr[0:1, :] = h_carry
    xr_scr[0:8, :] = xr[sub - 8:sub, :]

    @pl.when(t == nt - 1)
    def _():
        nk_ref[0] = kr[sub - SWA_ROWS:sub, :]
        nv_ref[0] = vv[sub - SWA_ROWS:sub, :]
        nh_ref[0] = h_carry
        nc_ref[0] = xr[sub - (CONV_W - 1):sub, :]


def _const_spec(shape):
    zeros = (0,) * len(shape)
    return pl.BlockSpec(shape, lambda *_: zeros, pipeline_mode=pl.Buffered(1))


def _even_prompt(x, p):
    bsz, s_len, _ = x.shape
    tile = PROMPT_TILE
    cos_l, sin_l = _rope_tables(jnp.arange(tile))
    inv_row, sgn_row = _rope_rows()
    nt = s_len // tile
    key_rows = (tile // CHUNK + 2) * KEY_BLOCK
    consts = [p['g'], p['win'], p['qg'], p['kg'], p['sink'], p['convw'], p['convb'], p['wg'], p['ba'], p['bx'],
              p['lam'], p['wout']]
    out_shape = [
        jax.ShapeDtypeStruct((bsz, s_len, D_MODEL), F32),
        jax.ShapeDtypeStruct((bsz, SWA_ROWS, KV_W), F32),
        jax.ShapeDtypeStruct((bsz, SWA_ROWS, KV_W), F32),
        jax.ShapeDtypeStruct((bsz, 1, D_LRU), F32),
        jax.ShapeDtypeStruct((bsz, CONV_W - 1, D_LRU), F32),
    ]
    return pl.pallas_call(
        _even_prompt_kernel,
        out_shape=out_shape,
        grid=(bsz, nt),
        in_specs=[pl.BlockSpec((1, tile, D_MODEL), lambda b, t: (b, t, 0)),
                  _const_spec(cos_l.shape), _const_spec(sin_l.shape), _const_spec(inv_row.shape),
                  _const_spec(sgn_row.shape)] + [_const_spec(c.shape) for c in consts],
        out_specs=[pl.BlockSpec((1, tile, D_MODEL), lambda b, t: (b, t, 0)),
                   pl.BlockSpec((1, SWA_ROWS, KV_W), lambda b, t: (b, 0, 0)),
                   pl.BlockSpec((1, SWA_ROWS, KV_W), lambda b, t: (b, 0, 0)),
                   pl.BlockSpec((1, 1, D_LRU), lambda b, t: (b, 0, 0)),
                   pl.BlockSpec((1, CONV_W - 1, D_LRU), lambda b, t: (b, 0, 0))],
        scratch_shapes=[pltpu.VMEM((2, N_KV_HEADS, key_rows, LANES), BF16),
                        pltpu.VMEM((2, N_KV_HEADS, key_rows, 2 * LANES), BF16),
                        pltpu.VMEM((tile + 8, D_LRU), F32),
                        pltpu.VMEM((8, D_LRU), F32),
                        pltpu.VMEM((tile, Q_W + D_LRU), BF16)],
        compiler_params=pltpu.CompilerParams(dimension_semantics=("arbitrary", "arbitrary"),
                                             vmem_limit_bytes=VMEM_LIMIT),
        name="even_mixer_prompt",
    )(x, cos_l, sin_l, inv_row, sgn_row, *consts)


def _even_sample_kernel(x_ref, cos_ref, sin_ref, kc_ref, vc_ref, h0_ref, xprev_ref,
                        g_ref, win_ref, qg_ref, kg_ref, sink_ref, convw_ref, convb_ref,
                        wg_ref, ba_ref, bx_ref, lam_ref, wout_ref,
                        y_ref, nk_ref, nv_ref, h_ref, xr_ref,
                        q_scr, k_scr, v_scr, mix_scr):
    rows_all = x_ref.shape[0]
    nseq = kc_ref.shape[0]
    t_len = rows_all // nseq
    x = x_ref[...]
    xn = _rms_rows(x, g_ref[...]).astype(BF16)
    cos_t = cos_ref[...]
    sin_t = sin_ref[...]

    hq = jnp.dot(xn, win_ref[:, 0:Q_W], preferred_element_type=F32)
    for n in range(Q_W // LANES):
        qt = _head_norm(hq[:, n * LANES:(n + 1) * LANES], qg_ref[0:1, :], HEAD_DIM ** -0.5)
        q_scr[:, n * LANES:(n + 1) * LANES] = _rope(qt, cos_t, sin_t)
    hkv = jnp.dot(xn, win_ref[:, Q_W:Q_W + 2 * KV_W], preferred_element_type=F32)
    k_scr[...] = _rope(_head_norm(hkv[:, 0:KV_W], kg_ref[0:1, :]), cos_t, sin_t)
    v_scr[...] = hkv[:, KV_W:2 * KV_W]

    band = BAND_BLOCKS * KEY_BLOCK
    col = _lane_iota((2 * t_len, band))
    col_mask = jnp.logical_or(col < 2 * KEY_BLOCK, (col % CHUNK) < t_len)
    ones_band = _ones_diag(band)
    pad = jnp.zeros((CHUNK - t_len, LANES), F32)

    def seq_body(b, carry):
        r0 = pl.multiple_of(b * t_len, t_len)
        kc = kc_ref[b]
        vc = vc_ref[b]
        knew = k_scr[pl.ds(r0, t_len), :]
        vnew = v_scr[pl.ds(r0, t_len), :]
        nk_ref[b, 0:SWA_ROWS - t_len, :] = kc[t_len:SWA_ROWS]
        nk_ref[b, SWA_ROWS - t_len:SWA_ROWS, :] = knew
        nv_ref[b, 0:SWA_ROWS - t_len, :] = vc[t_len:SWA_ROWS]
        nv_ref[b, SWA_ROWS - t_len:SWA_ROWS, :] = vnew
        kfull = jnp.concatenate([kc, knew, pad], axis=0)
        vfull = jnp.concatenate([vc, vnew, pad], axis=0)
        for j in range(N_KV_HEADS):
            ktop, kbot = _split_diag(_dup_head(kfull, j))
            vtop, vbot = _split_diag(_dup_head(vfull, j))
            kparts, vparts = [], []
            for c in range(BAND_BLOCKS):
                rs = slice(c * CHUNK, (c + 1) * CHUNK)
                kparts += [ktop[rs], kbot[rs]]
                vparts += [vtop[rs], vbot[rs]]
            kb = jnp.concatenate(kparts, axis=0).astype(BF16)
            vb = jnp.concatenate([jnp.concatenate(vparts, axis=0).astype(BF16), ones_band], axis=1)
            c0 = 2 * j * LANES
            qs = jnp.concatenate([q_scr[pl.ds(r0, t_len), c0:c0 + LANES],
                                  q_scr[pl.ds(r0, t_len), c0 + LANES:c0 + 2 * LANES]], axis=0).astype(BF16)
            sink_rows = jnp.concatenate(
                [jnp.broadcast_to(sink_ref[2 * j:2 * j + 1, :], (t_len, LANES)),
                 jnp.broadcast_to(sink_ref[2 * j + 1:2 * j + 2, :], (t_len, LANES))], axis=0)
            o = _attend(qs, kb, vb, sink_rows, col_mask)
            mix_scr[pl.ds(r0, t_len), c0:c0 + LANES] = o[0:t_len]
            mix_scr[pl.ds(r0, t_len), c0 + LANES:c0 + 2 * LANES] = o[t_len:2 * t_len]
        return carry

    lax.fori_loop(0, nseq, seq_body, 0)

    o3 = Q_W + 2 * KV_W
    xr = jnp.dot(xn, win_ref[:, o3:o3 + D_LRU], preferred_element_type=F32)
    gr = jnp.dot(xn, win_ref[:, o3 + D_LRU:o3 + 2 * D_LRU], preferred_element_type=F32)
    tmod = _row_iota((rows_all, LANES)) % t_len
    tmod_w = _row_iota((rows_all, D_LRU)) % t_len
    xprev = xprev_ref[...]
    xs = []
    for i in range(CONV_W - 1):
        k = CONV_W - 1 - i
        own = pltpu.roll(xr, k, 0)
        cached = xprev if k == CONV_W - 1 else pltpu.roll(xprev, rows_all - (CONV_W - 1 - k), 0)
        xs.append(jnp.where(tmod_w >= k, own, cached))
    xs.append(xr)
    rec, h = _lru_branch(xs, gr, h0_ref[...], t_len, tmod, convw_ref, convb_ref, wg_ref, ba_ref, bx_ref, lam_ref)
    mix_scr[:, Q_W:Q_W + D_LRU] = rec
    xr_ref[...] = xr
    h_ref[...] = h
    y_ref[...] = x + jnp.dot(mix_scr[...].astype(BF16), wout_ref[...], preferred_element_type=F32)


def _even_sample(x2, cos_t, sin_t, kc, vc, h0_rows, xprev, p):
    rows_all = x2.shape[0]
    nseq = kc.shape[0]
    consts = [p['g'], p['win'], p['qg'], p['kg'], p['sink'], p['convw'], p['convb'], p['wg'], p['ba'], p['bx'],
              p['lam'], p['wout']]
    ins = [x2, cos_t, sin_t, kc, vc, h0_rows, xprev] + consts
    out_shape = [
        jax.ShapeDtypeStruct((rows_all, D_MODEL), F32),
        jax.ShapeDtypeStruct((nseq, SWA_ROWS, KV_W), F32),
        jax.ShapeDtypeStruct((nseq, SWA_ROWS, KV_W), F32),
        jax.ShapeDtypeStruct((rows_all, D_LRU), F32),
        jax.ShapeDtypeStruct((rows_all, D_LRU), F32),
    ]
    return pl.pallas_call(
        _even_sample_kernel,
        out_shape=out_shape,
        grid=(1,),
        in_specs=[_const_spec(a.shape) for a in ins],
        out_specs=[pl.BlockSpec(s.shape, lambda i, n=len(s.shape): (0,) * n) for s in out_shape],
        scratch_shapes=[pltpu.VMEM((rows_all, Q_W), F32),
                        pltpu.VMEM((rows_all, KV_W), F32),
                        pltpu.VMEM((rows_all, KV_W), F32),
                        pltpu.VMEM((rows_all, Q_W + D_LRU), F32)],
        compiler_params=pltpu.CompilerParams(dimension_semantics=("arbitrary",), vmem_limit_bytes=VMEM_LIMIT),
        name="even_mixer_sample",
    )(*ins)


def _ffn_kernel(xp_ref, xs_ref, g_ref, w1_ref, w2_ref, yp_ref, ys_ref):
    i = pl.program_id(0)
    x = jnp.where(i == 0, xs_ref[...], xp_ref[...])
    xn = _rms_rows(x, g_ref[...]).astype(BF16)
    acc = x
    step = D_MODEL
    for c in range(D_FF // step):
        hcol = jnp.dot(xn, w1_ref[:, c * step:(c + 1) * step], preferred_element_type=F32)
        hcol = jnp.square(jnp.maximum(hcol, 0.0)).astype(BF16)
        acc = acc + jnp.dot(hcol, w2_ref[c * step:(c + 1) * step, :], preferred_element_type=F32)

    yp_ref[...] = acc

    @pl.when(i == 0)
    def _():
        ys_ref[...] = yp_ref[...]


def _sample_then_tiles_specs(tile, width):
    prompt = pl.BlockSpec((tile, width), lambda i: (jnp.maximum(i - 1, 0), 0))
    sample = pl.BlockSpec((tile, width), lambda i: (0, 0))
    return prompt, sample


def _ffn(xp, xs, g, w1, w2, layer):
    tile = xs.shape[0]
    n_tiles = xp.shape[0] // tile
    xp_spec, xs_spec = _sample_then_tiles_specs(tile, D_MODEL)
    layer_spec = lambda shape: pl.BlockSpec((None,) + shape[1:], lambda i: (layer, 0, 0),
                                            pipeline_mode=pl.Buffered(1))
    return pl.pallas_call(
        _ffn_kernel,
        out_shape=[jax.ShapeDtypeStruct(xp.shape, F32), jax.ShapeDtypeStruct(xs.shape, F32)],
        grid=(n_tiles + 1,),
        in_specs=[xp_spec, xs_spec, _const_spec(g.shape), layer_spec(w1.shape), layer_spec(w2.shape)],
        out_specs=list(_sample_then_tiles_specs(tile, D_MODEL)),
        compiler_params=pltpu.CompilerParams(dimension_semantics=("arbitrary",), vmem_limit_bytes=VMEM_LIMIT),
        name="channel_mlp",
    )(xp, xs, g, w1, w2)


def _gmlp_kernel(seg_s, xp_ref, xs_ref, g_ref, win_ref, vg_ref, wsp_ref, wss_ref, biasp_ref, biass_ref, wout_ref,
                 yp_ref, ys_ref, v_ref):
    i = pl.program_id(0)
    is_s = i == 0
    tile = xp_ref.shape[0]
    x = jnp.where(is_s, xs_ref[...], xp_ref[...])
    xn = _rms_rows(x, g_ref[...]).astype(BF16)
    zv = jnp.dot(xn, win_ref[:, D_C:2 * D_C], preferred_element_type=F32)
    u_all = jnp.dot(xn, win_ref[:, 0:D_C], preferred_element_type=F32)
    zv = _gelu(zv)
    zc = zv - jnp.mean(zv, axis=-1, keepdims=True)
    v_all = zc * lax.rsqrt(jnp.mean(zc * zc, axis=-1, keepdims=True) + EPS) * vg_ref[...]
    r = _row_iota((CHUNK_MLP, CHUNK_MLP))
    col = _lane_iota((CHUNK_MLP, CHUNK_MLP))
    keep_p = col <= r
    keep_s = jnp.logical_and((r // seg_s) == (col // seg_s), keep_p)
    vb = v_all.astype(BF16)
    sv = []
    for gi in range(C_GROUPS):
        w = jnp.where(is_s, jnp.where(keep_s, wss_ref[gi], 0.0), jnp.where(keep_p, wsp_ref[gi], 0.0)).astype(BF16)
        lanes = slice(gi * LANES, (gi + 1) * LANES)
        bias = jnp.where(is_s, biass_ref[:, lanes], biasp_ref[:, lanes])
        blocks = [jnp.dot(w, vb[ch * CHUNK_MLP:(ch + 1) * CHUNK_MLP, lanes], preferred_element_type=F32) + bias
                  for ch in range(tile // CHUNK_MLP)]
        sv.append(jnp.concatenate(blocks, axis=0))
    acc = x
    step = MXU_WIDTH
    per_step = step // LANES
    for c in range(D_C // step):
        cols = slice(c * step, (c + 1) * step)
        gate = (_gelu(u_all[:, cols]) * jnp.concatenate(sv[c * per_step:(c + 1) * per_step], axis=1)).astype(BF16)
        acc = acc + jnp.dot(gate, wout_ref[cols, :], preferred_element_type=F32)

    yp_ref[...] = acc

    @pl.when(is_s)
    def _():
        ys_ref[...] = yp_ref[...]
        v_ref[...] = v_all


def _gmlp(xp, xs, seg_s, g, win, vg, ws_p, ws_s, bias_p, bias_s, wout):
    tile = xs.shape[0]
    n_tiles = xp.shape[0] // tile
    xp_spec, xs_spec = _sample_then_tiles_specs(tile, D_MODEL)
    consts = (g, win, vg, ws_p, ws_s, bias_p, bias_s, wout)
    return pl.pallas_call(
        functools.partial(_gmlp_kernel, seg_s),
        out_shape=[jax.ShapeDtypeStruct(xp.shape, F32), jax.ShapeDtypeStruct(xs.shape, F32),
                   jax.ShapeDtypeStruct((tile, D_C), F32)],
        grid=(n_tiles + 1,),
        in_specs=[xp_spec, xs_spec] + [_const_spec(a.shape) for a in consts],
        out_specs=list(_sample_then_tiles_specs(tile, D_MODEL)) + [pl.BlockSpec((tile, D_C), lambda i: (0, 0))],
        compiler_params=pltpu.CompilerParams(dimension_semantics=("arbitrary",), vmem_limit_bytes=VMEM_LIMIT),
        name="gmlp_mixer",
    )(xp, xs, *consts)


def _rope_inv():
    half = HEAD_DIM // 2
    return ROPE_THETA ** (-jnp.arange(half, dtype=F32) / half)


def _rope_rows():
    inv = _rope_inv()
    sgn = jnp.ones((HEAD_DIM // 2,), F32)
    return jnp.concatenate([inv] * 4)[None, :], jnp.concatenate([-sgn, sgn, -sgn, sgn])[None, :]


def _rope_tables(pos):
    inv = _rope_inv()
    ang = pos.astype(F32)[:, None] * inv[None, :]
    cos = jnp.cos(ang)
    sin = jnp.sin(ang)
    return jnp.concatenate([cos] * 4, axis=1), jnp.concatenate([-sin, sin, -sin, sin], axis=1)


def _swap_head_halves(a):
    half = HEAD_DIM // 2
    return a.reshape(a.shape[:-1] + (-1, 2, half))[..., ::-1, :].reshape(a.shape)


def _gate_weights(wa, wx):
    def half_diag(w, hh):
        out = jnp.zeros((D_LRU // 2, D_LRU // 2), F32)
        for n in range(LRU_BLOCKS // 2):
            out = out.at[n * LRU_BW:(n + 1) * LRU_BW, n * LRU_BW:(n + 1) * LRU_BW].set(w[hh * 4 + n])
        return out
    return jnp.stack([jnp.concatenate([half_diag(wa, hh), half_diag(wx, hh)], axis=1)
                      for hh in range(2)]).astype(BF16)


def kernel(x_prompt, x_sample, cache_swa_k, cache_swa_v, state_lru_h, state_lru_conv, e_norm_g, e_w_in, e_q_norm_g, e_k_norm_g, e_sinks, e_conv_w, e_conv_b, e_gate_a_w, e_gate_a_b, e_gate_x_w, e_gate_x_b, e_lru_lambda, e_w_out, o_norm_g, o_w_in, o_v_norm_g, o_spatial_w, o_spatial_b, o_w_out, ffn_norm_g, ffn_w1, ffn_w2):
    bsz, s_len, _ = x_prompt.shape
    nseq, t_len, _ = x_sample.shape
    past_len = PAST_LEN
    row = lambda a: a.reshape(1, -1)

    e = 0
    p = {
        'g': row(e_norm_g[e]),
        'win': jnp.concatenate([e_w_in[e], _swap_head_halves(e_w_in[e][:, :Q_W + KV_W])], axis=1).astype(BF16),
        'qg': jnp.tile(jnp.stack([e_q_norm_g[e], _swap_head_halves(e_q_norm_g[e])]), (1, LANES // HEAD_DIM)),
        'kg': jnp.tile(jnp.stack([e_k_norm_g[e], _swap_head_halves(e_k_norm_g[e])]), (1, LANES // HEAD_DIM)),
        'sink': jnp.repeat(e_sinks[e], HEAD_DIM).reshape(N_Q_HEADS // 2, LANES),
        'convw': e_conv_w[e],
        'convb': row(e_conv_b[e]),
        'wg': _gate_weights(e_gate_a_w[e], e_gate_x_w[e]),
        'ba': row(e_gate_a_b[e]),
        'bx': row(e_gate_x_b[e]),
        'lam': row(e_lru_lambda[e]),
        'wout': e_w_out[e].astype(BF16),
    }
    yp, nk_p, nv_p, nh_p, nc_p = _even_prompt(x_prompt, p)

    cos_s, sin_s = _rope_tables(past_len + jnp.arange(t_len))
    cos_s = jnp.tile(cos_s, (nseq, 1))
    sin_s = jnp.tile(sin_s, (nseq, 1))
    kc = cache_swa_k[e].reshape(nseq, SWA_ROWS, KV_W)
    vc = cache_swa_v[e].reshape(nseq, SWA_ROWS, KV_W)
    h0_rows = jnp.repeat(state_lru_h[e], t_len, axis=0)
    xprev = jnp.pad(state_lru_conv[e], ((0, 0), (0, t_len - (CONV_W - 1)), (0, 0))).reshape(nseq * t_len, D_LRU)
    ys, nk_s, nv_s, h_s, xr_s = _even_sample(x_sample.reshape(nseq * t_len, D_MODEL), cos_s, sin_s, kc, vc,
                                             h0_rows, xprev, p)
    nh_s = h_s.reshape(nseq, t_len, D_LRU)[:, -1]
    nc_s = xr_s.reshape(nseq, t_len, D_LRU)[:, t_len - (CONV_W - 1):]

    w1 = ffn_w1.astype(BF16)
    w2 = ffn_w2.astype(BF16)
    yp, ys = _ffn(yp.reshape(bsz * s_len, D_MODEL), ys, row(ffn_norm_g[0]), w1, w2, 0)

    o = 0
    bias_p = jnp.repeat(o_spatial_b[o].T, LANES, axis=1)
    reps = CHUNK_MLP // t_len
    ws_s = jnp.tile(o_spatial_w[o][:, :t_len, :t_len], (1, reps, reps))
    bias_s = jnp.tile(jnp.repeat(o_spatial_b[o][:, :t_len].T, LANES, axis=1), (reps, 1))
    yp, ys, gv_s = _gmlp(yp, ys, t_len, row(o_norm_g[o]), o_w_in[o].astype(BF16), row(o_v_norm_g[o]),
                         o_spatial_w[o], ws_s, bias_p, bias_s, o_w_out[o].astype(BF16))

    yp, ys = _ffn(yp, ys, row(ffn_norm_g[1]), w1, w2, 1)

    kv_shape = (1, -1, SWA_ROWS, N_KV_HEADS, HEAD_DIM)
    return (yp.reshape(bsz, s_len, D_MODEL), ys.reshape(nseq, t_len, D_MODEL),
            nk_p.reshape(kv_shape), nv_p.reshape(kv_shape), nh_p.reshape(1, bsz, D_LRU), nc_p[None],
            nk_s.reshape(kv_shape), nv_s.reshape(kv_shape), nh_s[None], nc_s[None],
            gv_s.reshape(1, nseq, t_len, D_C))
```

```python
import functools
import math

import jax
import jax.numpy as jnp
from jax import lax
from jax.experimental import pallas as pl
from jax.experimental.pallas import tpu as pltpu

F32 = jnp.float32
BF16 = jnp.bfloat16

D_MODEL = 1024
CHUNK = 64
HEAD_DIM = 64
N_Q_HEADS = 8
N_KV_HEADS = 2
SWA_ROWS = 128
PAST_LEN = 4096
ROPE_THETA = 10000.0
NEG = -1e30
D_LRU = 512
LRU_BLOCKS = 8
LRU_BW = D_LRU // LRU_BLOCKS
CONV_W = 4
LRU_C = 8.0
Q_W = N_Q_HEADS * HEAD_DIM
KV_W = N_KV_HEADS * HEAD_DIM
CHUNK_MLP = 128
D_C = D_MODEL
C_GROUPS = 8
D_FF = 4 * D_MODEL
EPS = 1e-6

LANES = 128
SUBLANES = 8
MXU_WIDTH = 256
KEY_BLOCK = 2 * CHUNK
BAND_BLOCKS = 3
VMEM_LIMIT = 56 * 1024 * 1024

PROMPT_TILE = 512
PROMPT_SUBTILE = 256
ROW_TILE = 512


def _gelu(x):
    k1 = -2.0 * math.sqrt(2.0 / math.pi) * math.log2(math.e)
    return x / (1.0 + jnp.exp2(x * (k1 + (k1 * 0.044715) * (x * x))))


def _rms_rows(x, g):
    return x * lax.rsqrt(jnp.mean(x * x, axis=-1, keepdims=True) + EPS) * g


def _lane_iota(shape):
    return lax.broadcasted_iota(jnp.int32, shape, 1)


def _row_iota(shape):
    return lax.broadcasted_iota(jnp.int32, shape, 0)


def _head_norm(xt, gain, scale=1.0):
    sq = xt * xt
    lo = _lane_iota(xt.shape) < HEAD_DIM
    s0 = jnp.sum(jnp.where(lo, sq, 0.0), axis=1, keepdims=True)
    s1 = jnp.sum(jnp.where(lo, 0.0, sq), axis=1, keepdims=True)
    ss = jnp.where(lo, s0, s1)
    return xt * lax.rsqrt(ss + HEAD_DIM * EPS) * (gain * (scale * math.sqrt(HEAD_DIM)))


def _head_ones():
    return ((_row_iota((LANES, LANES)) // HEAD_DIM) == (_lane_iota((LANES, LANES)) // HEAD_DIM)).astype(BF16)


def _norm_rope(x, xp, g_cos, g_sin, ones_bd):
    ss = jnp.dot((x * x).astype(BF16), ones_bd, preferred_element_type=F32)
    return (x * g_cos + xp * g_sin) * lax.rsqrt(ss + HEAD_DIM * EPS)


def _rope(xt, cos_t, sin_t):
    first_half = (_lane_iota(xt.shape) % HEAD_DIM) < (HEAD_DIM // 2)
    partner = jnp.where(first_half, pltpu.roll(xt, LANES - HEAD_DIM // 2, 1), pltpu.roll(xt, HEAD_DIM // 2, 1))
    return xt * cos_t + partner * sin_t


def _dup_head(t, j):
    lo = _lane_iota(t.shape) < HEAD_DIM
    swapped = pltpu.roll(t, HEAD_DIM, 1)
    return jnp.where(lo, t, swapped) if j == 0 else jnp.where(lo, swapped, t)


def _split_diag(t):
    lo = _lane_iota(t.shape) < HEAD_DIM
    return jnp.where(lo, t, 0.0), jnp.where(lo, 0.0, t)


def _ones_diag(rows):
    r = _row_iota((rows, LANES))
    return (((r % KEY_BLOCK) // CHUNK) == (_lane_iota((rows, LANES)) // HEAD_DIM)).astype(BF16)


def _attend(qs, kb, vb, sink_rows, col_mask):
    e, m = _attend_weights(_attend_scores(qs, kb, col_mask), sink_rows)
    return _attend_output(e, m, vb, sink_rows)


def _attend_scores(qs, kb, col_mask):
    s = lax.dot_general(qs, kb, (((1,), (1,)), ((), ())), preferred_element_type=F32)
    return s if col_mask is None else jnp.where(col_mask, s, NEG)


def _attend_weights(s, sink_rows):
    mx = jnp.maximum(jnp.maximum(s[:, :LANES], s[:, LANES:2 * LANES]), s[:, 2 * LANES:])
    lo = _lane_iota(mx.shape) < HEAD_DIM
    m0 = jnp.max(jnp.where(lo, mx, -jnp.inf), axis=1, keepdims=True)
    m1 = jnp.max(jnp.where(lo, -jnp.inf, mx), axis=1, keepdims=True)
    m = jnp.maximum(jnp.where(lo, m0, m1), sink_rows)
    e = jnp.exp(s - jnp.concatenate([m, m, m], axis=1))
    return e.astype(BF16), m


def _attend_output(e, m, vb, sink_rows):
    od = jnp.dot(e, vb, preferred_element_type=F32)
    den = od[:, LANES:] + jnp.exp(sink_rows - m)
    return od[:, :LANES] / den


def _lru_scan(a, b, seg, tmod):
    d = 1
    while d < seg:
        ok = tmod >= d
        b = jnp.where(ok, a * pltpu.roll(b, d, 0), 0.0) + b
        if 2 * d < seg:
            a = jnp.where(ok, a * pltpu.roll(a, d, 0), a)
        d *= 2
    return b


def _lru_scan_tile(a, b, h_prev):
    rows = a.shape[0]
    groups = rows // SUBLANES
    a3 = a.reshape(groups, SUBLANES, LANES)
    b3 = b.reshape(groups, SUBLANES, LANES)
    sub = lax.broadcasted_iota(jnp.int32, a3.shape, 1)
    d = 1
    while d < SUBLANES:
        ok = sub >= d
        b3 = jnp.where(ok, a3 * pltpu.roll(b3, d, 1), 0.0) + b3
        a3 = jnp.where(ok, a3 * pltpu.roll(a3, d, 1), a3)
        d *= 2
    carry = jnp.broadcast_to(h_prev, (SUBLANES, LANES))
    out = []
    for g in range(groups):
        hg = a3[g] * carry + b3[g]
        out.append(hg)
        carry = jnp.broadcast_to(hg[SUBLANES - 1:SUBLANES, :], (SUBLANES, LANES))
    return jnp.concatenate(out, axis=0)


def _lru_branch(xs, gr, h_in, seg, tmod, convw_ref, convb_ref, wg_ref, ba_ref, bx_ref, lam_ref):
    xc, gates = _lru_gates(xs, convw_ref, convb_ref, wg_ref)
    parts = [_lru_piece(k, xc, gates, gr, h_in, seg, tmod, ba_ref, bx_ref, lam_ref) for k in range(D_LRU // LANES)]
    return jnp.concatenate([p[0] for p in parts], axis=1), jnp.concatenate([p[1] for p in parts], axis=1)


def _lru_gates(xs, convw_ref, convb_ref, wg_ref):
    xc = xs[0] * convw_ref[0:1, :] + convb_ref[...]
    for i in range(1, CONV_W):
        xc = xc + xs[i] * convw_ref[i:i + 1, :]
    xcb = xc.astype(BF16)
    half = D_LRU // 2
    return xc, [jnp.dot(xcb[:, hh * half:(hh + 1) * half], wg_ref[hh], preferred_element_type=F32)
                for hh in range(2)]


def _lru_piece(k, xc, gates, gr, h_in, seg, tmod, ba_ref, bx_ref, lam_ref):
    half = D_LRU // 2
    hh, tt = divmod(k, half // LANES)
    sl = slice(k * LANES, (k + 1) * LANES)
    g = gates[hh]
    r = jax.nn.sigmoid(g[:, tt * LANES:(tt + 1) * LANES] + ba_ref[:, sl])
    gi = jax.nn.sigmoid(g[:, half + tt * LANES:half + (tt + 1) * LANES] + bx_ref[:, sl])
    log_a = r * (-LRU_C * jax.nn.softplus(-lam_ref[:, sl]))
    a = jnp.exp(log_a)
    b = jnp.sqrt(-jnp.tanh(log_a) * (a * a + 1.0)) * (gi * xc[:, sl])
    if seg is None:
        h = _lru_scan_tile(a, b, h_in[:, sl])
    else:
        h = _lru_scan(a, b + jnp.where(tmod == 0, a * h_in[:, sl], 0.0), seg, tmod)
    return h * _gelu(gr[:, sl]), h


def _even_prompt_kernel(x_ref, cos_ref, sin_ref, inv_ref, sgn_ref, g_ref, win_ref, qg_ref, kg_ref, sink_ref, convw_ref, convb_ref,
                        wg_ref, ba_ref, bx_ref, lam_ref, wout_ref,
                        y_ref, nk_ref, nv_ref, nh_ref, nc_ref,
                        kb_scr, vb_scr, xr_scr, h_scr, mix_scr):
    t = pl.program_id(1)
    nt = pl.num_programs(1)
    tile = x_ref.shape[1]
    n_chunks = tile // CHUNK
    hist = 2 * KEY_BLOCK
    slot = t % 2

    @pl.when(t == 0)
    def _():
        ones = _ones_diag(vb_scr.shape[2])
        for j in range(N_KV_HEADS):
            kb_scr[0, j, 0:hist, :] = jnp.zeros((hist, LANES), BF16)
            vb_scr[0, j, 0:hist, 0:LANES] = jnp.zeros((hist, LANES), BF16)
            for s in range(2):
                vb_scr[s, j, :, LANES:2 * LANES] = ones
        xr_scr[0:8, :] = jnp.zeros((8, D_LRU), F32)
        h_scr[...] = jnp.zeros(h_scr.shape, F32)

    sub = PROMPT_SUBTILE
    n_sub = tile // sub
    sub_chunks = sub // CHUNK
    o3 = Q_W + 2 * KV_W
    xs_in, proj = [], []
    for u in range(n_sub):
        x = x_ref[0, u * sub:(u + 1) * sub, :]
        xn = _rms_rows(x, g_ref[...]).astype(BF16)
        xs_in.append(x)
        o5 = o3 + 2 * D_LRU
        proj.append((jnp.dot(xn, win_ref[:, 0:Q_W], preferred_element_type=F32),
                     jnp.dot(xn, win_ref[:, Q_W:o3], preferred_element_type=F32),
                     jnp.dot(xn, win_ref[:, o3:o3 + D_LRU], preferred_element_type=F32),
                     jnp.dot(xn, win_ref[:, o3 + D_LRU:o5], preferred_element_type=F32),
                     jnp.dot(xn, win_ref[:, o5:o5 + Q_W], preferred_element_type=F32),
                     jnp.dot(xn, win_ref[:, o5 + Q_W:o5 + Q_W + KV_W], preferred_element_type=F32)))
    for u in range(n_sub):
        xr_scr[8 + u * sub:8 + (u + 1) * sub, :] = proj[u][2]

    base = (t * tile).astype(F32) * inv_ref[...]
    cos_b = jnp.cos(base)
    sin_b = jnp.sin(base) * sgn_ref[...]

    band = BAND_BLOCKS * KEY_BLOCK
    col = _lane_iota((2 * CHUNK, band))
    sink_rows = [jnp.concatenate(
        [jnp.broadcast_to(sink_ref[2 * j:2 * j + 1, :], (CHUNK, LANES)),
         jnp.broadcast_to(sink_ref[2 * j + 1:2 * j + 2, :], (CHUNK, LANES))], axis=0) for j in range(N_KV_HEADS)]
    ones_bd = _head_ones()
    norm_c = math.sqrt(HEAD_DIM)
    q_all = []
    for u in range(n_sub):
        hq, hkv, xr, gr, hqp, hkp = proj[u]
        cos_l = cos_ref[u * sub:(u + 1) * sub, :]
        sin_l = sin_ref[u * sub:(u + 1) * sub, :]
        cos_t = cos_b * cos_l - sin_b * sin_l
        sin_t = sin_b * cos_l + cos_b * sin_l
        q_cos = cos_t * (qg_ref[0:1, :] * (norm_c * HEAD_DIM ** -0.5))
        q_sin = sin_t * (qg_ref[1:2, :] * (norm_c * HEAD_DIM ** -0.5))
        q_tiles = [_norm_rope(hq[:, n * LANES:(n + 1) * LANES], hqp[:, n * LANES:(n + 1) * LANES],
                              q_cos, q_sin, ones_bd).astype(BF16) for n in range(Q_W // LANES)]
        kr = _norm_rope(hkv[:, 0:KV_W], hkp, cos_t * (kg_ref[0:1, :] * norm_c), sin_t * (kg_ref[1:2, :] * norm_c),
                        ones_bd)
        vv = hkv[:, KV_W:2 * KV_W]
        for j in range(N_KV_HEADS):
            ktop, kbot = _split_diag(_dup_head(kr, j))
            vtop, vbot = _split_diag(_dup_head(vv, j))
            for i in range(sub_chunks):
                r0 = hist + (u * sub_chunks + i) * KEY_BLOCK
                rows = slice(i * CHUNK, (i + 1) * CHUNK)
                kb_scr[slot, j, r0:r0 + CHUNK, :] = ktop[rows].astype(BF16)
                kb_scr[slot, j, r0 + CHUNK:r0 + KEY_BLOCK, :] = kbot[rows].astype(BF16)
                vb_scr[slot, j, r0:r0 + CHUNK, 0:LANES] = vtop[rows].astype(BF16)
                vb_scr[slot, j, r0 + CHUNK:r0 + KEY_BLOCK, 0:LANES] = vbot[rows].astype(BF16)
        q_all.append(q_tiles)

    pairs = [(i, j) for i in range(sub_chunks) for j in range(N_KV_HEADS)]
    scores_all, gates_all = [], []
    for u in range(n_sub):
        q_tiles = q_all[u]
        scores = []
        for i, j in pairs:
            rows = slice(i * CHUNK, (i + 1) * CHUNK)
            ci = u * sub_chunks + i
            col_mask = None
            if ci < BAND_BLOCKS - 1:
                col_mask = col >= (1 - jnp.minimum(t, 1)) * ((BAND_BLOCKS - 1 - ci) * KEY_BLOCK)
            qs = jnp.concatenate([q_tiles[2 * j][rows], q_tiles[2 * j + 1][rows]], axis=0)
            scores.append(_attend_scores(qs, kb_scr[slot, j, ci * KEY_BLOCK:ci * KEY_BLOCK + band, :], col_mask))
        scores_all.append(scores)
        first = 8 + u * sub
        taps = [xr_scr[first - (CONV_W - 1 - i):first - (CONV_W - 1 - i) + sub, :] for i in range(CONV_W - 1)]
        gates_all.append(_lru_gates(taps + [proj[u][2]], convw_ref, convb_ref, wg_ref))

    h_carry = h_scr[0:1, :]
    for u in range(n_sub):
        xr, gr = proj[u][2], proj[u][3]
        scores = scores_all[u]
        xc, gates = gates_all[u]
        n_pieces = D_LRU // LANES
        per_piece = len(pairs) // n_pieces
        weights, lru = [], []
        for k in range(n_pieces):
            weights += [_attend_weights(scores[p], sink_rows[pairs[p][1]])
                        for p in range(k * per_piece, (k + 1) * per_piece)]
            lru.append(_lru_piece(k, xc, gates, gr, h_carry, None, None, ba_ref, bx_ref, lam_ref))
        for (e, m), (i, j) in zip(weights, pairs):
            ci = u * sub_chunks + i
            rows = slice(u * sub + i * CHUNK, u * sub + (i + 1) * CHUNK)
            o = _attend_output(e, m, vb_scr[slot, j, ci * KEY_BLOCK:ci * KEY_BLOCK + band, :], sink_rows[j])
            c0 = 2 * j * LANES
            mix_scr[rows, c0:c0 + LANES] = o[0:CHUNK].astype(BF16)
            mix_scr[rows, c0 + LANES:c0 + 2 * LANES] = o[CHUNK:2 * CHUNK].astype(BF16)

        h_carry = jnp.concatenate([h[sub - 1:sub, :] for _, h in lru], axis=1)
        rows = slice(u * sub, (u + 1) * sub)
        for k, (rec, _) in enumerate(lru):
            mix_scr[rows, Q_W + k * LANES:Q_W + (k + 1) * LANES] = rec.astype(BF16)

        y_att = jnp.dot(mix_scr[rows, 0:Q_W], wout_ref[0:Q_W, :], preferred_element_type=F32)
        y_rec = jnp.dot(mix_scr[rows, Q_W:Q_W + D_LRU], wout_ref[Q_W:Q_W + D_LRU, :], preferred_element_type=F32)
        y_ref[0, rows, :] = xs_in[u] + y_att + y_rec

    last = n_chunks * KEY_BLOCK
    for j in range(N_KV_HEADS):
        kb_scr[1 - slot, j, 0:hist, :] = kb_scr[slot, j, last:last + hist, :]
        vb_scr[1 - slot, j, 0:hist, 0:LANES] = vb_scr[slot, j, last:last + hist, 0:LANES]
    h_scr[0:1, :] = h_carry
    xr_scr[0:8, :] = xr[sub - 8:sub, :]

    @pl.when(t == nt - 1)
    def _():
        nk_ref[0] = kr[sub - SWA_ROWS:sub, :]
        nv_ref[0] = vv[sub - SWA_ROWS:sub, :]
        nh_ref[0] = h_carry
        nc_ref[0] = xr[sub - (CONV_W - 1):sub, :]


def _const_spec(shape):
    zeros = (0,) * len(shape)
    return pl.BlockSpec(shape, lambda *_: zeros, pipeline_mode=pl.Buffered(1))


def _even_prompt(x, p):
    bsz, s_len, _ = x.shape
    tile = PROMPT_TILE
    cos_l, sin_l = _rope_tables(jnp.arange(tile))
    inv_row, sgn_row = _rope_rows()
    nt = s_len // tile
    key_rows = (tile // CHUNK + 2) * KEY_BLOCK
    consts = [p['g'], p['win'], p['qg'], p['kg'], p['sink'], p['convw'], p['convb'], p['wg'], p['ba'], p['bx'],
              p['lam'], p['wout']]
    out_shape = [
        jax.ShapeDtypeStruct((bsz, s_len, D_MODEL), F32),
        jax.ShapeDtypeStruct((bsz, SWA_ROWS, KV_W), F32),
        jax.ShapeDtypeStruct((bsz, SWA_ROWS, KV_W), F32),
        jax.ShapeDtypeStruct((bsz, 1, D_LRU), F32),
        jax.ShapeDtypeStruct((bsz, CONV_W - 1, D_LRU), F32),
    ]
    return pl.pallas_call(
        _even_prompt_kernel,
        out_shape=out_shape,
        grid=(bsz, nt),
        in_specs=[pl.BlockSpec((1, tile, D_MODEL), lambda b, t: (b, t, 0)),
                  _const_spec(cos_l.shape), _const_spec(sin_l.shape), _const_spec(inv_row.shape),
                  _const_spec(sgn_row.shape)] + [_const_spec(c.shape) for c in consts],
        out_specs=[pl.BlockSpec((1, tile, D_MODEL), lambda b, t: (b, t, 0)),
                   pl.BlockSpec((1, SWA_ROWS, KV_W), lambda b, t: (b, 0, 0)),
                   pl.BlockSpec((1, SWA_ROWS, KV_W), lambda b, t: (b, 0, 0)),
                   pl.BlockSpec((1, 1, D_LRU), lambda b, t: (b, 0, 0)),
                   pl.BlockSpec((1, CONV_W - 1, D_LRU), lambda b, t: (b, 0, 0))],
        scratch_shapes=[pltpu.VMEM((2, N_KV_HEADS, key_rows, LANES), BF16),
                        pltpu.VMEM((2, N_KV_HEADS, key_rows, 2 * LANES), BF16),
                        pltpu.VMEM((tile + 8, D_LRU), F32),
                        pltpu.VMEM((8, D_LRU), F32),
                        pltpu.VMEM((tile, Q_W + D_LRU), BF16)],
        compiler_params=pltpu.CompilerParams(dimension_semantics=("arbitrary", "arbitrary"),
                                             vmem_limit_bytes=VMEM_LIMIT),
        name="even_mixer_prompt",
    )(x, cos_l, sin_l, inv_row, sgn_row, *consts)


def _even_sample_kernel(x_ref, cos_ref, sin_ref, kc_ref, vc_ref, h0_ref, xprev_ref,
                        g_ref, win_ref, qg_ref, kg_ref, sink_ref, convw_ref, convb_ref,
                        wg_ref, ba_ref, bx_ref, lam_ref, wout_ref,
                        y_ref, nk_ref, nv_ref, h_ref, xr_ref,
                        q_scr, k_scr, v_scr, mix_scr):
    rows_all = x_ref.shape[0]
    nseq = kc_ref.shape[0]
    t_len = rows_all // nseq
    x = x_ref[...]
    xn = _rms_rows(x, g_ref[...]).astype(BF16)
    cos_t = cos_ref[...]
    sin_t = sin_ref[...]

    hq = jnp.dot(xn, win_ref[:, 0:Q_W], preferred_element_type=F32)
    for n in range(Q_W // LANES):
        qt = _head_norm(hq[:, n * LANES:(n + 1) * LANES], qg_ref[0:1, :], HEAD_DIM ** -0.5)
        q_scr[:, n * LANES:(n + 1) * LANES] = _rope(qt, cos_t, sin_t)
    hkv = jnp.dot(xn, win_ref[:, Q_W:Q_W + 2 * KV_W], preferred_element_type=F32)
    k_scr[...] = _rope(_head_norm(hkv[:, 0:KV_W], kg_ref[0:1, :]), cos_t, sin_t)
    v_scr[...] = hkv[:, KV_W:2 * KV_W]

    band = BAND_BLOCKS * KEY_BLOCK
    col = _lane_iota((2 * t_len, band))
    col_mask = jnp.logical_or(col < 2 * KEY_BLOCK, (col % CHUNK) < t_len)
    ones_band = _ones_diag(band)
    pad = jnp.zeros((CHUNK - t_len, LANES), F32)

    def seq_body(b, carry):
        r0 = pl.multiple_of(b * t_len, t_len)
        kc = kc_ref[b]
        vc = vc_ref[b]
        knew = k_scr[pl.ds(r0, t_len), :]
        vnew = v_scr[pl.ds(r0, t_len), :]
        nk_ref[b, 0:SWA_ROWS - t_len, :] = kc[t_len:SWA_ROWS]
        nk_ref[b, SWA_ROWS - t_len:SWA_ROWS, :] = knew
        nv_ref[b, 0:SWA_ROWS - t_len, :] = vc[t_len:SWA_ROWS]
        nv_ref[b, SWA_ROWS - t_len:SWA_ROWS, :] = vnew
        kfull = jnp.concatenate([kc, knew, pad], axis=0)
        vfull = jnp.concatenate([vc, vnew, pad], axis=0)
        for j in range(N_KV_HEADS):
            ktop, kbot = _split_diag(_dup_head(kfull, j))
            vtop, vbot = _split_diag(_dup_head(vfull, j))
            kparts, vparts = [], []
            for c in range(BAND_BLOCKS):
                rs = slice(c * CHUNK, (c + 1) * CHUNK)
                kparts += [ktop[rs], kbot[rs]]
                vparts += [vtop[rs], vbot[rs]]
            kb = jnp.concatenate(kparts, axis=0).astype(BF16)
            vb = jnp.concatenate([jnp.concatenate(vparts, axis=0).astype(BF16), ones_band], axis=1)
            c0 = 2 * j * LANES
            qs = jnp.concatenate([q_scr[pl.ds(r0, t_len), c0:c0 + LANES],
                                  q_scr[pl.ds(r0, t_len), c0 + LANES:c0 + 2 * LANES]], axis=0).astype(BF16)
            sink_rows = jnp.concatenate(
                [jnp.broadcast_to(sink_ref[2 * j:2 * j + 1, :], (t_len, LANES)),
                 jnp.broadcast_to(sink_ref[2 * j + 1:2 * j + 2, :], (t_len, LANES))], axis=0)
            o = _attend(qs, kb, vb, sink_rows, col_mask)
            mix_scr[pl.ds(r0, t_len), c0:c0 + LANES] = o[0:t_len]
            mix_scr[pl.ds(r0, t_len), c0 + LANES:c0 + 2 * LANES] = o[t_len:2 * t_len]
        return carry

    lax.fori_loop(0, nseq, seq_body, 0)

    o3 = Q_W + 2 * KV_W
    xr = jnp.dot(xn, win_ref[:, o3:o3 + D_LRU], preferred_element_type=F32)
    gr = jnp.dot(xn, win_ref[:, o3 + D_LRU:o3 + 2 * D_LRU], preferred_element_type=F32)
    tmod = _row_iota((rows_all, LANES)) % t_len
    tmod_w = _row_iota((rows_all, D_LRU)) % t_len
    xprev = xprev_ref[...]
    xs = []
    for i in range(CONV_W - 1):
        k = CONV_W - 1 - i
        own = pltpu.roll(xr, k, 0)
        cached = xprev if k == CONV_W - 1 else pltpu.roll(xprev, rows_all - (CONV_W - 1 - k), 0)
        xs.append(jnp.where(tmod_w >= k, own, cached))
    xs.append(xr)
    rec, h = _lru_branch(xs, gr, h0_ref[...], t_len, tmod, convw_ref, convb_ref, wg_ref, ba_ref, bx_ref, lam_ref)
    mix_scr[:, Q_W:Q_W + D_LRU] = rec
    xr_ref[...] = xr
    h_ref[...] = h
    y_ref[...] = x + jnp.dot(mix_scr[...].astype(BF16), wout_ref[...], preferred_element_type=F32)


def _even_sample(x2, cos_t, sin_t, kc, vc, h0_rows, xprev, p):
    rows_all = x2.shape[0]
    nseq = kc.shape[0]
    consts = [p['g'], p['win'], p['qg'], p['kg'], p['sink'], p['convw'], p['convb'], p['wg'], p['ba'], p['bx'],
              p['lam'], p['wout']]
    ins = [x2, cos_t, sin_t, kc, vc, h0_rows, xprev] + consts
    out_shape = [
        jax.ShapeDtypeStruct((rows_all, D_MODEL), F32),
        jax.ShapeDtypeStruct((nseq, SWA_ROWS, KV_W), F32),
        jax.ShapeDtypeStruct((nseq, SWA_ROWS, KV_W), F32),
        jax.ShapeDtypeStruct((rows_all, D_LRU), F32),
        jax.ShapeDtypeStruct((rows_all, D_LRU), F32),
    ]
    return pl.pallas_call(
        _even_sample_kernel,
        out_shape=out_shape,
        grid=(1,),
        in_specs=[_const_spec(a.shape) for a in ins],
        out_specs=[pl.BlockSpec(s.shape, lambda i, n=len(s.shape): (0,) * n) for s in out_shape],
        scratch_shapes=[pltpu.VMEM((rows_all, Q_W), F32),
                        pltpu.VMEM((rows_all, KV_W), F32),
                        pltpu.VMEM((rows_all, KV_W), F32),
                        pltpu.VMEM((rows_all, Q_W + D_LRU), F32)],
        compiler_params=pltpu.CompilerParams(dimension_semantics=("arbitrary",), vmem_limit_bytes=VMEM_LIMIT),
        name="even_mixer_sample",
    )(*ins)


def _ffn_kernel(xp_ref, xs_ref, g_ref, w1_ref, w2_ref, yp_ref, ys_ref):
    i = pl.program_id(0)
    x = jnp.where(i == 0, xs_ref[...], xp_ref[...])
    xn = _rms_rows(x, g_ref[...]).astype(BF16)
    acc = x
    step = D_MODEL
    for c in range(D_FF // step):
        hcol = jnp.dot(xn, w1_ref[:, c * step:(c + 1) * step].astype(BF16), preferred_element_type=F32)
        hcol = jnp.square(jnp.maximum(hcol, 0.0)).astype(BF16)
        acc = acc + jnp.dot(hcol, w2_ref[c * step:(c + 1) * step, :].astype(BF16), preferred_element_type=F32)

    yp_ref[...] = acc

    @pl.when(i == 0)
    def _():
        ys_ref[...] = yp_ref[...]


def _sample_then_tiles_specs(tile, width):
    prompt = pl.BlockSpec((tile, width), lambda i: (jnp.maximum(i - 1, 0), 0))
    sample = pl.BlockSpec((tile, width), lambda i: (0, 0))
    return prompt, sample


def _ffn(xp, xs, g, w1, w2, layer):
    tile = xs.shape[0]
    n_tiles = xp.shape[0] // tile
    xp_spec, xs_spec = _sample_then_tiles_specs(tile, D_MODEL)
    layer_spec = lambda shape: pl.BlockSpec((None,) + shape[1:], lambda i: (layer, 0, 0),
                                            pipeline_mode=pl.Buffered(1))
    return pl.pallas_call(
        _ffn_kernel,
        out_shape=[jax.ShapeDtypeStruct(xp.shape, F32), jax.ShapeDtypeStruct(xs.shape, F32)],
        grid=(n_tiles + 1,),
        in_specs=[xp_spec, xs_spec, _const_spec(g.shape), layer_spec(w1.shape), layer_spec(w2.shape)],
        out_specs=list(_sample_then_tiles_specs(tile, D_MODEL)),
        compiler_params=pltpu.CompilerParams(dimension_semantics=("arbitrary",), vmem_limit_bytes=VMEM_LIMIT),
        name="channel_mlp",
    )(xp, xs, g, w1, w2)


def _gmlp_kernel(seg_s, xp_ref, xs_ref, g_ref, win_ref, vg_ref, wsp_ref, wss_ref, biasp_ref, biass_ref, wout_ref,
                 yp_ref, ys_ref, v_ref):
    i = pl.program_id(0)
    is_s = i == 0
    tile = xp_ref.shape[0]
    x = jnp.where(is_s, xs_ref[...], xp_ref[...])
    xn = _rms_rows(x, g_ref[...]).astype(BF16)
    zv = jnp.dot(xn, win_ref[:, D_C:2 * D_C], preferred_element_type=F32)
    u_all = jnp.dot(xn, win_ref[:, 0:D_C], preferred_element_type=F32)
    zv = _gelu(zv)
    zc = zv - jnp.mean(zv, axis=-1, keepdims=True)
    v_all = zc * lax.rsqrt(jnp.mean(zc * zc, axis=-1, keepdims=True) + EPS) * vg_ref[...]
    r = _row_iota((CHUNK_MLP, CHUNK_MLP))
    col = _lane_iota((CHUNK_MLP, CHUNK_MLP))
    keep_p = col <= r
    keep_s = jnp.logical_and((r // seg_s) == (col // seg_s), keep_p)
    vb = v_all.astype(BF16)
    sv = []
    for gi in range(C_GROUPS):
        w = jnp.where(is_s, jnp.where(keep_s, wss_ref[gi], 0.0), jnp.where(keep_p, wsp_ref[gi], 0.0)).astype(BF16)
        lanes = slice(gi * LANES, (gi + 1) * LANES)
        bias = jnp.where(is_s, biass_ref[:, lanes], biasp_ref[:, lanes])
        blocks = [jnp.dot(w, vb[ch * CHUNK_MLP:(ch + 1) * CHUNK_MLP, lanes], preferred_element_type=F32) + bias
                  for ch in range(tile // CHUNK_MLP)]
        sv.append(jnp.concatenate(blocks, axis=0))
    acc = x
    step = MXU_WIDTH
    per_step = step // LANES
    for c in range(D_C // step):
        cols = slice(c * step, (c + 1) * step)
        gate = (_gelu(u_all[:, cols]) * jnp.concatenate(sv[c * per_step:(c + 1) * per_step], axis=1)).astype(BF16)
        acc = acc + jnp.dot(gate, wout_ref[cols, :], preferred_element_type=F32)

    yp_ref[...] = acc

    @pl.when(is_s)
    def _():
        ys_ref[...] = yp_ref[...]
        v_ref[...] = v_all


def _gmlp(xp, xs, seg_s, g, win, vg, ws_p, ws_s, bias_p, bias_s, wout):
    tile = xs.shape[0]
    n_tiles = xp.shape[0] // tile
    xp_spec, xs_spec = _sample_then_tiles_specs(tile, D_MODEL)
    consts = (g, win, vg, ws_p, ws_s, bias_p, bias_s, wout)
    return pl.pallas_call(
        functools.partial(_gmlp_kernel, seg_s),
        out_shape=[jax.ShapeDtypeStruct(xp.shape, F32), jax.ShapeDtypeStruct(xs.shape, F32),
                   jax.ShapeDtypeStruct((tile, D_C), F32)],
        grid=(n_tiles + 1,),
        in_specs=[xp_spec, xs_spec] + [_const_spec(a.shape) for a in consts],
        out_specs=list(_sample_then_tiles_specs(tile, D_MODEL)) + [pl.BlockSpec((tile, D_C), lambda i: (0, 0))],
        compiler_params=pltpu.CompilerParams(dimension_semantics=("arbitrary",), vmem_limit_bytes=VMEM_LIMIT),
        name="gmlp_mixer",
    )(xp, xs, *consts)


def _rope_inv():
    half = HEAD_DIM // 2
    return ROPE_THETA ** (-jnp.arange(half, dtype=F32) / half)


def _rope_rows():
    inv = _rope_inv()
    sgn = jnp.ones((HEAD_DIM // 2,), F32)
    return jnp.concatenate([inv] * 4)[None, :], jnp.concatenate([-sgn, sgn, -sgn, sgn])[None, :]


def _rope_tables(pos):
    inv = _rope_inv()
    ang = pos.astype(F32)[:, None] * inv[None, :]
    cos = jnp.cos(ang)
    sin = jnp.sin(ang)
    return jnp.concatenate([cos] * 4, axis=1), jnp.concatenate([-sin, sin, -sin, sin], axis=1)


def _swap_head_halves(a):
    half = HEAD_DIM // 2
    return a.reshape(a.shape[:-1] + (-1, 2, half))[..., ::-1, :].reshape(a.shape)


def _gate_weights(wa, wx):
    def half_diag(w, hh):
        out = jnp.zeros((D_LRU // 2, D_LRU // 2), F32)
        for n in range(LRU_BLOCKS // 2):
            out = out.at[n * LRU_BW:(n + 1) * LRU_BW, n * LRU_BW:(n + 1) * LRU_BW].set(w[hh * 4 + n])
        return out
    return jnp.stack([jnp.concatenate([half_diag(wa, hh), half_diag(wx, hh)], axis=1)
                      for hh in range(2)]).astype(BF16)


def kernel(x_prompt, x_sample, cache_swa_k, cache_swa_v, state_lru_h, state_lru_conv, e_norm_g, e_w_in, e_q_norm_g, e_k_norm_g, e_sinks, e_conv_w, e_conv_b, e_gate_a_w, e_gate_a_b, e_gate_x_w, e_gate_x_b, e_lru_lambda, e_w_out, o_norm_g, o_w_in, o_v_norm_g, o_spatial_w, o_spatial_b, o_w_out, ffn_norm_g, ffn_w1, ffn_w2):
    bsz, s_len, _ = x_prompt.shape
    nseq, t_len, _ = x_sample.shape
    past_len = PAST_LEN
    row = lambda a: a.reshape(1, -1)

    e = 0
    p = {
        'g': row(e_norm_g[e]),
        'win': jnp.concatenate([e_w_in[e], _swap_head_halves(e_w_in[e][:, :Q_W + KV_W])], axis=1).astype(BF16),
        'qg': jnp.tile(jnp.stack([e_q_norm_g[e], _swap_head_halves(e_q_norm_g[e])]), (1, LANES // HEAD_DIM)),
        'kg': jnp.tile(jnp.stack([e_k_norm_g[e], _swap_head_halves(e_k_norm_g[e])]), (1, LANES // HEAD_DIM)),
        'sink': jnp.repeat(e_sinks[e], HEAD_DIM).reshape(N_Q_HEADS // 2, LANES),
        'convw': e_conv_w[e],
        'convb': row(e_conv_b[e]),
        'wg': _gate_weights(e_gate_a_w[e], e_gate_x_w[e]),
        'ba': row(e_gate_a_b[e]),
        'bx': row(e_gate_x_b[e]),
        'lam': row(e_lru_lambda[e]),
        'wout': e_w_out[e].astype(BF16),
    }
    yp, nk_p, nv_p, nh_p, nc_p = _even_prompt(x_prompt, p)

    cos_s, sin_s = _rope_tables(past_len + jnp.arange(t_len))
    cos_s = jnp.tile(cos_s, (nseq, 1))
    sin_s = jnp.tile(sin_s, (nseq, 1))
    kc = cache_swa_k[e].reshape(nseq, SWA_ROWS, KV_W)
    vc = cache_swa_v[e].reshape(nseq, SWA_ROWS, KV_W)
    h0_rows = jnp.repeat(state_lru_h[e], t_len, axis=0)
    xprev = jnp.pad(state_lru_conv[e], ((0, 0), (0, t_len - (CONV_W - 1)), (0, 0))).reshape(nseq * t_len, D_LRU)
    ys, nk_s, nv_s, h_s, xr_s = _even_sample(x_sample.reshape(nseq * t_len, D_MODEL), cos_s, sin_s, kc, vc,
                                             h0_rows, xprev, p)
    nh_s = h_s.reshape(nseq, t_len, D_LRU)[:, -1]
    nc_s = xr_s.reshape(nseq, t_len, D_LRU)[:, t_len - (CONV_W - 1):]

    w1, w2 = ffn_w1, ffn_w2
    yp, ys = _ffn(yp.reshape(bsz * s_len, D_MODEL), ys, row(ffn_norm_g[0]), w1, w2, 0)

    o = 0
    bias_p = jnp.repeat(o_spatial_b[o].T, LANES, axis=1)
    reps = CHUNK_MLP // t_len
    ws_s = jnp.tile(o_spatial_w[o][:, :t_len, :t_len], (1, reps, reps))
    bias_s = jnp.tile(jnp.repeat(o_spatial_b[o][:, :t_len].T, LANES, axis=1), (reps, 1))
    yp, ys, gv_s = _gmlp(yp, ys, t_len, row(o_norm_g[o]), o_w_in[o].astype(BF16), row(o_v_norm_g[o]),
                         o_spatial_w[o], ws_s, bias_p, bias_s, o_w_out[o].astype(BF16))

    yp, ys = _ffn(yp, ys, row(ffn_norm_g[1]), w1, w2, 1)

    kv_shape = (1, -1, SWA_ROWS, N_KV_HEADS, HEAD_DIM)
    return (yp.reshape(bsz, s_len, D_MODEL), ys.reshape(nseq, t_len, D_MODEL),
            nk_p.reshape(kv_shape), nv_p.reshape(kv_shape), nh_p.reshape(1, bsz, D_LRU), nc_p[None],
            nk_s.reshape(kv_shape), nv_s.reshape(kv_shape), nh_s[None], nc_s[None],
            gv_s.reshape(1, nseq, t_len, D_C))
```

```python
import functools
import math

import jax
import jax.numpy as jnp
from jax import lax
from jax.experimental import pallas as pl
from jax.experimental.pallas import tpu as pltpu

F32 = jnp.float32
BF16 = jnp.bfloat16

D_MODEL = 1024
CHUNK = 64
HEAD_DIM = 64
N_Q_HEADS = 8
N_KV_HEADS = 2
SWA_ROWS = 128
PAST_LEN = 4096
ROPE_THETA = 10000.0
NEG = -1e30
D_LRU = 512
LRU_BLOCKS = 8
LRU_BW = D_LRU // LRU_BLOCKS
CONV_W = 4
LRU_C = 8.0
Q_W = N_Q_HEADS * HEAD_DIM
KV_W = N_KV_HEADS * HEAD_DIM
CHUNK_MLP = 128
D_C = D_MODEL
C_GROUPS = 8
D_FF = 4 * D_MODEL
EPS = 1e-6

LANES = 128
SUBLANES = 8
MXU_WIDTH = 256
KEY_BLOCK = 2 * CHUNK
BAND_BLOCKS = 3
VMEM_LIMIT = 56 * 1024 * 1024

PROMPT_TILE = 512
PROMPT_SUBTILE = 256
ROW_TILE = 512
SAMPLE_GROUP = 8


def _gelu(x):
    k1 = -2.0 * math.sqrt(2.0 / math.pi) * math.log2(math.e)
    return x / (1.0 + jnp.exp2(x * (k1 + (k1 * 0.044715) * (x * x))))


def _rms_rows(x, g):
    return x * lax.rsqrt(jnp.mean(x * x, axis=-1, keepdims=True) + EPS) * g


def _lane_iota(shape):
    return lax.broadcasted_iota(jnp.int32, shape, 1)


def _row_iota(shape):
    return lax.broadcasted_iota(jnp.int32, shape, 0)


def _head_ones():
    return ((_row_iota((LANES, LANES)) // HEAD_DIM) == (_lane_iota((LANES, LANES)) // HEAD_DIM)).astype(BF16)


def _norm_rope(x, xp, g_cos, g_sin, ones_bd):
    ss = jnp.dot((x * x).astype(BF16), ones_bd, preferred_element_type=F32)
    return (x * g_cos + xp * g_sin) * lax.rsqrt(ss + HEAD_DIM * EPS)


def _dup_head(t, j):
    lo = _lane_iota(t.shape) < HEAD_DIM
    swapped = pltpu.roll(t, HEAD_DIM, 1)
    return jnp.where(lo, t, swapped) if j == 0 else jnp.where(lo, swapped, t)


def _split_diag(t):
    lo = _lane_iota(t.shape) < HEAD_DIM
    return jnp.where(lo, t, 0.0), jnp.where(lo, 0.0, t)


def _ones_diag(rows):
    r = _row_iota((rows, LANES))
    return (((r % KEY_BLOCK) // CHUNK) == (_lane_iota((rows, LANES)) // HEAD_DIM)).astype(BF16)


def _attend_scores(qs, kb, col_mask):
    s = lax.dot_general(qs, kb, (((1,), (1,)), ((), ())), preferred_element_type=F32)
    return s if col_mask is None else jnp.where(col_mask, s, NEG)


def _attend_weights(s, sink_rows):
    mx = jnp.maximum(jnp.maximum(s[:, :LANES], s[:, LANES:2 * LANES]), s[:, 2 * LANES:])
    lo = _lane_iota(mx.shape) < HEAD_DIM
    m0 = jnp.max(jnp.where(lo, mx, -jnp.inf), axis=1, keepdims=True)
    m1 = jnp.max(jnp.where(lo, -jnp.inf, mx), axis=1, keepdims=True)
    m = jnp.maximum(jnp.where(lo, m0, m1), sink_rows)
    e = jnp.exp(s - jnp.concatenate([m, m, m], axis=1))
    return e.astype(BF16), m


def _attend_output(e, m, vb, sink_rows):
    od = jnp.dot(e, vb, preferred_element_type=F32)
    den = od[:, LANES:] + jnp.exp(sink_rows - m)
    return od[:, :LANES] / den


def _lru_scan(a, b, seg, tmod):
    d = 1
    while d < seg:
        ok = tmod >= d
        b = jnp.where(ok, a * pltpu.roll(b, d, 0), 0.0) + b
        if 2 * d < seg:
            a = jnp.where(ok, a * pltpu.roll(a, d, 0), a)
        d *= 2
    return b


def _lru_scan_tile(a, b, h_prev):
    rows = a.shape[0]
    groups = rows // SUBLANES
    a3 = a.reshape(groups, SUBLANES, LANES)
    b3 = b.reshape(groups, SUBLANES, LANES)
    sub = lax.broadcasted_iota(jnp.int32, a3.shape, 1)
    d = 1
    while d < SUBLANES:
        ok = sub >= d
        b3 = jnp.where(ok, a3 * pltpu.roll(b3, d, 1), 0.0) + b3
        a3 = jnp.where(ok, a3 * pltpu.roll(a3, d, 1), a3)
        d *= 2
    carry = jnp.broadcast_to(h_prev, (SUBLANES, LANES))
    out = []
    for g in range(groups):
        hg = a3[g] * carry + b3[g]
        out.append(hg)
        carry = jnp.broadcast_to(hg[SUBLANES - 1:SUBLANES, :], (SUBLANES, LANES))
    return jnp.concatenate(out, axis=0)


def _lru_branch(xs, gr, h_in, seg, tmod, convw_ref, convb_ref, wg_ref, ba_ref, bx_ref, lam_ref):
    xc, gates = _lru_gates(xs, convw_ref, convb_ref, wg_ref)
    parts = [_lru_piece(k, xc, gates, gr, h_in, seg, tmod, ba_ref, bx_ref, lam_ref) for k in range(D_LRU // LANES)]
    return jnp.concatenate([p[0] for p in parts], axis=1), jnp.concatenate([p[1] for p in parts], axis=1)


def _lru_gates(xs, convw_ref, convb_ref, wg_ref):
    xc = xs[0] * convw_ref[0:1, :] + convb_ref[...]
    for i in range(1, CONV_W):
        xc = xc + xs[i] * convw_ref[i:i + 1, :]
    xcb = xc.astype(BF16)
    half = D_LRU // 2
    return xc, [jnp.dot(xcb[:, hh * half:(hh + 1) * half], wg_ref[hh], preferred_element_type=F32)
                for hh in range(2)]


def _lru_piece(k, xc, gates, gr, h_in, seg, tmod, ba_ref, bx_ref, lam_ref):
    half = D_LRU // 2
    hh, tt = divmod(k, half // LANES)
    sl = slice(k * LANES, (k + 1) * LANES)
    g = gates[hh]
    r = jax.nn.sigmoid(g[:, tt * LANES:(tt + 1) * LANES] + ba_ref[:, sl])
    gi = jax.nn.sigmoid(g[:, half + tt * LANES:half + (tt + 1) * LANES] + bx_ref[:, sl])
    log_a = r * (-LRU_C * jax.nn.softplus(-lam_ref[:, sl]))
    a = jnp.exp(log_a)
    b = jnp.sqrt(-jnp.tanh(log_a) * (a * a + 1.0)) * (gi * xc[:, sl])
    if seg is None:
        h = _lru_scan_tile(a, b, h_in[:, sl])
    else:
        h = _lru_scan(a, b + jnp.where(tmod == 0, a * h_in[:, sl], 0.0), seg, tmod)
    return h * _gelu(gr[:, sl]), h


def _even_prompt_kernel(x_ref, cos_ref, sin_ref, inv_ref, sgn_ref, g_ref, win_ref, qg_ref, kg_ref, sink_ref, convw_ref, convb_ref,
                        wg_ref, ba_ref, bx_ref, lam_ref, wout_ref,
                        y_ref, nk_ref, nv_ref, nh_ref, nc_ref,
                        kb_scr, vb_scr, xr_scr, h_scr, mix_scr):
    t = pl.program_id(1)
    nt = pl.num_programs(1)
    tile = x_ref.shape[1]
    n_chunks = tile // CHUNK
    hist = 2 * KEY_BLOCK
    slot = t % 2

    @pl.when(t == 0)
    def _():
        ones = _ones_diag(vb_scr.shape[2])
        for j in range(N_KV_HEADS):
            kb_scr[0, j, 0:hist, :] = jnp.zeros((hist, LANES), BF16)
            vb_scr[0, j, 0:hist, 0:LANES] = jnp.zeros((hist, LANES), BF16)
            for s in range(2):
                vb_scr[s, j, :, LANES:2 * LANES] = ones
        xr_scr[0:8, :] = jnp.zeros((8, D_LRU), F32)
        h_scr[...] = jnp.zeros(h_scr.shape, F32)

    sub = PROMPT_SUBTILE
    n_sub = tile // sub
    sub_chunks = sub // CHUNK
    o3 = Q_W + 2 * KV_W
    xs_in, proj = [], []
    for u in range(n_sub):
        x = x_ref[0, u * sub:(u + 1) * sub, :]
        xn = _rms_rows(x, g_ref[...]).astype(BF16)
        xs_in.append(x)
        o5 = o3 + 2 * D_LRU
        proj.append((jnp.dot(xn, win_ref[:, 0:Q_W], preferred_element_type=F32),
                     jnp.dot(xn, win_ref[:, Q_W:o3], preferred_element_type=F32),
                     jnp.dot(xn, win_ref[:, o3:o3 + D_LRU], preferred_element_type=F32),
                     jnp.dot(xn, win_ref[:, o3 + D_LRU:o5], preferred_element_type=F32),
                     jnp.dot(xn, win_ref[:, o5:o5 + Q_W], preferred_element_type=F32),
                     jnp.dot(xn, win_ref[:, o5 + Q_W:o5 + Q_W + KV_W], preferred_element_type=F32)))
    for u in range(n_sub):
        xr_scr[8 + u * sub:8 + (u + 1) * sub, :] = proj[u][2]

    base = (t * tile).astype(F32) * inv_ref[...]
    cos_b = jnp.cos(base)
    sin_b = jnp.sin(base) * sgn_ref[...]

    band = BAND_BLOCKS * KEY_BLOCK
    col = _lane_iota((2 * CHUNK, band))
    sink_rows = [jnp.concatenate(
        [jnp.broadcast_to(sink_ref[2 * j:2 * j + 1, :], (CHUNK, LANES)),
         jnp.broadcast_to(sink_ref[2 * j + 1:2 * j + 2, :], (CHUNK, LANES))], axis=0) for j in range(N_KV_HEADS)]
    ones_bd = _head_ones()
    norm_c = math.sqrt(HEAD_DIM)
    q_all = []
    for u in range(n_sub):
        hq, hkv, xr, gr, hqp, hkp = proj[u]
        cos_l = cos_ref[u * sub:(u + 1) * sub, :]
        sin_l = sin_ref[u * sub:(u + 1) * sub, :]
        cos_t = cos_b * cos_l - sin_b * sin_l
        sin_t = sin_b * cos_l + cos_b * sin_l
        q_cos = cos_t * (qg_ref[0:1, :] * (norm_c * HEAD_DIM ** -0.5))
        q_sin = sin_t * (qg_ref[1:2, :] * (norm_c * HEAD_DIM ** -0.5))
        q_tiles = [_norm_rope(hq[:, n * LANES:(n + 1) * LANES], hqp[:, n * LANES:(n + 1) * LANES],
                              q_cos, q_sin, ones_bd).astype(BF16) for n in range(Q_W // LANES)]
        kr = _norm_rope(hkv[:, 0:KV_W], hkp, cos_t * (kg_ref[0:1, :] * norm_c), sin_t * (kg_ref[1:2, :] * norm_c),
                        ones_bd)
        vv = hkv[:, KV_W:2 * KV_W]
        for j in range(N_KV_HEADS):
            ktop, kbot = _split_diag(_dup_head(kr, j))
            vtop, vbot = _split_diag(_dup_head(vv, j))
            for i in range(sub_chunks):
                r0 = hist + (u * sub_chunks + i) * KEY_BLOCK
                rows = slice(i * CHUNK, (i + 1) * CHUNK)
                kb_scr[slot, j, r0:r0 + CHUNK, :] = ktop[rows].astype(BF16)
                kb_scr[slot, j, r0 + CHUNK:r0 + KEY_BLOCK, :] = kbot[rows].astype(BF16)
                vb_scr[slot, j, r0:r0 + CHUNK, 0:LANES] = vtop[rows].astype(BF16)
                vb_scr[slot, j, r0 + CHUNK:r0 + KEY_BLOCK, 0:LANES] = vbot[rows].astype(BF16)
        q_all.append(q_tiles)

    pairs = [(i, j) for i in range(sub_chunks) for j in range(N_KV_HEADS)]
    scores_all, gates_all = [], []
    for u in range(n_sub):
        q_tiles = q_all[u]
        scores = []
        for i, j in pairs:
            rows = slice(i * CHUNK, (i + 1) * CHUNK)
            ci = u * sub_chunks + i
            col_mask = None
            if ci < BAND_BLOCKS - 1:
                col_mask = col >= (1 - jnp.minimum(t, 1)) * ((BAND_BLOCKS - 1 - ci) * KEY_BLOCK)
            qs = jnp.concatenate([q_tiles[2 * j][rows], q_tiles[2 * j + 1][rows]], axis=0)
            scores.append(_attend_scores(qs, kb_scr[slot, j, ci * KEY_BLOCK:ci * KEY_BLOCK + band, :], col_mask))
        scores_all.append(scores)
        first = 8 + u * sub
        taps = [xr_scr[first - (CONV_W - 1 - i):first - (CONV_W - 1 - i) + sub, :] for i in range(CONV_W - 1)]
        gates_all.append(_lru_gates(taps + [proj[u][2]], convw_ref, convb_ref, wg_ref))

    h_carry = h_scr[0:1, :]
    for u in range(n_sub):
        xr, gr = proj[u][2], proj[u][3]
        scores = scores_all[u]
        xc, gates = gates_all[u]
        n_pieces = D_LRU // LANES
        per_piece = len(pairs) // n_pieces
        weights, lru = [], []
        for k in range(n_pieces):
            weights += [_attend_weights(scores[p], sink_rows[pairs[p][1]])
                        for p in range(k * per_piece, (k + 1) * per_piece)]
            lru.append(_lru_piece(k, xc, gates, gr, h_carry, None, None, ba_ref, bx_ref, lam_ref))
        for (e, m), (i, j) in zip(weights, pairs):
            ci = u * sub_chunks + i
            rows = slice(u * sub + i * CHUNK, u * sub + (i + 1) * CHUNK)
            o = _attend_output(e, m, vb_scr[slot, j, ci * KEY_BLOCK:ci * KEY_BLOCK + band, :], sink_rows[j])
            c0 = 2 * j * LANES
            mix_scr[rows, c0:c0 + LANES] = o[0:CHUNK].astype(BF16)
            mix_scr[rows, c0 + LANES:c0 + 2 * LANES] = o[CHUNK:2 * CHUNK].astype(BF16)

        h_carry = jnp.concatenate([h[sub - 1:sub, :] for _, h in lru], axis=1)
        rows = slice(u * sub, (u + 1) * sub)
        for k, (rec, _) in enumerate(lru):
            mix_scr[rows, Q_W + k * LANES:Q_W + (k + 1) * LANES] = rec.astype(BF16)

        y_att = jnp.dot(mix_scr[rows, 0:Q_W], wout_ref[0:Q_W, :], preferred_element_type=F32)
        y_rec = jnp.dot(mix_scr[rows, Q_W:Q_W + D_LRU], wout_ref[Q_W:Q_W + D_LRU, :], preferred_element_type=F32)
        y_ref[0, rows, :] = xs_in[u] + y_att + y_rec

    last = n_chunks * KEY_BLOCK
    for j in range(N_KV_HEADS):
        kb_scr[1 - slot, j, 0:hist, :] = kb_scr[slot, j, last:last + hist, :]
        vb_scr[1 - slot, j, 0:hist, 0:LANES] = vb_scr[slot, j, last:last + hist, 0:LANES]
    h_scr[0:1, :] = h_carry
    xr_scr[0:8, :] = xr[sub - 8:sub, :]

    @pl.when(t == nt - 1)
    def _():
        nk_ref[0] = kr[sub - SWA_ROWS:sub, :]
        nv_ref[0] = vv[sub - SWA_ROWS:sub, :]
        nh_ref[0] = h_carry
        nc_ref[0] = xr[sub - (CONV_W - 1):sub, :]


def _const_spec(shape):
    zeros = (0,) * len(shape)
    return pl.BlockSpec(shape, lambda *_: zeros, pipeline_mode=pl.Buffered(1))


def _even_prompt(x, p):
    bsz, s_len, _ = x.shape
    tile = PROMPT_TILE
    cos_l, sin_l = _rope_tables(jnp.arange(tile))
    inv_row, sgn_row = _rope_rows()
    nt = s_len // tile
    key_rows = (tile // CHUNK + 2) * KEY_BLOCK
    consts = [p['g'], p['win'], p['qg'], p['kg'], p['sink'], p['convw'], p['convb'], p['wg'], p['ba'], p['bx'],
              p['lam'], p['wout']]
    out_shape = [
        jax.ShapeDtypeStruct((bsz, s_len, D_MODEL), F32),
        jax.ShapeDtypeStruct((bsz, SWA_ROWS, KV_W), F32),
        jax.ShapeDtypeStruct((bsz, SWA_ROWS, KV_W), F32),
        jax.ShapeDtypeStruct((bsz, 1, D_LRU), F32),
        jax.ShapeDtypeStruct((bsz, CONV_W - 1, D_LRU), F32),
    ]
    return pl.pallas_call(
        _even_prompt_kernel,
        out_shape=out_shape,
        grid=(bsz, nt),
        in_specs=[pl.BlockSpec((1, tile, D_MODEL), lambda b, t: (b, t, 0)),
                  _const_spec(cos_l.shape), _const_spec(sin_l.shape), _const_spec(inv_row.shape),
                  _const_spec(sgn_row.shape)] + [_const_spec(c.shape) for c in consts],
        out_specs=[pl.BlockSpec((1, tile, D_MODEL), lambda b, t: (b, t, 0)),
                   pl.BlockSpec((1, SWA_ROWS, KV_W), lambda b, t: (b, 0, 0)),
                   pl.BlockSpec((1, SWA_ROWS, KV_W), lambda b, t: (b, 0, 0)),
                   pl.BlockSpec((1, 1, D_LRU), lambda b, t: (b, 0, 0)),
                   pl.BlockSpec((1, CONV_W - 1, D_LRU), lambda b, t: (b, 0, 0))],
        scratch_shapes=[pltpu.VMEM((2, N_KV_HEADS, key_rows, LANES), BF16),
                        pltpu.VMEM((2, N_KV_HEADS, key_rows, 2 * LANES), BF16),
                        pltpu.VMEM((tile + 8, D_LRU), F32),
                        pltpu.VMEM((8, D_LRU), F32),
                        pltpu.VMEM((tile, Q_W + D_LRU), BF16)],
        compiler_params=pltpu.CompilerParams(dimension_semantics=("arbitrary", "arbitrary"),
                                             vmem_limit_bytes=VMEM_LIMIT),
        name="even_mixer_prompt",
    )(x, cos_l, sin_l, inv_row, sgn_row, *consts)


def _even_sample_kernel(x_ref, cos_ref, sin_ref, kc_ref, vc_ref, h0_ref, xprev_ref,
                        g_ref, win_ref, qg_ref, kg_ref, sink_ref, convw_ref, convb_ref,
                        wg_ref, ba_ref, bx_ref, lam_ref, wout_ref,
                        y_ref, nk_ref, nv_ref, h_ref, xr_ref,
                        q_scr, k_scr, v_scr, mix_scr):
    rows_all = x_ref.shape[0]
    nseq = kc_ref.shape[0]
    t_len = rows_all // nseq
    x = x_ref[...]
    xn = _rms_rows(x, g_ref[...]).astype(BF16)
    cos_t = cos_ref[...]
    sin_t = sin_ref[...]

    o3 = Q_W + 2 * KV_W
    o5 = o3 + 2 * D_LRU
    hq = jnp.dot(xn, win_ref[:, 0:Q_W], preferred_element_type=F32)
    hkv = jnp.dot(xn, win_ref[:, Q_W:o3], preferred_element_type=F32)
    xr = jnp.dot(xn, win_ref[:, o3:o3 + D_LRU], preferred_element_type=F32)
    gr = jnp.dot(xn, win_ref[:, o3 + D_LRU:o5], preferred_element_type=F32)
    hqp = jnp.dot(xn, win_ref[:, o5:o5 + Q_W], preferred_element_type=F32)
    hkp = jnp.dot(xn, win_ref[:, o5 + Q_W:o5 + Q_W + KV_W], preferred_element_type=F32)
    ones_bd = _head_ones()
    norm_c = math.sqrt(HEAD_DIM)
    q_cos = cos_t * (qg_ref[0:1, :] * (norm_c * HEAD_DIM ** -0.5))
    q_sin = sin_t * (qg_ref[1:2, :] * (norm_c * HEAD_DIM ** -0.5))
    for n in range(Q_W // LANES):
        lanes = slice(n * LANES, (n + 1) * LANES)
        q_scr[:, lanes] = _norm_rope(hq[:, lanes], hqp[:, lanes], q_cos, q_sin, ones_bd)
    k_scr[...] = _norm_rope(hkv[:, 0:KV_W], hkp, cos_t * (kg_ref[0:1, :] * norm_c),
                            sin_t * (kg_ref[1:2, :] * norm_c), ones_bd)
    v_scr[...] = hkv[:, KV_W:2 * KV_W]

    band = BAND_BLOCKS * KEY_BLOCK
    col = _lane_iota((2 * t_len, band))
    col_mask = jnp.logical_or(col < 2 * KEY_BLOCK, (col % CHUNK) < t_len)
    ones_band = _ones_diag(band)
    pad = jnp.zeros((CHUNK - t_len, LANES), F32)

    sink_rows = [jnp.concatenate(
        [jnp.broadcast_to(sink_ref[2 * j:2 * j + 1, :], (t_len, LANES)),
         jnp.broadcast_to(sink_ref[2 * j + 1:2 * j + 2, :], (t_len, LANES))], axis=0) for j in range(N_KV_HEADS)]

    def group_body(gi, carry):
        work = []
        for s in range(SAMPLE_GROUP):
            b = gi * SAMPLE_GROUP + s
            r0 = pl.multiple_of(b * t_len, t_len)
            kc = kc_ref[b]
            vc = vc_ref[b]
            knew = k_scr[pl.ds(r0, t_len), :]
            vnew = v_scr[pl.ds(r0, t_len), :]
            nk_ref[b, 0:SWA_ROWS - t_len, :] = kc[t_len:SWA_ROWS]
            nk_ref[b, SWA_ROWS - t_len:SWA_ROWS, :] = knew
            nv_ref[b, 0:SWA_ROWS - t_len, :] = vc[t_len:SWA_ROWS]
            nv_ref[b, SWA_ROWS - t_len:SWA_ROWS, :] = vnew
            kfull = jnp.concatenate([kc, knew, pad], axis=0)
            vfull = jnp.concatenate([vc, vnew, pad], axis=0)
            for j in range(N_KV_HEADS):
                ktop, kbot = _split_diag(_dup_head(kfull, j))
                vtop, vbot = _split_diag(_dup_head(vfull, j))
                kparts, vparts = [], []
                for c in range(BAND_BLOCKS):
                    rs = slice(c * CHUNK, (c + 1) * CHUNK)
                    kparts += [ktop[rs], kbot[rs]]
                    vparts += [vtop[rs], vbot[rs]]
                kb = jnp.concatenate(kparts, axis=0).astype(BF16)
                vb = jnp.concatenate([jnp.concatenate(vparts, axis=0).astype(BF16), ones_band], axis=1)
                c0 = 2 * j * LANES
                qs = jnp.concatenate([q_scr[pl.ds(r0, t_len), c0:c0 + LANES],
                                      q_scr[pl.ds(r0, t_len), c0 + LANES:c0 + 2 * LANES]], axis=0).astype(BF16)
                work.append((r0, j, vb, _attend_scores(qs, kb, col_mask)))
        weights = [_attend_weights(s, sink_rows[j]) for _, j, _, s in work]
        for (r0, j, vb, _), (e, m) in zip(work, weights):
            o = _attend_output(e, m, vb, sink_rows[j])
            c0 = 2 * j * LANES
            mix_scr[pl.ds(r0, t_len), c0:c0 + LANES] = o[0:t_len]
            mix_scr[pl.ds(r0, t_len), c0 + LANES:c0 + 2 * LANES] = o[t_len:2 * t_len]
        return carry

    lax.fori_loop(0, nseq // SAMPLE_GROUP, group_body, 0)

    tmod =_row_iota((rows_all, LANES)) % t_len
    tmod_w = _row_iota((rows_all, D_LRU)) % t_len
    xprev = xprev_ref[...]
    xs = []
    for i in range(CONV_W - 1):
        k = CONV_W - 1 - i
        own = pltpu.roll(xr, k, 0)
        cached = xprev if k == CONV_W - 1 else pltpu.roll(xprev, rows_all - (CONV_W - 1 - k), 0)
        xs.append(jnp.where(tmod_w >= k, own, cached))
    xs.append(xr)
    rec, h = _lru_branch(xs, gr, h0_ref[...], t_len, tmod, convw_ref, convb_ref, wg_ref, ba_ref, bx_ref, lam_ref)
    mix_scr[:, Q_W:Q_W + D_LRU] = rec
    xr_ref[...] = xr
    h_ref[...] = h
    y_ref[...] = x + jnp.dot(mix_scr[...].astype(BF16), wout_ref[...], preferred_element_type=F32)


def _even_sample(x2, cos_t, sin_t, kc, vc, h0_rows, xprev, p):
    rows_all = x2.shape[0]
    nseq = kc.shape[0]
    consts = [p['g'], p['win'], p['qg'], p['kg'], p['sink'], p['convw'], p['convb'], p['wg'], p['ba'], p['bx'],
              p['lam'], p['wout']]
    ins = [x2, cos_t, sin_t, kc, vc, h0_rows, xprev] + consts
    out_shape = [
        jax.ShapeDtypeStruct((rows_all, D_MODEL), F32),
        jax.ShapeDtypeStruct((nseq, SWA_ROWS, KV_W), F32),
        jax.ShapeDtypeStruct((nseq, SWA_ROWS, KV_W), F32),
        jax.ShapeDtypeStruct((rows_all, D_LRU), F32),
        jax.ShapeDtypeStruct((rows_all, D_LRU), F32),
    ]
    return pl.pallas_call(
        _even_sample_kernel,
        out_shape=out_shape,
        grid=(1,),
        in_specs=[_const_spec(a.shape) for a in ins],
        out_specs=[pl.BlockSpec(s.shape, lambda i, n=len(s.shape): (0,) * n) for s in out_shape],
        scratch_shapes=[pltpu.VMEM((rows_all, Q_W), F32),
                        pltpu.VMEM((rows_all, KV_W), F32),
                        pltpu.VMEM((rows_all, KV_W), F32),
                        pltpu.VMEM((rows_all, Q_W + D_LRU), F32)],
        compiler_params=pltpu.CompilerParams(dimension_semantics=("arbitrary",), vmem_limit_bytes=VMEM_LIMIT),
        name="even_mixer_sample",
    )(*ins)


def _ffn_kernel(xp_ref, xs_ref, g_ref, w1_ref, w2_ref, yp_ref, ys_ref):
    i = pl.program_id(0)
    x = jnp.where(i == 0, xs_ref[...], xp_ref[...])
    xn = _rms_rows(x, g_ref[...]).astype(BF16)
    acc = x
    step = D_MODEL
    for c in range(D_FF // step):
        hcol = jnp.dot(xn, w1_ref[:, c * step:(c + 1) * step].astype(BF16), preferred_element_type=F32)
        hcol = jnp.square(jnp.maximum(hcol, 0.0)).astype(BF16)
        acc = acc + jnp.dot(hcol, w2_ref[c * step:(c + 1) * step, :].astype(BF16), preferred_element_type=F32)

    yp_ref[...] = acc

    @pl.when(i == 0)
    def _():
        ys_ref[...] = yp_ref[...]


def _sample_then_tiles_specs(tile, width):
    prompt = pl.BlockSpec((tile, width), lambda i: (jnp.maximum(i - 1, 0), 0))
    sample = pl.BlockSpec((tile, width), lambda i: (0, 0))
    return prompt, sample


def _ffn(xp, xs, g, w1, w2, layer):
    tile = xs.shape[0]
    n_tiles = xp.shape[0] // tile
    xp_spec, xs_spec = _sample_then_tiles_specs(tile, D_MODEL)
    layer_spec = lambda shape: pl.BlockSpec((None,) + shape[1:], lambda i: (layer, 0, 0),
                                            pipeline_mode=pl.Buffered(1))
    return pl.pallas_call(
        _ffn_kernel,
        out_shape=[jax.ShapeDtypeStruct(xp.shape, F32), jax.ShapeDtypeStruct(xs.shape, F32)],
        grid=(n_tiles + 1,),
        in_specs=[xp_spec, xs_spec, _const_spec(g.shape), layer_spec(w1.shape), layer_spec(w2.shape)],
        out_specs=list(_sample_then_tiles_specs(tile, D_MODEL)),
        compiler_params=pltpu.CompilerParams(dimension_semantics=("arbitrary",), vmem_limit_bytes=VMEM_LIMIT),
        name="channel_mlp",
    )(xp, xs, g, w1, w2)


def _gmlp_kernel(seg_s, xp_ref, xs_ref, g_ref, win_ref, vg_ref, wsp_ref, wss_ref, biasp_ref, biass_ref, wout_ref,
                 yp_ref, ys_ref, v_ref):
    i = pl.program_id(0)
    is_s = i == 0
    tile = xp_ref.shape[0]
    x = jnp.where(is_s, xs_ref[...], xp_ref[...])
    xn = _rms_rows(x, g_ref[...]).astype(BF16)
    zv = jnp.dot(xn, win_ref[:, D_C:2 * D_C], preferred_element_type=F32)
    u_all = jnp.dot(xn, win_ref[:, 0:D_C], preferred_element_type=F32)
    zv = _gelu(zv)
    zc = zv - jnp.mean(zv, axis=-1, keepdims=True)
    v_all = zc * lax.rsqrt(jnp.mean(zc * zc, axis=-1, keepdims=True) + EPS) * vg_ref[...]
    r = _row_iota((CHUNK_MLP, CHUNK_MLP))
    col = _lane_iota((CHUNK_MLP, CHUNK_MLP))
    keep_p = col <= r
    keep_s = jnp.logical_and((r // seg_s) == (col // seg_s), keep_p)
    vb = v_all.astype(BF16)
    sv = []
    for gi in range(C_GROUPS):
        w = jnp.where(is_s, jnp.where(keep_s, wss_ref[gi], 0.0), jnp.where(keep_p, wsp_ref[gi], 0.0)).astype(BF16)
        lanes = slice(gi * LANES, (gi + 1) * LANES)
        bias = jnp.where(is_s, biass_ref[:, lanes], biasp_ref[:, lanes])
        blocks = [jnp.dot(w, vb[ch * CHUNK_MLP:(ch + 1) * CHUNK_MLP, lanes], preferred_element_type=F32) + bias
                  for ch in range(tile // CHUNK_MLP)]
        sv.append(jnp.concatenate(blocks, axis=0))
    acc = x
    step = MXU_WIDTH
    per_step = step // LANES
    for c in range(D_C // step):
        cols = slice(c * step, (c + 1) * step)
        gate = (_gelu(u_all[:, cols]) * jnp.concatenate(sv[c * per_step:(c + 1) * per_step], axis=1)).astype(BF16)
        acc = acc + jnp.dot(gate, wout_ref[cols, :], preferred_element_type=F32)

    yp_ref[...] = acc

    @pl.when(is_s)
    def _():
        ys_ref[...] = yp_ref[...]
        v_ref[...] = v_all


def _gmlp(xp, xs, seg_s, g, win, vg, ws_p, ws_s, bias_p, bias_s, wout):
    tile = xs.shape[0]
    n_tiles = xp.shape[0] // tile
    xp_spec, xs_spec = _sample_then_tiles_specs(tile, D_MODEL)
    consts = (g, win, vg, ws_p, ws_s, bias_p, bias_s, wout)
    return pl.pallas_call(
        functools.partial(_gmlp_kernel, seg_s),
        out_shape=[jax.ShapeDtypeStruct(xp.shape, F32), jax.ShapeDtypeStruct(xs.shape, F32),
                   jax.ShapeDtypeStruct((tile, D_C), F32)],
        grid=(n_tiles + 1,),
        in_specs=[xp_spec, xs_spec] + [_const_spec(a.shape) for a in consts],
        out_specs=list(_sample_then_tiles_specs(tile, D_MODEL)) + [pl.BlockSpec((tile, D_C), lambda i: (0, 0))],
        compiler_params=pltpu.CompilerParams(dimension_semantics=("arbitrary",), vmem_limit_bytes=VMEM_LIMIT),
        name="gmlp_mixer",
    )(xp, xs, *consts)


def _rope_inv():
    half = HEAD_DIM // 2
    return ROPE_THETA ** (-jnp.arange(half, dtype=F32) / half)


def _rope_rows():
    inv = _rope_inv()
    sgn = jnp.ones((HEAD_DIM // 2,), F32)
    return jnp.concatenate([inv] * 4)[None, :], jnp.concatenate([-sgn, sgn, -sgn, sgn])[None, :]


def _rope_tables(pos):
    inv = _rope_inv()
    ang = pos.astype(F32)[:, None] * inv[None, :]
    cos = jnp.cos(ang)
    sin = jnp.sin(ang)
    return jnp.concatenate([cos] * 4, axis=1), jnp.concatenate([-sin, sin, -sin, sin], axis=1)


def _swap_head_halves(a):
    half = HEAD_DIM // 2
    return a.reshape(a.shape[:-1] + (-1, 2, half))[..., ::-1, :].reshape(a.shape)


def _gate_weights(wa, wx):
    nb = LRU_BLOCKS // 2
    eye = jnp.eye(nb, dtype=bool)

    def half_diag(w, hh):
        blocks = w[hh * nb:(hh + 1) * nb]
        return jnp.where(eye[:, None, :, None], blocks[:, :, None, :], 0.0).reshape(nb * LRU_BW, nb * LRU_BW)
    return jnp.stack([jnp.concatenate([half_diag(wa, hh), half_diag(wx, hh)], axis=1)
                      for hh in range(2)]).astype(BF16)


def kernel(x_prompt, x_sample, cache_swa_k, cache_swa_v, state_lru_h, state_lru_conv, e_norm_g, e_w_in, e_q_norm_g, e_k_norm_g, e_sinks, e_conv_w, e_conv_b, e_gate_a_w, e_gate_a_b, e_gate_x_w, e_gate_x_b, e_lru_lambda, e_w_out, o_norm_g, o_w_in, o_v_norm_g, o_spatial_w, o_spatial_b, o_w_out, ffn_norm_g, ffn_w1, ffn_w2):
    bsz, s_len, _ = x_prompt.shape
    nseq, t_len, _ = x_sample.shape
    past_len = PAST_LEN
    row = lambda a: a.reshape(1, -1)

    e = 0
    p = {
        'g': row(e_norm_g[e]),
        'win': jnp.concatenate([e_w_in[e], _swap_head_halves(e_w_in[e][:, :Q_W + KV_W])], axis=1).astype(BF16),
        'qg': jnp.tile(jnp.stack([e_q_norm_g[e], _swap_head_halves(e_q_norm_g[e])]), (1, LANES // HEAD_DIM)),
        'kg': jnp.tile(jnp.stack([e_k_norm_g[e], _swap_head_halves(e_k_norm_g[e])]), (1, LANES // HEAD_DIM)),
        'sink': jnp.repeat(e_sinks[e], HEAD_DIM).reshape(N_Q_HEADS // 2, LANES),
        'convw': e_conv_w[e],
        'convb': row(e_conv_b[e]),
        'wg': _gate_weights(e_gate_a_w[e], e_gate_x_w[e]),
        'ba': row(e_gate_a_b[e]),
        'bx': row(e_gate_x_b[e]),
        'lam': row(e_lru_lambda[e]),
        'wout': e_w_out[e].astype(BF16),
    }
    yp, nk_p, nv_p, nh_p, nc_p = _even_prompt(x_prompt, p)

    cos_s, sin_s = _rope_tables(past_len + jnp.arange(t_len))
    cos_s = jnp.tile(cos_s, (nseq, 1))
    sin_s = jnp.tile(sin_s, (nseq, 1))
    kc = cache_swa_k[e].reshape(nseq, SWA_ROWS, KV_W)
    vc = cache_swa_v[e].reshape(nseq, SWA_ROWS, KV_W)
    h0_rows = jnp.repeat(state_lru_h[e], t_len, axis=0)
    xprev = jnp.pad(state_lru_conv[e], ((0, 0), (0, t_len - (CONV_W - 1)), (0, 0))).reshape(nseq * t_len, D_LRU)
    ys, nk_s, nv_s, h_s, xr_s = _even_sample(x_sample.reshape(nseq * t_len, D_MODEL), cos_s, sin_s, kc, vc,
                                             h0_rows, xprev, p)
    nh_s = h_s.reshape(nseq, t_len, D_LRU)[:, -1]
    nc_s = xr_s.reshape(nseq, t_len, D_LRU)[:, t_len - (CONV_W - 1):]

    w1, w2 = ffn_w1, ffn_w2
    yp, ys = _ffn(yp.reshape(bsz * s_len, D_MODEL), ys, row(ffn_norm_g[0]), w1, w2, 0)

    o = 0
    bias_p = jnp.repeat(o_spatial_b[o].T, LANES, axis=1)
    reps = CHUNK_MLP // t_len
    ws_s = jnp.tile(o_spatial_w[o][:, :t_len, :t_len], (1, reps, reps))
    bias_s = jnp.tile(jnp.repeat(o_spatial_b[o][:, :t_len].T, LANES, axis=1), (reps, 1))
    yp, ys, gv_s = _gmlp(yp, ys, t_len, row(o_norm_g[o]), o_w_in[o].astype(BF16), row(o_v_norm_g[o]),
                         o_spatial_w[o], ws_s, bias_p, bias_s, o_w_out[o].astype(BF16))

    yp, ys = _ffn(yp, ys, row(ffn_norm_g[1]), w1, w2, 1)

    kv_shape = (1, -1, SWA_ROWS, N_KV_HEADS, HEAD_DIM)
    return (yp.reshape(bsz, s_len, D_MODEL), ys.reshape(nseq, t_len, D_MODEL),
            nk_p.reshape(kv_shape), nv_p.reshape(kv_shape), nh_p.reshape(1, bsz, D_LRU), nc_p[None],
            nk_s.reshape(kv_shape), nv_s.reshape(kv_shape), nh_s[None], nc_s[None],
            gv_s.reshape(1, nseq, t_len, D_C))
```

```python
import functools
import math

import jax
import jax.numpy as jnp
from jax import lax
from jax.experimental import pallas as pl
from jax.experimental.pallas import tpu as pltpu

F32 = jnp.float32
BF16 = jnp.bfloat16

D_MODEL = 1024
CHUNK = 64
HEAD_DIM = 64
N_Q_HEADS = 8
N_KV_HEADS = 2
SWA_ROWS = 128
PAST_LEN = 4096
ROPE_THETA = 10000.0
NEG = -1e30
D_LRU = 512
LRU_BLOCKS = 8
LRU_BW = D_LRU // LRU_BLOCKS
CONV_W = 4
LRU_C = 8.0
Q_W = N_Q_HEADS * HEAD_DIM
KV_W = N_KV_HEADS * HEAD_DIM
CHUNK_MLP = 128
D_C = D_MODEL
C_GROUPS = 8
D_FF = 4 * D_MODEL
EPS = 1e-6

LANES = 128
SUBLANES = 8
MXU_WIDTH = 256
KEY_BLOCK = 2 * CHUNK
BAND_BLOCKS = 3
VMEM_LIMIT = 56 * 1024 * 1024

PROMPT_TILE = 512
PROMPT_SUBTILE = 256
ROW_TILE = 512
GMLP_SUBTILE = 256
SAMPLE_GROUP = 8


def _gelu(x):
    k1 = -2.0 * math.sqrt(2.0 / math.pi) * math.log2(math.e)
    return x / (1.0 + jnp.exp2(x * (k1 + (k1 * 0.044715) * (x * x))))


def _rms_rows(x, g):
    return x * lax.rsqrt(jnp.mean(x * x, axis=-1, keepdims=True) + EPS) * g


def _lane_iota(shape):
    return lax.broadcasted_iota(jnp.int32, shape, 1)


def _row_iota(shape):
    return lax.broadcasted_iota(jnp.int32, shape, 0)


def _head_ones():
    return ((_row_iota((LANES, LANES)) // HEAD_DIM) == (_lane_iota((LANES, LANES)) // HEAD_DIM)).astype(BF16)


def _norm_rope(x, xp, g_cos, g_sin, ones_bd):
    ss = jnp.dot((x * x).astype(BF16), ones_bd, preferred_element_type=F32)
    return (x * g_cos + xp * g_sin) * lax.rsqrt(ss + HEAD_DIM * EPS)


def _dup_head(t, j):
    lo = _lane_iota(t.shape) < HEAD_DIM
    swapped = pltpu.roll(t, HEAD_DIM, 1)
    return jnp.where(lo, t, swapped) if j == 0 else jnp.where(lo, swapped, t)


def _split_diag(t):
    lo = _lane_iota(t.shape) < HEAD_DIM
    return jnp.where(lo, t, 0.0), jnp.where(lo, 0.0, t)


def _ones_diag(rows):
    r = _row_iota((rows, LANES))
    return (((r % KEY_BLOCK) // CHUNK) == (_lane_iota((rows, LANES)) // HEAD_DIM)).astype(BF16)


def _attend_scores(qs, kb, col_mask):
    s = lax.dot_general(qs, kb, (((1,), (1,)), ((), ())), preferred_element_type=F32)
    return s if col_mask is None else jnp.where(col_mask, s, NEG)


def _attend_weights(s, sink_rows):
    mx = jnp.maximum(jnp.maximum(s[:, :LANES], s[:, LANES:2 * LANES]), s[:, 2 * LANES:])
    lo = _lane_iota(mx.shape) < HEAD_DIM
    m0 = jnp.max(jnp.where(lo, mx, -jnp.inf), axis=1, keepdims=True)
    m1 = jnp.max(jnp.where(lo, -jnp.inf, mx), axis=1, keepdims=True)
    m = jnp.maximum(jnp.where(lo, m0, m1), sink_rows)
    e = jnp.exp(s - jnp.concatenate([m, m, m], axis=1))
    return e.astype(BF16), m


def _attend_output(e, m, vb, sink_rows):
    return _attend_normalise(jnp.dot(e, vb, preferred_element_type=F32), m, sink_rows)


def _attend_normalise(od, m, sink_rows):
    den = od[:, LANES:] + jnp.exp(sink_rows - m)
    return od[:, :LANES] / den


def _lru_scan(a, b, seg, tmod):
    d = 1
    while d < seg:
        ok = tmod >= d
        b = jnp.where(ok, a * pltpu.roll(b, d, 0), 0.0) + b
        if 2 * d < seg:
            a = jnp.where(ok, a * pltpu.roll(a, d, 0), a)
        d *= 2
    return b


def _lru_scan_tile(a, b, h_prev):
    rows = a.shape[0]
    groups = rows // SUBLANES
    a3 = a.reshape(groups, SUBLANES, LANES)
    b3 = b.reshape(groups, SUBLANES, LANES)
    sub = lax.broadcasted_iota(jnp.int32, a3.shape, 1)
    d = 1
    while d < SUBLANES:
        ok = sub >= d
        b3 = jnp.where(ok, a3 * pltpu.roll(b3, d, 1), 0.0) + b3
        a3 = jnp.where(ok, a3 * pltpu.roll(a3, d, 1), a3)
        d *= 2
    carry = jnp.broadcast_to(h_prev, (SUBLANES, LANES))
    out = []
    for g in range(groups):
        hg = a3[g] * carry + b3[g]
        out.append(hg)
        carry = jnp.broadcast_to(hg[SUBLANES - 1:SUBLANES, :], (SUBLANES, LANES))
    return jnp.concatenate(out, axis=0)


def _lru_branch(xs, gr, h_in, seg, tmod, convw_ref, convb_ref, wg_ref, ba_ref, bx_ref, lam_ref):
    xc, gates = _lru_gates(xs, convw_ref, convb_ref, wg_ref)
    parts = [_lru_piece(k, xc, gates, gr, h_in, seg, tmod, ba_ref, bx_ref, lam_ref) for k in range(D_LRU // LANES)]
    return jnp.concatenate([p[0] for p in parts], axis=1), jnp.concatenate([p[1] for p in parts], axis=1)


def _lru_gates(xs, convw_ref, convb_ref, wg_ref):
    xc = xs[0] * convw_ref[0:1, :] + convb_ref[...]
    for i in range(1, CONV_W):
        xc = xc + xs[i] * convw_ref[i:i + 1, :]
    xcb = xc.astype(BF16)
    half = D_LRU // 2
    return xc, [jnp.dot(xcb[:, hh * half:(hh + 1) * half], wg_ref[hh], preferred_element_type=F32)
                for hh in range(2)]


def _lru_piece(k, xc, gates, gr, h_in, seg, tmod, ba_ref, bx_ref, lam_ref):
    half = D_LRU // 2
    hh, tt = divmod(k, half // LANES)
    sl = slice(k * LANES, (k + 1) * LANES)
    g = gates[hh]
    r = jax.nn.sigmoid(g[:, tt * LANES:(tt + 1) * LANES] + ba_ref[:, sl])
    gi = jax.nn.sigmoid(g[:, half + tt * LANES:half + (tt + 1) * LANES] + bx_ref[:, sl])
    log_a = r * (-LRU_C * jax.nn.softplus(-lam_ref[:, sl]))
    a = jnp.exp(log_a)
    b = jnp.sqrt(-jnp.tanh(log_a) * (a * a + 1.0)) * (gi * xc[:, sl])
    if seg is None:
        h = _lru_scan_tile(a, b, h_in[:, sl])
    else:
        h = _lru_scan(a, b + jnp.where(tmod == 0, a * h_in[:, sl], 0.0), seg, tmod)
    return h * _gelu(gr[:, sl]), h


def _even_prompt_kernel(x_ref, cos_ref, sin_ref, inv_ref, sgn_ref, g_ref, win_ref, qg_ref, kg_ref, sink_ref, convw_ref, convb_ref,
                        wg_ref, ba_ref, bx_ref, lam_ref, wout_ref,
                        y_ref, nk_ref, nv_ref, nh_ref, nc_ref,
                        kb_scr, vb_scr, xr_scr, h_scr, mix_scr):
    t = pl.program_id(1)
    nt = pl.num_programs(1)
    tile = x_ref.shape[1]
    n_chunks = tile // CHUNK
    hist = 2 * KEY_BLOCK
    slot = t % 2

    @pl.when(t == 0)
    def _():
        ones = _ones_diag(vb_scr.shape[2])
        for j in range(N_KV_HEADS):
            kb_scr[0, j, 0:hist, :] = jnp.zeros((hist, LANES), BF16)
            vb_scr[0, j, 0:hist, 0:LANES] = jnp.zeros((hist, LANES), BF16)
            for s in range(2):
                vb_scr[s, j, :, LANES:2 * LANES] = ones
        xr_scr[0:8, :] = jnp.zeros((8, D_LRU), F32)
        h_scr[...] = jnp.zeros(h_scr.shape, F32)

    sub = PROMPT_SUBTILE
    n_sub = tile // sub
    sub_chunks = sub // CHUNK
    o3 = Q_W + 2 * KV_W
    xs_in, proj = [], []
    for u in range(n_sub):
        x = x_ref[0, u * sub:(u + 1) * sub, :]
        xn = _rms_rows(x, g_ref[...]).astype(BF16)
        xs_in.append(x)
        o5 = o3 + 2 * D_LRU
        proj.append((jnp.dot(xn, win_ref[:, 0:Q_W], preferred_element_type=F32),
                     jnp.dot(xn, win_ref[:, Q_W:o3], preferred_element_type=F32),
                     jnp.dot(xn, win_ref[:, o3:o3 + D_LRU], preferred_element_type=F32),
                     jnp.dot(xn, win_ref[:, o3 + D_LRU:o5], preferred_element_type=F32),
                     jnp.dot(xn, win_ref[:, o5:o5 + Q_W], preferred_element_type=F32),
                     jnp.dot(xn, win_ref[:, o5 + Q_W:o5 + Q_W + KV_W], preferred_element_type=F32)))
    for u in range(n_sub):
        xr_scr[8 + u * sub:8 + (u + 1) * sub, :] = proj[u][2]

    base = (t * tile).astype(F32) * inv_ref[...]
    cos_b = jnp.cos(base)
    sin_b = jnp.sin(base) * sgn_ref[...]

    band = BAND_BLOCKS * KEY_BLOCK
    col = _lane_iota((2 * CHUNK, band))
    sink_rows = [jnp.concatenate(
        [jnp.broadcast_to(sink_ref[2 * j:2 * j + 1, :], (CHUNK, LANES)),
         jnp.broadcast_to(sink_ref[2 * j + 1:2 * j + 2, :], (CHUNK, LANES))], axis=0) for j in range(N_KV_HEADS)]
    ones_bd = _head_ones()
    norm_c = math.sqrt(HEAD_DIM)
    q_all = []
    for u in range(n_sub):
        hq, hkv, xr, gr, hqp, hkp = proj[u]
        cos_l = cos_ref[u * sub:(u + 1) * sub, :]
        sin_l = sin_ref[u * sub:(u + 1) * sub, :]
        cos_t = cos_b * cos_l - sin_b * sin_l
        sin_t = sin_b * cos_l + cos_b * sin_l
        q_cos = cos_t * (qg_ref[0:1, :] * (norm_c * HEAD_DIM ** -0.5))
        q_sin = sin_t * (qg_ref[1:2, :] * (norm_c * HEAD_DIM ** -0.5))
        q_tiles = [_norm_rope(hq[:, n * LANES:(n + 1) * LANES], hqp[:, n * LANES:(n + 1) * LANES],
                              q_cos, q_sin, ones_bd).astype(BF16) for n in range(Q_W // LANES)]
        kr = _norm_rope(hkv[:, 0:KV_W], hkp, cos_t * (kg_ref[0:1, :] * norm_c), sin_t * (kg_ref[1:2, :] * norm_c),
                        ones_bd)
        vv = hkv[:, KV_W:2 * KV_W]
        for j in range(N_KV_HEADS):
            ktop, kbot = _split_diag(_dup_head(kr, j))
            vtop, vbot = _split_diag(_dup_head(vv, j))
            for i in range(sub_chunks):
                r0 = hist + (u * sub_chunks + i) * KEY_BLOCK
                rows = slice(i * CHUNK, (i + 1) * CHUNK)
                kb_scr[slot, j, r0:r0 + CHUNK, :] = ktop[rows].astype(BF16)
                kb_scr[slot, j, r0 + CHUNK:r0 + KEY_BLOCK, :] = kbot[rows].astype(BF16)
                vb_scr[slot, j, r0:r0 + CHUNK, 0:LANES] = vtop[rows].astype(BF16)
                vb_scr[slot, j, r0 + CHUNK:r0 + KEY_BLOCK, 0:LANES] = vbot[rows].astype(BF16)
        q_all.append(q_tiles)

    pairs = [(i, j) for i in range(sub_chunks) for j in range(N_KV_HEADS)]
    scores_all, gates_all = [], []
    union = (sub_chunks + BAND_BLOCKS - 1) * KEY_BLOCK
    for u in range(n_sub):
        q_tiles = q_all[u]
        s_union = []
        for j in range(N_KV_HEADS):
            qs = jnp.concatenate([q_tiles[2 * j + tt][i * CHUNK:(i + 1) * CHUNK]
                                  for i in range(sub_chunks) for tt in range(2)], axis=0)
            first_block = u * sub_chunks * KEY_BLOCK
            s_union.append(_attend_scores(qs, kb_scr[slot, j, first_block:first_block + union, :], None))
        scores = []
        for i, j in pairs:
            s = s_union[j][i * KEY_BLOCK:(i + 1) * KEY_BLOCK, i * KEY_BLOCK:i * KEY_BLOCK + band]
            ci = u * sub_chunks + i
            if ci < BAND_BLOCKS - 1:
                s = jnp.where(col >= (1 - jnp.minimum(t, 1)) * ((BAND_BLOCKS - 1 - ci) * KEY_BLOCK), s, NEG)
            scores.append(s)
        scores_all.append(scores)
        first = 8 + u * sub
        taps = [xr_scr[first - (CONV_W - 1 - i):first - (CONV_W - 1 - i) + sub, :] for i in range(CONV_W - 1)]
        gates_all.append(_lru_gates(taps + [proj[u][2]], convw_ref, convb_ref, wg_ref))

    h_carry = h_scr[0:1, :]
    for u in range(n_sub):
        xr, gr = proj[u][2], proj[u][3]
        scores = scores_all[u]
        xc, gates = gates_all[u]
        n_pieces = D_LRU // LANES
        per_piece = len(pairs) // n_pieces
        weights, lru = [], []
        for k in range(n_pieces):
            weights += [_attend_weights(scores[p], sink_rows[pairs[p][1]])
                        for p in range(k * per_piece, (k + 1) * per_piece)]
            lru.append(_lru_piece(k, xc, gates, gr, h_carry, None, None, ba_ref, bx_ref, lam_ref))
        first_block = u * sub_chunks * KEY_BLOCK
        for j in range(N_KV_HEADS):
            p_rows = []
            for i in range(sub_chunks):
                e = weights[i * N_KV_HEADS + j][0]
                pieces = ([jnp.zeros((KEY_BLOCK, i * KEY_BLOCK), BF16)] if i else []) + [e]
                if i < sub_chunks - 1:
                    pieces.append(jnp.zeros((KEY_BLOCK, (sub_chunks - 1 - i) * KEY_BLOCK), BF16))
                p_rows.append(jnp.concatenate(pieces, axis=1))
            od = jnp.dot(jnp.concatenate(p_rows, axis=0), vb_scr[slot, j, first_block:first_block + union, :],
                         preferred_element_type=F32)
            for i in range(sub_chunks):
                m = weights[i * N_KV_HEADS + j][1]
                o = _attend_normalise(od[i * KEY_BLOCK:(i + 1) * KEY_BLOCK], m, sink_rows[j])
                rows = slice(u * sub + i * CHUNK, u * sub + (i + 1) * CHUNK)
                c0 = 2 * j * LANES
                mix_scr[rows, c0:c0 + LANES] = o[0:CHUNK].astype(BF16)
                mix_scr[rows, c0 + LANES:c0 + 2 * LANES] = o[CHUNK:2 * CHUNK].astype(BF16)

        h_carry = jnp.concatenate([h[sub - 1:sub, :] for _, h in lru], axis=1)
        rows = slice(u * sub, (u + 1) * sub)
        for k, (rec, _) in enumerate(lru):
            mix_scr[rows, Q_W + k * LANES:Q_W + (k + 1) * LANES] = rec.astype(BF16)

        y_att = jnp.dot(mix_scr[rows, 0:Q_W], wout_ref[0:Q_W, :], preferred_element_type=F32)
        y_rec = jnp.dot(mix_scr[rows, Q_W:Q_W + D_LRU], wout_ref[Q_W:Q_W + D_LRU, :], preferred_element_type=F32)
        y_ref[0, rows, :] = xs_in[u] + y_att + y_rec

    last = n_chunks * KEY_BLOCK
    for j in range(N_KV_HEADS):
        kb_scr[1 - slot, j, 0:hist, :] = kb_scr[slot, j, last:last + hist, :]
        vb_scr[1 - slot, j, 0:hist, 0:LANES] = vb_scr[slot, j, last:last + hist, 0:LANES]
    h_scr[0:1, :] = h_carry
    xr_scr[0:8, :] = xr[sub - 8:sub, :]

    @pl.when(t == nt - 1)
    def _():
        nk_ref[0] = kr[sub - SWA_ROWS:sub, :]
        nv_ref[0] = vv[sub - SWA_ROWS:sub, :]
        nh_ref[0] = h_carry
        nc_ref[0] = xr[sub - (CONV_W - 1):sub, :]


def _const_spec(shape):
    zeros = (0,) * len(shape)
    return pl.BlockSpec(shape, lambda *_: zeros, pipeline_mode=pl.Buffered(1))


def _even_prompt(x, p):
    bsz, s_len, _ = x.shape
    tile = PROMPT_TILE
    cos_l, sin_l = _rope_tables(jnp.arange(tile))
    inv_row, sgn_row = _rope_rows()
    nt = s_len // tile
    key_rows = (tile // CHUNK + 2) * KEY_BLOCK
    consts = [p['g'], p['win'], p['qg'], p['kg'], p['sink'], p['convw'], p['convb'], p['wg'], p['ba'], p['bx'],
              p['lam'], p['wout']]
    out_shape = [
        jax.ShapeDtypeStruct((bsz, s_len, D_MODEL), F32),
        jax.ShapeDtypeStruct((bsz, SWA_ROWS, KV_W), F32),
        jax.ShapeDtypeStruct((bsz, SWA_ROWS, KV_W), F32),
        jax.ShapeDtypeStruct((bsz, 1, D_LRU), F32),
        jax.ShapeDtypeStruct((bsz, CONV_W - 1, D_LRU), F32),
    ]
    return pl.pallas_call(
        _even_prompt_kernel,
        out_shape=out_shape,
        grid=(bsz, nt),
        in_specs=[pl.BlockSpec((1, tile, D_MODEL), lambda b, t: (b, t, 0)),
                  _const_spec(cos_l.shape), _const_spec(sin_l.shape), _const_spec(inv_row.shape),
                  _const_spec(sgn_row.shape)] + [_const_spec(c.shape) for c in consts],
        out_specs=[pl.BlockSpec((1, tile, D_MODEL), lambda b, t: (b, t, 0)),
                   pl.BlockSpec((1, SWA_ROWS, KV_W), lambda b, t: (b, 0, 0)),
                   pl.BlockSpec((1, SWA_ROWS, KV_W), lambda b, t: (b, 0, 0)),
                   pl.BlockSpec((1, 1, D_LRU), lambda b, t: (b, 0, 0)),
                   pl.BlockSpec((1, CONV_W - 1, D_LRU), lambda b, t: (b, 0, 0))],
        scratch_shapes=[pltpu.VMEM((2, N_KV_HEADS, key_rows, LANES), BF16),
                        pltpu.VMEM((2, N_KV_HEADS, key_rows, 2 * LANES), BF16),
                        pltpu.VMEM((tile + 8, D_LRU), F32),
                        pltpu.VMEM((8, D_LRU), F32),
                        pltpu.VMEM((tile, Q_W + D_LRU), BF16)],
        compiler_params=pltpu.CompilerParams(dimension_semantics=("arbitrary", "arbitrary"),
                                             vmem_limit_bytes=VMEM_LIMIT),
        name="even_mixer_prompt",
    )(x, cos_l, sin_l, inv_row, sgn_row, *consts)


def _even_sample_kernel(x_ref, cos_ref, sin_ref, kc_ref, vc_ref, h0_ref, xprev_ref,
                        g_ref, win_ref, qg_ref, kg_ref, sink_ref, convw_ref, convb_ref,
                        wg_ref, ba_ref, bx_ref, lam_ref, wout_ref,
                        y_ref, nk_ref, nv_ref, h_ref, xr_ref,
                        q_scr, k_scr, v_scr, mix_scr):
    rows_all = x_ref.shape[0]
    nseq = kc_ref.shape[0]
    t_len = rows_all // nseq
    x = x_ref[...]
    xn = _rms_rows(x, g_ref[...]).astype(BF16)
    cos_t = cos_ref[...]
    sin_t = sin_ref[...]

    o3 = Q_W + 2 * KV_W
    o5 = o3 + 2 * D_LRU
    hq = jnp.dot(xn, win_ref[:, 0:Q_W], preferred_element_type=F32)
    hkv = jnp.dot(xn, win_ref[:, Q_W:o3], preferred_element_type=F32)
    xr = jnp.dot(xn, win_ref[:, o3:o3 + D_LRU], preferred_element_type=F32)
    gr = jnp.dot(xn, win_ref[:, o3 + D_LRU:o5], preferred_element_type=F32)
    hqp = jnp.dot(xn, win_ref[:, o5:o5 + Q_W], preferred_element_type=F32)
    hkp = jnp.dot(xn, win_ref[:, o5 + Q_W:o5 + Q_W + KV_W], preferred_element_type=F32)
    ones_bd = _head_ones()
    norm_c = math.sqrt(HEAD_DIM)
    q_cos = cos_t * (qg_ref[0:1, :] * (norm_c * HEAD_DIM ** -0.5))
    q_sin = sin_t * (qg_ref[1:2, :] * (norm_c * HEAD_DIM ** -0.5))
    for n in range(Q_W // LANES):
        lanes = slice(n * LANES, (n + 1) * LANES)
        q_scr[:, lanes] = _norm_rope(hq[:, lanes], hqp[:, lanes], q_cos, q_sin, ones_bd)
    k_scr[...] = _norm_rope(hkv[:, 0:KV_W], hkp, cos_t * (kg_ref[0:1, :] * norm_c),
                            sin_t * (kg_ref[1:2, :] * norm_c), ones_bd)
    v_scr[...] = hkv[:, KV_W:2 * KV_W]

    band = BAND_BLOCKS * KEY_BLOCK
    col = _lane_iota((2 * t_len, band))
    col_mask = jnp.logical_or(col < 2 * KEY_BLOCK, (col % CHUNK) < t_len)
    ones_band = _ones_diag(band)
    pad = jnp.zeros((CHUNK - t_len, LANES), F32)

    sink_rows = [jnp.concatenate(
        [jnp.broadcast_to(sink_ref[2 * j:2 * j + 1, :], (t_len, LANES)),
         jnp.broadcast_to(sink_ref[2 * j + 1:2 * j + 2, :], (t_len, LANES))], axis=0) for j in range(N_KV_HEADS)]

    def group_body(gi, carry):
        work = []
        for s in range(SAMPLE_GROUP):
            b = gi * SAMPLE_GROUP + s
            r0 = pl.multiple_of(b * t_len, t_len)
            kc = kc_ref[b]
            vc = vc_ref[b]
            knew = k_scr[pl.ds(r0, t_len), :]
            vnew = v_scr[pl.ds(r0, t_len), :]
            nk_ref[b, 0:SWA_ROWS - t_len, :] = kc[t_len:SWA_ROWS]
            nk_ref[b, SWA_ROWS - t_len:SWA_ROWS, :] = knew
            nv_ref[b, 0:SWA_ROWS - t_len, :] = vc[t_len:SWA_ROWS]
            nv_ref[b, SWA_ROWS - t_len:SWA_ROWS, :] = vnew
            kfull = jnp.concatenate([kc, knew, pad], axis=0)
            vfull = jnp.concatenate([vc, vnew, pad], axis=0)
            for j in range(N_KV_HEADS):
                ktop, kbot = _split_diag(_dup_head(kfull, j))
                vtop, vbot = _split_diag(_dup_head(vfull, j))
                kparts, vparts = [], []
                for c in range(BAND_BLOCKS):
                    rs = slice(c * CHUNK, (c + 1) * CHUNK)
                    kparts += [ktop[rs], kbot[rs]]
                    vparts += [vtop[rs], vbot[rs]]
                kb = jnp.concatenate(kparts, axis=0).astype(BF16)
                vb = jnp.concatenate([jnp.concatenate(vparts, axis=0).astype(BF16), ones_band], axis=1)
                c0 = 2 * j * LANES
                qs = jnp.concatenate([q_scr[pl.ds(r0, t_len), c0:c0 + LANES],
                                      q_scr[pl.ds(r0, t_len), c0 + LANES:c0 + 2 * LANES]], axis=0).astype(BF16)
                work.append((r0, j, vb, _attend_scores(qs, kb, col_mask)))
        weights = [_attend_weights(s, sink_rows[j]) for _, j, _, s in work]
        for (r0, j, vb, _), (e, m) in zip(work, weights):
            o = _attend_output(e, m, vb, sink_rows[j])
            c0 = 2 * j * LANES
            mix_scr[pl.ds(r0, t_len), c0:c0 + LANES] = o[0:t_len]
            mix_scr[pl.ds(r0, t_len), c0 + LANES:c0 + 2 * LANES] = o[t_len:2 * t_len]
        return carry

    lax.fori_loop(0, nseq // SAMPLE_GROUP, group_body, 0)

    tmod =_row_iota((rows_all, LANES)) % t_len
    tmod_w = _row_iota((rows_all, D_LRU)) % t_len
    xprev = xprev_ref[...]
    xs = []
    for i in range(CONV_W - 1):
        k = CONV_W - 1 - i
        own = pltpu.roll(xr, k, 0)
        cached = xprev if k == CONV_W - 1 else pltpu.roll(xprev, rows_all - (CONV_W - 1 - k), 0)
        xs.append(jnp.where(tmod_w >= k, own, cached))
    xs.append(xr)
    rec, h = _lru_branch(xs, gr, h0_ref[...], t_len, tmod, convw_ref, convb_ref, wg_ref, ba_ref, bx_ref, lam_ref)
    mix_scr[:, Q_W:Q_W + D_LRU] = rec
    xr_ref[...] = xr
    h_ref[...] = h
    y_ref[...] = x + jnp.dot(mix_scr[...].astype(BF16), wout_ref[...], preferred_element_type=F32)


def _even_sample(x2, cos_t, sin_t, kc, vc, h0_rows, xprev, p):
    rows_all = x2.shape[0]
    nseq = kc.shape[0]
    consts = [p['g'], p['win'], p['qg'], p['kg'], p['sink'], p['convw'], p['convb'], p['wg'], p['ba'], p['bx'],
              p['lam'], p['wout']]
    ins = [x2, cos_t, sin_t, kc, vc, h0_rows, xprev] + consts
    out_shape = [
        jax.ShapeDtypeStruct((rows_all, D_MODEL), F32),
        jax.ShapeDtypeStruct((nseq, SWA_ROWS, KV_W), F32),
        jax.ShapeDtypeStruct((nseq, SWA_ROWS, KV_W), F32),
        jax.ShapeDtypeStruct((rows_all, D_LRU), F32),
        jax.ShapeDtypeStruct((rows_all, D_LRU), F32),
    ]
    return pl.pallas_call(
        _even_sample_kernel,
        out_shape=out_shape,
        grid=(1,),
        in_specs=[_const_spec(a.shape) for a in ins],
        out_specs=[pl.BlockSpec(s.shape, lambda i, n=len(s.shape): (0,) * n) for s in out_shape],
        scratch_shapes=[pltpu.VMEM((rows_all, Q_W), F32),
                        pltpu.VMEM((rows_all, KV_W), F32),
                        pltpu.VMEM((rows_all, KV_W), F32),
                        pltpu.VMEM((rows_all, Q_W + D_LRU), F32)],
        compiler_params=pltpu.CompilerParams(dimension_semantics=("arbitrary",), vmem_limit_bytes=VMEM_LIMIT),
        name="even_mixer_sample",
    )(*ins)


def _ffn_kernel(xp_ref, xs_ref, g_ref, w1_ref, w2_ref, yp_ref, ys_ref):
    i = pl.program_id(0)
    x = jnp.where(i == 0, xs_ref[...], xp_ref[...])
    xn = _rms_rows(x, g_ref[...]).astype(BF16)
    acc = x
    step = D_MODEL
    for c in range(D_FF // step):
        hcol = jnp.dot(xn, w1_ref[:, c * step:(c + 1) * step].astype(BF16), preferred_element_type=F32)
        hcol = jnp.square(jnp.maximum(hcol, 0.0)).astype(BF16)
        acc = acc + jnp.dot(hcol, w2_ref[c * step:(c + 1) * step, :].astype(BF16), preferred_element_type=F32)

    yp_ref[...] = acc

    @pl.when(i == 0)
    def _():
        ys_ref[...] = yp_ref[...]


def _sample_then_tiles_specs(tile, width):
    prompt = pl.BlockSpec((tile, width), lambda i: (jnp.maximum(i - 1, 0), 0))
    sample = pl.BlockSpec((tile, width), lambda i: (0, 0))
    return prompt, sample


def _ffn(xp, xs, g, w1, w2, layer):
    tile = xs.shape[0]
    n_tiles = xp.shape[0] // tile
    xp_spec, xs_spec = _sample_then_tiles_specs(tile, D_MODEL)
    layer_spec = lambda shape: pl.BlockSpec((None,) + shape[1:], lambda i: (layer, 0, 0),
                                            pipeline_mode=pl.Buffered(1))
    return pl.pallas_call(
        _ffn_kernel,
        out_shape=[jax.ShapeDtypeStruct(xp.shape, F32), jax.ShapeDtypeStruct(xs.shape, F32)],
        grid=(n_tiles + 1,),
        in_specs=[xp_spec, xs_spec, _const_spec(g.shape), layer_spec(w1.shape), layer_spec(w2.shape)],
        out_specs=list(_sample_then_tiles_specs(tile, D_MODEL)),
        compiler_params=pltpu.CompilerParams(dimension_semantics=("arbitrary",), vmem_limit_bytes=VMEM_LIMIT),
        name="channel_mlp",
    )(xp, xs, g, w1, w2)


def _gmlp_kernel(seg_s, xp_ref, xs_ref, g_ref, win_ref, vg_ref, wsp_ref, wss_ref, biasp_ref, biass_ref, wout_ref,
                 yp_ref, ys_ref, v_ref, v_scr):
    i = pl.program_id(0)
    is_s = i == 0
    tile = xp_ref.shape[0]
    sub = GMLP_SUBTILE
    n_sub = tile // sub
    xs_in, proj = [], []
    for u in range(n_sub):
        rows = slice(u * sub, (u + 1) * sub)
        x = jnp.where(is_s, xs_ref[rows, :], xp_ref[rows, :])
        xn = _rms_rows(x, g_ref[...]).astype(BF16)
        xs_in.append(x)
        proj.append((jnp.dot(xn, win_ref[:, D_C:2 * D_C], preferred_element_type=F32),
                     jnp.dot(xn, win_ref[:, 0:D_C], preferred_element_type=F32)))
    r = _row_iota((CHUNK_MLP, CHUNK_MLP))
    col = _lane_iota((CHUNK_MLP, CHUNK_MLP))
    keep_p = col <= r
    keep_s = jnp.logical_and((r // seg_s) == (col // seg_s), keep_p)
    ws = [jnp.where(is_s, jnp.where(keep_s, wss_ref[gi], 0.0), jnp.where(keep_p, wsp_ref[gi], 0.0)).astype(BF16)
          for gi in range(C_GROUPS)]
    step = MXU_WIDTH
    per_step = step // LANES
    for u in range(n_sub):
        rows = slice(u * sub, (u + 1) * sub)
        zv = _gelu(proj[u][0])
        zc = zv - jnp.mean(zv, axis=-1, keepdims=True)
        v = zc * lax.rsqrt(jnp.mean(zc * zc, axis=-1, keepdims=True) + EPS) * vg_ref[...]
        v_scr[rows, :] = v
        vb = v.astype(BF16)
        sv = []
        for gi in range(C_GROUPS):
            lanes = slice(gi * LANES, (gi + 1) * LANES)
            bias = jnp.where(is_s, biass_ref[:, lanes], biasp_ref[:, lanes])
            blocks = [jnp.dot(ws[gi], vb[ch * CHUNK_MLP:(ch + 1) * CHUNK_MLP, lanes],
                              preferred_element_type=F32) + bias for ch in range(sub // CHUNK_MLP)]
            sv.append(jnp.concatenate(blocks, axis=0))
        acc = xs_in[u]
        for c in range(D_C // step):
            cols = slice(c * step, (c + 1) * step)
            gate = (_gelu(proj[u][1][:, cols]) *
                    jnp.concatenate(sv[c * per_step:(c + 1) * per_step], axis=1)).astype(BF16)
            acc = acc + jnp.dot(gate, wout_ref[cols, :], preferred_element_type=F32)
        yp_ref[rows, :] = acc

    @pl.when(is_s)
    def _():
        ys_ref[...] = yp_ref[...]
        v_ref[...] = v_scr[...]


def _gmlp(xp, xs, seg_s, g, win, vg, ws_p, ws_s, bias_p, bias_s, wout):
    tile = xs.shape[0]
    n_tiles = xp.shape[0] // tile
    xp_spec, xs_spec = _sample_then_tiles_specs(tile, D_MODEL)
    consts = (g, win, vg, ws_p, ws_s, bias_p, bias_s, wout)
    return pl.pallas_call(
        functools.partial(_gmlp_kernel, seg_s),
        out_shape=[jax.ShapeDtypeStruct(xp.shape, F32), jax.ShapeDtypeStruct(xs.shape, F32),
                   jax.ShapeDtypeStruct((tile, D_C), F32)],
        grid=(n_tiles + 1,),
        in_specs=[xp_spec, xs_spec] + [_const_spec(a.shape) for a in consts],
        out_specs=list(_sample_then_tiles_specs(tile, D_MODEL)) + [pl.BlockSpec((tile, D_C), lambda i: (0, 0))],
        scratch_shapes=[pltpu.VMEM((tile, D_C), F32)],
        compiler_params=pltpu.CompilerParams(dimension_semantics=("arbitrary",), vmem_limit_bytes=VMEM_LIMIT),
        name="gmlp_mixer",
    )(xp, xs, *consts)


def _rope_inv():
    half = HEAD_DIM // 2
    return ROPE_THETA ** (-jnp.arange(half, dtype=F32) / half)


def _rope_rows():
    inv = _rope_inv()
    sgn = jnp.ones((HEAD_DIM // 2,), F32)
    return jnp.concatenate([inv] * 4)[None, :], jnp.concatenate([-sgn, sgn, -sgn, sgn])[None, :]


def _rope_tables(pos):
    inv = _rope_inv()
    ang = pos.astype(F32)[:, None] * inv[None, :]
    cos = jnp.cos(ang)
    sin = jnp.sin(ang)
    return jnp.concatenate([cos] * 4, axis=1), jnp.concatenate([-sin, sin, -sin, sin], axis=1)


def _swap_head_halves(a):
    half = HEAD_DIM // 2
    return a.reshape(a.shape[:-1] + (-1, 2, half))[..., ::-1, :].reshape(a.shape)


def _gate_weights(wa, wx):
    nb = LRU_BLOCKS // 2
    eye = jnp.eye(nb, dtype=bool)

    def half_diag(w, hh):
        blocks = w[hh * nb:(hh + 1) * nb]
        return jnp.where(eye[:, None, :, None], blocks[:, :, None, :], 0.0).reshape(nb * LRU_BW, nb * LRU_BW)
    return jnp.stack([jnp.concatenate([half_diag(wa, hh), half_diag(wx, hh)], axis=1)
                      for hh in range(2)]).astype(BF16)


def kernel(x_prompt, x_sample, cache_swa_k, cache_swa_v, state_lru_h, state_lru_conv, e_norm_g, e_w_in, e_q_norm_g, e_k_norm_g, e_sinks, e_conv_w, e_conv_b, e_gate_a_w, e_gate_a_b, e_gate_x_w, e_gate_x_b, e_lru_lambda, e_w_out, o_norm_g, o_w_in, o_v_norm_g, o_spatial_w, o_spatial_b, o_w_out, ffn_norm_g, ffn_w1, ffn_w2):
    bsz, s_len, _ = x_prompt.shape
    nseq, t_len, _ = x_sample.shape
    past_len = PAST_LEN
    row = lambda a: a.reshape(1, -1)

    e = 0
    p = {
        'g': row(e_norm_g[e]),
        'win': jnp.concatenate([e_w_in[e], _swap_head_halves(e_w_in[e][:, :Q_W + KV_W])], axis=1).astype(BF16),
        'qg': jnp.tile(jnp.stack([e_q_norm_g[e], _swap_head_halves(e_q_norm_g[e])]), (1, LANES // HEAD_DIM)),
        'kg': jnp.tile(jnp.stack([e_k_norm_g[e], _swap_head_halves(e_k_norm_g[e])]), (1, LANES // HEAD_DIM)),
        'sink': jnp.repeat(e_sinks[e], HEAD_DIM).reshape(N_Q_HEADS // 2, LANES),
        'convw': e_conv_w[e],
        'convb': row(e_conv_b[e]),
        'wg': _gate_weights(e_gate_a_w[e], e_gate_x_w[e]),
        'ba': row(e_gate_a_b[e]),
        'bx': row(e_gate_x_b[e]),
        'lam': row(e_lru_lambda[e]),
        'wout': e_w_out[e].astype(BF16),
    }
    yp, nk_p, nv_p, nh_p, nc_p = _even_prompt(x_prompt, p)

    cos_s, sin_s = _rope_tables(past_len + jnp.arange(t_len))
    cos_s = jnp.tile(cos_s, (nseq, 1))
    sin_s = jnp.tile(sin_s, (nseq, 1))
    kc = cache_swa_k[e].reshape(nseq, SWA_ROWS, KV_W)
    vc = cache_swa_v[e].reshape(nseq, SWA_ROWS, KV_W)
    h0_rows = jnp.repeat(state_lru_h[e], t_len, axis=0)
    xprev = jnp.pad(state_lru_conv[e], ((0, 0), (0, t_len - (CONV_W - 1)), (0, 0))).reshape(nseq * t_len, D_LRU)
    ys, nk_s, nv_s, h_s, xr_s = _even_sample(x_sample.reshape(nseq * t_len, D_MODEL), cos_s, sin_s, kc, vc,
                                             h0_rows, xprev, p)
    nh_s = h_s.reshape(nseq, t_len, D_LRU)[:, -1]
    nc_s = xr_s.reshape(nseq, t_len, D_LRU)[:, t_len - (CONV_W - 1):]

    w1, w2 = ffn_w1, ffn_w2
    yp, ys = _ffn(yp.reshape(bsz * s_len, D_MODEL), ys, row(ffn_norm_g[0]), w1, w2, 0)

    o = 0
    bias_p = jnp.repeat(o_spatial_b[o].T, LANES, axis=1)
    reps = CHUNK_MLP // t_len
    ws_s = jnp.tile(o_spatial_w[o][:, :t_len, :t_len], (1, reps, reps))
    bias_s = jnp.tile(jnp.repeat(o_spatial_b[o][:, :t_len].T, LANES, axis=1), (reps, 1))
    yp, ys, gv_s = _gmlp(yp, ys, t_len, row(o_norm_g[o]), o_w_in[o].astype(BF16), row(o_v_norm_g[o]),
                         o_spatial_w[o], ws_s, bias_p, bias_s, o_w_out[o].astype(BF16))

    yp, ys = _ffn(yp, ys, row(ffn_norm_g[1]), w1, w2, 1)

    kv_shape = (1, -1, SWA_ROWS, N_KV_HEADS, HEAD_DIM)
    return (yp.reshape(bsz, s_len, D_MODEL), ys.reshape(nseq, t_len, D_MODEL),
            nk_p.reshape(kv_shape), nv_p.reshape(kv_shape), nh_p.reshape(1, bsz, D_LRU), nc_p[None],
            nk_s.reshape(kv_shape), nv_s.reshape(kv_shape), nh_s[None], nc_s[None],
            gv_s.reshape(1, nseq, t_len, D_C))
```

```python
import functools
import math

import jax
import jax.numpy as jnp
from jax import lax
from jax.experimental import pallas as pl
from jax.experimental.pallas import tpu as pltpu

F32 = jnp.float32
BF16 = jnp.bfloat16

D_MODEL = 1024
CHUNK = 64
HEAD_DIM = 64
N_Q_HEADS = 8
N_KV_HEADS = 2
SWA_ROWS = 128
PAST_LEN = 4096
ROPE_THETA = 10000.0
NEG = -1e30
D_LRU = 512
LRU_BLOCKS = 8
LRU_BW = D_LRU // LRU_BLOCKS
CONV_W = 4
LRU_C = 8.0
Q_W = N_Q_HEADS * HEAD_DIM
KV_W = N_KV_HEADS * HEAD_DIM
CHUNK_MLP = 128
D_C = D_MODEL
C_GROUPS = 8
D_FF = 4 * D_MODEL
EPS = 1e-6

LANES = 128
SUBLANES = 8
MXU_WIDTH = 256
KEY_BLOCK = 2 * CHUNK
BAND_BLOCKS = 3
VMEM_LIMIT = 56 * 1024 * 1024

PROMPT_TILE = 512
PROMPT_SUBTILE = 256
ROW_TILE = 512
GMLP_SUBTILE = 512
SAMPLE_GROUP = 8


def _gelu(x):
    k1 = -2.0 * math.sqrt(2.0 / math.pi) * math.log2(math.e)
    return x / (1.0 + jnp.exp2(x * (k1 + (k1 * 0.044715) * (x * x))))


def _rms_rows(x, g):
    return x * lax.rsqrt(jnp.mean(x * x, axis=-1, keepdims=True) + EPS) * g


def _lane_iota(shape):
    return lax.broadcasted_iota(jnp.int32, shape, 1)


def _row_iota(shape):
    return lax.broadcasted_iota(jnp.int32, shape, 0)


def _head_ones():
    return ((_row_iota((LANES, LANES)) // HEAD_DIM) == (_lane_iota((LANES, LANES)) // HEAD_DIM)).astype(BF16)


def _norm_rope(x, xp, g_cos, g_sin, ones_bd):
    ss = jnp.dot((x * x).astype(BF16), ones_bd, preferred_element_type=F32)
    return (x * g_cos + xp * g_sin) * lax.rsqrt(ss + HEAD_DIM * EPS)


def _dup_head(t, j):
    lo = _lane_iota(t.shape) < HEAD_DIM
    swapped = pltpu.roll(t, HEAD_DIM, 1)
    return jnp.where(lo, t, swapped) if j == 0 else jnp.where(lo, swapped, t)


def _split_diag(t):
    lo = _lane_iota(t.shape) < HEAD_DIM
    return jnp.where(lo, t, 0.0), jnp.where(lo, 0.0, t)


def _ones_diag(rows):
    r = _row_iota((rows, LANES))
    return (((r % KEY_BLOCK) // CHUNK) == (_lane_iota((rows, LANES)) // HEAD_DIM)).astype(BF16)


def _attend_scores(qs, kb, col_mask):
    s = lax.dot_general(qs, kb, (((1,), (1,)), ((), ())), preferred_element_type=F32)
    return s if col_mask is None else jnp.where(col_mask, s, NEG)


def _attend_weights(s, sink_rows):
    mx = jnp.maximum(jnp.maximum(s[:, :LANES], s[:, LANES:2 * LANES]), s[:, 2 * LANES:])
    lo = _lane_iota(mx.shape) < HEAD_DIM
    m0 = jnp.max(jnp.where(lo, mx, -jnp.inf), axis=1, keepdims=True)
    m1 = jnp.max(jnp.where(lo, -jnp.inf, mx), axis=1, keepdims=True)
    m = jnp.maximum(jnp.where(lo, m0, m1), sink_rows)
    e = jnp.exp(s - jnp.concatenate([m, m, m], axis=1))
    return e.astype(BF16), m


def _attend_output(e, m, vb, sink_rows):
    od = jnp.dot(e, vb, preferred_element_type=F32)
    den = od[:, LANES:] + jnp.exp(sink_rows - m)
    return od[:, :LANES] / den


def _lru_scan(a, b, seg, tmod):
    d = 1
    while d < seg:
        ok = tmod >= d
        b = jnp.where(ok, a * pltpu.roll(b, d, 0), 0.0) + b
        if 2 * d < seg:
            a = jnp.where(ok, a * pltpu.roll(a, d, 0), a)
        d *= 2
    return b


def _lru_scan_tile(a, b, h_prev):
    rows = a.shape[0]
    groups = rows // SUBLANES
    a3 = a.reshape(groups, SUBLANES, LANES)
    b3 = b.reshape(groups, SUBLANES, LANES)
    sub = lax.broadcasted_iota(jnp.int32, a3.shape, 1)
    d = 1
    while d < SUBLANES:
        ok = sub >= d
        b3 = jnp.where(ok, a3 * pltpu.roll(b3, d, 1), 0.0) + b3
        a3 = jnp.where(ok, a3 * pltpu.roll(a3, d, 1), a3)
        d *= 2
    carry = jnp.broadcast_to(h_prev, (SUBLANES, LANES))
    out = []
    for g in range(groups):
        hg = a3[g] * carry + b3[g]
        out.append(hg)
        carry = jnp.broadcast_to(hg[SUBLANES - 1:SUBLANES, :], (SUBLANES, LANES))
    return jnp.concatenate(out, axis=0)


def _lru_branch(xs, gr, h_in, seg, tmod, convw_ref, convb_ref, wg_ref, ba_ref, bx_ref, lam_ref):
    xc, gates = _lru_gates(xs, convw_ref, convb_ref, wg_ref)
    parts = [_lru_piece(k, xc, gates, gr, h_in, seg, tmod, ba_ref, bx_ref, lam_ref) for k in range(D_LRU // LANES)]
    return jnp.concatenate([p[0] for p in parts], axis=1), jnp.concatenate([p[1] for p in parts], axis=1)


def _lru_gates(xs, convw_ref, convb_ref, wg_ref):
    xc = xs[0] * convw_ref[0:1, :] + convb_ref[...]
    for i in range(1, CONV_W):
        xc = xc + xs[i] * convw_ref[i:i + 1, :]
    xcb = xc.astype(BF16)
    half = D_LRU // 2
    return xc, [jnp.dot(xcb[:, hh * half:(hh + 1) * half], wg_ref[hh], preferred_element_type=F32)
                for hh in range(2)]


def _lru_piece(k, xc, gates, gr, h_in, seg, tmod, ba_ref, bx_ref, lam_ref):
    half = D_LRU // 2
    hh, tt = divmod(k, half // LANES)
    sl = slice(k * LANES, (k + 1) * LANES)
    g = gates[hh]
    r = jax.nn.sigmoid(g[:, tt * LANES:(tt + 1) * LANES] + ba_ref[:, sl])
    gi = jax.nn.sigmoid(g[:, half + tt * LANES:half + (tt + 1) * LANES] + bx_ref[:, sl])
    log_a = r * (-LRU_C * jax.nn.softplus(-lam_ref[:, sl]))
    a = jnp.exp(log_a)
    b = jnp.sqrt(-jnp.tanh(log_a) * (a * a + 1.0)) * (gi * xc[:, sl])
    if seg is None:
        h = _lru_scan_tile(a, b, h_in[:, sl])
    else:
        h = _lru_scan(a, b + jnp.where(tmod == 0, a * h_in[:, sl], 0.0), seg, tmod)
    return h * _gelu(gr[:, sl]), h


def _even_prompt_kernel(x_ref, cos_ref, sin_ref, inv_ref, sgn_ref, g_ref, win_ref, qg_ref, kg_ref, sink_ref, convw_ref, convb_ref,
                        wg_ref, ba_ref, bx_ref, lam_ref, wout_ref,
                        y_ref, nk_ref, nv_ref, nh_ref, nc_ref,
                        kb_scr, vb_scr, xr_scr, h_scr, mix_scr):
    t = pl.program_id(1)
    nt = pl.num_programs(1)
    tile = x_ref.shape[1]
    n_chunks = tile // CHUNK
    hist = 2 * KEY_BLOCK
    slot = t % 2

    @pl.when(t == 0)
    def _():
        ones = _ones_diag(vb_scr.shape[2])
        for j in range(N_KV_HEADS):
            kb_scr[0, j, 0:hist, :] = jnp.zeros((hist, LANES), BF16)
            vb_scr[0, j, 0:hist, 0:LANES] = jnp.zeros((hist, LANES), BF16)
            for s in range(2):
                vb_scr[s, j, :, LANES:2 * LANES] = ones
        xr_scr[0:8, :] = jnp.zeros((8, D_LRU), F32)
        h_scr[...] = jnp.zeros(h_scr.shape, F32)

    sub = PROMPT_SUBTILE
    n_sub = tile // sub
    sub_chunks = sub // CHUNK
    o3 = Q_W + 2 * KV_W
    xs_in, proj = [], []
    for u in range(n_sub):
        x = x_ref[0, u * sub:(u + 1) * sub, :]
        xn = _rms_rows(x, g_ref[...]).astype(BF16)
        xs_in.append(x)
        o5 = o3 + 2 * D_LRU
        proj.append((jnp.dot(xn, win_ref[:, 0:Q_W], preferred_element_type=F32),
                     jnp.dot(xn, win_ref[:, Q_W:o3], preferred_element_type=F32),
                     jnp.dot(xn, win_ref[:, o3:o3 + D_LRU], preferred_element_type=F32),
                     jnp.dot(xn, win_ref[:, o3 + D_LRU:o5], preferred_element_type=F32),
                     jnp.dot(xn, win_ref[:, o5:o5 + Q_W], preferred_element_type=F32),
                     jnp.dot(xn, win_ref[:, o5 + Q_W:o5 + Q_W + KV_W], preferred_element_type=F32)))
    for u in range(n_sub):
        xr_scr[8 + u * sub:8 + (u + 1) * sub, :] = proj[u][2]

    base = (t * tile).astype(F32) * inv_ref[...]
    cos_b = jnp.cos(base)
    sin_b = jnp.sin(base) * sgn_ref[...]

    band = BAND_BLOCKS * KEY_BLOCK
    col = _lane_iota((2 * CHUNK, band))
    sink_rows = [jnp.concatenate(
        [jnp.broadcast_to(sink_ref[2 * j:2 * j + 1, :], (CHUNK, LANES)),
         jnp.broadcast_to(sink_ref[2 * j + 1:2 * j + 2, :], (CHUNK, LANES))], axis=0) for j in range(N_KV_HEADS)]
    ones_bd = _head_ones()
    norm_c = math.sqrt(HEAD_DIM)
    q_all = []
    for u in range(n_sub):
        hq, hkv, xr, gr, hqp, hkp = proj[u]
        cos_l = cos_ref[u * sub:(u + 1) * sub, :]
        sin_l = sin_ref[u * sub:(u + 1) * sub, :]
        cos_t = cos_b * cos_l - sin_b * sin_l
        sin_t = sin_b * cos_l + cos_b * sin_l
        q_cos = cos_t * (qg_ref[0:1, :] * (norm_c * HEAD_DIM ** -0.5))
        q_sin = sin_t * (qg_ref[1:2, :] * (norm_c * HEAD_DIM ** -0.5))
        q_tiles = [_norm_rope(hq[:, n * LANES:(n + 1) * LANES], hqp[:, n * LANES:(n + 1) * LANES],
                              q_cos, q_sin, ones_bd).astype(BF16) for n in range(Q_W // LANES)]
        kr = _norm_rope(hkv[:, 0:KV_W], hkp, cos_t * (kg_ref[0:1, :] * norm_c), sin_t * (kg_ref[1:2, :] * norm_c),
                        ones_bd)
        vv = hkv[:, KV_W:2 * KV_W]
        for j in range(N_KV_HEADS):
            ktop, kbot = _split_diag(_dup_head(kr, j))
            vtop, vbot = _split_diag(_dup_head(vv, j))
            for i in range(sub_chunks):
                r0 = hist + (u * sub_chunks + i) * KEY_BLOCK
                rows = slice(i * CHUNK, (i + 1) * CHUNK)
                kb_scr[slot, j, r0:r0 + CHUNK, :] = ktop[rows].astype(BF16)
                kb_scr[slot, j, r0 + CHUNK:r0 + KEY_BLOCK, :] = kbot[rows].astype(BF16)
                vb_scr[slot, j, r0:r0 + CHUNK, 0:LANES] = vtop[rows].astype(BF16)
                vb_scr[slot, j, r0 + CHUNK:r0 + KEY_BLOCK, 0:LANES] = vbot[rows].astype(BF16)
        q_all.append(q_tiles)

    pairs = [(i, j) for i in range(sub_chunks) for j in range(N_KV_HEADS)]
    scores_all, gates_all = [], []
    for u in range(n_sub):
        q_tiles = q_all[u]
        scores = []
        for i, j in pairs:
            rows = slice(i * CHUNK, (i + 1) * CHUNK)
            ci = u * sub_chunks + i
            col_mask = None
            if ci < BAND_BLOCKS - 1:
                col_mask = col >= (1 - jnp.minimum(t, 1)) * ((BAND_BLOCKS - 1 - ci) * KEY_BLOCK)
            qs = jnp.concatenate([q_tiles[2 * j][rows], q_tiles[2 * j + 1][rows]], axis=0)
            scores.append(_attend_scores(qs, kb_scr[slot, j, ci * KEY_BLOCK:ci * KEY_BLOCK + band, :], col_mask))
        scores_all.append(scores)
        first = 8 + u * sub
        taps = [xr_scr[first - (CONV_W - 1 - i):first - (CONV_W - 1 - i) + sub, :] for i in range(CONV_W - 1)]
        gates_all.append(_lru_gates(taps + [proj[u][2]], convw_ref, convb_ref, wg_ref))

    h_carry = h_scr[0:1, :]
    for u in range(n_sub):
        xr, gr = proj[u][2], proj[u][3]
        scores = scores_all[u]
        xc, gates = gates_all[u]
        n_pieces = D_LRU // LANES
        per_piece = len(pairs) // n_pieces
        weights, lru = [], []
        for k in range(n_pieces):
            weights += [_attend_weights(scores[p], sink_rows[pairs[p][1]])
                        for p in range(k * per_piece, (k + 1) * per_piece)]
            lru.append(_lru_piece(k, xc, gates, gr, h_carry, None, None, ba_ref, bx_ref, lam_ref))
        for (e, m), (i, j) in zip(weights, pairs):
            ci = u * sub_chunks + i
            rows = slice(u * sub + i * CHUNK, u * sub + (i + 1) * CHUNK)
            o = _attend_output(e, m, vb_scr[slot, j, ci * KEY_BLOCK:ci * KEY_BLOCK + band, :], sink_rows[j])
            c0 = 2 * j * LANES
            mix_scr[rows, c0:c0 + LANES] = o[0:CHUNK].astype(BF16)
            mix_scr[rows, c0 + LANES:c0 + 2 * LANES] = o[CHUNK:2 * CHUNK].astype(BF16)

        h_carry = jnp.concatenate([h[sub - 1:sub, :] for _, h in lru], axis=1)
        rows = slice(u * sub, (u + 1) * sub)
        for k, (rec, _) in enumerate(lru):
            mix_scr[rows, Q_W + k * LANES:Q_W + (k + 1) * LANES] = rec.astype(BF16)

        y_att = jnp.dot(mix_scr[rows, 0:Q_W], wout_ref[0:Q_W, :], preferred_element_type=F32)
        y_rec = jnp.dot(mix_scr[rows, Q_W:Q_W + D_LRU], wout_ref[Q_W:Q_W + D_LRU, :], preferred_element_type=F32)
        y_ref[0, rows, :] = xs_in[u] + y_att + y_rec

    last = n_chunks * KEY_BLOCK
    for j in range(N_KV_HEADS):
        kb_scr[1 - slot, j, 0:hist, :] = kb_scr[slot, j, last:last + hist, :]
        vb_scr[1 - slot, j, 0:hist, 0:LANES] = vb_scr[slot, j, last:last + hist, 0:LANES]
    h_scr[0:1, :] = h_carry
    xr_scr[0:8, :] = xr[sub - 8:sub, :]

    @pl.when(t == nt - 1)
    def _():
        nk_ref[0] = kr[sub - SWA_ROWS:sub, :]
        nv_ref[0] = vv[sub - SWA_ROWS:sub, :]
        nh_ref[0] = h_carry
        nc_ref[0] = xr[sub - (CONV_W - 1):sub, :]


def _const_spec(shape):
    zeros = (0,) * len(shape)
    return pl.BlockSpec(shape, lambda *_: zeros, pipeline_mode=pl.Buffered(1))


def _even_prompt(x, p):
    bsz, s_len, _ = x.shape
    tile = PROMPT_TILE
    cos_l, sin_l = _rope_tables(jnp.arange(tile))
    inv_row, sgn_row = _rope_rows()
    nt = s_len // tile
    key_rows = (tile // CHUNK + 2) * KEY_BLOCK
    consts = [p['g'], p['win'], p['qg'], p['kg'], p['sink'], p['convw'], p['convb'], p['wg'], p['ba'], p['bx'],
              p['lam'], p['wout']]
    out_shape = [
        jax.ShapeDtypeStruct((bsz, s_len, D_MODEL), F32),
        jax.ShapeDtypeStruct((bsz, SWA_ROWS, KV_W), F32),
        jax.ShapeDtypeStruct((bsz, SWA_ROWS, KV_W), F32),
        jax.ShapeDtypeStruct((bsz, 1, D_LRU), F32),
        jax.ShapeDtypeStruct((bsz, CONV_W - 1, D_LRU), F32),
    ]
    return pl.pallas_call(
        _even_prompt_kernel,
        out_shape=out_shape,
        grid=(bsz, nt),
        in_specs=[pl.BlockSpec((1, tile, D_MODEL), lambda b, t: (b, t, 0)),
                  _const_spec(cos_l.shape), _const_spec(sin_l.shape), _const_spec(inv_row.shape),
                  _const_spec(sgn_row.shape)] + [_const_spec(c.shape) for c in consts],
        out_specs=[pl.BlockSpec((1, tile, D_MODEL), lambda b, t: (b, t, 0)),
                   pl.BlockSpec((1, SWA_ROWS, KV_W), lambda b, t: (b, 0, 0)),
                   pl.BlockSpec((1, SWA_ROWS, KV_W), lambda b, t: (b, 0, 0)),
                   pl.BlockSpec((1, 1, D_LRU), lambda b, t: (b, 0, 0)),
                   pl.BlockSpec((1, CONV_W - 1, D_LRU), lambda b, t: (b, 0, 0))],
        scratch_shapes=[pltpu.VMEM((2, N_KV_HEADS, key_rows, LANES), BF16),
                        pltpu.VMEM((2, N_KV_HEADS, key_rows, 2 * LANES), BF16),
                        pltpu.VMEM((tile + 8, D_LRU), F32),
                        pltpu.VMEM((8, D_LRU), F32),
                        pltpu.VMEM((tile, Q_W + D_LRU), BF16)],
        compiler_params=pltpu.CompilerParams(dimension_semantics=("arbitrary", "arbitrary"),
                                             vmem_limit_bytes=VMEM_LIMIT),
        name="even_mixer_prompt",
    )(x, cos_l, sin_l, inv_row, sgn_row, *consts)


def _even_sample_kernel(x_ref, cos_ref, sin_ref, kc_ref, vc_ref, h0_ref, xprev_ref,
                        g_ref, win_ref, qg_ref, kg_ref, sink_ref, convw_ref, convb_ref,
                        wg_ref, ba_ref, bx_ref, lam_ref, wout_ref,
                        y_ref, nk_ref, nv_ref, h_ref, xr_ref,
                        q_scr, k_scr, v_scr, mix_scr):
    rows_all = x_ref.shape[0]
    nseq = kc_ref.shape[0]
    t_len = rows_all // nseq
    x = x_ref[...]
    xn = _rms_rows(x, g_ref[...]).astype(BF16)
    cos_t = cos_ref[...]
    sin_t = sin_ref[...]

    o3 = Q_W + 2 * KV_W
    o5 = o3 + 2 * D_LRU
    hq = jnp.dot(xn, win_ref[:, 0:Q_W], preferred_element_type=F32)
    hkv = jnp.dot(xn, win_ref[:, Q_W:o3], preferred_element_type=F32)
    xr = jnp.dot(xn, win_ref[:, o3:o3 + D_LRU], preferred_element_type=F32)
    gr = jnp.dot(xn, win_ref[:, o3 + D_LRU:o5], preferred_element_type=F32)
    hqp = jnp.dot(xn, win_ref[:, o5:o5 + Q_W], preferred_element_type=F32)
    hkp = jnp.dot(xn, win_ref[:, o5 + Q_W:o5 + Q_W + KV_W], preferred_element_type=F32)
    ones_bd = _head_ones()
    norm_c = math.sqrt(HEAD_DIM)
    q_cos = cos_t * (qg_ref[0:1, :] * (norm_c * HEAD_DIM ** -0.5))
    q_sin = sin_t * (qg_ref[1:2, :] * (norm_c * HEAD_DIM ** -0.5))
    for n in range(Q_W // LANES):
        lanes = slice(n * LANES, (n + 1) * LANES)
        q_scr[:, lanes] = _norm_rope(hq[:, lanes], hqp[:, lanes], q_cos, q_sin, ones_bd)
    k_scr[...] = _norm_rope(hkv[:, 0:KV_W], hkp, cos_t * (kg_ref[0:1, :] * norm_c),
                            sin_t * (kg_ref[1:2, :] * norm_c), ones_bd)
    v_scr[...] = hkv[:, KV_W:2 * KV_W]

    band = BAND_BLOCKS * KEY_BLOCK
    col = _lane_iota((2 * t_len, band))
    col_mask = jnp.logical_or(col < 2 * KEY_BLOCK, (col % CHUNK) < t_len)
    ones_band = _ones_diag(band)
    pad = jnp.zeros((CHUNK - t_len, LANES), F32)

    sink_rows = [jnp.concatenate(
        [jnp.broadcast_to(sink_ref[2 * j:2 * j + 1, :], (t_len, LANES)),
         jnp.broadcast_to(sink_ref[2 * j + 1:2 * j + 2, :], (t_len, LANES))], axis=0) for j in range(N_KV_HEADS)]

    def group_body(gi, carry):
        work = []
        for s in range(SAMPLE_GROUP):
            b = gi * SAMPLE_GROUP + s
            r0 = pl.multiple_of(b * t_len, t_len)
            kc = kc_ref[b]
            vc = vc_ref[b]
            knew = k_scr[pl.ds(r0, t_len), :]
            vnew = v_scr[pl.ds(r0, t_len), :]
            nk_ref[b, 0:SWA_ROWS - t_len, :] = kc[t_len:SWA_ROWS]
            nk_ref[b, SWA_ROWS - t_len:SWA_ROWS, :] = knew
            nv_ref[b, 0:SWA_ROWS - t_len, :] = vc[t_len:SWA_ROWS]
            nv_ref[b, SWA_ROWS - t_len:SWA_ROWS, :] = vnew
            kfull = jnp.concatenate([kc, knew, pad], axis=0)
            vfull = jnp.concatenate([vc, vnew, pad], axis=0)
            for j in range(N_KV_HEADS):
                ktop, kbot = _split_diag(_dup_head(kfull, j))
                vtop, vbot = _split_diag(_dup_head(vfull, j))
                kparts, vparts = [], []
                for c in range(BAND_BLOCKS):
                    rs = slice(c * CHUNK, (c + 1) * CHUNK)
                    kparts += [ktop[rs], kbot[rs]]
                    vparts += [vtop[rs], vbot[rs]]
                kb = jnp.concatenate(kparts, axis=0).astype(BF16)
                vb = jnp.concatenate([jnp.concatenate(vparts, axis=0).astype(BF16), ones_band], axis=1)
                c0 = 2 * j * LANES
                qs = jnp.concatenate([q_scr[pl.ds(r0, t_len), c0:c0 + LANES],
                                      q_scr[pl.ds(r0, t_len), c0 + LANES:c0 + 2 * LANES]], axis=0).astype(BF16)
                work.append((r0, j, vb, _attend_scores(qs, kb, col_mask)))
        weights = [_attend_weights(s, sink_rows[j]) for _, j, _, s in work]
        for (r0, j, vb, _), (e, m) in zip(work, weights):
            o = _attend_output(e, m, vb, sink_rows[j])
            c0 = 2 * j * LANES
            mix_scr[pl.ds(r0, t_len), c0:c0 + LANES] = o[0:t_len]
            mix_scr[pl.ds(r0, t_len), c0 + LANES:c0 + 2 * LANES] = o[t_len:2 * t_len]
        return carry

    lax.fori_loop(0, nseq // SAMPLE_GROUP, group_body, 0)

    tmod =_row_iota((rows_all, LANES)) % t_len
    tmod_w = _row_iota((rows_all, D_LRU)) % t_len
    xprev = xprev_ref[...]
    xs = []
    for i in range(CONV_W - 1):
        k = CONV_W - 1 - i
        own = pltpu.roll(xr, k, 0)
        cached = xprev if k == CONV_W - 1 else pltpu.roll(xprev, rows_all - (CONV_W - 1 - k), 0)
        xs.append(jnp.where(tmod_w >= k, own, cached))
    xs.append(xr)
    rec, h = _lru_branch(xs, gr, h0_ref[...], t_len, tmod, convw_ref, convb_ref, wg_ref, ba_ref, bx_ref, lam_ref)
    mix_scr[:, Q_W:Q_W + D_LRU] = rec
    xr_ref[...] = xr
    h_ref[...] = h
    y_ref[...] = x + jnp.dot(mix_scr[...].astype(BF16), wout_ref[...], preferred_element_type=F32)


def _even_sample(x2, cos_t, sin_t, kc, vc, h0_rows, xprev, p):
    rows_all = x2.shape[0]
    nseq = kc.shape[0]
    consts = [p['g'], p['win'], p['qg'], p['kg'], p['sink'], p['convw'], p['convb'], p['wg'], p['ba'], p['bx'],
              p['lam'], p['wout']]
    ins = [x2, cos_t, sin_t, kc, vc, h0_rows, xprev] + consts
    out_shape = [
        jax.ShapeDtypeStruct((rows_all, D_MODEL), F32),
        jax.ShapeDtypeStruct((nseq, SWA_ROWS, KV_W), F32),
        jax.ShapeDtypeStruct((nseq, SWA_ROWS, KV_W), F32),
        jax.ShapeDtypeStruct((rows_all, D_LRU), F32),
        jax.ShapeDtypeStruct((rows_all, D_LRU), F32),
    ]
    return pl.pallas_call(
        _even_sample_kernel,
        out_shape=out_shape,
        grid=(1,),
        in_specs=[_const_spec(a.shape) for a in ins],
        out_specs=[pl.BlockSpec(s.shape, lambda i, n=len(s.shape): (0,) * n) for s in out_shape],
        scratch_shapes=[pltpu.VMEM((rows_all, Q_W), F32),
                        pltpu.VMEM((rows_all, KV_W), F32),
                        pltpu.VMEM((rows_all, KV_W), F32),
                        pltpu.VMEM((rows_all, Q_W + D_LRU), F32)],
        compiler_params=pltpu.CompilerParams(dimension_semantics=("arbitrary",), vmem_limit_bytes=VMEM_LIMIT),
        name="even_mixer_sample",
    )(*ins)


def _ffn_rows(x, g_ref, w1_ref, w2_ref):
    xn = _rms_rows(x, g_ref[...]).astype(BF16)
    out = None
    step = D_MODEL
    for c in range(D_FF // step):
        hcol = jnp.dot(xn, w1_ref[:, c * step:(c + 1) * step].astype(BF16), preferred_element_type=F32)
        hcol = jnp.square(jnp.maximum(hcol, 0.0)).astype(BF16)
        part = jnp.dot(hcol, w2_ref[c * step:(c + 1) * step, :].astype(BF16), preferred_element_type=F32)
        out = part if out is None else out + part
    return x + out


def _ffn_kernel(xp_ref, xs_ref, g_ref, w1_ref, w2_ref, yp_ref, ys_ref):
    i = pl.program_id(0)
    yp_ref[...] = _ffn_rows(jnp.where(i == 0, xs_ref[...], xp_ref[...]), g_ref, w1_ref, w2_ref)

    @pl.when(i == 0)
    def _():
        ys_ref[...] = yp_ref[...]


def _sample_then_tiles_specs(tile, width):
    prompt = pl.BlockSpec((tile, width), lambda i: (jnp.maximum(i - 1, 0), 0))
    sample = pl.BlockSpec((tile, width), lambda i: (0, 0))
    return prompt, sample


def _ffn(xp, xs, g, w1, w2, layer):
    tile = xs.shape[0]
    n_tiles = xp.shape[0] // tile
    xp_spec, xs_spec = _sample_then_tiles_specs(tile, D_MODEL)
    layer_spec = lambda shape: pl.BlockSpec((None,) + shape[1:], lambda i: (layer, 0, 0),
                                            pipeline_mode=pl.Buffered(1))
    return pl.pallas_call(
        _ffn_kernel,
        out_shape=[jax.ShapeDtypeStruct(xp.shape, F32), jax.ShapeDtypeStruct(xs.shape, F32)],
        grid=(n_tiles + 1,),
        in_specs=[xp_spec, xs_spec, _const_spec(g.shape), layer_spec(w1.shape), layer_spec(w2.shape)],
        out_specs=list(_sample_then_tiles_specs(tile, D_MODEL)),
        compiler_params=pltpu.CompilerParams(dimension_semantics=("arbitrary",), vmem_limit_bytes=VMEM_LIMIT),
        name="channel_mlp",
    )(xp, xs, g, w1, w2)


def _gmlp_kernel(seg_s, xp_ref, xs_ref, g_ref, win_ref, vg_ref, wsp_ref, wss_ref, biasp_ref, biass_ref, wout_ref,
                 g2_ref, w1_ref, w2_ref, yp_ref, ys_ref, v_ref, v_scr):
    i = pl.program_id(0)
    is_s = i == 0
    tile = xp_ref.shape[0]
    sub = GMLP_SUBTILE
    n_sub = tile // sub
    xs_in, proj = [], []
    for u in range(n_sub):
        rows = slice(u * sub, (u + 1) * sub)
        x = jnp.where(is_s, xs_ref[rows, :], xp_ref[rows, :])
        xn = _rms_rows(x, g_ref[...]).astype(BF16)
        xs_in.append(x)
        proj.append((jnp.dot(xn, win_ref[:, D_C:2 * D_C], preferred_element_type=F32),
                     jnp.dot(xn, win_ref[:, 0:D_C], preferred_element_type=F32)))
    r = _row_iota((CHUNK_MLP, CHUNK_MLP))
    col = _lane_iota((CHUNK_MLP, CHUNK_MLP))
    keep_p = col <= r
    keep_s = jnp.logical_and((r // seg_s) == (col // seg_s), keep_p)
    ws = [jnp.where(is_s, jnp.where(keep_s, wss_ref[gi], 0.0), jnp.where(keep_p, wsp_ref[gi], 0.0)).astype(BF16)
          for gi in range(C_GROUPS)]
    step = MXU_WIDTH
    per_step = step // LANES
    for u in range(n_sub):
        rows = slice(u * sub, (u + 1) * sub)
        zv = _gelu(proj[u][0])
        zc = zv - jnp.mean(zv, axis=-1, keepdims=True)
        v = zc * lax.rsqrt(jnp.mean(zc * zc, axis=-1, keepdims=True) + EPS) * vg_ref[...]
        v_scr[rows, :] = v
        vb = v.astype(BF16)
        sv = []
        for gi in range(C_GROUPS):
            lanes = slice(gi * LANES, (gi + 1) * LANES)
            bias = jnp.where(is_s, biass_ref[:, lanes], biasp_ref[:, lanes])
            blocks = [jnp.dot(ws[gi], vb[ch * CHUNK_MLP:(ch + 1) * CHUNK_MLP, lanes],
                              preferred_element_type=F32) + bias for ch in range(sub // CHUNK_MLP)]
            sv.append(jnp.concatenate(blocks, axis=0))
        acc = xs_in[u]
        for c in range(D_C // step):
            cols = slice(c * step, (c + 1) * step)
            gate = (_gelu(proj[u][1][:, cols]) *
                    jnp.concatenate(sv[c * per_step:(c + 1) * per_step], axis=1)).astype(BF16)
            acc = acc + jnp.dot(gate, wout_ref[cols, :], preferred_element_type=F32)
        yp_ref[rows, :] = _ffn_rows(acc, g2_ref, w1_ref, w2_ref)

    @pl.when(is_s)
    def _():
        ys_ref[...] = yp_ref[...]
        v_ref[...] = v_scr[...]


def _gmlp(xp, xs, seg_s, g, win, vg, ws_p, ws_s, bias_p, bias_s, wout, g2, w1, w2):
    tile = xs.shape[0]
    n_tiles = xp.shape[0] // tile
    xp_spec, xs_spec = _sample_then_tiles_specs(tile, D_MODEL)
    consts = (g, win, vg, ws_p, ws_s, bias_p, bias_s, wout, g2, w1, w2)
    return pl.pallas_call(
        functools.partial(_gmlp_kernel, seg_s),
        out_shape=[jax.ShapeDtypeStruct(xp.shape, F32), jax.ShapeDtypeStruct(xs.shape, F32),
                   jax.ShapeDtypeStruct((tile, D_C), F32)],
        grid=(n_tiles + 1,),
        in_specs=[xp_spec, xs_spec] + [_const_spec(a.shape) for a in consts],
        out_specs=list(_sample_then_tiles_specs(tile, D_MODEL)) + [pl.BlockSpec((tile, D_C), lambda i: (0, 0))],
        scratch_shapes=[pltpu.VMEM((tile, D_C), F32)],
        compiler_params=pltpu.CompilerParams(dimension_semantics=("arbitrary",), vmem_limit_bytes=VMEM_LIMIT),
        name="gmlp_mixer",
    )(xp, xs, *consts)


def _rope_inv():
    half = HEAD_DIM // 2
    return ROPE_THETA ** (-jnp.arange(half, dtype=F32) / half)


def _rope_rows():
    inv = _rope_inv()
    sgn = jnp.ones((HEAD_DIM // 2,), F32)
    return jnp.concatenate([inv] * 4)[None, :], jnp.concatenate([-sgn, sgn, -sgn, sgn])[None, :]


def _rope_tables(pos):
    inv = _rope_inv()
    ang = pos.astype(F32)[:, None] * inv[None, :]
    cos = jnp.cos(ang)
    sin = jnp.sin(ang)
    return jnp.concatenate([cos] * 4, axis=1), jnp.concatenate([-sin, sin, -sin, sin], axis=1)


def _swap_head_halves(a):
    half = HEAD_DIM // 2
    return a.reshape(a.shape[:-1] + (-1, 2, half))[..., ::-1, :].reshape(a.shape)


def _gate_weights(wa, wx):
    nb = LRU_BLOCKS // 2
    eye = jnp.eye(nb, dtype=bool)

    def half_diag(w, hh):
        blocks = w[hh * nb:(hh + 1) * nb]
        return jnp.where(eye[:, None, :, None], blocks[:, :, None, :], 0.0).reshape(nb * LRU_BW, nb * LRU_BW)
    return jnp.stack([jnp.concatenate([half_diag(wa, hh), half_diag(wx, hh)], axis=1)
                      for hh in range(2)]).astype(BF16)


def kernel(x_prompt, x_sample, cache_swa_k, cache_swa_v, state_lru_h, state_lru_conv, e_norm_g, e_w_in, e_q_norm_g, e_k_norm_g, e_sinks, e_conv_w, e_conv_b, e_gate_a_w, e_gate_a_b, e_gate_x_w, e_gate_x_b, e_lru_lambda, e_w_out, o_norm_g, o_w_in, o_v_norm_g, o_spatial_w, o_spatial_b, o_w_out, ffn_norm_g, ffn_w1, ffn_w2):
    bsz, s_len, _ = x_prompt.shape
    nseq, t_len, _ = x_sample.shape
    past_len = PAST_LEN
    row = lambda a: a.reshape(1, -1)

    e = 0
    p = {
        'g': row(e_norm_g[e]),
        'win': jnp.concatenate([e_w_in[e], _swap_head_halves(e_w_in[e][:, :Q_W + KV_W])], axis=1).astype(BF16),
        'qg': jnp.tile(jnp.stack([e_q_norm_g[e], _swap_head_halves(e_q_norm_g[e])]), (1, LANES // HEAD_DIM)),
        'kg': jnp.tile(jnp.stack([e_k_norm_g[e], _swap_head_halves(e_k_norm_g[e])]), (1, LANES // HEAD_DIM)),
        'sink': jnp.repeat(e_sinks[e], HEAD_DIM).reshape(N_Q_HEADS // 2, LANES),
        'convw': e_conv_w[e],
        'convb': row(e_conv_b[e]),
        'wg': _gate_weights(e_gate_a_w[e], e_gate_x_w[e]),
        'ba': row(e_gate_a_b[e]),
        'bx': row(e_gate_x_b[e]),
        'lam': row(e_lru_lambda[e]),
        'wout': e_w_out[e].astype(BF16),
    }
    yp, nk_p, nv_p, nh_p, nc_p = _even_prompt(x_prompt, p)

    cos_s, sin_s = _rope_tables(past_len + jnp.arange(t_len))
    cos_s = jnp.tile(cos_s, (nseq, 1))
    sin_s = jnp.tile(sin_s, (nseq, 1))
    kc = cache_swa_k[e].reshape(nseq, SWA_ROWS, KV_W)
    vc = cache_swa_v[e].reshape(nseq, SWA_ROWS, KV_W)
    h0_rows = jnp.repeat(state_lru_h[e], t_len, axis=0)
    xprev = jnp.pad(state_lru_conv[e], ((0, 0), (0, t_len - (CONV_W - 1)), (0, 0))).reshape(nseq * t_len, D_LRU)
    ys, nk_s, nv_s, h_s, xr_s = _even_sample(x_sample.reshape(nseq * t_len, D_MODEL), cos_s, sin_s, kc, vc,
                                             h0_rows, xprev, p)
    nh_s = h_s.reshape(nseq, t_len, D_LRU)[:, -1]
    nc_s = xr_s.reshape(nseq, t_len, D_LRU)[:, t_len - (CONV_W - 1):]

    w1, w2 = ffn_w1, ffn_w2
    yp, ys = _ffn(yp.reshape(bsz * s_len, D_MODEL), ys, row(ffn_norm_g[0]), w1, w2, 0)

    o = 0
    bias_p = jnp.repeat(o_spatial_b[o].T, LANES, axis=1)
    reps = CHUNK_MLP // t_len
    ws_s = jnp.tile(o_spatial_w[o][:, :t_len, :t_len], (1, reps, reps))
    bias_s = jnp.tile(jnp.repeat(o_spatial_b[o][:, :t_len].T, LANES, axis=1), (reps, 1))
    yp, ys, gv_s = _gmlp(yp, ys, t_len, row(o_norm_g[o]), o_w_in[o].astype(BF16), row(o_v_norm_g[o]),
                         o_spatial_w[o], ws_s, bias_p, bias_s, o_w_out[o].astype(BF16),
                         row(ffn_norm_g[1]), ffn_w1[1].astype(BF16), ffn_w2[1].astype(BF16))

    kv_shape = (1, -1, SWA_ROWS, N_KV_HEADS, HEAD_DIM)
    return (yp.reshape(bsz, s_len, D_MODEL), ys.reshape(nseq, t_len, D_MODEL),
            nk_p.reshape(kv_shape), nv_p.reshape(kv_shape), nh_p.reshape(1, bsz, D_LRU), nc_p[None],
            nk_s.reshape(kv_shape), nv_s.reshape(kv_shape), nh_s[None], nc_s[None],
            gv_s.reshape(1, nseq, t_len, D_C))
```

```python
import functools
import math

import jax
import jax.numpy as jnp
from jax import lax
from jax.experimental import pallas as pl
from jax.experimental.pallas import tpu as pltpu

F32 = jnp.float32
BF16 = jnp.bfloat16

D_MODEL = 1024
CHUNK = 64
HEAD_DIM = 64
N_Q_HEADS = 8
N_KV_HEADS = 2
SWA_ROWS = 128
PAST_LEN = 4096
ROPE_THETA = 10000.0
NEG = -1e30
D_LRU = 512
LRU_BLOCKS = 8
LRU_BW = D_LRU // LRU_BLOCKS
CONV_W = 4
LRU_C = 8.0
Q_W = N_Q_HEADS * HEAD_DIM
KV_W = N_KV_HEADS * HEAD_DIM
CHUNK_MLP = 128
D_C = D_MODEL
C_GROUPS = 8
D_FF = 4 * D_MODEL
EPS = 1e-6

LANES = 128
SUBLANES = 8
MXU_WIDTH = 256
KEY_BLOCK = 2 * CHUNK
BAND_BLOCKS = 3
VMEM_LIMIT = 56 * 1024 * 1024

PROMPT_TILE = 1024
PROMPT_SUBTILE = 256
ROW_TILE = 512
GMLP_SUBTILE = 512
SAMPLE_GROUP = 8


def _gelu(x):
    k1 = -2.0 * math.sqrt(2.0 / math.pi) * math.log2(math.e)
    return x / (1.0 + jnp.exp2(x * (k1 + (k1 * 0.044715) * (x * x))))


def _rms_rows(x, g):
    return x * lax.rsqrt(jnp.mean(x * x, axis=-1, keepdims=True) + EPS) * g


def _lane_iota(shape):
    return lax.broadcasted_iota(jnp.int32, shape, 1)


def _row_iota(shape):
    return lax.broadcasted_iota(jnp.int32, shape, 0)


def _head_ones():
    return ((_row_iota((LANES, LANES)) // HEAD_DIM) == (_lane_iota((LANES, LANES)) // HEAD_DIM)).astype(BF16)


def _norm_rope(x, xp, g_cos, g_sin, ones_bd):
    ss = jnp.dot((x * x).astype(BF16), ones_bd, preferred_element_type=F32)
    return (x * g_cos + xp * g_sin) * lax.rsqrt(ss + HEAD_DIM * EPS)


def _dup_head(t, j):
    lo = _lane_iota(t.shape) < HEAD_DIM
    swapped = pltpu.roll(t, HEAD_DIM, 1)
    return jnp.where(lo, t, swapped) if j == 0 else jnp.where(lo, swapped, t)


def _split_diag(t):
    lo = _lane_iota(t.shape) < HEAD_DIM
    return jnp.where(lo, t, 0.0), jnp.where(lo, 0.0, t)


def _ones_diag(rows):
    r = _row_iota((rows, LANES))
    return (((r % KEY_BLOCK) // CHUNK) == (_lane_iota((rows, LANES)) // HEAD_DIM)).astype(BF16)


def _attend_scores(qs, kb, col_mask):
    s = lax.dot_general(qs, kb, (((1,), (1,)), ((), ())), preferred_element_type=F32)
    return s if col_mask is None else jnp.where(col_mask, s, NEG)


def _attend_weights(s, sink_rows):
    mx = jnp.maximum(jnp.maximum(s[:, :LANES], s[:, LANES:2 * LANES]), s[:, 2 * LANES:])
    lo = _lane_iota(mx.shape) < HEAD_DIM
    m0 = jnp.max(jnp.where(lo, mx, -jnp.inf), axis=1, keepdims=True)
    m1 = jnp.max(jnp.where(lo, -jnp.inf, mx), axis=1, keepdims=True)
    m = jnp.maximum(jnp.where(lo, m0, m1), sink_rows)
    e = jnp.exp(s - jnp.concatenate([m, m, m], axis=1))
    return e.astype(BF16), m


def _attend_output(e, m, vb, sink_rows):
    od = jnp.dot(e, vb, preferred_element_type=F32)
    den = od[:, LANES:] + jnp.exp(sink_rows - m)
    return od[:, :LANES] / den


def _lru_scan(a, b, seg, tmod):
    d = 1
    while d < seg:
        ok = tmod >= d
        b = jnp.where(ok, a * pltpu.roll(b, d, 0), 0.0) + b
        if 2 * d < seg:
            a = jnp.where(ok, a * pltpu.roll(a, d, 0), a)
        d *= 2
    return b


def _lru_scan_tile(a, b, h_prev):
    rows = a.shape[0]
    groups = rows // SUBLANES
    a3 = a.reshape(groups, SUBLANES, LANES)
    b3 = b.reshape(groups, SUBLANES, LANES)
    sub = lax.broadcasted_iota(jnp.int32, a3.shape, 1)
    d = 1
    while d < SUBLANES:
        ok = sub >= d
        b3 = jnp.where(ok, a3 * pltpu.roll(b3, d, 1), 0.0) + b3
        a3 = jnp.where(ok, a3 * pltpu.roll(a3, d, 1), a3)
        d *= 2
    carry = jnp.broadcast_to(h_prev, (SUBLANES, LANES))
    out = []
    for g in range(groups):
        hg = a3[g] * carry + b3[g]
        out.append(hg)
        carry = jnp.broadcast_to(hg[SUBLANES - 1:SUBLANES, :], (SUBLANES, LANES))
    return jnp.concatenate(out, axis=0)


def _lru_branch(xs, gr, h_in, seg, tmod, convw_ref, convb_ref, wg_ref, ba_ref, bx_ref, lam_ref):
    xc, gates = _lru_gates(xs, convw_ref, convb_ref, wg_ref)
    parts = [_lru_piece(k, xc, gates, gr, h_in, seg, tmod, ba_ref, bx_ref, lam_ref) for k in range(D_LRU // LANES)]
    return jnp.concatenate([p[0] for p in parts], axis=1), jnp.concatenate([p[1] for p in parts], axis=1)


def _lru_gates(xs, convw_ref, convb_ref, wg_ref):
    xc = xs[0] * convw_ref[0:1, :] + convb_ref[...]
    for i in range(1, CONV_W):
        xc = xc + xs[i] * convw_ref[i:i + 1, :]
    xcb = xc.astype(BF16)
    half = D_LRU // 2
    return xc, [jnp.dot(xcb[:, hh * half:(hh + 1) * half], wg_ref[hh], preferred_element_type=F32)
                for hh in range(2)]


def _lru_piece(k, xc, gates, gr, h_in, seg, tmod, ba_ref, bx_ref, lam_ref):
    half = D_LRU // 2
    hh, tt = divmod(k, half // LANES)
    sl = slice(k * LANES, (k + 1) * LANES)
    g = gates[hh]
    r = jax.nn.sigmoid(g[:, tt * LANES:(tt + 1) * LANES] + ba_ref[:, sl])
    gi = jax.nn.sigmoid(g[:, half + tt * LANES:half + (tt + 1) * LANES] + bx_ref[:, sl])
    log_a = r * (-LRU_C * jax.nn.softplus(-lam_ref[:, sl]))
    a = jnp.exp(log_a)
    b = jnp.sqrt(-jnp.tanh(log_a) * (a * a + 1.0)) * (gi * xc[:, sl])
    if seg is None:
        h = _lru_scan_tile(a, b, h_in[:, sl])
    else:
        h = _lru_scan(a, b + jnp.where(tmod == 0, a * h_in[:, sl], 0.0), seg, tmod)
    return h * _gelu(gr[:, sl]), h


def _even_prompt_kernel(x_ref, cos_ref, sin_ref, inv_ref, sgn_ref, g_ref, win_ref, qg_ref, kg_ref, sink_ref, convw_ref, convb_ref,
                        wg_ref, ba_ref, bx_ref, lam_ref, wout_ref,
                        y_ref, nk_ref, nv_ref, nh_ref, nc_ref,
                        kb_scr, vb_scr, xr_scr, h_scr, mix_scr):
    t = pl.program_id(1)
    nt = pl.num_programs(1)
    tile = x_ref.shape[1]
    n_chunks = tile // CHUNK
    hist = 2 * KEY_BLOCK
    slot = t % 2

    @pl.when(t == 0)
    def _():
        ones = _ones_diag(vb_scr.shape[2])
        for j in range(N_KV_HEADS):
            kb_scr[0, j, 0:hist, :] = jnp.zeros((hist, LANES), BF16)
            vb_scr[0, j, 0:hist, 0:LANES] = jnp.zeros((hist, LANES), BF16)
            for s in range(2):
                vb_scr[s, j, :, LANES:2 * LANES] = ones
        xr_scr[0:8, :] = jnp.zeros((8, D_LRU), F32)
        h_scr[...] = jnp.zeros(h_scr.shape, F32)

    sub = PROMPT_SUBTILE
    n_sub = tile // sub
    sub_chunks = sub // CHUNK
    o3 = Q_W + 2 * KV_W
    xs_in, proj = [], []
    for u in range(n_sub):
        x = x_ref[0, u * sub:(u + 1) * sub, :]
        xn = _rms_rows(x, g_ref[...]).astype(BF16)
        xs_in.append(x)
        o5 = o3 + 2 * D_LRU
        proj.append((jnp.dot(xn, win_ref[:, 0:Q_W], preferred_element_type=F32),
                     jnp.dot(xn, win_ref[:, Q_W:o3], preferred_element_type=F32),
                     jnp.dot(xn, win_ref[:, o3:o3 + D_LRU], preferred_element_type=F32),
                     jnp.dot(xn, win_ref[:, o3 + D_LRU:o5], preferred_element_type=F32),
                     jnp.dot(xn, win_ref[:, o5:o5 + Q_W], preferred_element_type=F32),
                     jnp.dot(xn, win_ref[:, o5 + Q_W:o5 + Q_W + KV_W], preferred_element_type=F32)))
    for u in range(n_sub):
        xr_scr[8 + u * sub:8 + (u + 1) * sub, :] = proj[u][2]

    base = (t * tile).astype(F32) * inv_ref[...]
    cos_b = jnp.cos(base)
    sin_b = jnp.sin(base) * sgn_ref[...]

    band = BAND_BLOCKS * KEY_BLOCK
    col = _lane_iota((2 * CHUNK, band))
    sink_rows = [jnp.concatenate(
        [jnp.broadcast_to(sink_ref[2 * j:2 * j + 1, :], (CHUNK, LANES)),
         jnp.broadcast_to(sink_ref[2 * j + 1:2 * j + 2, :], (CHUNK, LANES))], axis=0) for j in range(N_KV_HEADS)]
    ones_bd = _head_ones()
    norm_c = math.sqrt(HEAD_DIM)
    q_all = []
    for u in range(n_sub):
        hq, hkv, xr, gr, hqp, hkp = proj[u]
        cos_l = cos_ref[u * sub:(u + 1) * sub, :]
        sin_l = sin_ref[u * sub:(u + 1) * sub, :]
        cos_t = cos_b * cos_l - sin_b * sin_l
        sin_t = sin_b * cos_l + cos_b * sin_l
        q_cos = cos_t * (qg_ref[0:1, :] * (norm_c * HEAD_DIM ** -0.5))
        q_sin = sin_t * (qg_ref[1:2, :] * (norm_c * HEAD_DIM ** -0.5))
        q_tiles = [_norm_rope(hq[:, n * LANES:(n + 1) * LANES], hqp[:, n * LANES:(n + 1) * LANES],
                              q_cos, q_sin, ones_bd).astype(BF16) for n in range(Q_W // LANES)]
        kr = _norm_rope(hkv[:, 0:KV_W], hkp, cos_t * (kg_ref[0:1, :] * norm_c), sin_t * (kg_ref[1:2, :] * norm_c),
                        ones_bd)
        vv = hkv[:, KV_W:2 * KV_W]
        for j in range(N_KV_HEADS):
            ktop, kbot = _split_diag(_dup_head(kr, j))
            vtop, vbot = _split_diag(_dup_head(vv, j))
            for i in range(sub_chunks):
                r0 = hist + (u * sub_chunks + i) * KEY_BLOCK
                rows = slice(i * CHUNK, (i + 1) * CHUNK)
                kb_scr[slot, j, r0:r0 + CHUNK, :] = ktop[rows].astype(BF16)
                kb_scr[slot, j, r0 + CHUNK:r0 + KEY_BLOCK, :] = kbot[rows].astype(BF16)
                vb_scr[slot, j, r0:r0 + CHUNK, 0:LANES] = vtop[rows].astype(BF16)
                vb_scr[slot, j, r0 + CHUNK:r0 + KEY_BLOCK, 0:LANES] = vbot[rows].astype(BF16)
        q_all.append(q_tiles)

    pairs = [(i, j) for i in range(sub_chunks) for j in range(N_KV_HEADS)]
    scores_all, gates_all = [], []
    for u in range(n_sub):
        q_tiles = q_all[u]
        scores = []
        for i, j in pairs:
            rows = slice(i * CHUNK, (i + 1) * CHUNK)
            ci = u * sub_chunks + i
            col_mask = None
            if ci < BAND_BLOCKS - 1:
                col_mask = col >= (1 - jnp.minimum(t, 1)) * ((BAND_BLOCKS - 1 - ci) * KEY_BLOCK)
            qs = jnp.concatenate([q_tiles[2 * j][rows], q_tiles[2 * j + 1][rows]], axis=0)
            scores.append(_attend_scores(qs, kb_scr[slot, j, ci * KEY_BLOCK:ci * KEY_BLOCK + band, :], col_mask))
        scores_all.append(scores)
        first = 8 + u * sub
        taps = [xr_scr[first - (CONV_W - 1 - i):first - (CONV_W - 1 - i) + sub, :] for i in range(CONV_W - 1)]
        gates_all.append(_lru_gates(taps + [proj[u][2]], convw_ref, convb_ref, wg_ref))

    h_carry = h_scr[0:1, :]
    for u in range(n_sub):
        xr, gr = proj[u][2], proj[u][3]
        scores = scores_all[u]
        xc, gates = gates_all[u]
        n_pieces = D_LRU // LANES
        per_piece = len(pairs) // n_pieces
        weights, lru = [], []
        for k in range(n_pieces):
            weights += [_attend_weights(scores[p], sink_rows[pairs[p][1]])
                        for p in range(k * per_piece, (k + 1) * per_piece)]
            lru.append(_lru_piece(k, xc, gates, gr, h_carry, None, None, ba_ref, bx_ref, lam_ref))
        for (e, m), (i, j) in zip(weights, pairs):
            ci = u * sub_chunks + i
            rows = slice(u * sub + i * CHUNK, u * sub + (i + 1) * CHUNK)
            o = _attend_output(e, m, vb_scr[slot, j, ci * KEY_BLOCK:ci * KEY_BLOCK + band, :], sink_rows[j])
            c0 = 2 * j * LANES
            mix_scr[rows, c0:c0 + LANES] = o[0:CHUNK].astype(BF16)
            mix_scr[rows, c0 + LANES:c0 + 2 * LANES] = o[CHUNK:2 * CHUNK].astype(BF16)

        h_carry = jnp.concatenate([h[sub - 1:sub, :] for _, h in lru], axis=1)
        rows = slice(u * sub, (u + 1) * sub)
        for k, (rec, _) in enumerate(lru):
            mix_scr[rows, Q_W + k * LANES:Q_W + (k + 1) * LANES] = rec.astype(BF16)

        y_att = jnp.dot(mix_scr[rows, 0:Q_W], wout_ref[0:Q_W, :], preferred_element_type=F32)
        y_rec = jnp.dot(mix_scr[rows, Q_W:Q_W + D_LRU], wout_ref[Q_W:Q_W + D_LRU, :], preferred_element_type=F32)
        y_ref[0, rows, :] = xs_in[u] + y_att + y_rec

    last = n_chunks * KEY_BLOCK
    for j in range(N_KV_HEADS):
        kb_scr[1 - slot, j, 0:hist, :] = kb_scr[slot, j, last:last + hist, :]
        vb_scr[1 - slot, j, 0:hist, 0:LANES] = vb_scr[slot, j, last:last + hist, 0:LANES]
    h_scr[0:1, :] = h_carry
    xr_scr[0:8, :] = xr[sub - 8:sub, :]

    @pl.when(t == nt - 1)
    def _():
        nk_ref[0] = kr[sub - SWA_ROWS:sub, :]
        nv_ref[0] = vv[sub - SWA_ROWS:sub, :]
        nh_ref[0] = h_carry
        nc_ref[0] = xr[sub - (CONV_W - 1):sub, :]


def _const_spec(shape):
    zeros = (0,) * len(shape)
    return pl.BlockSpec(shape, lambda *_: zeros, pipeline_mode=pl.Buffered(1))


def _even_prompt(x, p):
    bsz, s_len, _ = x.shape
    tile = PROMPT_TILE
    cos_l, sin_l = _rope_tables(jnp.arange(tile))
    inv_row, sgn_row = _rope_rows()
    nt = s_len // tile
    key_rows = (tile // CHUNK + 2) * KEY_BLOCK
    consts = [p['g'], p['win'], p['qg'], p['kg'], p['sink'], p['convw'], p['convb'], p['wg'], p['ba'], p['bx'],
              p['lam'], p['wout']]
    out_shape = [
        jax.ShapeDtypeStruct((bsz, s_len, D_MODEL), F32),
        jax.ShapeDtypeStruct((bsz, SWA_ROWS, KV_W), F32),
        jax.ShapeDtypeStruct((bsz, SWA_ROWS, KV_W), F32),
        jax.ShapeDtypeStruct((bsz, 1, D_LRU), F32),
        jax.ShapeDtypeStruct((bsz, CONV_W - 1, D_LRU), F32),
    ]
    return pl.pallas_call(
        _even_prompt_kernel,
        out_shape=out_shape,
        grid=(bsz, nt),
        in_specs=[pl.BlockSpec((1, tile, D_MODEL), lambda b, t: (b, t, 0)),
                  _const_spec(cos_l.shape), _const_spec(sin_l.shape), _const_spec(inv_row.shape),
                  _const_spec(sgn_row.shape)] + [_const_spec(c.shape) for c in consts],
        out_specs=[pl.BlockSpec((1, tile, D_MODEL), lambda b, t: (b, t, 0)),
                   pl.BlockSpec((1, SWA_ROWS, KV_W), lambda b, t: (b, 0, 0)),
                   pl.BlockSpec((1, SWA_ROWS, KV_W), lambda b, t: (b, 0, 0)),
                   pl.BlockSpec((1, 1, D_LRU), lambda b, t: (b, 0, 0)),
                   pl.BlockSpec((1, CONV_W - 1, D_LRU), lambda b, t: (b, 0, 0))],
        scratch_shapes=[pltpu.VMEM((2, N_KV_HEADS, key_rows, LANES), BF16),
                        pltpu.VMEM((2, N_KV_HEADS, key_rows, 2 * LANES), BF16),
                        pltpu.VMEM((tile + 8, D_LRU), F32),
                        pltpu.VMEM((8, D_LRU), F32),
                        pltpu.VMEM((tile, Q_W + D_LRU), BF16)],
        compiler_params=pltpu.CompilerParams(dimension_semantics=("arbitrary", "arbitrary"),
                                             vmem_limit_bytes=VMEM_LIMIT),
        name="even_mixer_prompt",
    )(x, cos_l, sin_l, inv_row, sgn_row, *consts)


def _even_sample_kernel(x_ref, cos_ref, sin_ref, kc_ref, vc_ref, h0_ref, xprev_ref,
                        g_ref, win_ref, qg_ref, kg_ref, sink_ref, convw_ref, convb_ref,
                        wg_ref, ba_ref, bx_ref, lam_ref, wout_ref,
                        y_ref, nk_ref, nv_ref, h_ref, xr_ref,
                        q_scr, k_scr, v_scr, mix_scr):
    rows_all = x_ref.shape[0]
    nseq = kc_ref.shape[0]
    t_len = rows_all // nseq
    x = x_ref[...]
    xn = _rms_rows(x, g_ref[...]).astype(BF16)
    cos_t = cos_ref[...]
    sin_t = sin_ref[...]

    o3 = Q_W + 2 * KV_W
    o5 = o3 + 2 * D_LRU
    hq = jnp.dot(xn, win_ref[:, 0:Q_W], preferred_element_type=F32)
    hkv = jnp.dot(xn, win_ref[:, Q_W:o3], preferred_element_type=F32)
    xr = jnp.dot(xn, win_ref[:, o3:o3 + D_LRU], preferred_element_type=F32)
    gr = jnp.dot(xn, win_ref[:, o3 + D_LRU:o5], preferred_element_type=F32)
    hqp = jnp.dot(xn, win_ref[:, o5:o5 + Q_W], preferred_element_type=F32)
    hkp = jnp.dot(xn, win_ref[:, o5 + Q_W:o5 + Q_W + KV_W], preferred_element_type=F32)
    ones_bd = _head_ones()
    norm_c = math.sqrt(HEAD_DIM)
    q_cos = cos_t * (qg_ref[0:1, :] * (norm_c * HEAD_DIM ** -0.5))
    q_sin = sin_t * (qg_ref[1:2, :] * (norm_c * HEAD_DIM ** -0.5))
    for n in range(Q_W // LANES):
        lanes = slice(n * LANES, (n + 1) * LANES)
        q_scr[:, lanes] = _norm_rope(hq[:, lanes], hqp[:, lanes], q_cos, q_sin, ones_bd)
    k_scr[...] = _norm_rope(hkv[:, 0:KV_W], hkp, cos_t * (kg_ref[0:1, :] * norm_c),
                            sin_t * (kg_ref[1:2, :] * norm_c), ones_bd)
    v_scr[...] = hkv[:, KV_W:2 * KV_W]

    band = BAND_BLOCKS * KEY_BLOCK
    col = _lane_iota((2 * t_len, band))
    col_mask = jnp.logical_or(col < 2 * KEY_BLOCK, (col % CHUNK) < t_len)
    ones_band = _ones_diag(band)
    pad = jnp.zeros((CHUNK - t_len, LANES), F32)

    sink_rows = [jnp.concatenate(
        [jnp.broadcast_to(sink_ref[2 * j:2 * j + 1, :], (t_len, LANES)),
         jnp.broadcast_to(sink_ref[2 * j + 1:2 * j + 2, :], (t_len, LANES))], axis=0) for j in range(N_KV_HEADS)]

    def group_body(gi, carry):
        work = []
        for s in range(SAMPLE_GROUP):
            b = gi * SAMPLE_GROUP + s
            r0 = pl.multiple_of(b * t_len, t_len)
            kc = kc_ref[b]
            vc = vc_ref[b]
            knew = k_scr[pl.ds(r0, t_len), :]
            vnew = v_scr[pl.ds(r0, t_len), :]
            nk_ref[b, 0:SWA_ROWS - t_len, :] = kc[t_len:SWA_ROWS]
            nk_ref[b, SWA_ROWS - t_len:SWA_ROWS, :] = knew
            nv_ref[b, 0:SWA_ROWS - t_len, :] = vc[t_len:SWA_ROWS]
            nv_ref[b, SWA_ROWS - t_len:SWA_ROWS, :] = vnew
            kfull = jnp.concatenate([kc, knew, pad], axis=0)
            vfull = jnp.concatenate([vc, vnew, pad], axis=0)
            for j in range(N_KV_HEADS):
                ktop, kbot = _split_diag(_dup_head(kfull, j))
                vtop, vbot = _split_diag(_dup_head(vfull, j))
                kparts, vparts = [], []
                for c in range(BAND_BLOCKS):
                    rs = slice(c * CHUNK, (c + 1) * CHUNK)
                    kparts += [ktop[rs], kbot[rs]]
                    vparts += [vtop[rs], vbot[rs]]
                kb = jnp.concatenate(kparts, axis=0).astype(BF16)
                vb = jnp.concatenate([jnp.concatenate(vparts, axis=0).astype(BF16), ones_band], axis=1)
                c0 = 2 * j * LANES
                qs = jnp.concatenate([q_scr[pl.ds(r0, t_len), c0:c0 + LANES],
                                      q_scr[pl.ds(r0, t_len), c0 + LANES:c0 + 2 * LANES]], axis=0).astype(BF16)
                work.append((r0, j, vb, _attend_scores(qs, kb, col_mask)))
        weights = [_attend_weights(s, sink_rows[j]) for _, j, _, s in work]
        for (r0, j, vb, _), (e, m) in zip(work, weights):
            o = _attend_output(e, m, vb, sink_rows[j])
            c0 = 2 * j * LANES
            mix_scr[pl.ds(r0, t_len), c0:c0 + LANES] = o[0:t_len]
            mix_scr[pl.ds(r0, t_len), c0 + LANES:c0 + 2 * LANES] = o[t_len:2 * t_len]
        return carry

    lax.fori_loop(0, nseq // SAMPLE_GROUP, group_body, 0)

    tmod =_row_iota((rows_all, LANES)) % t_len
    tmod_w = _row_iota((rows_all, D_LRU)) % t_len
    xprev = xprev_ref[...]
    xs = []
    for i in range(CONV_W - 1):
        k = CONV_W - 1 - i
        own = pltpu.roll(xr, k, 0)
        cached = xprev if k == CONV_W - 1 else pltpu.roll(xprev, rows_all - (CONV_W - 1 - k), 0)
        xs.append(jnp.where(tmod_w >= k, own, cached))
    xs.append(xr)
    rec, h = _lru_branch(xs, gr, h0_ref[...], t_len, tmod, convw_ref, convb_ref, wg_ref, ba_ref, bx_ref, lam_ref)
    mix_scr[:, Q_W:Q_W + D_LRU] = rec
    xr_ref[...] = xr
    h_ref[...] = h
    y_ref[...] = x + jnp.dot(mix_scr[...].astype(BF16), wout_ref[...], preferred_element_type=F32)


def _even_sample(x2, cos_t, sin_t, kc, vc, h0_rows, xprev, p):
    rows_all = x2.shape[0]
    nseq = kc.shape[0]
    consts = [p['g'], p['win'], p['qg'], p['kg'], p['sink'], p['convw'], p['convb'], p['wg'], p['ba'], p['bx'],
              p['lam'], p['wout']]
    ins = [x2, cos_t, sin_t, kc, vc, h0_rows, xprev] + consts
    out_shape = [
        jax.ShapeDtypeStruct((rows_all, D_MODEL), F32),
        jax.ShapeDtypeStruct((nseq, SWA_ROWS, KV_W), F32),
        jax.ShapeDtypeStruct((nseq, SWA_ROWS, KV_W), F32),
        jax.ShapeDtypeStruct((rows_all, D_LRU), F32),
        jax.ShapeDtypeStruct((rows_all, D_LRU), F32),
    ]
    return pl.pallas_call(
        _even_sample_kernel,
        out_shape=out_shape,
        grid=(1,),
        in_specs=[_const_spec(a.shape) for a in ins],
        out_specs=[pl.BlockSpec(s.shape, lambda i, n=len(s.shape): (0,) * n) for s in out_shape],
        scratch_shapes=[pltpu.VMEM((rows_all, Q_W), F32),
                        pltpu.VMEM((rows_all, KV_W), F32),
                        pltpu.VMEM((rows_all, KV_W), F32),
                        pltpu.VMEM((rows_all, Q_W + D_LRU), F32)],
        compiler_params=pltpu.CompilerParams(dimension_semantics=("arbitrary",), vmem_limit_bytes=VMEM_LIMIT),
        name="even_mixer_sample",
    )(*ins)


def _ffn_rows(x, g_ref, w1_ref, w2_ref):
    xn = _rms_rows(x, g_ref[...]).astype(BF16)
    out = None
    step = D_MODEL
    for c in range(D_FF // step):
        hcol = jnp.dot(xn, w1_ref[:, c * step:(c + 1) * step].astype(BF16), preferred_element_type=F32)
        hcol = jnp.square(jnp.maximum(hcol, 0.0)).astype(BF16)
        part = jnp.dot(hcol, w2_ref[c * step:(c + 1) * step, :].astype(BF16), preferred_element_type=F32)
        out = part if out is None else out + part
    return x + out


def _ffn_kernel(xp_ref, xs_ref, g_ref, w1_ref, w2_ref, yp_ref, ys_ref):
    i = pl.program_id(0)
    yp_ref[...] = _ffn_rows(jnp.where(i == 0, xs_ref[...], xp_ref[...]), g_ref, w1_ref, w2_ref)

    @pl.when(i == 0)
    def _():
        ys_ref[...] = yp_ref[...]


def _sample_then_tiles_specs(tile, width):
    prompt = pl.BlockSpec((tile, width), lambda i: (jnp.maximum(i - 1, 0), 0))
    sample = pl.BlockSpec((tile, width), lambda i: (0, 0))
    return prompt, sample


def _ffn(xp, xs, g, w1, w2, layer):
    tile = xs.shape[0]
    n_tiles = xp.shape[0] // tile
    xp_spec, xs_spec = _sample_then_tiles_specs(tile, D_MODEL)
    layer_spec = lambda shape: pl.BlockSpec((None,) + shape[1:], lambda i: (layer, 0, 0),
                                            pipeline_mode=pl.Buffered(1))
    return pl.pallas_call(
        _ffn_kernel,
        out_shape=[jax.ShapeDtypeStruct(xp.shape, F32), jax.ShapeDtypeStruct(xs.shape, F32)],
        grid=(n_tiles + 1,),
        in_specs=[xp_spec, xs_spec, _const_spec(g.shape), layer_spec(w1.shape), layer_spec(w2.shape)],
        out_specs=list(_sample_then_tiles_specs(tile, D_MODEL)),
        compiler_params=pltpu.CompilerParams(dimension_semantics=("arbitrary",), vmem_limit_bytes=VMEM_LIMIT),
        name="channel_mlp",
    )(xp, xs, g, w1, w2)


def _gmlp_kernel(seg_s, xp_ref, xs_ref, g_ref, win_ref, vg_ref, wsp_ref, wss_ref, biasp_ref, biass_ref, wout_ref,
                 yp_ref, ys_ref, v_ref, v_scr):
    i = pl.program_id(0)
    is_s = i == 0
    tile = xp_ref.shape[0]
    sub = GMLP_SUBTILE
    n_sub = tile // sub
    xs_in, proj = [], []
    for u in range(n_sub):
        rows = slice(u * sub, (u + 1) * sub)
        x = jnp.where(is_s, xs_ref[rows, :], xp_ref[rows, :])
        xn = _rms_rows(x, g_ref[...]).astype(BF16)
        xs_in.append(x)
        proj.append((jnp.dot(xn, win_ref[:, D_C:2 * D_C].astype(BF16), preferred_element_type=F32),
                     jnp.dot(xn, win_ref[:, 0:D_C].astype(BF16), preferred_element_type=F32)))
    r = _row_iota((CHUNK_MLP, CHUNK_MLP))
    col = _lane_iota((CHUNK_MLP, CHUNK_MLP))
    keep_p = col <= r
    keep_s = jnp.logical_and((r // seg_s) == (col // seg_s), keep_p)
    ws = [jnp.where(is_s, jnp.where(keep_s, wss_ref[gi], 0.0), jnp.where(keep_p, wsp_ref[gi], 0.0)).astype(BF16)
          for gi in range(C_GROUPS)]
    step = MXU_WIDTH
    per_step = step // LANES
    for u in range(n_sub):
        rows = slice(u * sub, (u + 1) * sub)
        zv = _gelu(proj[u][0])
        zc = zv - jnp.mean(zv, axis=-1, keepdims=True)
        v = zc * lax.rsqrt(jnp.mean(zc * zc, axis=-1, keepdims=True) + EPS) * vg_ref[...]
        v_scr[rows, :] = v
        vb = v.astype(BF16)
        sv = []
        for gi in range(C_GROUPS):
            lanes = slice(gi * LANES, (gi + 1) * LANES)
            bias = jnp.where(is_s, biass_ref[:, lanes], biasp_ref[:, lanes])
            blocks = [jnp.dot(ws[gi], vb[ch * CHUNK_MLP:(ch + 1) * CHUNK_MLP, lanes],
                              preferred_element_type=F32) + bias for ch in range(sub // CHUNK_MLP)]
            sv.append(jnp.concatenate(blocks, axis=0))
        acc = xs_in[u]
        for c in range(D_C // step):
            cols = slice(c * step, (c + 1) * step)
            gate = (_gelu(proj[u][1][:, cols]) *
                    jnp.concatenate(sv[c * per_step:(c + 1) * per_step], axis=1)).astype(BF16)
            acc = acc + jnp.dot(gate, wout_ref[cols, :].astype(BF16), preferred_element_type=F32)
        yp_ref[rows, :] = acc

    @pl.when(is_s)
    def _():
        ys_ref[...] = yp_ref[...]
        v_ref[...] = v_scr[...]


def _gmlp(xp, xs, seg_s, g, win, vg, ws_p, ws_s, bias_p, bias_s, wout):
    tile = xs.shape[0]
    n_tiles = xp.shape[0] // tile
    xp_spec, xs_spec = _sample_then_tiles_specs(tile, D_MODEL)
    consts = (g, win, vg, ws_p, ws_s, bias_p, bias_s, wout)
    return pl.pallas_call(
        functools.partial(_gmlp_kernel, seg_s),
        out_shape=[jax.ShapeDtypeStruct(xp.shape, F32), jax.ShapeDtypeStruct(xs.shape, F32),
                   jax.ShapeDtypeStruct((tile, D_C), F32)],
        grid=(n_tiles + 1,),
        in_specs=[xp_spec, xs_spec] + [_const_spec(a.shape) for a in consts],
        out_specs=list(_sample_then_tiles_specs(tile, D_MODEL)) + [pl.BlockSpec((tile, D_C), lambda i: (0, 0))],
        scratch_shapes=[pltpu.VMEM((tile, D_C), F32)],
        compiler_params=pltpu.CompilerParams(dimension_semantics=("arbitrary",), vmem_limit_bytes=VMEM_LIMIT),
        name="gmlp_mixer",
    )(xp, xs, *consts)


def _rope_inv():
    half = HEAD_DIM // 2
    return ROPE_THETA ** (-jnp.arange(half, dtype=F32) / half)


def _rope_rows():
    inv = _rope_inv()
    sgn = jnp.ones((HEAD_DIM // 2,), F32)
    return jnp.concatenate([inv] * 4)[None, :], jnp.concatenate([-sgn, sgn, -sgn, sgn])[None, :]


def _rope_tables(pos):
    inv = _rope_inv()
    ang = pos.astype(F32)[:, None] * inv[None, :]
    cos = jnp.cos(ang)
    sin = jnp.sin(ang)
    return jnp.concatenate([cos] * 4, axis=1), jnp.concatenate([-sin, sin, -sin, sin], axis=1)


def _swap_head_halves(a):
    half = HEAD_DIM // 2
    return a.reshape(a.shape[:-1] + (-1, 2, half))[..., ::-1, :].reshape(a.shape)


def _gate_weights(wa, wx):
    nb = LRU_BLOCKS // 2
    eye = jnp.eye(nb, dtype=bool)

    def half_diag(w, hh):
        blocks = w[hh * nb:(hh + 1) * nb]
        return jnp.where(eye[:, None, :, None], blocks[:, :, None, :], 0.0).reshape(nb * LRU_BW, nb * LRU_BW)
    return jnp.stack([jnp.concatenate([half_diag(wa, hh), half_diag(wx, hh)], axis=1)
                      for hh in range(2)]).astype(BF16)


def kernel(x_prompt, x_sample, cache_swa_k, cache_swa_v, state_lru_h, state_lru_conv, e_norm_g, e_w_in, e_q_norm_g, e_k_norm_g, e_sinks, e_conv_w, e_conv_b, e_gate_a_w, e_gate_a_b, e_gate_x_w, e_gate_x_b, e_lru_lambda, e_w_out, o_norm_g, o_w_in, o_v_norm_g, o_spatial_w, o_spatial_b, o_w_out, ffn_norm_g, ffn_w1, ffn_w2):
    bsz, s_len, _ = x_prompt.shape
    nseq, t_len, _ = x_sample.shape
    past_len = PAST_LEN
    row = lambda a: a.reshape(1, -1)

    e = 0
    p = {
        'g': row(e_norm_g[e]),
        'win': jnp.concatenate([e_w_in[e], _swap_head_halves(e_w_in[e][:, :Q_W + KV_W])], axis=1).astype(BF16),
        'qg': jnp.tile(jnp.stack([e_q_norm_g[e], _swap_head_halves(e_q_norm_g[e])]), (1, LANES // HEAD_DIM)),
        'kg': jnp.tile(jnp.stack([e_k_norm_g[e], _swap_head_halves(e_k_norm_g[e])]), (1, LANES // HEAD_DIM)),
        'sink': jnp.repeat(e_sinks[e], HEAD_DIM).reshape(N_Q_HEADS // 2, LANES),
        'convw': e_conv_w[e],
        'convb': row(e_conv_b[e]),
        'wg': _gate_weights(e_gate_a_w[e], e_gate_x_w[e]),
        'ba': row(e_gate_a_b[e]),
        'bx': row(e_gate_x_b[e]),
        'lam': row(e_lru_lambda[e]),
        'wout': e_w_out[e].astype(BF16),
    }
    yp, nk_p, nv_p, nh_p, nc_p = _even_prompt(x_prompt, p)

    cos_s, sin_s = _rope_tables(past_len + jnp.arange(t_len))
    cos_s = jnp.tile(cos_s, (nseq, 1))
    sin_s = jnp.tile(sin_s, (nseq, 1))
    kc = cache_swa_k[e].reshape(nseq, SWA_ROWS, KV_W)
    vc = cache_swa_v[e].reshape(nseq, SWA_ROWS, KV_W)
    h0_rows = jnp.repeat(state_lru_h[e], t_len, axis=0)
    xprev = jnp.pad(state_lru_conv[e], ((0, 0), (0, t_len - (CONV_W - 1)), (0, 0))).reshape(nseq * t_len, D_LRU)
    ys, nk_s, nv_s, h_s, xr_s = _even_sample(x_sample.reshape(nseq * t_len, D_MODEL), cos_s, sin_s, kc, vc,
                                             h0_rows, xprev, p)
    nh_s = h_s.reshape(nseq, t_len, D_LRU)[:, -1]
    nc_s = xr_s.reshape(nseq, t_len, D_LRU)[:, t_len - (CONV_W - 1):]

    w1, w2 = ffn_w1, ffn_w2
    yp, ys = _ffn(yp.reshape(bsz * s_len, D_MODEL), ys, row(ffn_norm_g[0]), w1, w2, 0)

    o = 0
    bias_p = jnp.repeat(o_spatial_b[o].T, LANES, axis=1)
    reps = CHUNK_MLP // t_len
    ws_s = jnp.tile(o_spatial_w[o][:, :t_len, :t_len], (1, reps, reps))
    bias_s = jnp.tile(jnp.repeat(o_spatial_b[o][:, :t_len].T, LANES, axis=1), (reps, 1))
    yp, ys, gv_s = _gmlp(yp, ys, t_len, row(o_norm_g[o]), o_w_in[o], row(o_v_norm_g[o]),
                         o_spatial_w[o], ws_s, bias_p, bias_s, o_w_out[o])

    yp, ys = _ffn(yp, ys, row(ffn_norm_g[1]), w1, w2, 1)

    kv_shape = (1, -1, SWA_ROWS, N_KV_HEADS, HEAD_DIM)
    return (yp.reshape(bsz, s_len, D_MODEL), ys.reshape(nseq, t_len, D_MODEL),
            nk_p.reshape(kv_shape), nv_p.reshape(kv_shape), nh_p.reshape(1, bsz, D_LRU), nc_p[None],
            nk_s.reshape(kv_shape), nv_s.reshape(kv_shape), nh_s[None], nc_s[None],
            gv_s.reshape(1, nseq, t_len, D_C))
```

```python
import functools
import math

import jax
import jax.numpy as jnp
from jax import lax
from jax.experimental import pallas as pl
from jax.experimental.pallas import tpu as pltpu

F32 = jnp.float32
BF16 = jnp.bfloat16

D_MODEL = 1024
CHUNK = 64
HEAD_DIM = 64
N_Q_HEADS = 8
N_KV_HEADS = 2
SWA_ROWS = 128
PAST_LEN = 4096
ROPE_THETA = 10000.0
NEG = -1e30
D_LRU = 512
LRU_BLOCKS = 8
LRU_BW = D_LRU // LRU_BLOCKS
CONV_W = 4
LRU_C = 8.0
Q_W = N_Q_HEADS * HEAD_DIM
KV_W = N_KV_HEADS * HEAD_DIM
CHUNK_MLP = 128
D_C = D_MODEL
C_GROUPS = 8
D_FF = 4 * D_MODEL
EPS = 1e-6

LANES = 128
SUBLANES = 8
MXU_WIDTH = 256
KEY_BLOCK = 2 * CHUNK
BAND_BLOCKS = 3
VMEM_LIMIT = 56 * 1024 * 1024

PROMPT_TILE = 1024
PROMPT_SUBTILE = 256
GMLP_TILE = 1024
SAMPLE_GROUP = 8


def _gelu(x):
    k1 = -2.0 * math.sqrt(2.0 / math.pi) * math.log2(math.e)
    return x / (1.0 + jnp.exp2(x * (k1 + (k1 * 0.044715) * (x * x))))


def _rms_rows(x, g):
    return x * lax.rsqrt(jnp.mean(x * x, axis=-1, keepdims=True) + EPS) * g


def _lane_iota(shape):
    return lax.broadcasted_iota(jnp.int32, shape, 1)


def _row_iota(shape):
    return lax.broadcasted_iota(jnp.int32, shape, 0)


def _head_ones():
    return ((_row_iota((LANES, LANES)) // HEAD_DIM) == (_lane_iota((LANES, LANES)) // HEAD_DIM)).astype(BF16)


def _norm_rope(x, xp, g_cos, g_sin, ones_bd):
    ss = jnp.dot((x * x).astype(BF16), ones_bd, preferred_element_type=F32)
    return (x * g_cos + xp * g_sin) * lax.rsqrt(ss + HEAD_DIM * EPS)


def _dup_head(t, j):
    lo = _lane_iota(t.shape) < HEAD_DIM
    swapped = pltpu.roll(t, HEAD_DIM, 1)
    return jnp.where(lo, t, swapped) if j == 0 else jnp.where(lo, swapped, t)


def _split_diag(t):
    lo = _lane_iota(t.shape) < HEAD_DIM
    return jnp.where(lo, t, 0.0), jnp.where(lo, 0.0, t)


def _ones_diag(rows):
    r = _row_iota((rows, LANES))
    return (((r % KEY_BLOCK) // CHUNK) == (_lane_iota((rows, LANES)) // HEAD_DIM)).astype(BF16)


def _attend_scores(qs, kb, col_mask):
    s = lax.dot_general(qs, kb, (((1,), (1,)), ((), ())), preferred_element_type=F32)
    return s if col_mask is None else jnp.where(col_mask, s, NEG)


def _attend_weights(s, sink_rows):
    mx = jnp.maximum(jnp.maximum(s[:, :LANES], s[:, LANES:2 * LANES]), s[:, 2 * LANES:])
    lo = _lane_iota(mx.shape) < HEAD_DIM
    m0 = jnp.max(jnp.where(lo, mx, -jnp.inf), axis=1, keepdims=True)
    m1 = jnp.max(jnp.where(lo, -jnp.inf, mx), axis=1, keepdims=True)
    m = jnp.maximum(jnp.where(lo, m0, m1), sink_rows)
    e = jnp.exp(s - jnp.concatenate([m, m, m], axis=1))
    return e.astype(BF16), m


def _attend_output(e, m, vb, sink_rows):
    od = jnp.dot(e, vb, preferred_element_type=F32)
    den = od[:, LANES:] + jnp.exp(sink_rows - m)
    return od[:, :LANES] / den


def _lru_scan(a, b, seg, tmod):
    d = 1
    while d < seg:
        ok = tmod >= d
        b = jnp.where(ok, a * pltpu.roll(b, d, 0), 0.0) + b
        if 2 * d < seg:
            a = jnp.where(ok, a * pltpu.roll(a, d, 0), a)
        d *= 2
    return b


def _lru_scan_tile(a, b, h_prev):
    rows = a.shape[0]
    groups = rows // SUBLANES
    a3 = a.reshape(groups, SUBLANES, LANES)
    b3 = b.reshape(groups, SUBLANES, LANES)
    sub = lax.broadcasted_iota(jnp.int32, a3.shape, 1)
    d = 1
    while d < SUBLANES:
        ok = sub >= d
        b3 = jnp.where(ok, a3 * pltpu.roll(b3, d, 1), 0.0) + b3
        a3 = jnp.where(ok, a3 * pltpu.roll(a3, d, 1), a3)
        d *= 2
    carry = jnp.broadcast_to(h_prev, (SUBLANES, LANES))
    out = []
    for g in range(groups):
        hg = a3[g] * carry + b3[g]
        out.append(hg)
        carry = jnp.broadcast_to(hg[SUBLANES - 1:SUBLANES, :], (SUBLANES, LANES))
    return jnp.concatenate(out, axis=0)


def _lru_branch(xs, gr, h_in, seg, tmod, convw_ref, convb_ref, wg_ref, ba_ref, bx_ref, lam_ref):
    xc, gates = _lru_gates(xs, convw_ref, convb_ref, wg_ref)
    parts = [_lru_piece(k, xc, gates, gr, h_in, seg, tmod, ba_ref, bx_ref, lam_ref) for k in range(D_LRU // LANES)]
    return jnp.concatenate([p[0] for p in parts], axis=1), jnp.concatenate([p[1] for p in parts], axis=1)


def _lru_gates(xs, convw_ref, convb_ref, wg_ref):
    xc = xs[0] * convw_ref[0:1, :] + convb_ref[...]
    for i in range(1, CONV_W):
        xc = xc + xs[i] * convw_ref[i:i + 1, :]
    xcb = xc.astype(BF16)
    half = D_LRU // 2
    return xc, [jnp.dot(xcb[:, hh * half:(hh + 1) * half], wg_ref[hh], preferred_element_type=F32)
                for hh in range(2)]


def _lru_piece(k, xc, gates, gr, h_in, seg, tmod, ba_ref, bx_ref, lam_ref):
    half = D_LRU // 2
    hh, tt = divmod(k, half // LANES)
    sl = slice(k * LANES, (k + 1) * LANES)
    g = gates[hh]
    r = jax.nn.sigmoid(g[:, tt * LANES:(tt + 1) * LANES] + ba_ref[:, sl])
    gi = jax.nn.sigmoid(g[:, half + tt * LANES:half + (tt + 1) * LANES] + bx_ref[:, sl])
    log_a = r * (-LRU_C * jax.nn.softplus(-lam_ref[:, sl]))
    a = jnp.exp(log_a)
    b = jnp.sqrt(-jnp.tanh(log_a) * (a * a + 1.0)) * (gi * xc[:, sl])
    if seg is None:
        h = _lru_scan_tile(a, b, h_in[:, sl])
    else:
        h = _lru_scan(a, b + jnp.where(tmod == 0, a * h_in[:, sl], 0.0), seg, tmod)
    return h * _gelu(gr[:, sl]), h


def _even_prompt_kernel(x_ref, cos_ref, sin_ref, inv_ref, sgn_ref, g_ref, win_ref, qg_ref, kg_ref, sink_ref, convw_ref, convb_ref,
                        wg_ref, ba_ref, bx_ref, lam_ref, wout_ref,
                        y_ref, nk_ref, nv_ref, nh_ref, nc_ref,
                        kb_scr, vb_scr, xr_scr, h_scr, mix_scr):
    t = pl.program_id(1)
    nt = pl.num_programs(1)
    tile = x_ref.shape[1]
    n_chunks = tile // CHUNK
    hist = 2 * KEY_BLOCK
    slot = t % 2

    @pl.when(t == 0)
    def _():
        ones = _ones_diag(vb_scr.shape[2])
        for j in range(N_KV_HEADS):
            kb_scr[0, j, 0:hist, :] = jnp.zeros((hist, LANES), BF16)
            vb_scr[0, j, 0:hist, 0:LANES] = jnp.zeros((hist, LANES), BF16)
            for s in range(2):
                vb_scr[s, j, :, LANES:2 * LANES] = ones
        xr_scr[0:8, :] = jnp.zeros((8, D_LRU), F32)
        h_scr[...] = jnp.zeros(h_scr.shape, F32)

    sub = PROMPT_SUBTILE
    n_sub = tile // sub
    sub_chunks = sub // CHUNK
    o3 = Q_W + 2 * KV_W
    xs_in, proj = [], []
    for u in range(n_sub):
        x = x_ref[0, u * sub:(u + 1) * sub, :]
        xn = _rms_rows(x, g_ref[...]).astype(BF16)
        xs_in.append(x)
        o5 = o3 + 2 * D_LRU
        proj.append((jnp.dot(xn, win_ref[:, 0:Q_W], preferred_element_type=F32),
                     jnp.dot(xn, win_ref[:, Q_W:o3], preferred_element_type=F32),
                     jnp.dot(xn, win_ref[:, o3:o3 + D_LRU], preferred_element_type=F32),
                     jnp.dot(xn, win_ref[:, o3 + D_LRU:o5], preferred_element_type=F32),
                     jnp.dot(xn, win_ref[:, o5:o5 + Q_W], preferred_element_type=F32),
                     jnp.dot(xn, win_ref[:, o5 + Q_W:o5 + Q_W + KV_W], preferred_element_type=F32)))
    for u in range(n_sub):
        xr_scr[8 + u * sub:8 + (u + 1) * sub, :] = proj[u][2]

    base = (t * tile).astype(F32) * inv_ref[...]
    cos_b = jnp.cos(base)
    sin_b = jnp.sin(base) * sgn_ref[...]

    band = BAND_BLOCKS * KEY_BLOCK
    col = _lane_iota((2 * CHUNK, band))
    sink_rows = [jnp.concatenate(
        [jnp.broadcast_to(sink_ref[2 * j:2 * j + 1, :], (CHUNK, LANES)),
         jnp.broadcast_to(sink_ref[2 * j + 1:2 * j + 2, :], (CHUNK, LANES))], axis=0) for j in range(N_KV_HEADS)]
    ones_bd = _head_ones()
    norm_c = math.sqrt(HEAD_DIM)
    q_all = []
    for u in range(n_sub):
        hq, hkv, xr, gr, hqp, hkp = proj[u]
        cos_l = cos_ref[u * sub:(u + 1) * sub, :]
        sin_l = sin_ref[u * sub:(u + 1) * sub, :]
        cos_t = cos_b * cos_l - sin_b * sin_l
        sin_t = sin_b * cos_l + cos_b * sin_l
        q_cos = cos_t * (qg_ref[0:1, :] * (norm_c * HEAD_DIM ** -0.5))
        q_sin = sin_t * (qg_ref[1:2, :] * (norm_c * HEAD_DIM ** -0.5))
        q_tiles = [_norm_rope(hq[:, n * LANES:(n + 1) * LANES], hqp[:, n * LANES:(n + 1) * LANES],
                              q_cos, q_sin, ones_bd).astype(BF16) for n in range(Q_W // LANES)]
        kr = _norm_rope(hkv[:, 0:KV_W], hkp, cos_t * (kg_ref[0:1, :] * norm_c), sin_t * (kg_ref[1:2, :] * norm_c),
                        ones_bd)
        vv = hkv[:, KV_W:2 * KV_W]
        for j in range(N_KV_HEADS):
            ktop, kbot = _split_diag(_dup_head(kr, j))
            vtop, vbot = _split_diag(_dup_head(vv, j))
            for i in range(sub_chunks):
                r0 = hist + (u * sub_chunks + i) * KEY_BLOCK
                rows = slice(i * CHUNK, (i + 1) * CHUNK)
                kb_scr[slot, j, r0:r0 + CHUNK, :] = ktop[rows].astype(BF16)
                kb_scr[slot, j, r0 + CHUNK:r0 + KEY_BLOCK, :] = kbot[rows].astype(BF16)
                vb_scr[slot, j, r0:r0 + CHUNK, 0:LANES] = vtop[rows].astype(BF16)
                vb_scr[slot, j, r0 + CHUNK:r0 + KEY_BLOCK, 0:LANES] = vbot[rows].astype(BF16)
        q_all.append(q_tiles)

    pairs = [(i, j) for i in range(sub_chunks) for j in range(N_KV_HEADS)]
    scores_all, gates_all = [], []
    for u in range(n_sub):
        q_tiles = q_all[u]
        scores = []
        for i, j in pairs:
            rows = slice(i * CHUNK, (i + 1) * CHUNK)
            ci = u * sub_chunks + i
            col_mask = None
            if ci < BAND_BLOCKS - 1:
                col_mask = col >= (1 - jnp.minimum(t, 1)) * ((BAND_BLOCKS - 1 - ci) * KEY_BLOCK)
            qs = jnp.concatenate([q_tiles[2 * j][rows], q_tiles[2 * j + 1][rows]], axis=0)
            scores.append(_attend_scores(qs, kb_scr[slot, j, ci * KEY_BLOCK:ci * KEY_BLOCK + band, :], col_mask))
        scores_all.append(scores)
        first = 8 + u * sub
        taps = [xr_scr[first - (CONV_W - 1 - i):first - (CONV_W - 1 - i) + sub, :] for i in range(CONV_W - 1)]
        gates_all.append(_lru_gates(taps + [proj[u][2]], convw_ref, convb_ref, wg_ref))

    h_carry = h_scr[0:1, :]
    for u in range(n_sub):
        xr, gr = proj[u][2], proj[u][3]
        scores = scores_all[u]
        xc, gates = gates_all[u]
        n_pieces = D_LRU // LANES
        per_piece = len(pairs) // n_pieces
        weights, lru = [], []
        for k in range(n_pieces):
            weights += [_attend_weights(scores[p], sink_rows[pairs[p][1]])
                        for p in range(k * per_piece, (k + 1) * per_piece)]
            lru.append(_lru_piece(k, xc, gates, gr, h_carry, None, None, ba_ref, bx_ref, lam_ref))
        for (e, m), (i, j) in zip(weights, pairs):
            ci = u * sub_chunks + i
            rows = slice(u * sub + i * CHUNK, u * sub + (i + 1) * CHUNK)
            o = _attend_output(e, m, vb_scr[slot, j, ci * KEY_BLOCK:ci * KEY_BLOCK + band, :], sink_rows[j])
            c0 = 2 * j * LANES
            mix_scr[rows, c0:c0 + LANES] = o[0:CHUNK].astype(BF16)
            mix_scr[rows, c0 + LANES:c0 + 2 * LANES] = o[CHUNK:2 * CHUNK].astype(BF16)

        h_carry = jnp.concatenate([h[sub - 1:sub, :] for _, h in lru], axis=1)
        rows = slice(u * sub, (u + 1) * sub)
        for k, (rec, _) in enumerate(lru):
            mix_scr[rows, Q_W + k * LANES:Q_W + (k + 1) * LANES] = rec.astype(BF16)

        y_att = jnp.dot(mix_scr[rows, 0:Q_W], wout_ref[0:Q_W, :], preferred_element_type=F32)
        y_rec = jnp.dot(mix_scr[rows, Q_W:Q_W + D_LRU], wout_ref[Q_W:Q_W + D_LRU, :], preferred_element_type=F32)
        y_ref[0, rows, :] = xs_in[u] + y_att + y_rec

    last = n_chunks * KEY_BLOCK
    for j in range(N_KV_HEADS):
        kb_scr[1 - slot, j, 0:hist, :] = kb_scr[slot, j, last:last + hist, :]
        vb_scr[1 - slot, j, 0:hist, 0:LANES] = vb_scr[slot, j, last:last + hist, 0:LANES]
    h_scr[0:1, :] = h_carry
    xr_scr[0:8, :] = xr[sub - 8:sub, :]

    @pl.when(t == nt - 1)
    def _():
        nk_ref[0] = kr[sub - SWA_ROWS:sub, :]
        nv_ref[0] = vv[sub - SWA_ROWS:sub, :]
        nh_ref[0] = h_carry
        nc_ref[0] = xr[sub - (CONV_W - 1):sub, :]


def _const_spec(shape):
    zeros = (0,) * len(shape)
    return pl.BlockSpec(shape, lambda *_: zeros, pipeline_mode=pl.Buffered(1))


def _even_prompt(x, p):
    bsz, s_len, _ = x.shape
    tile = PROMPT_TILE
    cos_l, sin_l = _rope_tables(jnp.arange(tile))
    inv_row, sgn_row = _rope_rows()
    nt = s_len // tile
    key_rows = (tile // CHUNK + 2) * KEY_BLOCK
    consts = [p['g'], p['win'], p['qg'], p['kg'], p['sink'], p['convw'], p['convb'], p['wg'], p['ba'], p['bx'],
              p['lam'], p['wout']]
    out_shape = [
        jax.ShapeDtypeStruct((bsz, s_len, D_MODEL), F32),
        jax.ShapeDtypeStruct((bsz, SWA_ROWS, KV_W), F32),
        jax.ShapeDtypeStruct((bsz, SWA_ROWS, KV_W), F32),
        jax.ShapeDtypeStruct((bsz, 1, D_LRU), F32),
        jax.ShapeDtypeStruct((bsz, CONV_W - 1, D_LRU), F32),
    ]
    return pl.pallas_call(
        _even_prompt_kernel,
        out_shape=out_shape,
        grid=(bsz, nt),
        in_specs=[pl.BlockSpec((1, tile, D_MODEL), lambda b, t: (b, t, 0)),
                  _const_spec(cos_l.shape), _const_spec(sin_l.shape), _const_spec(inv_row.shape),
                  _const_spec(sgn_row.shape)] + [_const_spec(c.shape) for c in consts],
        out_specs=[pl.BlockSpec((1, tile, D_MODEL), lambda b, t: (b, t, 0)),
                   pl.BlockSpec((1, SWA_ROWS, KV_W), lambda b, t: (b, 0, 0)),
                   pl.BlockSpec((1, SWA_ROWS, KV_W), lambda b, t: (b, 0, 0)),
                   pl.BlockSpec((1, 1, D_LRU), lambda b, t: (b, 0, 0)),
                   pl.BlockSpec((1, CONV_W - 1, D_LRU), lambda b, t: (b, 0, 0))],
        scratch_shapes=[pltpu.VMEM((2, N_KV_HEADS, key_rows, LANES), BF16),
                        pltpu.VMEM((2, N_KV_HEADS, key_rows, 2 * LANES), BF16),
                        pltpu.VMEM((tile + 8, D_LRU), F32),
                        pltpu.VMEM((8, D_LRU), F32),
                        pltpu.VMEM((tile, Q_W + D_LRU), BF16)],
        compiler_params=pltpu.CompilerParams(dimension_semantics=("arbitrary", "arbitrary"),
                                             vmem_limit_bytes=VMEM_LIMIT),
        name="even_mixer_prompt",
    )(x, cos_l, sin_l, inv_row, sgn_row, *consts)


def _even_sample_kernel(x_ref, cos_ref, sin_ref, kc_ref, vc_ref, h0_ref, xprev_ref,
                        g_ref, win_ref, qg_ref, kg_ref, sink_ref, convw_ref, convb_ref,
                        wg_ref, ba_ref, bx_ref, lam_ref, wout_ref,
                        y_ref, nk_ref, nv_ref, h_ref, xr_ref,
                        q_scr, k_scr, v_scr, mix_scr):
    rows_all = x_ref.shape[0]
    nseq = kc_ref.shape[0]
    t_len = rows_all // nseq
    x = x_ref[...]
    xn = _rms_rows(x, g_ref[...]).astype(BF16)
    cos_t = cos_ref[...]
    sin_t = sin_ref[...]

    o3 = Q_W + 2 * KV_W
    o5 = o3 + 2 * D_LRU
    hq = jnp.dot(xn, win_ref[:, 0:Q_W], preferred_element_type=F32)
    hkv = jnp.dot(xn, win_ref[:, Q_W:o3], preferred_element_type=F32)
    xr = jnp.dot(xn, win_ref[:, o3:o3 + D_LRU], preferred_element_type=F32)
    gr = jnp.dot(xn, win_ref[:, o3 + D_LRU:o5], preferred_element_type=F32)
    hqp = jnp.dot(xn, win_ref[:, o5:o5 + Q_W], preferred_element_type=F32)
    hkp = jnp.dot(xn, win_ref[:, o5 + Q_W:o5 + Q_W + KV_W], preferred_element_type=F32)
    ones_bd = _head_ones()
    norm_c = math.sqrt(HEAD_DIM)
    q_cos = cos_t * (qg_ref[0:1, :] * (norm_c * HEAD_DIM ** -0.5))
    q_sin = sin_t * (qg_ref[1:2, :] * (norm_c * HEAD_DIM ** -0.5))
    for n in range(Q_W // LANES):
        lanes = slice(n * LANES, (n + 1) * LANES)
        q_scr[:, lanes] = _norm_rope(hq[:, lanes], hqp[:, lanes], q_cos, q_sin, ones_bd)
    k_scr[...] = _norm_rope(hkv[:, 0:KV_W], hkp, cos_t * (kg_ref[0:1, :] * norm_c),
                            sin_t * (kg_ref[1:2, :] * norm_c), ones_bd)
    v_scr[...] = hkv[:, KV_W:2 * KV_W]

    band = BAND_BLOCKS * KEY_BLOCK
    col = _lane_iota((2 * t_len, band))
    col_mask = jnp.logical_or(col < 2 * KEY_BLOCK, (col % CHUNK) < t_len)
    ones_band = _ones_diag(band)
    pad = jnp.zeros((CHUNK - t_len, LANES), F32)

    sink_rows = [jnp.concatenate(
        [jnp.broadcast_to(sink_ref[2 * j:2 * j + 1, :], (t_len, LANES)),
         jnp.broadcast_to(sink_ref[2 * j + 1:2 * j + 2, :], (t_len, LANES))], axis=0) for j in range(N_KV_HEADS)]

    def group_body(gi, carry):
        work = []
        for s in range(SAMPLE_GROUP):
            b = gi * SAMPLE_GROUP + s
            r0 = pl.multiple_of(b * t_len, t_len)
            kc = kc_ref[b]
            vc = vc_ref[b]
            knew = k_scr[pl.ds(r0, t_len), :]
            vnew = v_scr[pl.ds(r0, t_len), :]
            nk_ref[b, 0:SWA_ROWS - t_len, :] = kc[t_len:SWA_ROWS]
            nk_ref[b, SWA_ROWS - t_len:SWA_ROWS, :] = knew
            nv_ref[b, 0:SWA_ROWS - t_len, :] = vc[t_len:SWA_ROWS]
            nv_ref[b, SWA_ROWS - t_len:SWA_ROWS, :] = vnew
            kfull = jnp.concatenate([kc, knew, pad], axis=0)
            vfull = jnp.concatenate([vc, vnew, pad], axis=0)
            for j in range(N_KV_HEADS):
                ktop, kbot = _split_diag(_dup_head(kfull, j))
                vtop, vbot = _split_diag(_dup_head(vfull, j))
                kparts, vparts = [], []
                for c in range(BAND_BLOCKS):
                    rs = slice(c * CHUNK, (c + 1) * CHUNK)
                    kparts += [ktop[rs], kbot[rs]]
                    vparts += [vtop[rs], vbot[rs]]
                kb = jnp.concatenate(kparts, axis=0).astype(BF16)
                vb = jnp.concatenate([jnp.concatenate(vparts, axis=0).astype(BF16), ones_band], axis=1)
                c0 = 2 * j * LANES
                qs = jnp.concatenate([q_scr[pl.ds(r0, t_len), c0:c0 + LANES],
                                      q_scr[pl.ds(r0, t_len), c0 + LANES:c0 + 2 * LANES]], axis=0).astype(BF16)
                work.append((r0, j, vb, _attend_scores(qs, kb, col_mask)))
        weights = [_attend_weights(s, sink_rows[j]) for _, j, _, s in work]
        for (r0, j, vb, _), (e, m) in zip(work, weights):
            o = _attend_output(e, m, vb, sink_rows[j])
            c0 = 2 * j * LANES
            mix_scr[pl.ds(r0, t_len), c0:c0 + LANES] = o[0:t_len]
            mix_scr[pl.ds(r0, t_len), c0 + LANES:c0 + 2 * LANES] = o[t_len:2 * t_len]
        return carry

    lax.fori_loop(0, nseq // SAMPLE_GROUP, group_body, 0)

    tmod =_row_iota((rows_all, LANES)) % t_len
    tmod_w = _row_iota((rows_all, D_LRU)) % t_len
    xprev = xprev_ref[...]
    xs = []
    for i in range(CONV_W - 1):
        k = CONV_W - 1 - i
        own = pltpu.roll(xr, k, 0)
        cached = xprev if k == CONV_W - 1 else pltpu.roll(xprev, rows_all - (CONV_W - 1 - k), 0)
        xs.append(jnp.where(tmod_w >= k, own, cached))
    xs.append(xr)
    rec, h = _lru_branch(xs, gr, h0_ref[...], t_len, tmod, convw_ref, convb_ref, wg_ref, ba_ref, bx_ref, lam_ref)
    mix_scr[:, Q_W:Q_W + D_LRU] = rec
    xr_ref[...] = xr
    h_ref[...] = h
    y_ref[...] = x + jnp.dot(mix_scr[...].astype(BF16), wout_ref[...], preferred_element_type=F32)


def _even_sample(x2, cos_t, sin_t, kc, vc, h0_rows, xprev, p):
    rows_all = x2.shape[0]
    nseq = kc.shape[0]
    consts = [p['g'], p['win'], p['qg'], p['kg'], p['sink'], p['convw'], p['convb'], p['wg'], p['ba'], p['bx'],
              p['lam'], p['wout']]
    ins = [x2, cos_t, sin_t, kc, vc, h0_rows, xprev] + consts
    out_shape = [
        jax.ShapeDtypeStruct((rows_all, D_MODEL), F32),
        jax.ShapeDtypeStruct((nseq, SWA_ROWS, KV_W), F32),
        jax.ShapeDtypeStruct((nseq, SWA_ROWS, KV_W), F32),
        jax.ShapeDtypeStruct((rows_all, D_LRU), F32),
        jax.ShapeDtypeStruct((rows_all, D_LRU), F32),
    ]
    return pl.pallas_call(
        _even_sample_kernel,
        out_shape=out_shape,
        grid=(1,),
        in_specs=[_const_spec(a.shape) for a in ins],
        out_specs=[pl.BlockSpec(s.shape, lambda i, n=len(s.shape): (0,) * n) for s in out_shape],
        scratch_shapes=[pltpu.VMEM((rows_all, Q_W), F32),
                        pltpu.VMEM((rows_all, KV_W), F32),
                        pltpu.VMEM((rows_all, KV_W), F32),
                        pltpu.VMEM((rows_all, Q_W + D_LRU), F32)],
        compiler_params=pltpu.CompilerParams(dimension_semantics=("arbitrary",), vmem_limit_bytes=VMEM_LIMIT),
        name="even_mixer_sample",
    )(*ins)


def _ffn_rows(x, g_ref, w1_ref, w2_ref):
    xn = _rms_rows(x, g_ref[...]).astype(BF16)
    out = None
    step = D_MODEL
    for c in range(D_FF // step):
        hcol = jnp.dot(xn, w1_ref[:, c * step:(c + 1) * step].astype(BF16), preferred_element_type=F32)
        hcol = jnp.square(jnp.maximum(hcol, 0.0)).astype(BF16)
        part = jnp.dot(hcol, w2_ref[c * step:(c + 1) * step, :].astype(BF16), preferred_element_type=F32)
        out = part if out is None else out + part
    return x + out


def _ffn_kernel(xp_ref, xs_ref, g_ref, w1_ref, w2_ref, yp_ref, ys_ref):
    i = pl.program_id(0)
    yp_ref[...] = _ffn_rows(jnp.where(i == 0, xs_ref[...], xp_ref[...]), g_ref, w1_ref, w2_ref)

    @pl.when(i == 0)
    def _():
        ys_ref[...] = yp_ref[...]


def _sample_then_tiles_specs(tile, width, sample_rows=None):
    prompt = pl.BlockSpec((tile, width), lambda i: (jnp.maximum(i - 1, 0), 0))
    sample = pl.BlockSpec((sample_rows or tile, width), lambda i: (0, 0))
    return prompt, sample


def _ffn(xp, xs, g, w1, w2, layer):
    tile = xs.shape[0]
    n_tiles = xp.shape[0] // tile
    xp_spec, xs_spec = _sample_then_tiles_specs(tile, D_MODEL)
    layer_spec = lambda shape: pl.BlockSpec((None,) + shape[1:], lambda i: (layer, 0, 0),
                                            pipeline_mode=pl.Buffered(1))
    return pl.pallas_call(
        _ffn_kernel,
        out_shape=[jax.ShapeDtypeStruct(xp.shape, F32), jax.ShapeDtypeStruct(xs.shape, F32)],
        grid=(n_tiles + 1,),
        in_specs=[xp_spec, xs_spec, _const_spec(g.shape), layer_spec(w1.shape), layer_spec(w2.shape)],
        out_specs=list(_sample_then_tiles_specs(tile, D_MODEL)),
        compiler_params=pltpu.CompilerParams(dimension_semantics=("arbitrary",), vmem_limit_bytes=VMEM_LIMIT),
        name="channel_mlp",
    )(xp, xs, g, w1, w2)


def _gmlp_kernel(seg_s, xp_ref, xs_ref, g_ref, win_ref, vg_ref, wsp_ref, wss_ref, biasp_ref, biass_ref, wout_ref,
                 yp_ref, ys_ref, v_ref, v_scr):
    i = pl.program_id(0)
    is_s = i == 0
    tile = xp_ref.shape[0]
    sub = xs_ref.shape[0]
    n_sub = tile // sub
    xs_in, proj = [], []
    for u in range(n_sub):
        rows = slice(u * sub, (u + 1) * sub)
        x = jnp.where(is_s, xs_ref[...], xp_ref[rows, :]) if u == 0 else xp_ref[rows, :]
        xn = _rms_rows(x, g_ref[...]).astype(BF16)
        xs_in.append(x)
        proj.append((jnp.dot(xn, win_ref[:, D_C:2 * D_C].astype(BF16), preferred_element_type=F32),
                     jnp.dot(xn, win_ref[:, 0:D_C].astype(BF16), preferred_element_type=F32)))
    r = _row_iota((CHUNK_MLP, CHUNK_MLP))
    col = _lane_iota((CHUNK_MLP, CHUNK_MLP))
    keep_p = col <= r
    keep_s = jnp.logical_and((r // seg_s) == (col // seg_s), keep_p)
    ws = [jnp.where(is_s, jnp.where(keep_s, wss_ref[gi], 0.0), jnp.where(keep_p, wsp_ref[gi], 0.0)).astype(BF16)
          for gi in range(C_GROUPS)]
    step = MXU_WIDTH
    per_step = step // LANES
    for u in range(n_sub):
        rows = slice(u * sub, (u + 1) * sub)
        zv = _gelu(proj[u][0])
        zc = zv - jnp.mean(zv, axis=-1, keepdims=True)
        v = zc * lax.rsqrt(jnp.mean(zc * zc, axis=-1, keepdims=True) + EPS) * vg_ref[...]
        if u == 0:
            v_scr[...] = v
        vb = v.astype(BF16)
        sv = []
        for gi in range(C_GROUPS):
            lanes = slice(gi * LANES, (gi + 1) * LANES)
            bias = jnp.where(is_s, biass_ref[:, lanes], biasp_ref[:, lanes])
            blocks = [jnp.dot(ws[gi], vb[ch * CHUNK_MLP:(ch + 1) * CHUNK_MLP, lanes],
                              preferred_element_type=F32) + bias for ch in range(sub // CHUNK_MLP)]
            sv.append(jnp.concatenate(blocks, axis=0))
        acc = xs_in[u]
        for c in range(D_C // step):
            cols = slice(c * step, (c + 1) * step)
            gate = (_gelu(proj[u][1][:, cols]) *
                    jnp.concatenate(sv[c * per_step:(c + 1) * per_step], axis=1)).astype(BF16)
            acc = acc + jnp.dot(gate, wout_ref[cols, :].astype(BF16), preferred_element_type=F32)
        yp_ref[rows, :] = acc

    @pl.when(is_s)
    def _():
        ys_ref[...] = yp_ref[0:sub, :]
        v_ref[...] = v_scr[...]


def _gmlp(xp, xs, seg_s, g, win, vg, ws_p, ws_s, bias_p, bias_s, wout):
    rows_s = xs.shape[0]
    tile = GMLP_TILE
    n_tiles = xp.shape[0] // tile
    xp_spec, xs_spec = _sample_then_tiles_specs(tile, D_MODEL, rows_s)
    consts = (g, win, vg, ws_p, ws_s, bias_p, bias_s, wout)
    return pl.pallas_call(
        functools.partial(_gmlp_kernel, seg_s),
        out_shape=[jax.ShapeDtypeStruct(xp.shape, F32), jax.ShapeDtypeStruct(xs.shape, F32),
                   jax.ShapeDtypeStruct((rows_s, D_C), F32)],
        grid=(n_tiles + 1,),
        in_specs=[xp_spec, xs_spec] + [_const_spec(a.shape) for a in consts],
        out_specs=list(_sample_then_tiles_specs(tile, D_MODEL, rows_s)) +
                  [pl.BlockSpec((rows_s, D_C), lambda i: (0, 0))],
        scratch_shapes=[pltpu.VMEM((rows_s, D_C), F32)],
        compiler_params=pltpu.CompilerParams(dimension_semantics=("arbitrary",), vmem_limit_bytes=VMEM_LIMIT),
        name="gmlp_mixer",
    )(xp, xs, *consts)


def _rope_inv():
    half = HEAD_DIM // 2
    return ROPE_THETA ** (-jnp.arange(half, dtype=F32) / half)


def _rope_rows():
    inv = _rope_inv()
    sgn = jnp.ones((HEAD_DIM // 2,), F32)
    return jnp.concatenate([inv] * 4)[None, :], jnp.concatenate([-sgn, sgn, -sgn, sgn])[None, :]


def _rope_tables(pos):
    inv = _rope_inv()
    ang = pos.astype(F32)[:, None] * inv[None, :]
    cos = jnp.cos(ang)
    sin = jnp.sin(ang)
    return jnp.concatenate([cos] * 4, axis=1), jnp.concatenate([-sin, sin, -sin, sin], axis=1)


def _swap_head_halves(a):
    half = HEAD_DIM // 2
    return a.reshape(a.shape[:-1] + (-1, 2, half))[..., ::-1, :].reshape(a.shape)


def _gate_weights(wa, wx):
    nb = LRU_BLOCKS // 2
    eye = jnp.eye(nb, dtype=bool)

    def half_diag(w, hh):
        blocks = w[hh * nb:(hh + 1) * nb]
        return jnp.where(eye[:, None, :, None], blocks[:, :, None, :], 0.0).reshape(nb * LRU_BW, nb * LRU_BW)
    return jnp.stack([jnp.concatenate([half_diag(wa, hh), half_diag(wx, hh)], axis=1)
                      for hh in range(2)]).astype(BF16)


def kernel(x_prompt, x_sample, cache_swa_k, cache_swa_v, state_lru_h, state_lru_conv, e_norm_g, e_w_in, e_q_norm_g, e_k_norm_g, e_sinks, e_conv_w, e_conv_b, e_gate_a_w, e_gate_a_b, e_gate_x_w, e_gate_x_b, e_lru_lambda, e_w_out, o_norm_g, o_w_in, o_v_norm_g, o_spatial_w, o_spatial_b, o_w_out, ffn_norm_g, ffn_w1, ffn_w2):
    bsz, s_len, _ = x_prompt.shape
    nseq, t_len, _ = x_sample.shape
    past_len = PAST_LEN
    row = lambda a: a.reshape(1, -1)

    e = 0
    p = {
        'g': row(e_norm_g[e]),
        'win': jnp.concatenate([e_w_in[e], _swap_head_halves(e_w_in[e][:, :Q_W + KV_W])], axis=1).astype(BF16),
        'qg': jnp.tile(jnp.stack([e_q_norm_g[e], _swap_head_halves(e_q_norm_g[e])]), (1, LANES // HEAD_DIM)),
        'kg': jnp.tile(jnp.stack([e_k_norm_g[e], _swap_head_halves(e_k_norm_g[e])]), (1, LANES // HEAD_DIM)),
        'sink': jnp.repeat(e_sinks[e], HEAD_DIM).reshape(N_Q_HEADS // 2, LANES),
        'convw': e_conv_w[e],
        'convb': row(e_conv_b[e]),
        'wg': _gate_weights(e_gate_a_w[e], e_gate_x_w[e]),
        'ba': row(e_gate_a_b[e]),
        'bx': row(e_gate_x_b[e]),
        'lam': row(e_lru_lambda[e]),
        'wout': e_w_out[e].astype(BF16),
    }
    yp, nk_p, nv_p, nh_p, nc_p = _even_prompt(x_prompt, p)

    cos_s, sin_s = _rope_tables(past_len + jnp.arange(t_len))
    cos_s = jnp.tile(cos_s, (nseq, 1))
    sin_s = jnp.tile(sin_s, (nseq, 1))
    kc = cache_swa_k[e].reshape(nseq, SWA_ROWS, KV_W)
    vc = cache_swa_v[e].reshape(nseq, SWA_ROWS, KV_W)
    h0_rows = jnp.repeat(state_lru_h[e], t_len, axis=0)
    xprev = jnp.pad(state_lru_conv[e], ((0, 0), (0, t_len - (CONV_W - 1)), (0, 0))).reshape(nseq * t_len, D_LRU)
    ys, nk_s, nv_s, h_s, xr_s = _even_sample(x_sample.reshape(nseq * t_len, D_MODEL), cos_s, sin_s, kc, vc,
                                             h0_rows, xprev, p)
    nh_s = h_s.reshape(nseq, t_len, D_LRU)[:, -1]
    nc_s = xr_s.reshape(nseq, t_len, D_LRU)[:, t_len - (CONV_W - 1):]

    w1, w2 = ffn_w1, ffn_w2
    yp, ys = _ffn(yp.reshape(bsz * s_len, D_MODEL), ys, row(ffn_norm_g[0]), w1, w2, 0)

    o = 0
    bias_p = jnp.repeat(o_spatial_b[o].T, LANES, axis=1)
    reps = CHUNK_MLP // t_len
    ws_s = jnp.tile(o_spatial_w[o][:, :t_len, :t_len], (1, reps, reps))
    bias_s = jnp.tile(jnp.repeat(o_spatial_b[o][:, :t_len].T, LANES, axis=1), (reps, 1))
    yp, ys, gv_s = _gmlp(yp, ys, t_len, row(o_norm_g[o]), o_w_in[o], row(o_v_norm_g[o]),
                         o_spatial_w[o], ws_s, bias_p, bias_s, o_w_out[o])

    yp, ys = _ffn(yp, ys, row(ffn_norm_g[1]), w1, w2, 1)

    kv_shape = (1, -1, SWA_ROWS, N_KV_HEADS, HEAD_DIM)
    return (yp.reshape(bsz, s_len, D_MODEL), ys.reshape(nseq, t_len, D_MODEL),
            nk_p.reshape(kv_shape), nv_p.reshape(kv_shape), nh_p.reshape(1, bsz, D_LRU), nc_p[None],
            nk_s.reshape(kv_shape), nv_s.reshape(kv_shape), nh_s[None], nc_s[None],
            gv_s.reshape(1, nseq, t_len, D_C))
```

```python
import functools
import math

import jax
import jax.numpy as jnp
from jax import lax
from jax.experimental import pallas as pl
from jax.experimental.pallas import tpu as pltpu

F32 = jnp.float32
BF16 = jnp.bfloat16

D_MODEL = 1024
CHUNK = 64
HEAD_DIM = 64
N_Q_HEADS = 8
N_KV_HEADS = 2
SWA_ROWS = 128
PAST_LEN = 4096
ROPE_THETA = 10000.0
NEG = -1e30
D_LRU = 512
LRU_BLOCKS = 8
LRU_BW = D_LRU // LRU_BLOCKS
CONV_W = 4
LRU_C = 8.0
Q_W = N_Q_HEADS * HEAD_DIM
KV_W = N_KV_HEADS * HEAD_DIM
CHUNK_MLP = 128
D_C = D_MODEL
C_GROUPS = 8
D_FF = 4 * D_MODEL
EPS = 1e-6

LANES = 128
SUBLANES = 8
MXU_WIDTH = 256
KEY_BLOCK = 2 * CHUNK
BAND_BLOCKS = 3
VMEM_LIMIT = 56 * 1024 * 1024

PROMPT_TILE = 1024
PROMPT_SUBTILE = 256
GMLP_TILE = 1024
SAMPLE_GROUP = 8


def _gelu(x):
    k1 = -2.0 * math.sqrt(2.0 / math.pi) * math.log2(math.e)
    return x / (1.0 + jnp.exp2(x * (k1 + (k1 * 0.044715) * (x * x))))


def _rms_rows(x, g):
    return x * lax.rsqrt(jnp.mean(x * x, axis=-1, keepdims=True) + EPS) * g


def _lane_iota(shape):
    return lax.broadcasted_iota(jnp.int32, shape, 1)


def _row_iota(shape):
    return lax.broadcasted_iota(jnp.int32, shape, 0)


def _head_ones():
    return ((_row_iota((LANES, LANES)) // HEAD_DIM) == (_lane_iota((LANES, LANES)) // HEAD_DIM)).astype(BF16)


def _norm_rope(x, xp, g_cos, g_sin, ones_bd):
    ss = jnp.dot((x * x).astype(BF16), ones_bd, preferred_element_type=F32)
    return (x * g_cos + xp * g_sin) * lax.rsqrt(ss + HEAD_DIM * EPS)


def _diag_blocks(t):
    lo = _lane_iota(t.shape) < HEAD_DIM
    swapped = pltpu.roll(t, HEAD_DIM, 1)
    return [(jnp.where(lo, t, 0.0), jnp.where(lo, 0.0, swapped)),
            (jnp.where(lo, swapped, 0.0), jnp.where(lo, 0.0, t))]


def _ones_diag(rows):
    r = _row_iota((rows, LANES))
    return (((r % KEY_BLOCK) // CHUNK) == (_lane_iota((rows, LANES)) // HEAD_DIM)).astype(BF16)


def _attend_scores(qs, kb, col_mask):
    s = lax.dot_general(qs, kb, (((1,), (1,)), ((), ())), preferred_element_type=F32)
    return s if col_mask is None else jnp.where(col_mask, s, NEG)


def _attend_weights(s, sink_rows):
    mx = jnp.maximum(jnp.maximum(s[:, :LANES], s[:, LANES:2 * LANES]), s[:, 2 * LANES:])
    lo = _lane_iota(mx.shape) < HEAD_DIM
    m0 = jnp.max(jnp.where(lo, mx, -jnp.inf), axis=1, keepdims=True)
    m1 = jnp.max(jnp.where(lo, -jnp.inf, mx), axis=1, keepdims=True)
    m = jnp.maximum(jnp.where(lo, m0, m1), sink_rows)
    e = jnp.exp(s - jnp.concatenate([m, m, m], axis=1))
    return e.astype(BF16), m


def _attend_output(e, m, vb, sink_rows):
    od = jnp.dot(e, vb, preferred_element_type=F32)
    den = od[:, LANES:] + jnp.exp(sink_rows - m)
    return od[:, :LANES] / den


def _lru_scan(a, b, seg, tmod):
    d = 1
    while d < seg:
        ok = tmod >= d
        b = jnp.where(ok, a * pltpu.roll(b, d, 0), 0.0) + b
        if 2 * d < seg:
            a = jnp.where(ok, a * pltpu.roll(a, d, 0), a)
        d *= 2
    return b


def _lru_scan_tile(a, b, h_prev):
    rows = a.shape[0]
    groups = rows // SUBLANES
    a3 = a.reshape(groups, SUBLANES, LANES)
    b3 = b.reshape(groups, SUBLANES, LANES)
    sub = lax.broadcasted_iota(jnp.int32, a3.shape, 1)
    d = 1
    while d < SUBLANES:
        ok = sub >= d
        b3 = jnp.where(ok, a3 * pltpu.roll(b3, d, 1), 0.0) + b3
        a3 = jnp.where(ok, a3 * pltpu.roll(a3, d, 1), a3)
        d *= 2
    carry = jnp.broadcast_to(h_prev, (SUBLANES, LANES))
    out = []
    for g in range(groups):
        hg = a3[g] * carry + b3[g]
        out.append(hg)
        carry = jnp.broadcast_to(hg[SUBLANES - 1:SUBLANES, :], (SUBLANES, LANES))
    return jnp.concatenate(out, axis=0)


def _lru_branch(xs, gr, h_in, seg, tmod, convw_ref, convb_ref, wg_ref, ba_ref, bx_ref, lam_ref):
    xc, gates = _lru_gates(xs, convw_ref, convb_ref, wg_ref)
    parts = [_lru_piece(k, xc, gates, gr, h_in, seg, tmod, ba_ref, bx_ref, lam_ref) for k in range(D_LRU // LANES)]
    return jnp.concatenate([p[0] for p in parts], axis=1), jnp.concatenate([p[1] for p in parts], axis=1)


def _lru_gates(xs, convw_ref, convb_ref, wg_ref):
    xc = xs[0] * convw_ref[0:1, :] + convb_ref[...]
    for i in range(1, CONV_W):
        xc = xc + xs[i] * convw_ref[i:i + 1, :]
    xcb = xc.astype(BF16)
    half = D_LRU // 2
    return xc, [jnp.dot(xcb[:, hh * half:(hh + 1) * half], wg_ref[hh], preferred_element_type=F32)
                for hh in range(2)]


def _lru_piece(k, xc, gates, gr, h_in, seg, tmod, ba_ref, bx_ref, lam_ref):
    half = D_LRU // 2
    hh, tt = divmod(k, half // LANES)
    sl = slice(k * LANES, (k + 1) * LANES)
    g = gates[hh]
    r = jax.nn.sigmoid(g[:, tt * LANES:(tt + 1) * LANES] + ba_ref[:, sl])
    gi = jax.nn.sigmoid(g[:, half + tt * LANES:half + (tt + 1) * LANES] + bx_ref[:, sl])
    log_a = r * (-LRU_C * jax.nn.softplus(-lam_ref[:, sl]))
    a = jnp.exp(log_a)
    b = jnp.sqrt(-jnp.tanh(log_a) * (a * a + 1.0)) * (gi * xc[:, sl])
    if seg is None:
        h = _lru_scan_tile(a, b, h_in[:, sl])
    else:
        h = _lru_scan(a, b + jnp.where(tmod == 0, a * h_in[:, sl], 0.0), seg, tmod)
    return h * _gelu(gr[:, sl]), h


def _even_prompt_kernel(x_ref, cos_ref, sin_ref, inv_ref, sgn_ref, g_ref, win_ref, qg_ref, kg_ref, sink_ref, convw_ref, convb_ref,
                        wg_ref, ba_ref, bx_ref, lam_ref, wout_ref,
                        y_ref, nk_ref, nv_ref, nh_ref, nc_ref,
                        kb_scr, vb_scr, xr_scr, h_scr, mix_scr):
    t = pl.program_id(1)
    nt = pl.num_programs(1)
    tile = x_ref.shape[1]
    n_chunks = tile // CHUNK
    hist = 2 * KEY_BLOCK
    slot = t % 2

    @pl.when(t == 0)
    def _():
        ones = _ones_diag(vb_scr.shape[2])
        for j in range(N_KV_HEADS):
            kb_scr[0, j, 0:hist, :] = jnp.zeros((hist, LANES), BF16)
            vb_scr[0, j, 0:hist, 0:LANES] = jnp.zeros((hist, LANES), BF16)
            for s in range(2):
                vb_scr[s, j, :, LANES:2 * LANES] = ones
        xr_scr[0:SUBLANES, :] = jnp.zeros((SUBLANES, D_LRU), F32)
        h_scr[...] = jnp.zeros(h_scr.shape, F32)

    sub = PROMPT_SUBTILE
    n_sub = tile // sub
    sub_chunks = sub // CHUNK
    o3 = Q_W + 2 * KV_W
    xs_in, proj = [], []
    for u in range(n_sub):
        x = x_ref[0, u * sub:(u + 1) * sub, :]
        xn = _rms_rows(x, g_ref[...]).astype(BF16)
        xs_in.append(x)
        o5 = o3 + 2 * D_LRU
        proj.append((jnp.dot(xn, win_ref[:, 0:Q_W], preferred_element_type=F32),
                     jnp.dot(xn, win_ref[:, Q_W:o3], preferred_element_type=F32),
                     jnp.dot(xn, win_ref[:, o3:o3 + D_LRU], preferred_element_type=F32),
                     jnp.dot(xn, win_ref[:, o3 + D_LRU:o5], preferred_element_type=F32),
                     jnp.dot(xn, win_ref[:, o5:o5 + Q_W], preferred_element_type=F32),
                     jnp.dot(xn, win_ref[:, o5 + Q_W:o5 + Q_W + KV_W], preferred_element_type=F32)))
    for u in range(n_sub):
        xr_scr[SUBLANES + u * sub:SUBLANES + (u + 1) * sub, :] = proj[u][2]

    base = (t * tile).astype(F32) * inv_ref[...]
    cos_b = jnp.cos(base)
    sin_b = jnp.sin(base) * sgn_ref[...]

    band = BAND_BLOCKS * KEY_BLOCK
    col = _lane_iota((2 * CHUNK, band))
    sink_rows = [jnp.concatenate(
        [jnp.broadcast_to(sink_ref[2 * j:2 * j + 1, :], (CHUNK, LANES)),
         jnp.broadcast_to(sink_ref[2 * j + 1:2 * j + 2, :], (CHUNK, LANES))], axis=0) for j in range(N_KV_HEADS)]
    ones_bd = _head_ones()
    norm_c = math.sqrt(HEAD_DIM)
    q_all = []
    for u in range(n_sub):
        hq, hkv, xr, gr, hqp, hkp = proj[u]
        cos_l = cos_ref[u * sub:(u + 1) * sub, :]
        sin_l = sin_ref[u * sub:(u + 1) * sub, :]
        cos_t = cos_b * cos_l - sin_b * sin_l
        sin_t = sin_b * cos_l + cos_b * sin_l
        q_cos = cos_t * (qg_ref[0:1, :] * (norm_c * HEAD_DIM ** -0.5))
        q_sin = sin_t * (qg_ref[1:2, :] * (norm_c * HEAD_DIM ** -0.5))
        q_tiles = [_norm_rope(hq[:, n * LANES:(n + 1) * LANES], hqp[:, n * LANES:(n + 1) * LANES],
                              q_cos, q_sin, ones_bd).astype(BF16) for n in range(Q_W // LANES)]
        kr = _norm_rope(hkv[:, 0:KV_W], hkp, cos_t * (kg_ref[0:1, :] * norm_c), sin_t * (kg_ref[1:2, :] * norm_c),
                        ones_bd)
        vv = hkv[:, KV_W:2 * KV_W]
        k_blocks, v_blocks = _diag_blocks(kr), _diag_blocks(vv)
        for j in range(N_KV_HEADS):
            (ktop, kbot), (vtop, vbot) = k_blocks[j], v_blocks[j]
            for i in range(sub_chunks):
                r0 = hist + (u * sub_chunks + i) * KEY_BLOCK
                rows = slice(i * CHUNK, (i + 1) * CHUNK)
                kb_scr[slot, j, r0:r0 + CHUNK, :] = ktop[rows].astype(BF16)
                kb_scr[slot, j, r0 + CHUNK:r0 + KEY_BLOCK, :] = kbot[rows].astype(BF16)
                vb_scr[slot, j, r0:r0 + CHUNK, 0:LANES] = vtop[rows].astype(BF16)
                vb_scr[slot, j, r0 + CHUNK:r0 + KEY_BLOCK, 0:LANES] = vbot[rows].astype(BF16)
        q_all.append(q_tiles)

    pairs = [(i, j) for i in range(sub_chunks) for j in range(N_KV_HEADS)]
    scores_all, gates_all = [], []
    for u in range(n_sub):
        q_tiles = q_all[u]
        scores = []
        for i, j in pairs:
            rows = slice(i * CHUNK, (i + 1) * CHUNK)
            ci = u * sub_chunks + i
            col_mask = None
            if ci < BAND_BLOCKS - 1:
                col_mask = col >= (1 - jnp.minimum(t, 1)) * ((BAND_BLOCKS - 1 - ci) * KEY_BLOCK)
            qs = jnp.concatenate([q_tiles[2 * j][rows], q_tiles[2 * j + 1][rows]], axis=0)
            scores.append(_attend_scores(qs, kb_scr[slot, j, ci * KEY_BLOCK:ci * KEY_BLOCK + band, :], col_mask))
        scores_all.append(scores)
        first = SUBLANES + u * sub
        taps = [xr_scr[first - (CONV_W - 1 - i):first - (CONV_W - 1 - i) + sub, :] for i in range(CONV_W - 1)]
        gates_all.append(_lru_gates(taps + [proj[u][2]], convw_ref, convb_ref, wg_ref))

    h_carry = h_scr[0:1, :]
    for u in range(n_sub):
        xr, gr = proj[u][2], proj[u][3]
        scores = scores_all[u]
        xc, gates = gates_all[u]
        n_pieces = D_LRU // LANES
        per_piece = len(pairs) // n_pieces
        weights, lru = [], []
        for k in range(n_pieces):
            weights += [_attend_weights(scores[p], sink_rows[pairs[p][1]])
                        for p in range(k * per_piece, (k + 1) * per_piece)]
            lru.append(_lru_piece(k, xc, gates, gr, h_carry, None, None, ba_ref, bx_ref, lam_ref))
        for (e, m), (i, j) in zip(weights, pairs):
            ci = u * sub_chunks + i
            rows = slice(u * sub + i * CHUNK, u * sub + (i + 1) * CHUNK)
            o = _attend_output(e, m, vb_scr[slot, j, ci * KEY_BLOCK:ci * KEY_BLOCK + band, :], sink_rows[j])
            c0 = 2 * j * LANES
            mix_scr[rows, c0:c0 + LANES] = o[0:CHUNK].astype(BF16)
            mix_scr[rows, c0 + LANES:c0 + 2 * LANES] = o[CHUNK:2 * CHUNK].astype(BF16)

        h_carry = jnp.concatenate([h[sub - 1:sub, :] for _, h in lru], axis=1)
        rows = slice(u * sub, (u + 1) * sub)
        for k, (rec, _) in enumerate(lru):
            mix_scr[rows, Q_W + k * LANES:Q_W + (k + 1) * LANES] = rec.astype(BF16)

        y_att = jnp.dot(mix_scr[rows, 0:Q_W], wout_ref[0:Q_W, :], preferred_element_type=F32)
        y_rec = jnp.dot(mix_scr[rows, Q_W:Q_W + D_LRU], wout_ref[Q_W:Q_W + D_LRU, :], preferred_element_type=F32)
        y_ref[0, rows, :] = xs_in[u] + y_att + y_rec

    last = n_chunks * KEY_BLOCK
    for j in range(N_KV_HEADS):
        kb_scr[1 - slot, j, 0:hist, :] = kb_scr[slot, j, last:last + hist, :]
        vb_scr[1 - slot, j, 0:hist, 0:LANES] = vb_scr[slot, j, last:last + hist, 0:LANES]
    h_scr[0:1, :] = h_carry
    xr_scr[0:SUBLANES, :] = xr[sub - SUBLANES:sub, :]

    @pl.when(t == nt - 1)
    def _():
        nk_ref[0] = kr[sub - SWA_ROWS:sub, :]
        nv_ref[0] = vv[sub - SWA_ROWS:sub, :]
        nh_ref[0] = h_carry
        nc_ref[0] = xr[sub - (CONV_W - 1):sub, :]


def _const_spec(shape):
    zeros = (0,) * len(shape)
    return pl.BlockSpec(shape, lambda *_: zeros, pipeline_mode=pl.Buffered(1))


def _even_prompt(x, p):
    bsz, s_len, _ = x.shape
    tile = PROMPT_TILE
    cos_l, sin_l = _rope_tables(jnp.arange(tile))
    inv_row, sgn_row = _rope_rows()
    nt = s_len // tile
    key_rows = (tile // CHUNK + 2) * KEY_BLOCK
    consts = [p['g'], p['win'], p['qg'], p['kg'], p['sink'], p['convw'], p['convb'], p['wg'], p['ba'], p['bx'],
              p['lam'], p['wout']]
    out_shape = [
        jax.ShapeDtypeStruct((bsz, s_len, D_MODEL), F32),
        jax.ShapeDtypeStruct((bsz, SWA_ROWS, KV_W), F32),
        jax.ShapeDtypeStruct((bsz, SWA_ROWS, KV_W), F32),
        jax.ShapeDtypeStruct((bsz, 1, D_LRU), F32),
        jax.ShapeDtypeStruct((bsz, CONV_W - 1, D_LRU), F32),
    ]
    return pl.pallas_call(
        _even_prompt_kernel,
        out_shape=out_shape,
        grid=(bsz, nt),
        in_specs=[pl.BlockSpec((1, tile, D_MODEL), lambda b, t: (b, t, 0)),
                  _const_spec(cos_l.shape), _const_spec(sin_l.shape), _const_spec(inv_row.shape),
                  _const_spec(sgn_row.shape)] + [_const_spec(c.shape) for c in consts],
        out_specs=[pl.BlockSpec((1, tile, D_MODEL), lambda b, t: (b, t, 0)),
                   pl.BlockSpec((1, SWA_ROWS, KV_W), lambda b, t: (b, 0, 0)),
                   pl.BlockSpec((1, SWA_ROWS, KV_W), lambda b, t: (b, 0, 0)),
                   pl.BlockSpec((1, 1, D_LRU), lambda b, t: (b, 0, 0)),
                   pl.BlockSpec((1, CONV_W - 1, D_LRU), lambda b, t: (b, 0, 0))],
        scratch_shapes=[pltpu.VMEM((2, N_KV_HEADS, key_rows, LANES), BF16),
                        pltpu.VMEM((2, N_KV_HEADS, key_rows, 2 * LANES), BF16),
                        pltpu.VMEM((tile + SUBLANES, D_LRU), F32),
                        pltpu.VMEM((SUBLANES, D_LRU), F32),
                        pltpu.VMEM((tile, Q_W + D_LRU), BF16)],
        compiler_params=pltpu.CompilerParams(dimension_semantics=("arbitrary", "arbitrary"),
                                             vmem_limit_bytes=VMEM_LIMIT),
        name="even_mixer_prompt",
    )(x, cos_l, sin_l, inv_row, sgn_row, *consts)


def _even_sample_kernel(x_ref, cos_ref, sin_ref, kc_ref, vc_ref, h0_ref, xprev_ref,
                        g_ref, win_ref, qg_ref, kg_ref, sink_ref, convw_ref, convb_ref,
                        wg_ref, ba_ref, bx_ref, lam_ref, wout_ref,
                        y_ref, nk_ref, nv_ref, h_ref, xr_ref,
                        q_scr, k_scr, v_scr, mix_scr):
    rows_all = x_ref.shape[0]
    nseq = kc_ref.shape[0]
    t_len = rows_all // nseq
    x = x_ref[...]
    xn = _rms_rows(x, g_ref[...]).astype(BF16)
    cos_t = cos_ref[...]
    sin_t = sin_ref[...]

    o3 = Q_W + 2 * KV_W
    o5 = o3 + 2 * D_LRU
    hq = jnp.dot(xn, win_ref[:, 0:Q_W], preferred_element_type=F32)
    hkv = jnp.dot(xn, win_ref[:, Q_W:o3], preferred_element_type=F32)
    xr = jnp.dot(xn, win_ref[:, o3:o3 + D_LRU], preferred_element_type=F32)
    gr = jnp.dot(xn, win_ref[:, o3 + D_LRU:o5], preferred_element_type=F32)
    hqp = jnp.dot(xn, win_ref[:, o5:o5 + Q_W], preferred_element_type=F32)
    hkp = jnp.dot(xn, win_ref[:, o5 + Q_W:o5 + Q_W + KV_W], preferred_element_type=F32)
    ones_bd = _head_ones()
    norm_c = math.sqrt(HEAD_DIM)
    q_cos = cos_t * (qg_ref[0:1, :] * (norm_c * HEAD_DIM ** -0.5))
    q_sin = sin_t * (qg_ref[1:2, :] * (norm_c * HEAD_DIM ** -0.5))
    for n in range(Q_W // LANES):
        lanes = slice(n * LANES, (n + 1) * LANES)
        q_scr[:, lanes] = _norm_rope(hq[:, lanes], hqp[:, lanes], q_cos, q_sin, ones_bd)
    k_scr[...] = _norm_rope(hkv[:, 0:KV_W], hkp, cos_t * (kg_ref[0:1, :] * norm_c),
                            sin_t * (kg_ref[1:2, :] * norm_c), ones_bd)
    v_scr[...] = hkv[:, KV_W:2 * KV_W]

    band = BAND_BLOCKS * KEY_BLOCK
    col = _lane_iota((2 * t_len, band))
    col_mask = jnp.logical_or(col < 2 * KEY_BLOCK, (col % CHUNK) < t_len)
    ones_band = _ones_diag(band)
    pad = jnp.zeros((CHUNK - t_len, LANES), F32)

    sink_rows = [jnp.concatenate(
        [jnp.broadcast_to(sink_ref[2 * j:2 * j + 1, :], (t_len, LANES)),
         jnp.broadcast_to(sink_ref[2 * j + 1:2 * j + 2, :], (t_len, LANES))], axis=0) for j in range(N_KV_HEADS)]

    def group_body(gi, carry):
        work = []
        for s in range(SAMPLE_GROUP):
            b = gi * SAMPLE_GROUP + s
            r0 = pl.multiple_of(b * t_len, t_len)
            kc = kc_ref[b]
            vc = vc_ref[b]
            knew = k_scr[pl.ds(r0, t_len), :]
            vnew = v_scr[pl.ds(r0, t_len), :]
            nk_ref[b, 0:SWA_ROWS - t_len, :] = kc[t_len:SWA_ROWS]
            nk_ref[b, SWA_ROWS - t_len:SWA_ROWS, :] = knew
            nv_ref[b, 0:SWA_ROWS - t_len, :] = vc[t_len:SWA_ROWS]
            nv_ref[b, SWA_ROWS - t_len:SWA_ROWS, :] = vnew
            kfull = jnp.concatenate([kc, knew, pad], axis=0)
            vfull = jnp.concatenate([vc, vnew, pad], axis=0)
            k_blocks, v_blocks = _diag_blocks(kfull), _diag_blocks(vfull)
            for j in range(N_KV_HEADS):
                (ktop, kbot), (vtop, vbot) = k_blocks[j], v_blocks[j]
                kparts, vparts = [], []
                for c in range(BAND_BLOCKS):
                    rs = slice(c * CHUNK, (c + 1) * CHUNK)
                    kparts += [ktop[rs], kbot[rs]]
                    vparts += [vtop[rs], vbot[rs]]
                kb = jnp.concatenate(kparts, axis=0).astype(BF16)
                vb = jnp.concatenate([jnp.concatenate(vparts, axis=0).astype(BF16), ones_band], axis=1)
                c0 = 2 * j * LANES
                qs = jnp.concatenate([q_scr[pl.ds(r0, t_len), c0:c0 + LANES],
                                      q_scr[pl.ds(r0, t_len), c0 + LANES:c0 + 2 * LANES]], axis=0).astype(BF16)
                work.append((r0, j, vb, _attend_scores(qs, kb, col_mask)))
        weights = [_attend_weights(s, sink_rows[j]) for _, j, _, s in work]
        for (r0, j, vb, _), (e, m) in zip(work, weights):
            o = _attend_output(e, m, vb, sink_rows[j])
            c0 = 2 * j * LANES
            mix_scr[pl.ds(r0, t_len), c0:c0 + LANES] = o[0:t_len]
            mix_scr[pl.ds(r0, t_len), c0 + LANES:c0 + 2 * LANES] = o[t_len:2 * t_len]
        return carry

    lax.fori_loop(0, nseq // SAMPLE_GROUP, group_body, 0)

    tmod =_row_iota((rows_all, LANES)) % t_len
    tmod_w = _row_iota((rows_all, D_LRU)) % t_len
    xprev = xprev_ref[...]
    xs = []
    for i in range(CONV_W - 1):
        k = CONV_W - 1 - i
        own = pltpu.roll(xr, k, 0)
        cached = xprev if k == CONV_W - 1 else pltpu.roll(xprev, rows_all - (CONV_W - 1 - k), 0)
        xs.append(jnp.where(tmod_w >= k, own, cached))
    xs.append(xr)
    rec, h = _lru_branch(xs, gr, h0_ref[...], t_len, tmod, convw_ref, convb_ref, wg_ref, ba_ref, bx_ref, lam_ref)
    mix_scr[:, Q_W:Q_W + D_LRU] = rec
    xr_ref[...] = xr
    h_ref[...] = h
    y_ref[...] = x + jnp.dot(mix_scr[...].astype(BF16), wout_ref[...], preferred_element_type=F32)


def _even_sample(x2, cos_t, sin_t, kc, vc, h0_rows, xprev, p):
    rows_all = x2.shape[0]
    nseq = kc.shape[0]
    consts = [p['g'], p['win'], p['qg'], p['kg'], p['sink'], p['convw'], p['convb'], p['wg'], p['ba'], p['bx'],
              p['lam'], p['wout']]
    ins = [x2, cos_t, sin_t, kc, vc, h0_rows, xprev] + consts
    out_shape = [
        jax.ShapeDtypeStruct((rows_all, D_MODEL), F32),
        jax.ShapeDtypeStruct((nseq, SWA_ROWS, KV_W), F32),
        jax.ShapeDtypeStruct((nseq, SWA_ROWS, KV_W), F32),
        jax.ShapeDtypeStruct((rows_all, D_LRU), F32),
        jax.ShapeDtypeStruct((rows_all, D_LRU), F32),
    ]
    return pl.pallas_call(
        _even_sample_kernel,
        out_shape=out_shape,
        grid=(1,),
        in_specs=[_const_spec(a.shape) for a in ins],
        out_specs=[pl.BlockSpec(s.shape, lambda i, n=len(s.shape): (0,) * n) for s in out_shape],
        scratch_shapes=[pltpu.VMEM((rows_all, Q_W), F32),
                        pltpu.VMEM((rows_all, KV_W), F32),
                        pltpu.VMEM((rows_all, KV_W), F32),
                        pltpu.VMEM((rows_all, Q_W + D_LRU), F32)],
        compiler_params=pltpu.CompilerParams(dimension_semantics=("arbitrary",), vmem_limit_bytes=VMEM_LIMIT),
        name="even_mixer_sample",
    )(*ins)


def _ffn_rows(x, g_ref, w1_ref, w2_ref):
    xn = _rms_rows(x, g_ref[...]).astype(BF16)
    out = None
    step = D_MODEL
    for c in range(D_FF // step):
        hcol = jnp.dot(xn, w1_ref[:, c * step:(c + 1) * step].astype(BF16), preferred_element_type=F32)
        hcol = jnp.square(jnp.maximum(hcol, 0.0)).astype(BF16)
        part = jnp.dot(hcol, w2_ref[c * step:(c + 1) * step, :].astype(BF16), preferred_element_type=F32)
        out = part if out is None else out + part
    return x + out


def _ffn_kernel(xp_ref, xs_ref, g_ref, w1_ref, w2_ref, yp_ref, ys_ref):
    i = pl.program_id(0)
    yp_ref[...] = _ffn_rows(jnp.where(i == 0, xs_ref[...], xp_ref[...]), g_ref, w1_ref, w2_ref)

    @pl.when(i == 0)
    def _():
        ys_ref[...] = yp_ref[...]


def _sample_then_tiles_specs(tile, width, sample_rows=None):
    prompt = pl.BlockSpec((tile, width), lambda i: (jnp.maximum(i - 1, 0), 0))
    sample = pl.BlockSpec((sample_rows or tile, width), lambda i: (0, 0))
    return prompt, sample


def _ffn(xp, xs, g, w1, w2, layer):
    tile = xs.shape[0]
    n_tiles = xp.shape[0] // tile
    xp_spec, xs_spec = _sample_then_tiles_specs(tile, D_MODEL)
    layer_spec = lambda shape: pl.BlockSpec((None,) + shape[1:], lambda i: (layer, 0, 0),
                                            pipeline_mode=pl.Buffered(1))
    return pl.pallas_call(
        _ffn_kernel,
        out_shape=[jax.ShapeDtypeStruct(xp.shape, F32), jax.ShapeDtypeStruct(xs.shape, F32)],
        grid=(n_tiles + 1,),
        in_specs=[xp_spec, xs_spec, _const_spec(g.shape), layer_spec(w1.shape), layer_spec(w2.shape)],
        out_specs=list(_sample_then_tiles_specs(tile, D_MODEL)),
        compiler_params=pltpu.CompilerParams(dimension_semantics=("arbitrary",), vmem_limit_bytes=VMEM_LIMIT),
        name="channel_mlp",
    )(xp, xs, g, w1, w2)


def _gmlp_kernel(seg_s, xp_ref, xs_ref, g_ref, win_ref, vg_ref, wsp_ref, wss_ref, biasp_ref, biass_ref, wout_ref,
                 yp_ref, ys_ref, v_ref, v_scr):
    i = pl.program_id(0)
    is_s = i == 0
    tile = xp_ref.shape[0]
    sub = xs_ref.shape[0]
    n_sub = tile // sub
    xs_in, proj = [], []
    for u in range(n_sub):
        rows = slice(u * sub, (u + 1) * sub)
        x = jnp.where(is_s, xs_ref[...], xp_ref[rows, :]) if u == 0 else xp_ref[rows, :]
        xn = _rms_rows(x, g_ref[...]).astype(BF16)
        xs_in.append(x)
        proj.append((jnp.dot(xn, win_ref[:, D_C:2 * D_C].astype(BF16), preferred_element_type=F32),
                     jnp.dot(xn, win_ref[:, 0:D_C].astype(BF16), preferred_element_type=F32)))
    r = _row_iota((CHUNK_MLP, CHUNK_MLP))
    col = _lane_iota((CHUNK_MLP, CHUNK_MLP))
    keep_p = col <= r
    keep_s = jnp.logical_and((r // seg_s) == (col // seg_s), keep_p)
    ws = [jnp.where(is_s, jnp.where(keep_s, wss_ref[gi], 0.0), jnp.where(keep_p, wsp_ref[gi], 0.0)).astype(BF16)
          for gi in range(C_GROUPS)]
    step = MXU_WIDTH
    per_step = step // LANES
    for u in range(n_sub):
        rows = slice(u * sub, (u + 1) * sub)
        zv = _gelu(proj[u][0])
        zc = zv - jnp.mean(zv, axis=-1, keepdims=True)
        v = zc * lax.rsqrt(jnp.mean(zc * zc, axis=-1, keepdims=True) + EPS) * vg_ref[...]
        if u == 0:
            v_scr[...] = v
        vb = v.astype(BF16)
        sv = []
        for gi in range(C_GROUPS):
            lanes = slice(gi * LANES, (gi + 1) * LANES)
            bias = jnp.where(is_s, biass_ref[:, lanes], biasp_ref[:, lanes])
            blocks = [jnp.dot(ws[gi], vb[ch * CHUNK_MLP:(ch + 1) * CHUNK_MLP, lanes],
                              preferred_element_type=F32) + bias for ch in range(sub // CHUNK_MLP)]
            sv.append(jnp.concatenate(blocks, axis=0))
        acc = xs_in[u]
        for c in range(D_C // step):
            cols = slice(c * step, (c + 1) * step)
            gate = (_gelu(proj[u][1][:, cols]) *
                    jnp.concatenate(sv[c * per_step:(c + 1) * per_step], axis=1)).astype(BF16)
            acc = acc + jnp.dot(gate, wout_ref[cols, :].astype(BF16), preferred_element_type=F32)
        yp_ref[rows, :] = acc

    @pl.when(is_s)
    def _():
        ys_ref[...] = yp_ref[0:sub, :]
        v_ref[...] = v_scr[...]


def _gmlp(xp, xs, seg_s, g, win, vg, ws_p, ws_s, bias_p, bias_s, wout):
    rows_s = xs.shape[0]
    tile = GMLP_TILE
    n_tiles = xp.shape[0] // tile
    xp_spec, xs_spec = _sample_then_tiles_specs(tile, D_MODEL, rows_s)
    consts = (g, win, vg, ws_p, ws_s, bias_p, bias_s, wout)
    return pl.pallas_call(
        functools.partial(_gmlp_kernel, seg_s),
        out_shape=[jax.ShapeDtypeStruct(xp.shape, F32), jax.ShapeDtypeStruct(xs.shape, F32),
                   jax.ShapeDtypeStruct((rows_s, D_C), F32)],
        grid=(n_tiles + 1,),
        in_specs=[xp_spec, xs_spec] + [_const_spec(a.shape) for a in consts],
        out_specs=list(_sample_then_tiles_specs(tile, D_MODEL, rows_s)) +
                  [pl.BlockSpec((rows_s, D_C), lambda i: (0, 0))],
        scratch_shapes=[pltpu.VMEM((rows_s, D_C), F32)],
        compiler_params=pltpu.CompilerParams(dimension_semantics=("arbitrary",), vmem_limit_bytes=VMEM_LIMIT),
        name="gmlp_mixer",
    )(xp, xs, *consts)


def _rope_inv():
    half = HEAD_DIM // 2
    return ROPE_THETA ** (-jnp.arange(half, dtype=F32) / half)


def _rope_rows():
    inv = _rope_inv()
    sgn = jnp.ones((HEAD_DIM // 2,), F32)
    return jnp.concatenate([inv] * 4)[None, :], jnp.concatenate([-sgn, sgn, -sgn, sgn])[None, :]


def _rope_tables(pos):
    inv_row, sgn_row = _rope_rows()
    ang = pos.astype(F32)[:, None] * inv_row
    return jnp.cos(ang), jnp.sin(ang) * sgn_row


def _swap_head_halves(a):
    half = HEAD_DIM // 2
    return a.reshape(a.shape[:-1] + (-1, 2, half))[..., ::-1, :].reshape(a.shape)


def _gate_weights(wa, wx):
    nb = LRU_BLOCKS // 2
    eye = jnp.eye(nb, dtype=bool)

    def half_diag(w, hh):
        blocks = w[hh * nb:(hh + 1) * nb]
        return jnp.where(eye[:, None, :, None], blocks[:, :, None, :], 0.0).reshape(nb * LRU_BW, nb * LRU_BW)
    return jnp.stack([jnp.concatenate([half_diag(wa, hh), half_diag(wx, hh)], axis=1)
                      for hh in range(2)]).astype(BF16)


def kernel(x_prompt, x_sample, cache_swa_k, cache_swa_v, state_lru_h, state_lru_conv, e_norm_g, e_w_in, e_q_norm_g, e_k_norm_g, e_sinks, e_conv_w, e_conv_b, e_gate_a_w, e_gate_a_b, e_gate_x_w, e_gate_x_b, e_lru_lambda, e_w_out, o_norm_g, o_w_in, o_v_norm_g, o_spatial_w, o_spatial_b, o_w_out, ffn_norm_g, ffn_w1, ffn_w2):
    bsz, s_len, _ = x_prompt.shape
    nseq, t_len, _ = x_sample.shape
    past_len = PAST_LEN
    row = lambda a: a.reshape(1, -1)

    e = 0
    p = {
        'g': row(e_norm_g[e]),
        'win': jnp.concatenate([e_w_in[e], _swap_head_halves(e_w_in[e][:, :Q_W + KV_W])], axis=1).astype(BF16),
        'qg': jnp.tile(jnp.stack([e_q_norm_g[e], _swap_head_halves(e_q_norm_g[e])]), (1, LANES // HEAD_DIM)),
        'kg': jnp.tile(jnp.stack([e_k_norm_g[e], _swap_head_halves(e_k_norm_g[e])]), (1, LANES // HEAD_DIM)),
        'sink': jnp.repeat(e_sinks[e], HEAD_DIM).reshape(N_Q_HEADS // 2, LANES),
        'convw': e_conv_w[e],
        'convb': row(e_conv_b[e]),
        'wg': _gate_weights(e_gate_a_w[e], e_gate_x_w[e]),
        'ba': row(e_gate_a_b[e]),
        'bx': row(e_gate_x_b[e]),
        'lam': row(e_lru_lambda[e]),
        'wout': e_w_out[e].astype(BF16),
    }
    yp, nk_p, nv_p, nh_p, nc_p = _even_prompt(x_prompt, p)

    cos_s, sin_s = _rope_tables(past_len + jnp.arange(t_len))
    cos_s = jnp.tile(cos_s, (nseq, 1))
    sin_s = jnp.tile(sin_s, (nseq, 1))
    kc = cache_swa_k[e].reshape(nseq, SWA_ROWS, KV_W)
    vc = cache_swa_v[e].reshape(nseq, SWA_ROWS, KV_W)
    h0_rows = jnp.repeat(state_lru_h[e], t_len, axis=0)
    xprev = jnp.pad(state_lru_conv[e], ((0, 0), (0, t_len - (CONV_W - 1)), (0, 0))).reshape(nseq * t_len, D_LRU)
    ys, nk_s, nv_s, h_s, xr_s = _even_sample(x_sample.reshape(nseq * t_len, D_MODEL), cos_s, sin_s, kc, vc,
                                             h0_rows, xprev, p)
    nh_s = h_s.reshape(nseq, t_len, D_LRU)[:, -1]
    nc_s = xr_s.reshape(nseq, t_len, D_LRU)[:, t_len - (CONV_W - 1):]

    w1, w2 = ffn_w1, ffn_w2
    yp, ys = _ffn(yp.reshape(bsz * s_len, D_MODEL), ys, row(ffn_norm_g[0]), w1, w2, 0)

    o = 0
    bias_p = jnp.repeat(o_spatial_b[o].T, LANES, axis=1)
    reps = CHUNK_MLP // t_len
    ws_s = jnp.tile(o_spatial_w[o][:, :t_len, :t_len], (1, reps, reps))
    bias_s = jnp.tile(jnp.repeat(o_spatial_b[o][:, :t_len].T, LANES, axis=1), (reps, 1))
    yp, ys, gv_s = _gmlp(yp, ys, t_len, row(o_norm_g[o]), o_w_in[o], row(o_v_norm_g[o]),
                         o_spatial_w[o], ws_s, bias_p, bias_s, o_w_out[o])

    yp, ys = _ffn(yp, ys, row(ffn_norm_g[1]), w1, w2, 1)

    kv_shape = (1, -1, SWA_ROWS, N_KV_HEADS, HEAD_DIM)
    return (yp.reshape(bsz, s_len, D_MODEL), ys.reshape(nseq, t_len, D_MODEL),
            nk_p.reshape(kv_shape), nv_p.reshape(kv_shape), nh_p.reshape(1, bsz, D_LRU), nc_p[None],
            nk_s.reshape(kv_shape), nv_s.reshape(kv_shape), nh_s[None], nc_s[None],
            gv_s.reshape(1, nseq, t_len, D_C))
```

```python
import functools
import math

import jax
import jax.numpy as jnp
from jax import lax
from jax.experimental import pallas as pl
from jax.experimental.pallas import tpu as pltpu

F32 = jnp.float32
BF16 = jnp.bfloat16

D_MODEL = 1024
CHUNK = 64
HEAD_DIM = 64
N_Q_HEADS = 8
N_KV_HEADS = 2
SWA_ROWS = 128
PAST_LEN = 4096
ROPE_THETA = 10000.0
NEG = -1e30
D_LRU = 512
LRU_BLOCKS = 8
LRU_BW = D_LRU // LRU_BLOCKS
CONV_W = 4
LRU_C = 8.0
Q_W = N_Q_HEADS * HEAD_DIM
KV_W = N_KV_HEADS * HEAD_DIM
CHUNK_MLP = 128
D_C = D_MODEL
C_GROUPS = 8
D_FF = 4 * D_MODEL
EPS = 1e-6

LANES = 128
SUBLANES = 8
MXU_WIDTH = 256
KEY_BLOCK = 2 * CHUNK
BAND_BLOCKS = 3
VMEM_LIMIT = 56 * 1024 * 1024

PROMPT_TILE = 1024
PROMPT_SUBTILE = 256
GMLP_TILE = 1024
SAMPLE_GROUP = 8


def _gelu(x):
    k1 = -2.0 * math.sqrt(2.0 / math.pi) * math.log2(math.e)
    return x / (1.0 + jnp.exp2(x * (k1 + (k1 * 0.044715) * (x * x))))


def _rms_rows(x, g):
    return x * lax.rsqrt(jnp.mean(x * x, axis=-1, keepdims=True) + EPS) * g


def _lane_iota(shape):
    return lax.broadcasted_iota(jnp.int32, shape, 1)


def _row_iota(shape):
    return lax.broadcasted_iota(jnp.int32, shape, 0)


def _head_ones():
    return ((_row_iota((LANES, LANES)) // HEAD_DIM) == (_lane_iota((LANES, LANES)) // HEAD_DIM)).astype(BF16)


def _norm_rope(x, xp, g_cos, g_sin, ones_bd):
    ss = jnp.dot((x * x).astype(BF16), ones_bd, preferred_element_type=F32)
    return (x * g_cos + xp * g_sin) * lax.rsqrt(ss + HEAD_DIM * EPS)


def _diag_blocks(t):
    lo = _lane_iota(t.shape) < HEAD_DIM
    swapped = pltpu.roll(t, HEAD_DIM, 1)
    return [(jnp.where(lo, t, 0.0), jnp.where(lo, 0.0, swapped)),
            (jnp.where(lo, swapped, 0.0), jnp.where(lo, 0.0, t))]


def _ones_diag(rows):
    r = _row_iota((rows, LANES))
    return (((r % KEY_BLOCK) // CHUNK) == (_lane_iota((rows, LANES)) // HEAD_DIM)).astype(BF16)


def _attend_scores(qs, kb, col_mask):
    s = lax.dot_general(qs, kb, (((1,), (1,)), ((), ())), preferred_element_type=F32)
    return s if col_mask is None else jnp.where(col_mask, s, NEG)


def _attend_weights(s, sink_rows):
    mx = jnp.maximum(jnp.maximum(s[:, :LANES], s[:, LANES:2 * LANES]), s[:, 2 * LANES:])
    lo = _lane_iota(mx.shape) < HEAD_DIM
    m0 = jnp.max(jnp.where(lo, mx, -jnp.inf), axis=1, keepdims=True)
    m1 = jnp.max(jnp.where(lo, -jnp.inf, mx), axis=1, keepdims=True)
    m = jnp.maximum(jnp.where(lo, m0, m1), sink_rows)
    e = jnp.exp(s - jnp.concatenate([m, m, m], axis=1))
    return e.astype(BF16), m


def _attend_output(e, m, vb, sink_rows):
    od = jnp.dot(e, vb, preferred_element_type=F32)
    den = od[:, LANES:] + jnp.exp(sink_rows - m)
    return od[:, :LANES] / den


def _lru_scan(a, b, seg, tmod):
    d = 1
    while d < seg:
        ok = tmod >= d
        b = jnp.where(ok, a * pltpu.roll(b, d, 0), 0.0) + b
        if 2 * d < seg:
            a = jnp.where(ok, a * pltpu.roll(a, d, 0), a)
        d *= 2
    return b


def _lru_scan_tile(a, b, h_prev):
    rows = a.shape[0]
    groups = rows // SUBLANES
    a3 = a.reshape(groups, SUBLANES, LANES)
    b3 = b.reshape(groups, SUBLANES, LANES)
    sub = lax.broadcasted_iota(jnp.int32, a3.shape, 1)
    d = 1
    while d < SUBLANES:
        ok = sub >= d
        b3 = jnp.where(ok, a3 * pltpu.roll(b3, d, 1), 0.0) + b3
        a3 = jnp.where(ok, a3 * pltpu.roll(a3, d, 1), a3)
        d *= 2
    carry = jnp.broadcast_to(h_prev, (SUBLANES, LANES))
    out = []
    for g in range(groups):
        hg = a3[g] * carry + b3[g]
        out.append(hg)
        carry = jnp.broadcast_to(hg[SUBLANES - 1:SUBLANES, :], (SUBLANES, LANES))
    return jnp.concatenate(out, axis=0)


def _lru_branch(xs, gr, h_in, seg, tmod, convw_ref, convb_ref, wg_ref, ba_ref, bx_ref, lam_ref):
    xc, gates = _lru_gates(xs, convw_ref, convb_ref, wg_ref)
    parts = [_lru_piece(k, xc, gates, gr, h_in, seg, tmod, ba_ref, bx_ref, lam_ref) for k in range(D_LRU // LANES)]
    return jnp.concatenate([p[0] for p in parts], axis=1), jnp.concatenate([p[1] for p in parts], axis=1)


def _lru_gates(xs, convw_ref, convb_ref, wg_ref):
    xc = xs[0] * convw_ref[0:1, :] + convb_ref[...]
    for i in range(1, CONV_W):
        xc = xc + xs[i] * convw_ref[i:i + 1, :]
    xcb = xc.astype(BF16)
    half = D_LRU // 2
    return xc, [jnp.dot(xcb[:, hh * half:(hh + 1) * half], wg_ref[hh], preferred_element_type=F32)
                for hh in range(2)]


def _lru_piece(k, xc, gates, gr, h_in, seg, tmod, ba_ref, bx_ref, lam_ref):
    half = D_LRU // 2
    hh, tt = divmod(k, half // LANES)
    sl = slice(k * LANES, (k + 1) * LANES)
    g = gates[hh]
    r = jax.nn.sigmoid(g[:, tt * LANES:(tt + 1) * LANES] + ba_ref[:, sl])
    gi = jax.nn.sigmoid(g[:, half + tt * LANES:half + (tt + 1) * LANES] + bx_ref[:, sl])
    log_a = r * (-LRU_C * jax.nn.softplus(-lam_ref[:, sl]))
    a = jnp.exp(log_a)
    b = jnp.sqrt(-jnp.tanh(log_a) * (a * a + 1.0)) * (gi * xc[:, sl])
    if seg is None:
        h = _lru_scan_tile(a, b, h_in[:, sl])
    else:
        h = _lru_scan(a, b + jnp.where(tmod == 0, a * h_in[:, sl], 0.0), seg, tmod)
    return h * _gelu(gr[:, sl]), h


def _even_prompt_kernel(x_ref, cos_ref, sin_ref, inv_ref, sgn_ref, g_ref, win_ref, qg_ref, kg_ref, sink_ref, convw_ref, convb_ref,
                        wg_ref, ba_ref, bx_ref, lam_ref, wout_ref,
                        y_ref, nk_ref, nv_ref, nh_ref, nc_ref,
                        kb_scr, vb_scr, xr_scr, h_scr, mix_scr):
    t = pl.program_id(1)
    nt = pl.num_programs(1)
    tile = x_ref.shape[1]
    n_chunks = tile // CHUNK
    hist = 2 * KEY_BLOCK
    slot = t % 2

    @pl.when(t == 0)
    def _():
        ones = _ones_diag(vb_scr.shape[2])
        for j in range(N_KV_HEADS):
            kb_scr[0, j, 0:hist, :] = jnp.zeros((hist, LANES), BF16)
            vb_scr[0, j, 0:hist, 0:LANES] = jnp.zeros((hist, LANES), BF16)
            for s in range(2):
                vb_scr[s, j, :, LANES:2 * LANES] = ones
        xr_scr[0:SUBLANES, :] = jnp.zeros((SUBLANES, D_LRU), F32)
        h_scr[...] = jnp.zeros(h_scr.shape, F32)

    sub = PROMPT_SUBTILE
    n_sub = tile // sub
    sub_chunks = sub // CHUNK
    o3 = Q_W + 2 * KV_W
    xs_in, proj = [], []
    for u in range(n_sub):
        x = x_ref[0, u * sub:(u + 1) * sub, :]
        xn = _rms_rows(x, g_ref[...]).astype(BF16)
        xs_in.append(x)
        o5 = o3 + 2 * D_LRU
        proj.append((jnp.dot(xn, win_ref[:, 0:Q_W], preferred_element_type=F32),
                     jnp.dot(xn, win_ref[:, Q_W:o3], preferred_element_type=F32),
                     jnp.dot(xn, win_ref[:, o3:o3 + D_LRU], preferred_element_type=F32),
                     jnp.dot(xn, win_ref[:, o3 + D_LRU:o5], preferred_element_type=F32),
                     jnp.dot(xn, win_ref[:, o5:o5 + Q_W], preferred_element_type=F32),
                     jnp.dot(xn, win_ref[:, o5 + Q_W:o5 + Q_W + KV_W], preferred_element_type=F32)))
    for u in range(n_sub):
        xr_scr[SUBLANES + u * sub:SUBLANES + (u + 1) * sub, :] = proj[u][2]

    base = (t * tile).astype(F32) * inv_ref[...]
    cos_b = jnp.cos(base)
    sin_b = jnp.sin(base) * sgn_ref[...]

    band = BAND_BLOCKS * KEY_BLOCK
    col = _lane_iota((2 * CHUNK, band))
    sink_rows = [jnp.concatenate(
        [jnp.broadcast_to(sink_ref[2 * j:2 * j + 1, :], (CHUNK, LANES)),
         jnp.broadcast_to(sink_ref[2 * j + 1:2 * j + 2, :], (CHUNK, LANES))], axis=0) for j in range(N_KV_HEADS)]
    ones_bd = _head_ones()
    norm_c = math.sqrt(HEAD_DIM)
    q_all = []
    for u in range(n_sub):
        hq, hkv, xr, gr, hqp, hkp = proj[u]
        cos_l = cos_ref[u * sub:(u + 1) * sub, :]
        sin_l = sin_ref[u * sub:(u + 1) * sub, :]
        cos_t = cos_b * cos_l - sin_b * sin_l
        sin_t = sin_b * cos_l + cos_b * sin_l
        q_cos = cos_t * (qg_ref[0:1, :] * (norm_c * HEAD_DIM ** -0.5))
        q_sin = sin_t * (qg_ref[1:2, :] * (norm_c * HEAD_DIM ** -0.5))
        q_tiles = [_norm_rope(hq[:, n * LANES:(n + 1) * LANES], hqp[:, n * LANES:(n + 1) * LANES],
                              q_cos, q_sin, ones_bd).astype(BF16) for n in range(Q_W // LANES)]
        kr = _norm_rope(hkv[:, 0:KV_W], hkp, cos_t * (kg_ref[0:1, :] * norm_c), sin_t * (kg_ref[1:2, :] * norm_c),
                        ones_bd)
        vv = hkv[:, KV_W:2 * KV_W]
        k_blocks, v_blocks = _diag_blocks(kr), _diag_blocks(vv)
        for j in range(N_KV_HEADS):
            (ktop, kbot), (vtop, vbot) = k_blocks[j], v_blocks[j]
            for i in range(sub_chunks):
                r0 = hist + (u * sub_chunks + i) * KEY_BLOCK
                rows = slice(i * CHUNK, (i + 1) * CHUNK)
                kb_scr[slot, j, r0:r0 + CHUNK, :] = ktop[rows].astype(BF16)
                kb_scr[slot, j, r0 + CHUNK:r0 + KEY_BLOCK, :] = kbot[rows].astype(BF16)
                vb_scr[slot, j, r0:r0 + CHUNK, 0:LANES] = vtop[rows].astype(BF16)
                vb_scr[slot, j, r0 + CHUNK:r0 + KEY_BLOCK, 0:LANES] = vbot[rows].astype(BF16)
        q_all.append(q_tiles)

    pairs = [(i, j) for i in range(sub_chunks) for j in range(N_KV_HEADS)]
    scores_all, gates_all = [], []
    for u in range(n_sub):
        q_tiles = q_all[u]
        scores = []
        for i, j in pairs:
            rows = slice(i * CHUNK, (i + 1) * CHUNK)
            ci = u * sub_chunks + i
            col_mask = None
            if ci < BAND_BLOCKS - 1:
                col_mask = col >= (1 - jnp.minimum(t, 1)) * ((BAND_BLOCKS - 1 - ci) * KEY_BLOCK)
            qs = jnp.concatenate([q_tiles[2 * j][rows], q_tiles[2 * j + 1][rows]], axis=0)
            scores.append(_attend_scores(qs, kb_scr[slot, j, ci * KEY_BLOCK:ci * KEY_BLOCK + band, :], col_mask))
        scores_all.append(scores)
        first = SUBLANES + u * sub
        taps = [xr_scr[first - (CONV_W - 1 - i):first - (CONV_W - 1 - i) + sub, :] for i in range(CONV_W - 1)]
        gates_all.append(_lru_gates(taps + [proj[u][2]], convw_ref, convb_ref, wg_ref))

    h_carry = h_scr[0:1, :]
    for u in range(n_sub):
        xr, gr = proj[u][2], proj[u][3]
        scores = scores_all[u]
        xc, gates = gates_all[u]
        n_pieces = D_LRU // LANES
        per_piece = len(pairs) // n_pieces
        weights, lru = [], []
        for k in range(n_pieces):
            weights += [_attend_weights(scores[p], sink_rows[pairs[p][1]])
                        for p in range(k * per_piece, (k + 1) * per_piece)]
            lru.append(_lru_piece(k, xc, gates, gr, h_carry, None, None, ba_ref, bx_ref, lam_ref))
        for (e, m), (i, j) in zip(weights, pairs):
            ci = u * sub_chunks + i
            rows = slice(u * sub + i * CHUNK, u * sub + (i + 1) * CHUNK)
            o = _attend_output(e, m, vb_scr[slot, j, ci * KEY_BLOCK:ci * KEY_BLOCK + band, :], sink_rows[j])
            c0 = 2 * j * LANES
            mix_scr[rows, c0:c0 + LANES] = o[0:CHUNK].astype(BF16)
            mix_scr[rows, c0 + LANES:c0 + 2 * LANES] = o[CHUNK:2 * CHUNK].astype(BF16)

        h_carry = jnp.concatenate([h[sub - 1:sub, :] for _, h in lru], axis=1)
        rows = slice(u * sub, (u + 1) * sub)
        for k, (rec, _) in enumerate(lru):
            mix_scr[rows, Q_W + k * LANES:Q_W + (k + 1) * LANES] = rec.astype(BF16)

        y_att = jnp.dot(mix_scr[rows, 0:Q_W], wout_ref[0:Q_W, :], preferred_element_type=F32)
        y_rec = jnp.dot(mix_scr[rows, Q_W:Q_W + D_LRU], wout_ref[Q_W:Q_W + D_LRU, :], preferred_element_type=F32)
        y_ref[0, rows, :] = xs_in[u] + y_att + y_rec

    last = n_chunks * KEY_BLOCK
    for j in range(N_KV_HEADS):
        kb_scr[1 - slot, j, 0:hist, :] = kb_scr[slot, j, last:last + hist, :]
        vb_scr[1 - slot, j, 0:hist, 0:LANES] = vb_scr[slot, j, last:last + hist, 0:LANES]
    h_scr[0:1, :] = h_carry
    xr_scr[0:SUBLANES, :] = xr[sub - SUBLANES:sub, :]

    @pl.when(t == nt - 1)
    def _():
        nk_ref[0] = kr[sub - SWA_ROWS:sub, :]
        nv_ref[0] = vv[sub - SWA_ROWS:sub, :]
        nh_ref[0] = h_carry
        nc_ref[0] = xr[sub - (CONV_W - 1):sub, :]


def _const_spec(shape):
    zeros = (0,) * len(shape)
    return pl.BlockSpec(shape, lambda *_: zeros, pipeline_mode=pl.Buffered(1))


def _even_prompt(x, p):
    bsz, s_len, _ = x.shape
    tile = PROMPT_TILE
    cos_l, sin_l = _rope_tables(jnp.arange(tile))
    inv_row, sgn_row = _rope_rows()
    nt = s_len // tile
    key_rows = (tile // CHUNK + 2) * KEY_BLOCK
    consts = [p['g'], p['win'], p['qg'], p['kg'], p['sink'], p['convw'], p['convb'], p['wg'], p['ba'], p['bx'],
              p['lam'], p['wout']]
    out_shape = [
        jax.ShapeDtypeStruct((bsz, s_len, D_MODEL), F32),
        jax.ShapeDtypeStruct((bsz, SWA_ROWS, KV_W), F32),
        jax.ShapeDtypeStruct((bsz, SWA_ROWS, KV_W), F32),
        jax.ShapeDtypeStruct((bsz, 1, D_LRU), F32),
        jax.ShapeDtypeStruct((bsz, CONV_W - 1, D_LRU), F32),
    ]
    return pl.pallas_call(
        _even_prompt_kernel,
        out_shape=out_shape,
        grid=(bsz, nt),
        in_specs=[pl.BlockSpec((1, tile, D_MODEL), lambda b, t: (b, t, 0)),
                  _const_spec(cos_l.shape), _const_spec(sin_l.shape), _const_spec(inv_row.shape),
                  _const_spec(sgn_row.shape)] + [_const_spec(c.shape) for c in consts],
        out_specs=[pl.BlockSpec((1, tile, D_MODEL), lambda b, t: (b, t, 0)),
                   pl.BlockSpec((1, SWA_ROWS, KV_W), lambda b, t: (b, 0, 0)),
                   pl.BlockSpec((1, SWA_ROWS, KV_W), lambda b, t: (b, 0, 0)),
                   pl.BlockSpec((1, 1, D_LRU), lambda b, t: (b, 0, 0)),
                   pl.BlockSpec((1, CONV_W - 1, D_LRU), lambda b, t: (b, 0, 0))],
        scratch_shapes=[pltpu.VMEM((2, N_KV_HEADS, key_rows, LANES), BF16),
                        pltpu.VMEM((2, N_KV_HEADS, key_rows, 2 * LANES), BF16),
                        pltpu.VMEM((tile + SUBLANES, D_LRU), F32),
                        pltpu.VMEM((SUBLANES, D_LRU), F32),
                        pltpu.VMEM((tile, Q_W + D_LRU), BF16)],
        compiler_params=pltpu.CompilerParams(dimension_semantics=("arbitrary", "arbitrary"),
                                             vmem_limit_bytes=VMEM_LIMIT),
        name="even_mixer_prompt",
    )(x, cos_l, sin_l, inv_row, sgn_row, *consts)


def _even_sample_kernel(x_ref, cos_ref, sin_ref, kc_ref, vc_ref, h0_ref, xprev_ref,
                        g_ref, win_ref, qg_ref, kg_ref, sink_ref, convw_ref, convb_ref,
                        wg_ref, ba_ref, bx_ref, lam_ref, wout_ref,
                        y_ref, nk_ref, nv_ref, h_ref, xr_ref,
                        q_scr, k_scr, v_scr, mix_scr):
    rows_all = x_ref.shape[0]
    nseq = kc_ref.shape[0]
    t_len = rows_all // nseq
    x = x_ref[...]
    xn = _rms_rows(x, g_ref[...]).astype(BF16)
    cos_t = cos_ref[...]
    sin_t = sin_ref[...]

    o3 = Q_W + 2 * KV_W
    o5 = o3 + 2 * D_LRU
    hq = jnp.dot(xn, win_ref[:, 0:Q_W], preferred_element_type=F32)
    hkv = jnp.dot(xn, win_ref[:, Q_W:o3], preferred_element_type=F32)
    xr = jnp.dot(xn, win_ref[:, o3:o3 + D_LRU], preferred_element_type=F32)
    gr = jnp.dot(xn, win_ref[:, o3 + D_LRU:o5], preferred_element_type=F32)
    hqp = jnp.dot(xn, win_ref[:, o5:o5 + Q_W], preferred_element_type=F32)
    hkp = jnp.dot(xn, win_ref[:, o5 + Q_W:o5 + Q_W + KV_W], preferred_element_type=F32)
    ones_bd = _head_ones()
    norm_c = math.sqrt(HEAD_DIM)
    q_cos = cos_t * (qg_ref[0:1, :] * (norm_c * HEAD_DIM ** -0.5))
    q_sin = sin_t * (qg_ref[1:2, :] * (norm_c * HEAD_DIM ** -0.5))
    for n in range(Q_W // LANES):
        lanes = slice(n * LANES, (n + 1) * LANES)
        q_scr[:, lanes] = _norm_rope(hq[:, lanes], hqp[:, lanes], q_cos, q_sin, ones_bd)
    k_scr[...] = _norm_rope(hkv[:, 0:KV_W], hkp, cos_t * (kg_ref[0:1, :] * norm_c),
                            sin_t * (kg_ref[1:2, :] * norm_c), ones_bd)
    v_scr[...] = hkv[:, KV_W:2 * KV_W]

    band = BAND_BLOCKS * KEY_BLOCK
    col = _lane_iota((2 * t_len, band))
    col_mask = jnp.logical_or(col < 2 * KEY_BLOCK, (col % CHUNK) < t_len)
    ones_band = _ones_diag(band)
    pad = jnp.zeros((CHUNK - t_len, LANES), F32)

    sink_rows = [jnp.concatenate(
        [jnp.broadcast_to(sink_ref[2 * j:2 * j + 1, :], (t_len, LANES)),
         jnp.broadcast_to(sink_ref[2 * j + 1:2 * j + 2, :], (t_len, LANES))], axis=0) for j in range(N_KV_HEADS)]

    def group_body(gi, carry):
        work = []
        for s in range(SAMPLE_GROUP):
            b = gi * SAMPLE_GROUP + s
            r0 = pl.multiple_of(b * t_len, t_len)
            kc = kc_ref[b]
            vc = vc_ref[b]
            knew = k_scr[pl.ds(r0, t_len), :]
            vnew = v_scr[pl.ds(r0, t_len), :]
            nk_ref[b, 0:SWA_ROWS - t_len, :] = kc[t_len:SWA_ROWS]
            nk_ref[b, SWA_ROWS - t_len:SWA_ROWS, :] = knew
            nv_ref[b, 0:SWA_ROWS - t_len, :] = vc[t_len:SWA_ROWS]
            nv_ref[b, SWA_ROWS - t_len:SWA_ROWS, :] = vnew
            kfull = jnp.concatenate([kc, knew, pad], axis=0)
            vfull = jnp.concatenate([vc, vnew, pad], axis=0)
            k_blocks, v_blocks = _diag_blocks(kfull), _diag_blocks(vfull)
            for j in range(N_KV_HEADS):
                (ktop, kbot), (vtop, vbot) = k_blocks[j], v_blocks[j]
                kparts, vparts = [], []
                for c in range(BAND_BLOCKS):
                    rs = slice(c * CHUNK, (c + 1) * CHUNK)
                    kparts += [ktop[rs], kbot[rs]]
                    vparts += [vtop[rs], vbot[rs]]
                kb = jnp.concatenate(kparts, axis=0).astype(BF16)
                vb = jnp.concatenate([jnp.concatenate(vparts, axis=0).astype(BF16), ones_band], axis=1)
                c0 = 2 * j * LANES
                qs = jnp.concatenate([q_scr[pl.ds(r0, t_len), c0:c0 + LANES],
                                      q_scr[pl.ds(r0, t_len), c0 + LANES:c0 + 2 * LANES]], axis=0).astype(BF16)
                work.append((r0, j, vb, _attend_scores(qs, kb, col_mask)))
        weights = [_attend_weights(s, sink_rows[j]) for _, j, _, s in work]
        for (r0, j, vb, _), (e, m) in zip(work, weights):
            o = _attend_output(e, m, vb, sink_rows[j])
            c0 = 2 * j * LANES
            mix_scr[pl.ds(r0, t_len), c0:c0 + LANES] = o[0:t_len]
            mix_scr[pl.ds(r0, t_len), c0 + LANES:c0 + 2 * LANES] = o[t_len:2 * t_len]
        return carry

    lax.fori_loop(0, nseq // SAMPLE_GROUP, group_body, 0)

    tmod =_row_iota((rows_all, LANES)) % t_len
    tmod_w = _row_iota((rows_all, D_LRU)) % t_len
    xprev = xprev_ref[...]
    xs = []
    for i in range(CONV_W - 1):
        k = CONV_W - 1 - i
        own = pltpu.roll(xr, k, 0)
        cached = xprev if k == CONV_W - 1 else pltpu.roll(xprev, rows_all - (CONV_W - 1 - k), 0)
        xs.append(jnp.where(tmod_w >= k, own, cached))
    xs.append(xr)
    rec, h = _lru_branch(xs, gr, h0_ref[...], t_len, tmod, convw_ref, convb_ref, wg_ref, ba_ref, bx_ref, lam_ref)
    mix_scr[:, Q_W:Q_W + D_LRU] = rec
    xr_ref[...] = xr
    h_ref[...] = h
    y_ref[...] = x + jnp.dot(mix_scr[...].astype(BF16), wout_ref[...], preferred_element_type=F32)


def _even_sample(x2, cos_t, sin_t, kc, vc, h0_rows, xprev, p):
    rows_all = x2.shape[0]
    nseq = kc.shape[0]
    consts = [p['g'], p['win'], p['qg'], p['kg'], p['sink'], p['convw'], p['convb'], p['wg'], p['ba'], p['bx'],
              p['lam'], p['wout']]
    ins = [x2, cos_t, sin_t, kc, vc, h0_rows, xprev] + consts
    out_shape = [
        jax.ShapeDtypeStruct((rows_all, D_MODEL), F32),
        jax.ShapeDtypeStruct((nseq, SWA_ROWS, KV_W), F32),
        jax.ShapeDtypeStruct((nseq, SWA_ROWS, KV_W), F32),
        jax.ShapeDtypeStruct((rows_all, D_LRU), F32),
        jax.ShapeDtypeStruct((rows_all, D_LRU), F32),
    ]
    return pl.pallas_call(
        _even_sample_kernel,
        out_shape=out_shape,
        grid=(1,),
        in_specs=[_const_spec(a.shape) for a in ins],
        out_specs=[pl.BlockSpec(s.shape, lambda i, n=len(s.shape): (0,) * n) for s in out_shape],
        scratch_shapes=[pltpu.VMEM((rows_all, Q_W), F32),
                        pltpu.VMEM((rows_all, KV_W), F32),
                        pltpu.VMEM((rows_all, KV_W), F32),
                        pltpu.VMEM((rows_all, Q_W + D_LRU), F32)],
        compiler_params=pltpu.CompilerParams(dimension_semantics=("arbitrary",), vmem_limit_bytes=VMEM_LIMIT),
        name="even_mixer_sample",
    )(*ins)


def _ffn_rows(x, g_ref, w1_ref, w2_ref):
    xn = _rms_rows(x, g_ref[...]).astype(BF16)
    out = None
    step = D_MODEL
    for c in range(D_FF // step):
        hcol = jnp.dot(xn, w1_ref[:, c * step:(c + 1) * step].astype(BF16), preferred_element_type=F32)
        hcol = jnp.square(jnp.maximum(hcol, 0.0)).astype(BF16)
        part = jnp.dot(hcol, w2_ref[c * step:(c + 1) * step, :].astype(BF16), preferred_element_type=F32)
        out = part if out is None else out + part
    return x + out


def _ffn_kernel(xp_ref, xs_ref, g_ref, w1_ref, w2_ref, yp_ref, ys_ref):
    i = pl.program_id(0)
    yp_ref[...] = _ffn_rows(jnp.where(i == 0, xs_ref[...], xp_ref[...]), g_ref, w1_ref, w2_ref)

    @pl.when(i == 0)
    def _():
        ys_ref[...] = yp_ref[...]


def _sample_then_tiles_specs(tile, width, sample_rows=None):
    prompt = pl.BlockSpec((tile, width), lambda i: (jnp.maximum(i - 1, 0), 0))
    sample = pl.BlockSpec((sample_rows or tile, width), lambda i: (0, 0))
    return prompt, sample


def _ffn(xp, xs, g, w1, w2, layer):
    tile = xs.shape[0]
    n_tiles = xp.shape[0] // tile
    xp_spec, xs_spec = _sample_then_tiles_specs(tile, D_MODEL)
    layer_spec = lambda shape: pl.BlockSpec((None,) + shape[1:], lambda i: (layer, 0, 0),
                                            pipeline_mode=pl.Buffered(1))
    return pl.pallas_call(
        _ffn_kernel,
        out_shape=[jax.ShapeDtypeStruct(xp.shape, F32), jax.ShapeDtypeStruct(xs.shape, F32)],
        grid=(n_tiles + 1,),
        in_specs=[xp_spec, xs_spec, layer_spec(g.shape), layer_spec(w1.shape), layer_spec(w2.shape)],
        out_specs=list(_sample_then_tiles_specs(tile, D_MODEL)),
        compiler_params=pltpu.CompilerParams(dimension_semantics=("arbitrary",), vmem_limit_bytes=VMEM_LIMIT),
        name="channel_mlp",
    )(xp, xs, g, w1, w2)


def _gmlp_kernel(seg_s, xp_ref, xs_ref, g_ref, win_ref, vg_ref, wsp_ref, wss_ref, biasp_ref, biass_ref, wout_ref,
                 yp_ref, ys_ref, v_ref, v_scr):
    i = pl.program_id(0)
    is_s = i == 0
    tile = xp_ref.shape[0]
    sub = xs_ref.shape[0]
    n_sub = tile // sub
    xs_in, proj = [], []
    for u in range(n_sub):
        rows = slice(u * sub, (u + 1) * sub)
        x = jnp.where(is_s, xs_ref[...], xp_ref[rows, :]) if u == 0 else xp_ref[rows, :]
        xn = _rms_rows(x, g_ref[...]).astype(BF16)
        xs_in.append(x)
        proj.append((jnp.dot(xn, win_ref[:, D_C:2 * D_C].astype(BF16), preferred_element_type=F32),
                     jnp.dot(xn, win_ref[:, 0:D_C].astype(BF16), preferred_element_type=F32)))
    r = _row_iota((CHUNK_MLP, CHUNK_MLP))
    col = _lane_iota((CHUNK_MLP, CHUNK_MLP))
    keep_p = col <= r
    keep_s = jnp.logical_and((r // seg_s) == (col // seg_s), keep_p)
    ws = [jnp.where(is_s, jnp.where(keep_s, wss_ref[gi], 0.0), jnp.where(keep_p, wsp_ref[gi], 0.0)).astype(BF16)
          for gi in range(C_GROUPS)]
    step = MXU_WIDTH
    per_step = step // LANES
    for u in range(n_sub):
        rows = slice(u * sub, (u + 1) * sub)
        zv = _gelu(proj[u][0])
        zc = zv - jnp.mean(zv, axis=-1, keepdims=True)
        v = zc * lax.rsqrt(jnp.mean(zc * zc, axis=-1, keepdims=True) + EPS) * vg_ref[...]
        if u == 0:
            v_scr[...] = v
        vb = v.astype(BF16)
        sv = []
        for gi in range(C_GROUPS):
            lanes = slice(gi * LANES, (gi + 1) * LANES)
            bias = jnp.where(is_s, biass_ref[:, lanes], biasp_ref[:, lanes])
            blocks = [jnp.dot(ws[gi], vb[ch * CHUNK_MLP:(ch + 1) * CHUNK_MLP, lanes],
                              preferred_element_type=F32) + bias for ch in range(sub // CHUNK_MLP)]
            sv.append(jnp.concatenate(blocks, axis=0))
        acc = xs_in[u]
        for c in range(D_C // step):
            cols = slice(c * step, (c + 1) * step)
            gate = (_gelu(proj[u][1][:, cols]) *
                    jnp.concatenate(sv[c * per_step:(c + 1) * per_step], axis=1)).astype(BF16)
            acc = acc + jnp.dot(gate, wout_ref[cols, :].astype(BF16), preferred_element_type=F32)
        yp_ref[rows, :] = acc

    @pl.when(is_s)
    def _():
        ys_ref[...] = yp_ref[0:sub, :]
        v_ref[...] = v_scr[...]


def _gmlp(xp, xs, seg_s, g, win, vg, ws_p, ws_s, bias_p, bias_s, wout):
    rows_s = xs.shape[0]
    tile = GMLP_TILE
    n_tiles = xp.shape[0] // tile
    xp_spec, xs_spec = _sample_then_tiles_specs(tile, D_MODEL, rows_s)
    consts = (g, win, vg, ws_p, ws_s, bias_p, bias_s, wout)
    return pl.pallas_call(
        functools.partial(_gmlp_kernel, seg_s),
        out_shape=[jax.ShapeDtypeStruct(xp.shape, F32), jax.ShapeDtypeStruct(xs.shape, F32),
                   jax.ShapeDtypeStruct((rows_s, D_C), F32)],
        grid=(n_tiles + 1,),
        in_specs=[xp_spec, xs_spec] + [_const_spec(a.shape) for a in consts],
        out_specs=list(_sample_then_tiles_specs(tile, D_MODEL, rows_s)) +
                  [pl.BlockSpec((rows_s, D_C), lambda i: (0, 0))],
        scratch_shapes=[pltpu.VMEM((rows_s, D_C), F32)],
        compiler_params=pltpu.CompilerParams(dimension_semantics=("arbitrary",), vmem_limit_bytes=VMEM_LIMIT),
        name="gmlp_mixer",
    )(xp, xs, *consts)


def _rope_inv():
    half = HEAD_DIM // 2
    return ROPE_THETA ** (-jnp.arange(half, dtype=F32) / half)


def _rope_rows():
    inv = _rope_inv()
    sgn = jnp.ones((HEAD_DIM // 2,), F32)
    return jnp.concatenate([inv] * 4)[None, :], jnp.concatenate([-sgn, sgn, -sgn, sgn])[None, :]


def _rope_tables(pos):
    inv_row, sgn_row = _rope_rows()
    ang = pos.astype(F32)[:, None] * inv_row
    return jnp.cos(ang), jnp.sin(ang) * sgn_row


def _swap_head_halves(a):
    half = HEAD_DIM // 2
    return a.reshape(a.shape[:-1] + (-1, 2, half))[..., ::-1, :].reshape(a.shape)


def _gain_rows(g):
    return jnp.tile(jnp.concatenate([g, _swap_head_halves(g)], axis=0), (1, LANES // HEAD_DIM))


def _gate_weights(wa, wx):
    nb = LRU_BLOCKS // 2
    eye = jnp.eye(nb, dtype=bool)

    def half_diag(w, hh):
        blocks = w[hh * nb:(hh + 1) * nb]
        return jnp.where(eye[:, None, :, None], blocks[:, :, None, :], 0.0).reshape(nb * LRU_BW, nb * LRU_BW)
    return jnp.stack([jnp.concatenate([half_diag(wa, hh), half_diag(wx, hh)], axis=1)
                      for hh in range(2)]).astype(BF16)


def kernel(x_prompt, x_sample, cache_swa_k, cache_swa_v, state_lru_h, state_lru_conv, e_norm_g, e_w_in, e_q_norm_g, e_k_norm_g, e_sinks, e_conv_w, e_conv_b, e_gate_a_w, e_gate_a_b, e_gate_x_w, e_gate_x_b, e_lru_lambda, e_w_out, o_norm_g, o_w_in, o_v_norm_g, o_spatial_w, o_spatial_b, o_w_out, ffn_norm_g, ffn_w1, ffn_w2):
    bsz, s_len, _ = x_prompt.shape
    nseq, t_len, _ = x_sample.shape
    past_len = PAST_LEN
    row = lambda a: a.reshape(1, -1)

    e = 0
    p = {
        'g': row(e_norm_g[e]),
        'win': jnp.concatenate([e_w_in[e], _swap_head_halves(e_w_in[e][:, :Q_W + KV_W])], axis=1).astype(BF16),
        'qg': _gain_rows(e_q_norm_g[e:e + 1]),
        'kg': _gain_rows(e_k_norm_g[e:e + 1]),
        'sink': jnp.repeat(e_sinks[e:e + 1], HEAD_DIM, axis=1).reshape(N_Q_HEADS // 2, LANES),
        'convw': e_conv_w[e],
        'convb': row(e_conv_b[e]),
        'wg': _gate_weights(e_gate_a_w[e], e_gate_x_w[e]),
        'ba': row(e_gate_a_b[e]),
        'bx': row(e_gate_x_b[e]),
        'lam': row(e_lru_lambda[e]),
        'wout': e_w_out[e].astype(BF16),
    }
    yp, nk_p, nv_p, nh_p, nc_p = _even_prompt(x_prompt, p)

    cos_s, sin_s = _rope_tables(past_len + jnp.arange(t_len))
    cos_s = jnp.tile(cos_s, (nseq, 1))
    sin_s = jnp.tile(sin_s, (nseq, 1))
    kc = cache_swa_k[e].reshape(nseq, SWA_ROWS, KV_W)
    vc = cache_swa_v[e].reshape(nseq, SWA_ROWS, KV_W)
    h0_rows = jnp.repeat(state_lru_h[e], t_len, axis=0)
    xprev = jnp.pad(state_lru_conv[e], ((0, 0), (0, t_len - (CONV_W - 1)), (0, 0))).reshape(nseq * t_len, D_LRU)
    ys, nk_s, nv_s, h_s, xr_s = _even_sample(x_sample.reshape(nseq * t_len, D_MODEL), cos_s, sin_s, kc, vc,
                                             h0_rows, xprev, p)
    nh_s = h_s.reshape(nseq, t_len, D_LRU)[:, -1]
    nc_s = xr_s.reshape(nseq, t_len, D_LRU)[:, t_len - (CONV_W - 1):]

    ffn_g = ffn_norm_g.reshape(-1, 1, D_MODEL)
    yp, ys = _ffn(yp.reshape(bsz * s_len, D_MODEL), ys, ffn_g, ffn_w1, ffn_w2, 0)

    o = 0
    bias_p = jnp.repeat(o_spatial_b[o].T, LANES, axis=1)
    reps = CHUNK_MLP // t_len
    ws_s = jnp.tile(o_spatial_w[o][:, :t_len, :t_len], (1, reps, reps))
    bias_s = jnp.tile(jnp.repeat(o_spatial_b[o][:, :t_len].T, LANES, axis=1), (reps, 1))
    yp, ys, gv_s = _gmlp(yp, ys, t_len, row(o_norm_g[o]), o_w_in[o], row(o_v_norm_g[o]),
                         o_spatial_w[o], ws_s, bias_p, bias_s, o_w_out[o])

    yp, ys = _ffn(yp, ys, ffn_g, ffn_w1, ffn_w2, 1)

    kv_shape = (1, -1, SWA_ROWS, N_KV_HEADS, HEAD_DIM)
    return (yp.reshape(bsz, s_len, D_MODEL), ys.reshape(nseq, t_len, D_MODEL),
            nk_p.reshape(kv_shape), nv_p.reshape(kv_shape), nh_p.reshape(1, bsz, D_LRU), nc_p[None],
            nk_s.reshape(kv_shape), nv_s.reshape(kv_shape), nh_s[None], nc_s[None],
            gv_s.reshape(1, nseq, t_len, D_C))
```

```python
import functools
import math

import jax
import jax.numpy as jnp
from jax import lax
from jax.experimental import pallas as pl
from jax.experimental.pallas import tpu as pltpu

F32 = jnp.float32
BF16 = jnp.bfloat16

D_MODEL = 1024
CHUNK = 64
HEAD_DIM = 64
N_Q_HEADS = 8
N_KV_HEADS = 2
SWA_ROWS = 128
PAST_LEN = 4096
ROPE_THETA = 10000.0
NEG = -1e30
D_LRU = 512
LRU_BLOCKS = 8
LRU_BW = D_LRU // LRU_BLOCKS
CONV_W = 4
LRU_C = 8.0
Q_W = N_Q_HEADS * HEAD_DIM
KV_W = N_KV_HEADS * HEAD_DIM
CHUNK_MLP = 128
D_C = D_MODEL
C_GROUPS = 8
D_FF = 4 * D_MODEL
EPS = 1e-6

LANES = 128
SUBLANES = 8
MXU_WIDTH = 256
KEY_BLOCK = 2 * CHUNK
BAND_BLOCKS = 3
VMEM_LIMIT = 56 * 1024 * 1024

PROMPT_TILE = 1024
PROMPT_SUBTILE = 256
GMLP_TILE = 1024
FFN_CHUNK = 1024
SAMPLE_GROUP = 8


def _gelu(x):
    k1 = -2.0 * math.sqrt(2.0 / math.pi) * math.log2(math.e)
    return x / (1.0 + jnp.exp2(x * (k1 + (k1 * 0.044715) * (x * x))))


def _rms_rows(x, g):
    return x * lax.rsqrt(jnp.mean(x * x, axis=-1, keepdims=True) + EPS) * g


def _lane_iota(shape):
    return lax.broadcasted_iota(jnp.int32, shape, 1)


def _row_iota(shape):
    return lax.broadcasted_iota(jnp.int32, shape, 0)


def _head_ones():
    return ((_row_iota((LANES, LANES)) // HEAD_DIM) == (_lane_iota((LANES, LANES)) // HEAD_DIM)).astype(BF16)


def _norm_rope(x, xp, g_cos, g_sin, ones_bd):
    ss = jnp.dot((x * x).astype(BF16), ones_bd, preferred_element_type=F32)
    return (x * g_cos + xp * g_sin) * lax.rsqrt(ss + HEAD_DIM * EPS)


def _diag_blocks(t):
    lo = _lane_iota(t.shape) < HEAD_DIM
    swapped = pltpu.roll(t, HEAD_DIM, 1)
    return [(jnp.where(lo, t, 0.0), jnp.where(lo, 0.0, swapped)),
            (jnp.where(lo, swapped, 0.0), jnp.where(lo, 0.0, t))]


def _ones_diag(rows):
    r = _row_iota((rows, LANES))
    return (((r % KEY_BLOCK) // CHUNK) == (_lane_iota((rows, LANES)) // HEAD_DIM)).astype(BF16)


def _attend_scores(qs, kb, col_mask):
    s = lax.dot_general(qs, kb, (((1,), (1,)), ((), ())), preferred_element_type=F32)
    return s if col_mask is None else jnp.where(col_mask, s, NEG)


def _attend_weights(s, sink_rows):
    mx = jnp.maximum(jnp.maximum(s[:, :LANES], s[:, LANES:2 * LANES]), s[:, 2 * LANES:])
    lo = _lane_iota(mx.shape) < HEAD_DIM
    m0 = jnp.max(jnp.where(lo, mx, -jnp.inf), axis=1, keepdims=True)
    m1 = jnp.max(jnp.where(lo, -jnp.inf, mx), axis=1, keepdims=True)
    m = jnp.maximum(jnp.where(lo, m0, m1), sink_rows)
    e = jnp.exp(s - jnp.concatenate([m, m, m], axis=1))
    return e.astype(BF16), m


def _attend_output(e, m, vb, sink_rows):
    od = jnp.dot(e, vb, preferred_element_type=F32)
    den = od[:, LANES:] + jnp.exp(sink_rows - m)
    return od[:, :LANES] / den


def _lru_scan(a, b, seg, tmod):
    d = 1
    while d < seg:
        ok = tmod >= d
        b = jnp.where(ok, a * pltpu.roll(b, d, 0), 0.0) + b
        if 2 * d < seg:
            a = jnp.where(ok, a * pltpu.roll(a, d, 0), a)
        d *= 2
    return b


def _lru_scan_tile(a, b, h_prev):
    rows = a.shape[0]
    groups = rows // SUBLANES
    a3 = a.reshape(groups, SUBLANES, LANES)
    b3 = b.reshape(groups, SUBLANES, LANES)
    sub = lax.broadcasted_iota(jnp.int32, a3.shape, 1)
    d = 1
    while d < SUBLANES:
        ok = sub >= d
        b3 = jnp.where(ok, a3 * pltpu.roll(b3, d, 1), 0.0) + b3
        a3 = jnp.where(ok, a3 * pltpu.roll(a3, d, 1), a3)
        d *= 2
    carry = jnp.broadcast_to(h_prev, (SUBLANES, LANES))
    out = []
    for g in range(groups):
        hg = a3[g] * carry + b3[g]
        out.append(hg)
        carry = jnp.broadcast_to(hg[SUBLANES - 1:SUBLANES, :], (SUBLANES, LANES))
    return jnp.concatenate(out, axis=0)


def _lru_branch(xs, gr, h_in, seg, tmod, convw_ref, convb_ref, wg_ref, ba_ref, bx_ref, lam_ref):
    xc, gates = _lru_gates(xs, convw_ref, convb_ref, wg_ref)
    parts = [_lru_piece(k, xc, gates, gr, h_in, seg, tmod, ba_ref, bx_ref, lam_ref) for k in range(D_LRU // LANES)]
    return jnp.concatenate([p[0] for p in parts], axis=1), jnp.concatenate([p[1] for p in parts], axis=1)


def _lru_gates(xs, convw_ref, convb_ref, wg_ref):
    xc = xs[0] * convw_ref[0:1, :] + convb_ref[...]
    for i in range(1, CONV_W):
        xc = xc + xs[i] * convw_ref[i:i + 1, :]
    xcb = xc.astype(BF16)
    half = D_LRU // 2
    return xc, [jnp.dot(xcb[:, hh * half:(hh + 1) * half], wg_ref[hh], preferred_element_type=F32)
                for hh in range(2)]


def _lru_piece(k, xc, gates, gr, h_in, seg, tmod, ba_ref, bx_ref, lam_ref):
    half = D_LRU // 2
    hh, tt = divmod(k, half // LANES)
    sl = slice(k * LANES, (k + 1) * LANES)
    g = gates[hh]
    r = jax.nn.sigmoid(g[:, tt * LANES:(tt + 1) * LANES] + ba_ref[:, sl])
    gi = jax.nn.sigmoid(g[:, half + tt * LANES:half + (tt + 1) * LANES] + bx_ref[:, sl])
    log_a = r * (-LRU_C * jax.nn.softplus(-lam_ref[:, sl]))
    a = jnp.exp(log_a)
    b = jnp.sqrt(-jnp.tanh(log_a) * (a * a + 1.0)) * (gi * xc[:, sl])
    if seg is None:
        h = _lru_scan_tile(a, b, h_in[:, sl])
    else:
        h = _lru_scan(a, b + jnp.where(tmod == 0, a * h_in[:, sl], 0.0), seg, tmod)
    return h * _gelu(gr[:, sl]), h


def _even_prompt_kernel(x_ref, cos_ref, sin_ref, inv_ref, sgn_ref, g_ref, win_ref, qg_ref, kg_ref, sink_ref, convw_ref, convb_ref,
                        wg_ref, ba_ref, bx_ref, lam_ref, wout_ref,
                        y_ref, nk_ref, nv_ref, nh_ref, nc_ref,
                        kb_scr, vb_scr, xr_scr, h_scr, mix_scr):
    t = pl.program_id(1)
    nt = pl.num_programs(1)
    tile = x_ref.shape[1]
    n_chunks = tile // CHUNK
    hist = 2 * KEY_BLOCK
    slot = t % 2

    @pl.when(t == 0)
    def _():
        ones = _ones_diag(vb_scr.shape[2])
        for j in range(N_KV_HEADS):
            kb_scr[0, j, 0:hist, :] = jnp.zeros((hist, LANES), BF16)
            vb_scr[0, j, 0:hist, 0:LANES] = jnp.zeros((hist, LANES), BF16)
            for s in range(2):
                vb_scr[s, j, :, LANES:2 * LANES] = ones
        xr_scr[0:SUBLANES, :] = jnp.zeros((SUBLANES, D_LRU), F32)
        h_scr[...] = jnp.zeros(h_scr.shape, F32)

    sub = PROMPT_SUBTILE
    n_sub = tile // sub
    sub_chunks = sub // CHUNK
    o3 = Q_W + 2 * KV_W
    xs_in, proj = [], []
    for u in range(n_sub):
        x = x_ref[0, u * sub:(u + 1) * sub, :]
        xn = _rms_rows(x, g_ref[...]).astype(BF16)
        xs_in.append(x)
        o5 = o3 + 2 * D_LRU
        proj.append((jnp.dot(xn, win_ref[:, 0:Q_W], preferred_element_type=F32),
                     jnp.dot(xn, win_ref[:, Q_W:o3], preferred_element_type=F32),
                     jnp.dot(xn, win_ref[:, o3:o3 + D_LRU], preferred_element_type=F32),
                     jnp.dot(xn, win_ref[:, o3 + D_LRU:o5], preferred_element_type=F32),
                     jnp.dot(xn, win_ref[:, o5:o5 + Q_W], preferred_element_type=F32),
                     jnp.dot(xn, win_ref[:, o5 + Q_W:o5 + Q_W + KV_W], preferred_element_type=F32)))
    for u in range(n_sub):
        xr_scr[SUBLANES + u * sub:SUBLANES + (u + 1) * sub, :] = proj[u][2]

    base = (t * tile).astype(F32) * inv_ref[...]
    cos_b = jnp.cos(base)
    sin_b = jnp.sin(base) * sgn_ref[...]

    band = BAND_BLOCKS * KEY_BLOCK
    col = _lane_iota((2 * CHUNK, band))
    sink_rows = [jnp.concatenate(
        [jnp.broadcast_to(sink_ref[2 * j:2 * j + 1, :], (CHUNK, LANES)),
         jnp.broadcast_to(sink_ref[2 * j + 1:2 * j + 2, :], (CHUNK, LANES))], axis=0) for j in range(N_KV_HEADS)]
    ones_bd = _head_ones()
    norm_c = math.sqrt(HEAD_DIM)
    q_all = []
    for u in range(n_sub):
        hq, hkv, xr, gr, hqp, hkp = proj[u]
        cos_l = cos_ref[u * sub:(u + 1) * sub, :]
        sin_l = sin_ref[u * sub:(u + 1) * sub, :]
        cos_t = cos_b * cos_l - sin_b * sin_l
        sin_t = sin_b * cos_l + cos_b * sin_l
        q_cos = cos_t * (qg_ref[0:1, :] * (norm_c * HEAD_DIM ** -0.5))
        q_sin = sin_t * (qg_ref[1:2, :] * (norm_c * HEAD_DIM ** -0.5))
        q_tiles = [_norm_rope(hq[:, n * LANES:(n + 1) * LANES], hqp[:, n * LANES:(n + 1) * LANES],
                              q_cos, q_sin, ones_bd).astype(BF16) for n in range(Q_W // LANES)]
        kr = _norm_rope(hkv[:, 0:KV_W], hkp, cos_t * (kg_ref[0:1, :] * norm_c), sin_t * (kg_ref[1:2, :] * norm_c),
                        ones_bd)
        vv = hkv[:, KV_W:2 * KV_W]
        k_blocks, v_blocks = _diag_blocks(kr), _diag_blocks(vv)
        for j in range(N_KV_HEADS):
            (ktop, kbot), (vtop, vbot) = k_blocks[j], v_blocks[j]
            for i in range(sub_chunks):
                r0 = hist + (u * sub_chunks + i) * KEY_BLOCK
                rows = slice(i * CHUNK, (i + 1) * CHUNK)
                kb_scr[slot, j, r0:r0 + CHUNK, :] = ktop[rows].astype(BF16)
                kb_scr[slot, j, r0 + CHUNK:r0 + KEY_BLOCK, :] = kbot[rows].astype(BF16)
                vb_scr[slot, j, r0:r0 + CHUNK, 0:LANES] = vtop[rows].astype(BF16)
                vb_scr[slot, j, r0 + CHUNK:r0 + KEY_BLOCK, 0:LANES] = vbot[rows].astype(BF16)
        q_all.append(q_tiles)

    pairs = [(i, j) for i in range(sub_chunks) for j in range(N_KV_HEADS)]
    scores_all, gates_all = [], []
    for u in range(n_sub):
        q_tiles = q_all[u]
        scores = []
        for i, j in pairs:
            rows = slice(i * CHUNK, (i + 1) * CHUNK)
            ci = u * sub_chunks + i
            col_mask = None
            if ci < BAND_BLOCKS - 1:
                col_mask = col >= (1 - jnp.minimum(t, 1)) * ((BAND_BLOCKS - 1 - ci) * KEY_BLOCK)
            qs = jnp.concatenate([q_tiles[2 * j][rows], q_tiles[2 * j + 1][rows]], axis=0)
            scores.append(_attend_scores(qs, kb_scr[slot, j, ci * KEY_BLOCK:ci * KEY_BLOCK + band, :], col_mask))
        scores_all.append(scores)
        first = SUBLANES + u * sub
        taps = [xr_scr[first - (CONV_W - 1 - i):first - (CONV_W - 1 - i) + sub, :] for i in range(CONV_W - 1)]
        gates_all.append(_lru_gates(taps + [proj[u][2]], convw_ref, convb_ref, wg_ref))

    h_carry = h_scr[0:1, :]
    for u in range(n_sub):
        xr, gr = proj[u][2], proj[u][3]
        scores = scores_all[u]
        xc, gates = gates_all[u]
        n_pieces = D_LRU // LANES
        per_piece = len(pairs) // n_pieces
        weights, lru = [], []
        for k in range(n_pieces):
            weights += [_attend_weights(scores[p], sink_rows[pairs[p][1]])
                        for p in range(k * per_piece, (k + 1) * per_piece)]
            lru.append(_lru_piece(k, xc, gates, gr, h_carry, None, None, ba_ref, bx_ref, lam_ref))
        for (e, m), (i, j) in zip(weights, pairs):
            ci = u * sub_chunks + i
            rows = slice(u * sub + i * CHUNK, u * sub + (i + 1) * CHUNK)
            o = _attend_output(e, m, vb_scr[slot, j, ci * KEY_BLOCK:ci * KEY_BLOCK + band, :], sink_rows[j])
            c0 = 2 * j * LANES
            mix_scr[rows, c0:c0 + LANES] = o[0:CHUNK].astype(BF16)
            mix_scr[rows, c0 + LANES:c0 + 2 * LANES] = o[CHUNK:2 * CHUNK].astype(BF16)

        h_carry = jnp.concatenate([h[sub - 1:sub, :] for _, h in lru], axis=1)
        rows = slice(u * sub, (u + 1) * sub)
        for k, (rec, _) in enumerate(lru):
            mix_scr[rows, Q_W + k * LANES:Q_W + (k + 1) * LANES] = rec.astype(BF16)

        y_att = jnp.dot(mix_scr[rows, 0:Q_W], wout_ref[0:Q_W, :], preferred_element_type=F32)
        y_rec = jnp.dot(mix_scr[rows, Q_W:Q_W + D_LRU], wout_ref[Q_W:Q_W + D_LRU, :], preferred_element_type=F32)
        y_ref[0, rows, :] = xs_in[u] + y_att + y_rec

    last = n_chunks * KEY_BLOCK
    for j in range(N_KV_HEADS):
        kb_scr[1 - slot, j, 0:hist, :] = kb_scr[slot, j, last:last + hist, :]
        vb_scr[1 - slot, j, 0:hist, 0:LANES] = vb_scr[slot, j, last:last + hist, 0:LANES]
    h_scr[0:1, :] = h_carry
    xr_scr[0:SUBLANES, :] = xr[sub - SUBLANES:sub, :]

    @pl.when(t == nt - 1)
    def _():
        nk_ref[0] = kr[sub - SWA_ROWS:sub, :]
        nv_ref[0] = vv[sub - SWA_ROWS:sub, :]
        nh_ref[0] = h_carry
        nc_ref[0] = xr[sub - (CONV_W - 1):sub, :]


def _const_spec(shape):
    zeros = (0,) * len(shape)
    return pl.BlockSpec(shape, lambda *_: zeros, pipeline_mode=pl.Buffered(1))


def _even_prompt(x, p):
    bsz, s_len, _ = x.shape
    tile = PROMPT_TILE
    cos_l, sin_l = _rope_tables(jnp.arange(tile))
    inv_row, sgn_row = _rope_rows()
    nt = s_len // tile
    key_rows = (tile // CHUNK + 2) * KEY_BLOCK
    consts = [p['g'], p['win'], p['qg'], p['kg'], p['sink'], p['convw'], p['convb'], p['wg'], p['ba'], p['bx'],
              p['lam'], p['wout']]
    out_shape = [
        jax.ShapeDtypeStruct((bsz, s_len, D_MODEL), F32),
        jax.ShapeDtypeStruct((bsz, SWA_ROWS, KV_W), F32),
        jax.ShapeDtypeStruct((bsz, SWA_ROWS, KV_W), F32),
        jax.ShapeDtypeStruct((bsz, 1, D_LRU), F32),
        jax.ShapeDtypeStruct((bsz, CONV_W - 1, D_LRU), F32),
    ]
    return pl.pallas_call(
        _even_prompt_kernel,
        out_shape=out_shape,
        grid=(bsz, nt),
        in_specs=[pl.BlockSpec((1, tile, D_MODEL), lambda b, t: (b, t, 0)),
                  _const_spec(cos_l.shape), _const_spec(sin_l.shape), _const_spec(inv_row.shape),
                  _const_spec(sgn_row.shape)] + [_const_spec(c.shape) for c in consts],
        out_specs=[pl.BlockSpec((1, tile, D_MODEL), lambda b, t: (b, t, 0)),
                   pl.BlockSpec((1, SWA_ROWS, KV_W), lambda b, t: (b, 0, 0)),
                   pl.BlockSpec((1, SWA_ROWS, KV_W), lambda b, t: (b, 0, 0)),
                   pl.BlockSpec((1, 1, D_LRU), lambda b, t: (b, 0, 0)),
                   pl.BlockSpec((1, CONV_W - 1, D_LRU), lambda b, t: (b, 0, 0))],
        scratch_shapes=[pltpu.VMEM((2, N_KV_HEADS, key_rows, LANES), BF16),
                        pltpu.VMEM((2, N_KV_HEADS, key_rows, 2 * LANES), BF16),
                        pltpu.VMEM((tile + SUBLANES, D_LRU), F32),
                        pltpu.VMEM((SUBLANES, D_LRU), F32),
                        pltpu.VMEM((tile, Q_W + D_LRU), BF16)],
        compiler_params=pltpu.CompilerParams(dimension_semantics=("arbitrary", "arbitrary"),
                                             vmem_limit_bytes=VMEM_LIMIT),
        name="even_mixer_prompt",
    )(x, cos_l, sin_l, inv_row, sgn_row, *consts)


def _even_sample_kernel(x_ref, cos_ref, sin_ref, kc_ref, vc_ref, h0_ref, xprev_ref,
                        g_ref, win_ref, qg_ref, kg_ref, sink_ref, convw_ref, convb_ref,
                        wg_ref, ba_ref, bx_ref, lam_ref, wout_ref,
                        y_ref, nk_ref, nv_ref, h_ref, xr_ref,
                        q_scr, k_scr, v_scr, mix_scr):
    rows_all = x_ref.shape[0]
    nseq = kc_ref.shape[0]
    t_len = rows_all // nseq
    x = x_ref[...]
    xn = _rms_rows(x, g_ref[...]).astype(BF16)
    cos_t = cos_ref[...]
    sin_t = sin_ref[...]

    o3 = Q_W + 2 * KV_W
    o5 = o3 + 2 * D_LRU
    hq = jnp.dot(xn, win_ref[:, 0:Q_W], preferred_element_type=F32)
    hkv = jnp.dot(xn, win_ref[:, Q_W:o3], preferred_element_type=F32)
    xr = jnp.dot(xn, win_ref[:, o3:o3 + D_LRU], preferred_element_type=F32)
    gr = jnp.dot(xn, win_ref[:, o3 + D_LRU:o5], preferred_element_type=F32)
    hqp = jnp.dot(xn, win_ref[:, o5:o5 + Q_W], preferred_element_type=F32)
    hkp = jnp.dot(xn, win_ref[:, o5 + Q_W:o5 + Q_W + KV_W], preferred_element_type=F32)
    ones_bd = _head_ones()
    norm_c = math.sqrt(HEAD_DIM)
    q_cos = cos_t * (qg_ref[0:1, :] * (norm_c * HEAD_DIM ** -0.5))
    q_sin = sin_t * (qg_ref[1:2, :] * (norm_c * HEAD_DIM ** -0.5))
    for n in range(Q_W // LANES):
        lanes = slice(n * LANES, (n + 1) * LANES)
        q_scr[:, lanes] = _norm_rope(hq[:, lanes], hqp[:, lanes], q_cos, q_sin, ones_bd)
    k_scr[...] = _norm_rope(hkv[:, 0:KV_W], hkp, cos_t * (kg_ref[0:1, :] * norm_c),
                            sin_t * (kg_ref[1:2, :] * norm_c), ones_bd)
    v_scr[...] = hkv[:, KV_W:2 * KV_W]

    band = BAND_BLOCKS * KEY_BLOCK
    col = _lane_iota((2 * t_len, band))
    col_mask = jnp.logical_or(col < 2 * KEY_BLOCK, (col % CHUNK) < t_len)
    ones_band = _ones_diag(band)
    pad = jnp.zeros((CHUNK - t_len, LANES), F32)

    sink_rows = [jnp.concatenate(
        [jnp.broadcast_to(sink_ref[2 * j:2 * j + 1, :], (t_len, LANES)),
         jnp.broadcast_to(sink_ref[2 * j + 1:2 * j + 2, :], (t_len, LANES))], axis=0) for j in range(N_KV_HEADS)]

    def group_body(gi, carry):
        work = []
        for s in range(SAMPLE_GROUP):
            b = gi * SAMPLE_GROUP + s
            r0 = pl.multiple_of(b * t_len, t_len)
            kc = kc_ref[b]
            vc = vc_ref[b]
            knew = k_scr[pl.ds(r0, t_len), :]
            vnew = v_scr[pl.ds(r0, t_len), :]
            nk_ref[b, 0:SWA_ROWS - t_len, :] = kc[t_len:SWA_ROWS]
            nk_ref[b, SWA_ROWS - t_len:SWA_ROWS, :] = knew
            nv_ref[b, 0:SWA_ROWS - t_len, :] = vc[t_len:SWA_ROWS]
            nv_ref[b, SWA_ROWS - t_len:SWA_ROWS, :] = vnew
            kfull = jnp.concatenate([kc, knew, pad], axis=0)
            vfull = jnp.concatenate([vc, vnew, pad], axis=0)
            k_blocks, v_blocks = _diag_blocks(kfull), _diag_blocks(vfull)
            for j in range(N_KV_HEADS):
                (ktop, kbot), (vtop, vbot) = k_blocks[j], v_blocks[j]
                kparts, vparts = [], []
                for c in range(BAND_BLOCKS):
                    rs = slice(c * CHUNK, (c + 1) * CHUNK)
                    kparts += [ktop[rs], kbot[rs]]
                    vparts += [vtop[rs], vbot[rs]]
                kb = jnp.concatenate(kparts, axis=0).astype(BF16)
                vb = jnp.concatenate([jnp.concatenate(vparts, axis=0).astype(BF16), ones_band], axis=1)
                c0 = 2 * j * LANES
                qs = jnp.concatenate([q_scr[pl.ds(r0, t_len), c0:c0 + LANES],
                                      q_scr[pl.ds(r0, t_len), c0 + LANES:c0 + 2 * LANES]], axis=0).astype(BF16)
                work.append((r0, j, vb, _attend_scores(qs, kb, col_mask)))
        weights = [_attend_weights(s, sink_rows[j]) for _, j, _, s in work]
        for (r0, j, vb, _), (e, m) in zip(work, weights):
            o = _attend_output(e, m, vb, sink_rows[j])
            c0 = 2 * j * LANES
            mix_scr[pl.ds(r0, t_len), c0:c0 + LANES] = o[0:t_len]
            mix_scr[pl.ds(r0, t_len), c0 + LANES:c0 + 2 * LANES] = o[t_len:2 * t_len]
        return carry

    lax.fori_loop(0, nseq // SAMPLE_GROUP, group_body, 0)

    tmod =_row_iota((rows_all, LANES)) % t_len
    tmod_w = _row_iota((rows_all, D_LRU)) % t_len
    xprev = xprev_ref[...]
    xs = []
    for i in range(CONV_W - 1):
        k = CONV_W - 1 - i
        own = pltpu.roll(xr, k, 0)
        cached = xprev if k == CONV_W - 1 else pltpu.roll(xprev, rows_all - (CONV_W - 1 - k), 0)
        xs.append(jnp.where(tmod_w >= k, own, cached))
    xs.append(xr)
    rec, h = _lru_branch(xs, gr, h0_ref[...], t_len, tmod, convw_ref, convb_ref, wg_ref, ba_ref, bx_ref, lam_ref)
    mix_scr[:, Q_W:Q_W + D_LRU] = rec
    xr_ref[...] = xr
    h_ref[...] = h
    y_ref[...] = x + jnp.dot(mix_scr[...].astype(BF16), wout_ref[...], preferred_element_type=F32)


def _even_sample(x2, cos_t, sin_t, kc, vc, h0_rows, xprev, p):
    rows_all = x2.shape[0]
    nseq = kc.shape[0]
    consts = [p['g'], p['win'], p['qg'], p['kg'], p['sink'], p['convw'], p['convb'], p['wg'], p['ba'], p['bx'],
              p['lam'], p['wout']]
    ins = [x2, cos_t, sin_t, kc, vc, h0_rows, xprev] + consts
    out_shape = [
        jax.ShapeDtypeStruct((rows_all, D_MODEL), F32),
        jax.ShapeDtypeStruct((nseq, SWA_ROWS, KV_W), F32),
        jax.ShapeDtypeStruct((nseq, SWA_ROWS, KV_W), F32),
        jax.ShapeDtypeStruct((rows_all, D_LRU), F32),
        jax.ShapeDtypeStruct((rows_all, D_LRU), F32),
    ]
    return pl.pallas_call(
        _even_sample_kernel,
        out_shape=out_shape,
        grid=(1,),
        in_specs=[_const_spec(a.shape) for a in ins],
        out_specs=[pl.BlockSpec(s.shape, lambda i, n=len(s.shape): (0,) * n) for s in out_shape],
        scratch_shapes=[pltpu.VMEM((rows_all, Q_W), F32),
                        pltpu.VMEM((rows_all, KV_W), F32),
                        pltpu.VMEM((rows_all, KV_W), F32),
                        pltpu.VMEM((rows_all, Q_W + D_LRU), F32)],
        compiler_params=pltpu.CompilerParams(dimension_semantics=("arbitrary",), vmem_limit_bytes=VMEM_LIMIT),
        name="even_mixer_sample",
    )(*ins)


def _ffn_rows(x, g_ref, w1_ref, w2_ref):
    xg = (x * g_ref[...]).astype(BF16)
    r2 = 1.0 / (jnp.mean(x * x, axis=-1, keepdims=True) + EPS)
    out = None
    step = FFN_CHUNK
    for c in range(D_FF // step):
        hcol = jnp.dot(xg, w1_ref[:, c * step:(c + 1) * step].astype(BF16), preferred_element_type=F32)
        hcol = jnp.square(jnp.maximum(hcol, 0.0)).astype(BF16)
        part = jnp.dot(hcol, w2_ref[c * step:(c + 1) * step, :].astype(BF16), preferred_element_type=F32)
        out = part if out is None else out + part
    return x + out * r2


def _ffn_kernel(xp_ref, xs_ref, g_ref, w1_ref, w2_ref, yp_ref, ys_ref):
    i = pl.program_id(0)
    yp_ref[...] = _ffn_rows(jnp.where(i == 0, xs_ref[...], xp_ref[...]), g_ref, w1_ref, w2_ref)

    @pl.when(i == 0)
    def _():
        ys_ref[...] = yp_ref[...]


def _sample_then_tiles_specs(tile, width, sample_rows=None):
    prompt = pl.BlockSpec((tile, width), lambda i: (jnp.maximum(i - 1, 0), 0))
    sample = pl.BlockSpec((sample_rows or tile, width), lambda i: (0, 0))
    return prompt, sample


def _ffn(xp, xs, g, w1, w2, layer):
    tile = xs.shape[0]
    n_tiles = xp.shape[0] // tile
    xp_spec, xs_spec = _sample_then_tiles_specs(tile, D_MODEL)
    layer_spec = lambda shape: pl.BlockSpec((None,) + shape[1:], lambda i: (layer, 0, 0),
                                            pipeline_mode=pl.Buffered(1))
    return pl.pallas_call(
        _ffn_kernel,
        out_shape=[jax.ShapeDtypeStruct(xp.shape, F32), jax.ShapeDtypeStruct(xs.shape, F32)],
        grid=(n_tiles + 1,),
        in_specs=[xp_spec, xs_spec, layer_spec(g.shape), layer_spec(w1.shape), layer_spec(w2.shape)],
        out_specs=list(_sample_then_tiles_specs(tile, D_MODEL)),
        compiler_params=pltpu.CompilerParams(dimension_semantics=("arbitrary",), vmem_limit_bytes=VMEM_LIMIT),
        name="channel_mlp",
    )(xp, xs, g, w1, w2)


def _gmlp_kernel(seg_s, xp_ref, xs_ref, g_ref, win_ref, vg_ref, wsp_ref, wss_ref, biasp_ref, biass_ref, wout_ref,
                 yp_ref, ys_ref, v_ref, v_scr):
    i = pl.program_id(0)
    is_s = i == 0
    tile = xp_ref.shape[0]
    sub = xs_ref.shape[0]
    n_sub = tile // sub
    xs_in, proj = [], []
    for u in range(n_sub):
        rows = slice(u * sub, (u + 1) * sub)
        x = jnp.where(is_s, xs_ref[...], xp_ref[rows, :]) if u == 0 else xp_ref[rows, :]
        xn = _rms_rows(x, g_ref[...]).astype(BF16)
        xs_in.append(x)
        proj.append((jnp.dot(xn, win_ref[:, D_C:2 * D_C].astype(BF16), preferred_element_type=F32),
                     jnp.dot(xn, win_ref[:, 0:D_C].astype(BF16), preferred_element_type=F32)))
    r = _row_iota((CHUNK_MLP, CHUNK_MLP))
    col = _lane_iota((CHUNK_MLP, CHUNK_MLP))
    keep_p = col <= r
    keep_s = jnp.logical_and((r // seg_s) == (col // seg_s), keep_p)
    ws = [jnp.where(is_s, jnp.where(keep_s, wss_ref[gi], 0.0), jnp.where(keep_p, wsp_ref[gi], 0.0)).astype(BF16)
          for gi in range(C_GROUPS)]
    step = MXU_WIDTH
    per_step = step // LANES
    for u in range(n_sub):
        rows = slice(u * sub, (u + 1) * sub)
        zv = _gelu(proj[u][0])
        zc = zv - jnp.mean(zv, axis=-1, keepdims=True)
        v = zc * lax.rsqrt(jnp.mean(zc * zc, axis=-1, keepdims=True) + EPS) * vg_ref[...]
        if u == 0:
            v_scr[...] = v
        vb = v.astype(BF16)
        sv = []
        for gi in range(C_GROUPS):
            lanes = slice(gi * LANES, (gi + 1) * LANES)
            bias = jnp.where(is_s, biass_ref[:, lanes], biasp_ref[:, lanes])
            blocks = [jnp.dot(ws[gi], vb[ch * CHUNK_MLP:(ch + 1) * CHUNK_MLP, lanes],
                              preferred_element_type=F32) + bias for ch in range(sub // CHUNK_MLP)]
            sv.append(jnp.concatenate(blocks, axis=0))
        acc = xs_in[u]
        for c in range(D_C // step):
            cols = slice(c * step, (c + 1) * step)
            gate = (_gelu(proj[u][1][:, cols]) *
                    jnp.concatenate(sv[c * per_step:(c + 1) * per_step], axis=1)).astype(BF16)
            acc = acc + jnp.dot(gate, wout_ref[cols, :].astype(BF16), preferred_element_type=F32)
        yp_ref[rows, :] = acc

    @pl.when(is_s)
    def _():
        ys_ref[...] = yp_ref[0:sub, :]
        v_ref[...] = v_scr[...]


def _gmlp(xp, xs, seg_s, g, win, vg, ws_p, ws_s, bias_p, bias_s, wout):
    rows_s = xs.shape[0]
    tile = GMLP_TILE
    n_tiles = xp.shape[0] // tile
    xp_spec, xs_spec = _sample_then_tiles_specs(tile, D_MODEL, rows_s)
    consts = (g, win, vg, ws_p, ws_s, bias_p, bias_s, wout)
    return pl.pallas_call(
        functools.partial(_gmlp_kernel, seg_s),
        out_shape=[jax.ShapeDtypeStruct(xp.shape, F32), jax.ShapeDtypeStruct(xs.shape, F32),
                   jax.ShapeDtypeStruct((rows_s, D_C), F32)],
        grid=(n_tiles + 1,),
        in_specs=[xp_spec, xs_spec] + [_const_spec(a.shape) for a in consts],
        out_specs=list(_sample_then_tiles_specs(tile, D_MODEL, rows_s)) +
                  [pl.BlockSpec((rows_s, D_C), lambda i: (0, 0))],
        scratch_shapes=[pltpu.VMEM((rows_s, D_C), F32)],
        compiler_params=pltpu.CompilerParams(dimension_semantics=("arbitrary",), vmem_limit_bytes=VMEM_LIMIT),
        name="gmlp_mixer",
    )(xp, xs, *consts)


def _rope_inv():
    half = HEAD_DIM // 2
    return ROPE_THETA ** (-jnp.arange(half, dtype=F32) / half)


def _rope_rows():
    inv = _rope_inv()
    sgn = jnp.ones((HEAD_DIM // 2,), F32)
    return jnp.concatenate([inv] * 4)[None, :], jnp.concatenate([-sgn, sgn, -sgn, sgn])[None, :]


def _rope_tables(pos):
    inv_row, sgn_row = _rope_rows()
    ang = pos.astype(F32)[:, None] * inv_row
    return jnp.cos(ang), jnp.sin(ang) * sgn_row


def _swap_head_halves(a):
    half = HEAD_DIM // 2
    return a.reshape(a.shape[:-1] + (-1, 2, half))[..., ::-1, :].reshape(a.shape)


def _gain_rows(g):
    return jnp.tile(jnp.concatenate([g, _swap_head_halves(g)], axis=0), (1, LANES // HEAD_DIM))


def _gate_weights(wa, wx):
    nb = LRU_BLOCKS // 2
    eye = jnp.eye(nb, dtype=bool)

    def half_diag(w, hh):
        blocks = w[hh * nb:(hh + 1) * nb]
        return jnp.where(eye[:, None, :, None], blocks[:, :, None, :], 0.0).reshape(nb * LRU_BW, nb * LRU_BW)
    return jnp.stack([jnp.concatenate([half_diag(wa, hh), half_diag(wx, hh)], axis=1)
                      for hh in range(2)]).astype(BF16)


def kernel(x_prompt, x_sample, cache_swa_k, cache_swa_v, state_lru_h, state_lru_conv, e_norm_g, e_w_in, e_q_norm_g, e_k_norm_g, e_sinks, e_conv_w, e_conv_b, e_gate_a_w, e_gate_a_b, e_gate_x_w, e_gate_x_b, e_lru_lambda, e_w_out, o_norm_g, o_w_in, o_v_norm_g, o_spatial_w, o_spatial_b, o_w_out, ffn_norm_g, ffn_w1, ffn_w2):
    bsz, s_len, _ = x_prompt.shape
    nseq, t_len, _ = x_sample.shape
    past_len = PAST_LEN
    row = lambda a: a.reshape(1, -1)

    e = 0
    p = {
        'g': row(e_norm_g[e]),
        'win': jnp.concatenate([e_w_in[e], _swap_head_halves(e_w_in[e][:, :Q_W + KV_W])], axis=1).astype(BF16),
        'qg': _gain_rows(e_q_norm_g[e:e + 1]),
        'kg': _gain_rows(e_k_norm_g[e:e + 1]),
        'sink': jnp.repeat(e_sinks[e:e + 1], HEAD_DIM, axis=1).reshape(N_Q_HEADS // 2, LANES),
        'convw': e_conv_w[e],
        'convb': row(e_conv_b[e]),
        'wg': _gate_weights(e_gate_a_w[e], e_gate_x_w[e]),
        'ba': row(e_gate_a_b[e]),
        'bx': row(e_gate_x_b[e]),
        'lam': row(e_lru_lambda[e]),
        'wout': e_w_out[e].astype(BF16),
    }
    yp, nk_p, nv_p, nh_p, nc_p = _even_prompt(x_prompt, p)

    cos_s, sin_s = _rope_tables(past_len + jnp.arange(t_len))
    cos_s = jnp.tile(cos_s, (nseq, 1))
    sin_s = jnp.tile(sin_s, (nseq, 1))
    kc = cache_swa_k[e].reshape(nseq, SWA_ROWS, KV_W)
    vc = cache_swa_v[e].reshape(nseq, SWA_ROWS, KV_W)
    h0_rows = jnp.repeat(state_lru_h[e], t_len, axis=0)
    xprev = jnp.pad(state_lru_conv[e], ((0, 0), (0, t_len - (CONV_W - 1)), (0, 0))).reshape(nseq * t_len, D_LRU)
    ys, nk_s, nv_s, h_s, xr_s = _even_sample(x_sample.reshape(nseq * t_len, D_MODEL), cos_s, sin_s, kc, vc,
                                             h0_rows, xprev, p)
    nh_s = h_s.reshape(nseq, t_len, D_LRU)[:, -1]
    nc_s = xr_s.reshape(nseq, t_len, D_LRU)[:, t_len - (CONV_W - 1):]

    ffn_g = ffn_norm_g.reshape(-1, 1, D_MODEL)
    yp, ys = _ffn(yp.reshape(bsz * s_len, D_MODEL), ys, ffn_g, ffn_w1, ffn_w2, 0)

    o = 0
    bias_p = jnp.repeat(o_spatial_b[o].T, LANES, axis=1)
    reps = CHUNK_MLP // t_len
    ws_s = jnp.tile(o_spatial_w[o][:, :t_len, :t_len], (1, reps, reps))
    bias_s = jnp.tile(jnp.repeat(o_spatial_b[o][:, :t_len].T, LANES, axis=1), (reps, 1))
    yp, ys, gv_s = _gmlp(yp, ys, t_len, row(o_norm_g[o]), o_w_in[o], row(o_v_norm_g[o]),
                         o_spatial_w[o], ws_s, bias_p, bias_s, o_w_out[o])

    yp, ys = _ffn(yp, ys, ffn_g, ffn_w1, ffn_w2, 1)

    kv_shape = (1, -1, SWA_ROWS, N_KV_HEADS, HEAD_DIM)
    return (yp.reshape(bsz, s_len, D_MODEL), ys.reshape(nseq, t_len, D_MODEL),
            nk_p.reshape(kv_shape), nv_p.reshape(kv_shape), nh_p.reshape(1, bsz, D_LRU), nc_p[None],
            nk_s.reshape(kv_shape), nv_s.reshape(kv_shape), nh_s[None], nc_s[None],
            gv_s.reshape(1, nseq, t_len, D_C))
```

```python
import functools
import math

import jax
import jax.numpy as jnp
from jax import lax
from jax.experimental import pallas as pl
from jax.experimental.pallas import tpu as pltpu

F32 = jnp.float32
BF16 = jnp.bfloat16

D_MODEL = 1024
CHUNK = 64
HEAD_DIM = 64
N_Q_HEADS = 8
N_KV_HEADS = 2
SWA_ROWS = 128
PAST_LEN = 4096
ROPE_THETA = 10000.0
NEG = -1e30
D_LRU = 512
LRU_BLOCKS = 8
LRU_BW = D_LRU // LRU_BLOCKS
CONV_W = 4
LRU_C = 8.0
Q_W = N_Q_HEADS * HEAD_DIM
KV_W = N_KV_HEADS * HEAD_DIM
CHUNK_MLP = 128
D_C = D_MODEL
C_GROUPS = 8
D_FF = 4 * D_MODEL
EPS = 1e-6

LANES = 128
SUBLANES = 8
MXU_WIDTH = 256
KEY_BLOCK = 2 * CHUNK
BAND_BLOCKS = 3
VMEM_LIMIT = 56 * 1024 * 1024

PROMPT_TILE = 1024
PROMPT_SUBTILE = 256
GMLP_TILE = 1024
FFN_CHUNK = 1024
SAMPLE_GROUP = 8


def _gelu(x):
    k1 = -2.0 * math.sqrt(2.0 / math.pi) * math.log2(math.e)
    return x / (1.0 + jnp.exp2(x * (k1 + (k1 * 0.044715) * (x * x))))


def _rms_rows(x, g):
    return x * lax.rsqrt(jnp.mean(x * x, axis=-1, keepdims=True) + EPS) * g


def _lane_iota(shape):
    return lax.broadcasted_iota(jnp.int32, shape, 1)


def _row_iota(shape):
    return lax.broadcasted_iota(jnp.int32, shape, 0)


def _head_ones():
    return ((_row_iota((LANES, LANES)) // HEAD_DIM) == (_lane_iota((LANES, LANES)) // HEAD_DIM)).astype(BF16)


def _norm_rope(x, xp, g_cos, g_sin, ones_bd):
    ss = jnp.dot((x * x).astype(BF16), ones_bd, preferred_element_type=F32)
    return (x * g_cos + xp * g_sin) * lax.rsqrt(ss + HEAD_DIM * EPS)


def _diag_blocks(t):
    lo = _lane_iota(t.shape) < HEAD_DIM
    swapped = pltpu.roll(t, HEAD_DIM, 1)
    return [(jnp.where(lo, t, 0.0), jnp.where(lo, 0.0, swapped)),
            (jnp.where(lo, swapped, 0.0), jnp.where(lo, 0.0, t))]


def _ones_diag(rows):
    r = _row_iota((rows, LANES))
    return (((r % KEY_BLOCK) // CHUNK) == (_lane_iota((rows, LANES)) // HEAD_DIM)).astype(BF16)


def _attend_scores(qs, kb, col_mask):
    s = lax.dot_general(qs, kb, (((1,), (1,)), ((), ())), preferred_element_type=F32)
    return s if col_mask is None else jnp.where(col_mask, s, NEG)


def _attend_weights(s, sink_rows):
    mx = jnp.maximum(jnp.maximum(s[:, :LANES], s[:, LANES:2 * LANES]), s[:, 2 * LANES:])
    lo = _lane_iota(mx.shape) < HEAD_DIM
    m0 = jnp.max(jnp.where(lo, mx, -jnp.inf), axis=1, keepdims=True)
    m1 = jnp.max(jnp.where(lo, -jnp.inf, mx), axis=1, keepdims=True)
    m = jnp.maximum(jnp.where(lo, m0, m1), sink_rows)
    e = jnp.exp(s - jnp.concatenate([m, m, m], axis=1))
    return e.astype(BF16), m


def _attend_output(e, m, vb, sink_rows):
    od = jnp.dot(e, vb, preferred_element_type=F32)
    den = od[:, LANES:] + jnp.exp(sink_rows - m)
    return od[:, :LANES] / den


def _lru_scan(a, b, seg, tmod):
    d = 1
    while d < seg:
        ok = tmod >= d
        b = jnp.where(ok, a * pltpu.roll(b, d, 0), 0.0) + b
        if 2 * d < seg:
            a = jnp.where(ok, a * pltpu.roll(a, d, 0), a)
        d *= 2
    return b


def _lru_scan_tile(a, b, h_prev):
    rows = a.shape[0]
    groups = rows // SUBLANES
    a3 = a.reshape(groups, SUBLANES, LANES)
    b3 = b.reshape(groups, SUBLANES, LANES)
    sub = lax.broadcasted_iota(jnp.int32, a3.shape, 1)
    d = 1
    while d < SUBLANES:
        ok = sub >= d
        b3 = jnp.where(ok, a3 * pltpu.roll(b3, d, 1), 0.0) + b3
        a3 = jnp.where(ok, a3 * pltpu.roll(a3, d, 1), a3)
        d *= 2
    carry = jnp.broadcast_to(h_prev, (SUBLANES, LANES))
    out = []
    for g in range(groups):
        hg = a3[g] * carry + b3[g]
        out.append(hg)
        carry = jnp.broadcast_to(hg[SUBLANES - 1:SUBLANES, :], (SUBLANES, LANES))
    return jnp.concatenate(out, axis=0)


def _lru_branch(xs, gr, h_in, seg, tmod, convw_ref, convb_ref, wg_ref, ba_ref, bx_ref, lam_ref):
    xc, gates = _lru_gates(xs, convw_ref, convb_ref, wg_ref)
    parts = [_lru_piece(k, xc, gates, gr, h_in, seg, tmod, ba_ref, bx_ref, lam_ref) for k in range(D_LRU // LANES)]
    return jnp.concatenate([p[0] for p in parts], axis=1), jnp.concatenate([p[1] for p in parts], axis=1)


def _lru_gates(xs, convw_ref, convb_ref, wg_ref):
    xc = xs[0] * convw_ref[0:1, :] + convb_ref[...]
    for i in range(1, CONV_W):
        xc = xc + xs[i] * convw_ref[i:i + 1, :]
    xcb = xc.astype(BF16)
    half = D_LRU // 2
    return xc, [jnp.dot(xcb[:, hh * half:(hh + 1) * half], wg_ref[hh], preferred_element_type=F32)
                for hh in range(2)]


def _lru_piece(k, xc, gates, gr, h_in, seg, tmod, ba_ref, bx_ref, lam_ref):
    half = D_LRU // 2
    hh, tt = divmod(k, half // LANES)
    sl = slice(k * LANES, (k + 1) * LANES)
    g = gates[hh]
    r = jax.nn.sigmoid(g[:, tt * LANES:(tt + 1) * LANES] + ba_ref[:, sl])
    gi = jax.nn.sigmoid(g[:, half + tt * LANES:half + (tt + 1) * LANES] + bx_ref[:, sl])
    log_a = r * (-LRU_C * jax.nn.softplus(-lam_ref[:, sl]))
    a = jnp.exp(log_a)
    b = jnp.sqrt(-jnp.tanh(log_a) * (a * a + 1.0)) * (gi * xc[:, sl])
    if seg is None:
        h = _lru_scan_tile(a, b, h_in[:, sl])
    else:
        h = _lru_scan(a, b + jnp.where(tmod == 0, a * h_in[:, sl], 0.0), seg, tmod)
    return h * _gelu(gr[:, sl]), h


def _even_prompt_kernel(x_ref, cos_ref, sin_ref, inv_ref, sgn_ref, g_ref, win_ref, qg_ref, kg_ref, sink_ref,
                        convw_ref, convb_ref, wg_ref, ba_ref, bx_ref, lam_ref, wout_ref,
                        y_ref, nk_ref, nv_ref, nh_ref, nc_ref,
                        kb_scr, vb_scr, xr_scr, h_scr, mix_scr):
    t = pl.program_id(1)
    nt = pl.num_programs(1)
    tile = x_ref.shape[1]
    n_chunks = tile // CHUNK
    hist = (BAND_BLOCKS - 1) * KEY_BLOCK
    slot = t % 2

    @pl.when(t == 0)
    def _():
        ones = _ones_diag(vb_scr.shape[2])
        for j in range(N_KV_HEADS):
            kb_scr[0, j, 0:hist, :] = jnp.zeros((hist, LANES), BF16)
            vb_scr[0, j, 0:hist, 0:LANES] = jnp.zeros((hist, LANES), BF16)
            for s in range(2):
                vb_scr[s, j, :, LANES:2 * LANES] = ones
        xr_scr[0:SUBLANES, :] = jnp.zeros((SUBLANES, D_LRU), F32)
        h_scr[...] = jnp.zeros(h_scr.shape, F32)

    sub = PROMPT_SUBTILE
    n_sub = tile // sub
    sub_chunks = sub // CHUNK
    o3 = Q_W + 2 * KV_W
    xs_in, proj = [], []
    for u in range(n_sub):
        x = x_ref[0, u * sub:(u + 1) * sub, :]
        xn = _rms_rows(x, g_ref[...]).astype(BF16)
        xs_in.append(x)
        o5 = o3 + 2 * D_LRU
        proj.append((jnp.dot(xn, win_ref[:, 0:Q_W], preferred_element_type=F32),
                     jnp.dot(xn, win_ref[:, Q_W:o3], preferred_element_type=F32),
                     jnp.dot(xn, win_ref[:, o3:o3 + D_LRU], preferred_element_type=F32),
                     jnp.dot(xn, win_ref[:, o3 + D_LRU:o5], preferred_element_type=F32),
                     jnp.dot(xn, win_ref[:, o5:o5 + Q_W], preferred_element_type=F32),
                     jnp.dot(xn, win_ref[:, o5 + Q_W:o5 + Q_W + KV_W], preferred_element_type=F32)))
    for u in range(n_sub):
        xr_scr[SUBLANES + u * sub:SUBLANES + (u + 1) * sub, :] = proj[u][2]

    base = (t * tile).astype(F32) * inv_ref[...]
    cos_b = jnp.cos(base)
    sin_b = jnp.sin(base) * sgn_ref[...]

    band = BAND_BLOCKS * KEY_BLOCK
    col = _lane_iota((2 * CHUNK, band))
    sink_rows = [jnp.concatenate(
        [jnp.broadcast_to(sink_ref[2 * j:2 * j + 1, :], (CHUNK, LANES)),
         jnp.broadcast_to(sink_ref[2 * j + 1:2 * j + 2, :], (CHUNK, LANES))], axis=0) for j in range(N_KV_HEADS)]
    ones_bd = _head_ones()
    norm_c = math.sqrt(HEAD_DIM)
    q_all = []
    for u in range(n_sub):
        hq, hkv, xr, gr, hqp, hkp = proj[u]
        cos_l = cos_ref[u * sub:(u + 1) * sub, :]
        sin_l = sin_ref[u * sub:(u + 1) * sub, :]
        cos_t = cos_b * cos_l - sin_b * sin_l
        sin_t = sin_b * cos_l + cos_b * sin_l
        q_cos = cos_t * (qg_ref[0:1, :] * (norm_c * HEAD_DIM ** -0.5))
        q_sin = sin_t * (qg_ref[1:2, :] * (norm_c * HEAD_DIM ** -0.5))
        q_tiles = [_norm_rope(hq[:, n * LANES:(n + 1) * LANES], hqp[:, n * LANES:(n + 1) * LANES],
                              q_cos, q_sin, ones_bd).astype(BF16) for n in range(Q_W // LANES)]
        kr = _norm_rope(hkv[:, 0:KV_W], hkp, cos_t * (kg_ref[0:1, :] * norm_c), sin_t * (kg_ref[1:2, :] * norm_c),
                        ones_bd)
        vv = hkv[:, KV_W:2 * KV_W]
        k_blocks, v_blocks = _diag_blocks(kr), _diag_blocks(vv)
        for j in range(N_KV_HEADS):
            (ktop, kbot), (vtop, vbot) = k_blocks[j], v_blocks[j]
            for i in range(sub_chunks):
                r0 = hist + (u * sub_chunks + i) * KEY_BLOCK
                rows = slice(i * CHUNK, (i + 1) * CHUNK)
                kb_scr[slot, j, r0:r0 + CHUNK, :] = ktop[rows].astype(BF16)
                kb_scr[slot, j, r0 + CHUNK:r0 + KEY_BLOCK, :] = kbot[rows].astype(BF16)
                vb_scr[slot, j, r0:r0 + CHUNK, 0:LANES] = vtop[rows].astype(BF16)
                vb_scr[slot, j, r0 + CHUNK:r0 + KEY_BLOCK, 0:LANES] = vbot[rows].astype(BF16)
        q_all.append(q_tiles)

    pairs = [(i, j) for i in range(sub_chunks) for j in range(N_KV_HEADS)]
    scores_all, gates_all = [], []
    for u in range(n_sub):
        q_tiles = q_all[u]
        scores = []
        for i, j in pairs:
            rows = slice(i * CHUNK, (i + 1) * CHUNK)
            ci = u * sub_chunks + i
            col_mask = None
            if ci < BAND_BLOCKS - 1:
                col_mask = col >= (1 - jnp.minimum(t, 1)) * ((BAND_BLOCKS - 1 - ci) * KEY_BLOCK)
            qs = jnp.concatenate([q_tiles[2 * j][rows], q_tiles[2 * j + 1][rows]], axis=0)
            scores.append(_attend_scores(qs, kb_scr[slot, j, ci * KEY_BLOCK:ci * KEY_BLOCK + band, :], col_mask))
        scores_all.append(scores)
        first = SUBLANES + u * sub
        taps = [xr_scr[first - (CONV_W - 1 - i):first - (CONV_W - 1 - i) + sub, :] for i in range(CONV_W - 1)]
        gates_all.append(_lru_gates(taps + [proj[u][2]], convw_ref, convb_ref, wg_ref))

    h_carry = h_scr[0:1, :]
    for u in range(n_sub):
        xr, gr = proj[u][2], proj[u][3]
        scores = scores_all[u]
        xc, gates = gates_all[u]
        n_pieces = D_LRU // LANES
        per_piece = len(pairs) // n_pieces
        weights, lru = [], []
        for k in range(n_pieces):
            weights += [_attend_weights(scores[p], sink_rows[pairs[p][1]])
                        for p in range(k * per_piece, (k + 1) * per_piece)]
            lru.append(_lru_piece(k, xc, gates, gr, h_carry, None, None, ba_ref, bx_ref, lam_ref))
        for (e, m), (i, j) in zip(weights, pairs):
            ci = u * sub_chunks + i
            rows = slice(u * sub + i * CHUNK, u * sub + (i + 1) * CHUNK)
            o = _attend_output(e, m, vb_scr[slot, j, ci * KEY_BLOCK:ci * KEY_BLOCK + band, :], sink_rows[j])
            c0 = 2 * j * LANES
            mix_scr[rows, c0:c0 + LANES] = o[0:CHUNK].astype(BF16)
            mix_scr[rows, c0 + LANES:c0 + 2 * LANES] = o[CHUNK:2 * CHUNK].astype(BF16)

        h_carry = jnp.concatenate([h[sub - 1:sub, :] for _, h in lru], axis=1)
        rows = slice(u * sub, (u + 1) * sub)
        for k, (rec, _) in enumerate(lru):
            mix_scr[rows, Q_W + k * LANES:Q_W + (k + 1) * LANES] = rec.astype(BF16)

        y_att = jnp.dot(mix_scr[rows, 0:Q_W], wout_ref[0:Q_W, :], preferred_element_type=F32)
        y_rec = jnp.dot(mix_scr[rows, Q_W:Q_W + D_LRU], wout_ref[Q_W:Q_W + D_LRU, :], preferred_element_type=F32)
        y_ref[0, rows, :] = xs_in[u] + y_att + y_rec

    last = n_chunks * KEY_BLOCK
    for j in range(N_KV_HEADS):
        kb_scr[1 - slot, j, 0:hist, :] = kb_scr[slot, j, last:last + hist, :]
        vb_scr[1 - slot, j, 0:hist, 0:LANES] = vb_scr[slot, j, last:last + hist, 0:LANES]
    h_scr[0:1, :] = h_carry
    xr_scr[0:SUBLANES, :] = xr[sub - SUBLANES:sub, :]

    @pl.when(t == nt - 1)
    def _():
        nk_ref[0] = kr[sub - SWA_ROWS:sub, :]
        nv_ref[0] = vv[sub - SWA_ROWS:sub, :]
        nh_ref[0] = h_carry
        nc_ref[0] = xr[sub - (CONV_W - 1):sub, :]


def _const_spec(shape):
    zeros = (0,) * len(shape)
    return pl.BlockSpec(shape, lambda *_: zeros, pipeline_mode=pl.Buffered(1))


def _even_prompt(x, p):
    bsz, s_len, _ = x.shape
    tile = PROMPT_TILE
    cos_l, sin_l = _rope_tables(jnp.arange(tile))
    inv_row, sgn_row = _rope_rows()
    nt = s_len // tile
    key_rows = (tile // CHUNK + BAND_BLOCKS - 1) * KEY_BLOCK
    consts = [p['g'], p['win'], p['qg'], p['kg'], p['sink'], p['convw'], p['convb'], p['wg'], p['ba'], p['bx'],
              p['lam'], p['wout']]
    out_shape = [
        jax.ShapeDtypeStruct((bsz, s_len, D_MODEL), F32),
        jax.ShapeDtypeStruct((bsz, SWA_ROWS, KV_W), F32),
        jax.ShapeDtypeStruct((bsz, SWA_ROWS, KV_W), F32),
        jax.ShapeDtypeStruct((bsz, 1, D_LRU), F32),
        jax.ShapeDtypeStruct((bsz, CONV_W - 1, D_LRU), F32),
    ]
    return pl.pallas_call(
        _even_prompt_kernel,
        out_shape=out_shape,
        grid=(bsz, nt),
        in_specs=[pl.BlockSpec((1, tile, D_MODEL), lambda b, t: (b, t, 0)),
                  _const_spec(cos_l.shape), _const_spec(sin_l.shape), _const_spec(inv_row.shape),
                  _const_spec(sgn_row.shape)] + [_const_spec(c.shape) for c in consts],
        out_specs=[pl.BlockSpec((1, tile, D_MODEL), lambda b, t: (b, t, 0)),
                   pl.BlockSpec((1, SWA_ROWS, KV_W), lambda b, t: (b, 0, 0)),
                   pl.BlockSpec((1, SWA_ROWS, KV_W), lambda b, t: (b, 0, 0)),
                   pl.BlockSpec((1, 1, D_LRU), lambda b, t: (b, 0, 0)),
                   pl.BlockSpec((1, CONV_W - 1, D_LRU), lambda b, t: (b, 0, 0))],
        scratch_shapes=[pltpu.VMEM((2, N_KV_HEADS, key_rows, LANES), BF16),
                        pltpu.VMEM((2, N_KV_HEADS, key_rows, 2 * LANES), BF16),
                        pltpu.VMEM((tile + SUBLANES, D_LRU), F32),
                        pltpu.VMEM((SUBLANES, D_LRU), F32),
                        pltpu.VMEM((tile, Q_W + D_LRU), BF16)],
        compiler_params=pltpu.CompilerParams(dimension_semantics=("arbitrary", "arbitrary"),
                                             vmem_limit_bytes=VMEM_LIMIT),
        name="even_mixer_prompt",
    )(x, cos_l, sin_l, inv_row, sgn_row, *consts)


def _even_sample_kernel(x_ref, cos_ref, sin_ref, kc_ref, vc_ref, h0_ref, xprev_ref,
                        g_ref, win_ref, qg_ref, kg_ref, sink_ref, convw_ref, convb_ref,
                        wg_ref, ba_ref, bx_ref, lam_ref, wout_ref,
                        y_ref, nk_ref, nv_ref, h_ref, xr_ref,
                        q_scr, k_scr, v_scr, mix_scr):
    rows_all = x_ref.shape[0]
    nseq = kc_ref.shape[0]
    t_len = rows_all // nseq
    x = x_ref[...]
    xn = _rms_rows(x, g_ref[...]).astype(BF16)
    cos_t = cos_ref[...]
    sin_t = sin_ref[...]

    o3 = Q_W + 2 * KV_W
    o5 = o3 + 2 * D_LRU
    hq = jnp.dot(xn, win_ref[:, 0:Q_W], preferred_element_type=F32)
    hkv = jnp.dot(xn, win_ref[:, Q_W:o3], preferred_element_type=F32)
    xr = jnp.dot(xn, win_ref[:, o3:o3 + D_LRU], preferred_element_type=F32)
    gr = jnp.dot(xn, win_ref[:, o3 + D_LRU:o5], preferred_element_type=F32)
    hqp = jnp.dot(xn, win_ref[:, o5:o5 + Q_W], preferred_element_type=F32)
    hkp = jnp.dot(xn, win_ref[:, o5 + Q_W:o5 + Q_W + KV_W], preferred_element_type=F32)
    ones_bd = _head_ones()
    norm_c = math.sqrt(HEAD_DIM)
    q_cos = cos_t * (qg_ref[0:1, :] * (norm_c * HEAD_DIM ** -0.5))
    q_sin = sin_t * (qg_ref[1:2, :] * (norm_c * HEAD_DIM ** -0.5))
    for n in range(Q_W // LANES):
        lanes = slice(n * LANES, (n + 1) * LANES)
        q_scr[:, lanes] = _norm_rope(hq[:, lanes], hqp[:, lanes], q_cos, q_sin, ones_bd)
    k_scr[...] = _norm_rope(hkv[:, 0:KV_W], hkp, cos_t * (kg_ref[0:1, :] * norm_c),
                            sin_t * (kg_ref[1:2, :] * norm_c), ones_bd)
    v_scr[...] = hkv[:, KV_W:2 * KV_W]

    band = BAND_BLOCKS * KEY_BLOCK
    col = _lane_iota((2 * t_len, band))
    col_mask = jnp.logical_or(col < 2 * KEY_BLOCK, (col % CHUNK) < t_len)
    ones_band = _ones_diag(band)
    pad = jnp.zeros((CHUNK - t_len, LANES), F32)

    sink_rows = [jnp.concatenate(
        [jnp.broadcast_to(sink_ref[2 * j:2 * j + 1, :], (t_len, LANES)),
         jnp.broadcast_to(sink_ref[2 * j + 1:2 * j + 2, :], (t_len, LANES))], axis=0) for j in range(N_KV_HEADS)]

    def group_body(gi, carry):
        work = []
        for s in range(SAMPLE_GROUP):
            b = gi * SAMPLE_GROUP + s
            r0 = pl.multiple_of(b * t_len, t_len)
            kc = kc_ref[b]
            vc = vc_ref[b]
            knew = k_scr[pl.ds(r0, t_len), :]
            vnew = v_scr[pl.ds(r0, t_len), :]
            nk_ref[b, 0:SWA_ROWS - t_len, :] = kc[t_len:SWA_ROWS]
            nk_ref[b, SWA_ROWS - t_len:SWA_ROWS, :] = knew
            nv_ref[b, 0:SWA_ROWS - t_len, :] = vc[t_len:SWA_ROWS]
            nv_ref[b, SWA_ROWS - t_len:SWA_ROWS, :] = vnew
            kfull = jnp.concatenate([kc, knew, pad], axis=0)
            vfull = jnp.concatenate([vc, vnew, pad], axis=0)
            k_blocks, v_blocks = _diag_blocks(kfull), _diag_blocks(vfull)
            for j in range(N_KV_HEADS):
                (ktop, kbot), (vtop, vbot) = k_blocks[j], v_blocks[j]
                kparts, vparts = [], []
                for c in range(BAND_BLOCKS):
                    rs = slice(c * CHUNK, (c + 1) * CHUNK)
                    kparts += [ktop[rs], kbot[rs]]
                    vparts += [vtop[rs], vbot[rs]]
                kb = jnp.concatenate(kparts, axis=0).astype(BF16)
                vb = jnp.concatenate([jnp.concatenate(vparts, axis=0).astype(BF16), ones_band], axis=1)
                c0 = 2 * j * LANES
                qs = jnp.concatenate([q_scr[pl.ds(r0, t_len), c0:c0 + LANES],
                                      q_scr[pl.ds(r0, t_len), c0 + LANES:c0 + 2 * LANES]], axis=0).astype(BF16)
                work.append((r0, j, vb, _attend_scores(qs, kb, col_mask)))
        weights = [_attend_weights(s, sink_rows[j]) for _, j, _, s in work]
        for (r0, j, vb, _), (e, m) in zip(work, weights):
            o = _attend_output(e, m, vb, sink_rows[j])
            c0 = 2 * j * LANES
            mix_scr[pl.ds(r0, t_len), c0:c0 + LANES] = o[0:t_len]
            mix_scr[pl.ds(r0, t_len), c0 + LANES:c0 + 2 * LANES] = o[t_len:2 * t_len]
        return carry

    lax.fori_loop(0, nseq // SAMPLE_GROUP, group_body, 0)

    tmod =_row_iota((rows_all, LANES)) % t_len
    tmod_w = _row_iota((rows_all, D_LRU)) % t_len
    xprev = xprev_ref[...]
    xs = []
    for i in range(CONV_W - 1):
        k = CONV_W - 1 - i
        own = pltpu.roll(xr, k, 0)
        cached = xprev if k == CONV_W - 1 else pltpu.roll(xprev, rows_all - (CONV_W - 1 - k), 0)
        xs.append(jnp.where(tmod_w >= k, own, cached))
    xs.append(xr)
    rec, h = _lru_branch(xs, gr, h0_ref[...], t_len, tmod, convw_ref, convb_ref, wg_ref, ba_ref, bx_ref, lam_ref)
    mix_scr[:, Q_W:Q_W + D_LRU] = rec
    xr_ref[...] = xr
    h_ref[...] = h
    y_ref[...] = x + jnp.dot(mix_scr[...].astype(BF16), wout_ref[...], preferred_element_type=F32)


def _even_sample(x2, cos_t, sin_t, kc, vc, h0_rows, xprev, p):
    rows_all = x2.shape[0]
    nseq = kc.shape[0]
    consts = [p['g'], p['win'], p['qg'], p['kg'], p['sink'], p['convw'], p['convb'], p['wg'], p['ba'], p['bx'],
              p['lam'], p['wout']]
    ins = [x2, cos_t, sin_t, kc, vc, h0_rows, xprev] + consts
    out_shape = [
        jax.ShapeDtypeStruct((rows_all, D_MODEL), F32),
        jax.ShapeDtypeStruct((nseq, SWA_ROWS, KV_W), F32),
        jax.ShapeDtypeStruct((nseq, SWA_ROWS, KV_W), F32),
        jax.ShapeDtypeStruct((rows_all, D_LRU), F32),
        jax.ShapeDtypeStruct((rows_all, D_LRU), F32),
    ]
    return pl.pallas_call(
        _even_sample_kernel,
        out_shape=out_shape,
        grid=(1,),
        in_specs=[_const_spec(a.shape) for a in ins],
        out_specs=[pl.BlockSpec(s.shape, lambda i, n=len(s.shape): (0,) * n) for s in out_shape],
        scratch_shapes=[pltpu.VMEM((rows_all, Q_W), F32),
                        pltpu.VMEM((rows_all, KV_W), F32),
                        pltpu.VMEM((rows_all, KV_W), F32),
                        pltpu.VMEM((rows_all, Q_W + D_LRU), F32)],
        compiler_params=pltpu.CompilerParams(dimension_semantics=("arbitrary",), vmem_limit_bytes=VMEM_LIMIT),
        name="even_mixer_sample",
    )(*ins)


def _ffn_rows(x, g_ref, w1_ref, w2_ref):
    xg = (x * g_ref[...]).astype(BF16)
    r2 = 1.0 / (jnp.mean(x * x, axis=-1, keepdims=True) + EPS)
    out = None
    step = FFN_CHUNK
    for c in range(D_FF // step):
        hcol = jnp.dot(xg, w1_ref[:, c * step:(c + 1) * step].astype(BF16), preferred_element_type=F32)
        hcol = jnp.square(jnp.maximum(hcol, 0.0)).astype(BF16)
        part = jnp.dot(hcol, w2_ref[c * step:(c + 1) * step, :].astype(BF16), preferred_element_type=F32)
        out = part if out is None else out + part
    return x + out * r2


def _ffn_kernel(xp_ref, xs_ref, g_ref, w1_ref, w2_ref, yp_ref, ys_ref):
    i = pl.program_id(0)
    yp_ref[...] = _ffn_rows(jnp.where(i == 0, xs_ref[...], xp_ref[...]), g_ref, w1_ref, w2_ref)

    @pl.when(i == 0)
    def _():
        ys_ref[...] = yp_ref[...]


def _sample_then_tiles_specs(tile, width, sample_rows=None):
    prompt = pl.BlockSpec((tile, width), lambda i: (jnp.maximum(i - 1, 0), 0))
    sample = pl.BlockSpec((sample_rows or tile, width), lambda i: (0, 0))
    return prompt, sample


def _ffn(xp, xs, g, w1, w2, layer):
    tile = xs.shape[0]
    n_tiles = xp.shape[0] // tile
    xp_spec, xs_spec = _sample_then_tiles_specs(tile, D_MODEL)
    layer_spec = lambda shape: pl.BlockSpec((None,) + shape[1:], lambda i: (layer, 0, 0),
                                            pipeline_mode=pl.Buffered(1))
    return pl.pallas_call(
        _ffn_kernel,
        out_shape=[jax.ShapeDtypeStruct(xp.shape, F32), jax.ShapeDtypeStruct(xs.shape, F32)],
        grid=(n_tiles + 1,),
        in_specs=[xp_spec, xs_spec, layer_spec(g.shape), layer_spec(w1.shape), layer_spec(w2.shape)],
        out_specs=list(_sample_then_tiles_specs(tile, D_MODEL)),
        compiler_params=pltpu.CompilerParams(dimension_semantics=("arbitrary",), vmem_limit_bytes=VMEM_LIMIT),
        name="channel_mlp",
    )(xp, xs, g, w1, w2)


def _gmlp_kernel(seg_s, xp_ref, xs_ref, g_ref, win_ref, vg_ref, wsp_ref, wss_ref, biasp_ref, biass_ref, wout_ref,
                 yp_ref, ys_ref, v_ref, v_scr):
    i = pl.program_id(0)
    is_s = i == 0
    tile = xp_ref.shape[0]
    sub = xs_ref.shape[0]
    n_sub = tile // sub
    xs_in, proj = [], []
    for u in range(n_sub):
        rows = slice(u * sub, (u + 1) * sub)
        x = jnp.where(is_s, xs_ref[...], xp_ref[rows, :]) if u == 0 else xp_ref[rows, :]
        xn = _rms_rows(x, g_ref[...]).astype(BF16)
        xs_in.append(x)
        proj.append((jnp.dot(xn, win_ref[:, D_C:2 * D_C].astype(BF16), preferred_element_type=F32),
                     jnp.dot(xn, win_ref[:, 0:D_C].astype(BF16), preferred_element_type=F32)))
    r = _row_iota((CHUNK_MLP, CHUNK_MLP))
    col = _lane_iota((CHUNK_MLP, CHUNK_MLP))
    keep_p = col <= r
    keep_s = jnp.logical_and((r // seg_s) == (col // seg_s), keep_p)
    ws = [jnp.where(is_s, jnp.where(keep_s, wss_ref[gi], 0.0), jnp.where(keep_p, wsp_ref[gi], 0.0)).astype(BF16)
          for gi in range(C_GROUPS)]
    step = MXU_WIDTH
    per_step = step // LANES
    for u in range(n_sub):
        rows = slice(u * sub, (u + 1) * sub)
        zv = _gelu(proj[u][0])
        zc = zv - jnp.mean(zv, axis=-1, keepdims=True)
        v = zc * lax.rsqrt(jnp.mean(zc * zc, axis=-1, keepdims=True) + EPS) * vg_ref[...]
        if u == 0:
            v_scr[...] = v
        vb = v.astype(BF16)
        sv = []
        for gi in range(C_GROUPS):
            lanes = slice(gi * LANES, (gi + 1) * LANES)
            bias = jnp.where(is_s, biass_ref[:, lanes], biasp_ref[:, lanes])
            blocks = [jnp.dot(ws[gi], vb[ch * CHUNK_MLP:(ch + 1) * CHUNK_MLP, lanes],
                              preferred_element_type=F32) + bias for ch in range(sub // CHUNK_MLP)]
            sv.append(jnp.concatenate(blocks, axis=0))
        acc = xs_in[u]
        for c in range(D_C // step):
            cols = slice(c * step, (c + 1) * step)
            gate = (_gelu(proj[u][1][:, cols]) *
                    jnp.concatenate(sv[c * per_step:(c + 1) * per_step], axis=1)).astype(BF16)
            acc = acc + jnp.dot(gate, wout_ref[cols, :].astype(BF16), preferred_element_type=F32)
        yp_ref[rows, :] = acc

    @pl.when(is_s)
    def _():
        ys_ref[...] = yp_ref[0:sub, :]
        v_ref[...] = v_scr[...]


def _gmlp(xp, xs, seg_s, g, win, vg, ws_p, ws_s, bias_p, bias_s, wout):
    rows_s = xs.shape[0]
    tile = GMLP_TILE
    n_tiles = xp.shape[0] // tile
    xp_spec, xs_spec = _sample_then_tiles_specs(tile, D_MODEL, rows_s)
    consts = (g, win, vg, ws_p, ws_s, bias_p, bias_s, wout)
    return pl.pallas_call(
        functools.partial(_gmlp_kernel, seg_s),
        out_shape=[jax.ShapeDtypeStruct(xp.shape, F32), jax.ShapeDtypeStruct(xs.shape, F32),
                   jax.ShapeDtypeStruct((rows_s, D_C), F32)],
        grid=(n_tiles + 1,),
        in_specs=[xp_spec, xs_spec] + [_const_spec(a.shape) for a in consts],
        out_specs=list(_sample_then_tiles_specs(tile, D_MODEL, rows_s)) +
                  [pl.BlockSpec((rows_s, D_C), lambda i: (0, 0))],
        scratch_shapes=[pltpu.VMEM((rows_s, D_C), F32)],
        compiler_params=pltpu.CompilerParams(dimension_semantics=("arbitrary",), vmem_limit_bytes=VMEM_LIMIT),
        name="gmlp_mixer",
    )(xp, xs, *consts)


def _rope_inv():
    half = HEAD_DIM // 2
    return ROPE_THETA ** (-jnp.arange(half, dtype=F32) / half)


def _rope_rows():
    inv = _rope_inv()
    sgn = jnp.ones((HEAD_DIM // 2,), F32)
    return jnp.concatenate([inv] * 4)[None, :], jnp.concatenate([-sgn, sgn, -sgn, sgn])[None, :]


def _rope_tables(pos):
    inv_row, sgn_row = _rope_rows()
    ang = pos.astype(F32)[:, None] * inv_row
    return jnp.cos(ang), jnp.sin(ang) * sgn_row


def _swap_head_halves(a):
    half = HEAD_DIM // 2
    return a.reshape(a.shape[:-1] + (-1, 2, half))[..., ::-1, :].reshape(a.shape)


def _gain_rows(g):
    return jnp.tile(jnp.concatenate([g, _swap_head_halves(g)], axis=0), (1, LANES // HEAD_DIM))


def _gate_weights(wa, wx):
    nb = LRU_BLOCKS // 2
    eye = jnp.eye(nb, dtype=bool)

    def half_diag(w, hh):
        blocks = w[hh * nb:(hh + 1) * nb]
        return jnp.where(eye[:, None, :, None], blocks[:, :, None, :], 0.0).reshape(nb * LRU_BW, nb * LRU_BW)
    return jnp.stack([jnp.concatenate([half_diag(wa, hh), half_diag(wx, hh)], axis=1)
                      for hh in range(2)]).astype(BF16)


def kernel(x_prompt, x_sample, cache_swa_k, cache_swa_v, state_lru_h, state_lru_conv, e_norm_g, e_w_in, e_q_norm_g, e_k_norm_g, e_sinks, e_conv_w, e_conv_b, e_gate_a_w, e_gate_a_b, e_gate_x_w, e_gate_x_b, e_lru_lambda, e_w_out, o_norm_g, o_w_in, o_v_norm_g, o_spatial_w, o_spatial_b, o_w_out, ffn_norm_g, ffn_w1, ffn_w2):
    bsz, s_len, _ = x_prompt.shape
    nseq, t_len, _ = x_sample.shape
    past_len = PAST_LEN
    row = lambda a: a.reshape(1, -1)

    e = 0
    p = {
        'g': row(e_norm_g[e]),
        'win': jnp.concatenate([e_w_in[e], _swap_head_halves(e_w_in[e][:, :Q_W + KV_W])], axis=1).astype(BF16),
        'qg': _gain_rows(e_q_norm_g[e:e + 1]),
        'kg': _gain_rows(e_k_norm_g[e:e + 1]),
        'sink': jnp.repeat(e_sinks[e:e + 1], HEAD_DIM, axis=1).reshape(N_Q_HEADS // 2, LANES),
        'convw': e_conv_w[e],
        'convb': row(e_conv_b[e]),
        'wg': _gate_weights(e_gate_a_w[e], e_gate_x_w[e]),
        'ba': row(e_gate_a_b[e]),
        'bx': row(e_gate_x_b[e]),
        'lam': row(e_lru_lambda[e]),
        'wout': e_w_out[e].astype(BF16),
    }
    yp, nk_p, nv_p, nh_p, nc_p = _even_prompt(x_prompt, p)

    cos_s, sin_s = _rope_tables(past_len + jnp.arange(t_len))
    cos_s = jnp.tile(cos_s, (nseq, 1))
    sin_s = jnp.tile(sin_s, (nseq, 1))
    kc = cache_swa_k[e].reshape(nseq, SWA_ROWS, KV_W)
    vc = cache_swa_v[e].reshape(nseq, SWA_ROWS, KV_W)
    h0_rows = jnp.repeat(state_lru_h[e], t_len, axis=0)
    xprev = jnp.pad(state_lru_conv[e], ((0, 0), (0, t_len - (CONV_W - 1)), (0, 0))).reshape(nseq * t_len, D_LRU)
    ys, nk_s, nv_s, h_s, xr_s = _even_sample(x_sample.reshape(nseq * t_len, D_MODEL), cos_s, sin_s, kc, vc,
                                             h0_rows, xprev, p)
    nh_s = h_s.reshape(nseq, t_len, D_LRU)[:, -1]
    nc_s = xr_s.reshape(nseq, t_len, D_LRU)[:, t_len - (CONV_W - 1):]

    ffn_g = ffn_norm_g.reshape(-1, 1, D_MODEL)
    yp, ys = _ffn(yp.reshape(bsz * s_len, D_MODEL), ys, ffn_g, ffn_w1, ffn_w2, 0)

    o = 0
    bias_p = jnp.repeat(o_spatial_b[o].T, LANES, axis=1)
    reps = CHUNK_MLP // t_len
    ws_s = jnp.tile(o_spatial_w[o][:, :t_len, :t_len], (1, reps, reps))
    bias_s = jnp.tile(jnp.repeat(o_spatial_b[o][:, :t_len].T, LANES, axis=1), (reps, 1))
    yp, ys, gv_s = _gmlp(yp, ys, t_len, row(o_norm_g[o]), o_w_in[o], row(o_v_norm_g[o]),
                         o_spatial_w[o], ws_s, bias_p, bias_s, o_w_out[o])

    yp, ys = _ffn(yp, ys, ffn_g, ffn_w1, ffn_w2, 1)

    kv_shape = (1, -1, SWA_ROWS, N_KV_HEADS, HEAD_DIM)
    return (yp.reshape(bsz, s_len, D_MODEL), ys.reshape(nseq, t_len, D_MODEL),
            nk_p.reshape(kv_shape), nv_p.reshape(kv_shape), nh_p.reshape(1, bsz, D_LRU), nc_p[None],
            nk_s.reshape(kv_shape), nv_s.reshape(kv_shape), nh_s[None], nc_s[None],
            gv_s.reshape(1, nseq, t_len, D_C))
```

```python
import functools
import math

import jax
import jax.numpy as jnp
from jax import lax
from jax.experimental import pallas as pl
from jax.experimental.pallas import tpu as pltpu

F32 = jnp.float32
BF16 = jnp.bfloat16

D_MODEL = 1024
CHUNK = 64
HEAD_DIM = 64
N_Q_HEADS = 8
N_KV_HEADS = 2
SWA_ROWS = 128
PAST_LEN = 4096
ROPE_THETA = 10000.0
NEG = -1e30
D_LRU = 512
LRU_BLOCKS = 8
LRU_BW = D_LRU // LRU_BLOCKS
CONV_W = 4
LRU_C = 8.0
Q_W = N_Q_HEADS * HEAD_DIM
KV_W = N_KV_HEADS * HEAD_DIM
CHUNK_MLP = 128
D_C = D_MODEL
C_GROUPS = 8
D_FF = 4 * D_MODEL
EPS = 1e-6
LOG2E = math.log2(math.e)
LOGIT_SCALE = HEAD_DIM ** -0.5 * LOG2E

LANES = 128
SUBLANES = 8
MXU_WIDTH = 256
KEY_BLOCK = 2 * CHUNK
BAND_BLOCKS = 3
VMEM_LIMIT = 56 * 1024 * 1024

PROMPT_TILE = 1024
PROMPT_SUBTILE = 256
GMLP_TILE = 1024
FFN_CHUNK = 1024
SAMPLE_GROUP = 8


def _gelu(x):
    k1 = -2.0 * math.sqrt(2.0 / math.pi) * math.log2(math.e)
    return x / (1.0 + jnp.exp2(x * (k1 + (k1 * 0.044715) * (x * x))))


def _rms_rows(x, g):
    return x * lax.rsqrt(jnp.mean(x * x, axis=-1, keepdims=True) + EPS) * g


def _lane_iota(shape):
    return lax.broadcasted_iota(jnp.int32, shape, 1)


def _row_iota(shape):
    return lax.broadcasted_iota(jnp.int32, shape, 0)


def _head_ones():
    return ((_row_iota((LANES, LANES)) // HEAD_DIM) == (_lane_iota((LANES, LANES)) // HEAD_DIM)).astype(BF16)


def _norm_rope(x, xp, g_cos, g_sin, ones_bd):
    ss = jnp.dot((x * x).astype(BF16), ones_bd, preferred_element_type=F32)
    return (x * g_cos + xp * g_sin) * lax.rsqrt(ss + HEAD_DIM * EPS)


def _diag_blocks(t):
    lo = _lane_iota(t.shape) < HEAD_DIM
    swapped = pltpu.roll(t, HEAD_DIM, 1)
    return [(jnp.where(lo, t, 0.0), jnp.where(lo, 0.0, swapped)),
            (jnp.where(lo, swapped, 0.0), jnp.where(lo, 0.0, t))]


def _ones_diag(rows):
    r = _row_iota((rows, LANES))
    return (((r % KEY_BLOCK) // CHUNK) == (_lane_iota((rows, LANES)) // HEAD_DIM)).astype(BF16)


def _attend_scores(qs, kb, col_mask):
    s = lax.dot_general(qs, kb, (((1,), (1,)), ((), ())), preferred_element_type=F32)
    return s if col_mask is None else jnp.where(col_mask, s, NEG)


def _attend_weights(s, sink_rows):
    mx = jnp.maximum(jnp.maximum(s[:, :LANES], s[:, LANES:2 * LANES]), s[:, 2 * LANES:])
    lo = _lane_iota(mx.shape) < HEAD_DIM
    m0 = jnp.max(jnp.where(lo, mx, -jnp.inf), axis=1, keepdims=True)
    m1 = jnp.max(jnp.where(lo, -jnp.inf, mx), axis=1, keepdims=True)
    m = jnp.maximum(jnp.where(lo, m0, m1), sink_rows)
    e = jnp.exp2(s - jnp.concatenate([m, m, m], axis=1))
    return e.astype(BF16), m


def _attend_output(e, m, vb, sink_rows):
    od = jnp.dot(e, vb, preferred_element_type=F32)
    den = od[:, LANES:] + jnp.exp2(sink_rows - m)
    return od[:, :LANES] / den


def _lru_scan(a, b, seg, tmod):
    d = 1
    while d < seg:
        ok = tmod >= d
        b = jnp.where(ok, a * pltpu.roll(b, d, 0), 0.0) + b
        if 2 * d < seg:
            a = jnp.where(ok, a * pltpu.roll(a, d, 0), a)
        d *= 2
    return b


def _lru_scan_tile(a, b, h_prev):
    rows = a.shape[0]
    groups = rows // SUBLANES
    a3 = a.reshape(groups, SUBLANES, LANES)
    b3 = b.reshape(groups, SUBLANES, LANES)
    sub = lax.broadcasted_iota(jnp.int32, a3.shape, 1)
    d = 1
    while d < SUBLANES:
        ok = sub >= d
        b3 = jnp.where(ok, a3 * pltpu.roll(b3, d, 1), 0.0) + b3
        a3 = jnp.where(ok, a3 * pltpu.roll(a3, d, 1), a3)
        d *= 2
    carry = jnp.broadcast_to(h_prev, (SUBLANES, LANES))
    out = []
    for g in range(groups):
        hg = a3[g] * carry + b3[g]
        out.append(hg)
        carry = jnp.broadcast_to(hg[SUBLANES - 1:SUBLANES, :], (SUBLANES, LANES))
    return jnp.concatenate(out, axis=0)


def _lru_branch(xs, gr, h_in, seg, tmod, convw_ref, convb_ref, wg_ref, ba_ref, bx_ref, lam_ref):
    xc, gates = _lru_gates(xs, convw_ref, convb_ref, wg_ref)
    parts = [_lru_piece(k, xc, gates, gr, h_in, seg, tmod, ba_ref, bx_ref, lam_ref) for k in range(D_LRU // LANES)]
    return jnp.concatenate([p[0] for p in parts], axis=1), jnp.concatenate([p[1] for p in parts], axis=1)


def _lru_gates(xs, convw_ref, convb_ref, wg_ref):
    xc = xs[0] * convw_ref[0:1, :] + convb_ref[...]
    for i in range(1, CONV_W):
        xc = xc + xs[i] * convw_ref[i:i + 1, :]
    xcb = xc.astype(BF16)
    half = D_LRU // 2
    return xc, [jnp.dot(xcb[:, hh * half:(hh + 1) * half], wg_ref[hh], preferred_element_type=F32)
                for hh in range(2)]


def _lru_piece(k, xc, gates, gr, h_in, seg, tmod, ba_ref, bx_ref, lam_ref):
    half = D_LRU // 2
    hh, tt = divmod(k, half // LANES)
    sl = slice(k * LANES, (k + 1) * LANES)
    g = gates[hh]
    r = jax.nn.sigmoid(g[:, tt * LANES:(tt + 1) * LANES] + ba_ref[:, sl])
    gi = jax.nn.sigmoid(g[:, half + tt * LANES:half + (tt + 1) * LANES] + bx_ref[:, sl])
    log_a = r * (-LRU_C * jax.nn.softplus(-lam_ref[:, sl]))
    a = jnp.exp(log_a)
    y = -jnp.tanh(log_a) * (a * a + 1.0)
    b = jnp.where(y > 0.0, y * lax.rsqrt(y), 0.0) * (gi * xc[:, sl])
    if seg is None:
        h = _lru_scan_tile(a, b, h_in[:, sl])
    else:
        h = _lru_scan(a, b + jnp.where(tmod == 0, a * h_in[:, sl], 0.0), seg, tmod)
    return h * _gelu(gr[:, sl]), h


def _even_prompt_kernel(x_ref, cos_ref, sin_ref, inv_ref, sgn_ref, g_ref, win_ref, qg_ref, kg_ref, sink_ref,
                        convw_ref, convb_ref, wg_ref, ba_ref, bx_ref, lam_ref, wout_ref,
                        y_ref, nk_ref, nv_ref, nh_ref, nc_ref,
                        kb_scr, vb_scr, xr_scr, h_scr, mix_scr):
    t = pl.program_id(1)
    nt = pl.num_programs(1)
    tile = x_ref.shape[1]
    n_chunks = tile // CHUNK
    hist = (BAND_BLOCKS - 1) * KEY_BLOCK
    slot = t % 2

    @pl.when(t == 0)
    def _():
        ones = _ones_diag(vb_scr.shape[2])
        for j in range(N_KV_HEADS):
            kb_scr[0, j, 0:hist, :] = jnp.zeros((hist, LANES), BF16)
            vb_scr[0, j, 0:hist, 0:LANES] = jnp.zeros((hist, LANES), BF16)
            for s in range(2):
                vb_scr[s, j, :, LANES:2 * LANES] = ones
        xr_scr[0:SUBLANES, :] = jnp.zeros((SUBLANES, D_LRU), F32)
        h_scr[...] = jnp.zeros(h_scr.shape, F32)

    sub = PROMPT_SUBTILE
    n_sub = tile // sub
    sub_chunks = sub // CHUNK
    o3 = Q_W + 2 * KV_W
    xs_in, proj = [], []
    for u in range(n_sub):
        x = x_ref[0, u * sub:(u + 1) * sub, :]
        xn = _rms_rows(x, g_ref[...]).astype(BF16)
        xs_in.append(x)
        o5 = o3 + 2 * D_LRU
        proj.append((jnp.dot(xn, win_ref[:, 0:Q_W], preferred_element_type=F32),
                     jnp.dot(xn, win_ref[:, Q_W:o3], preferred_element_type=F32),
                     jnp.dot(xn, win_ref[:, o3:o3 + D_LRU], preferred_element_type=F32),
                     jnp.dot(xn, win_ref[:, o3 + D_LRU:o5], preferred_element_type=F32),
                     jnp.dot(xn, win_ref[:, o5:o5 + Q_W], preferred_element_type=F32),
                     jnp.dot(xn, win_ref[:, o5 + Q_W:o5 + Q_W + KV_W], preferred_element_type=F32)))
    for u in range(n_sub):
        xr_scr[SUBLANES + u * sub:SUBLANES + (u + 1) * sub, :] = proj[u][2]

    base = (t * tile).astype(F32) * inv_ref[...]
    cos_b = jnp.cos(base)
    sin_b = jnp.sin(base) * sgn_ref[...]

    band = BAND_BLOCKS * KEY_BLOCK
    col = _lane_iota((2 * CHUNK, band))
    sink_rows = [jnp.concatenate(
        [jnp.broadcast_to(sink_ref[2 * j:2 * j + 1, :] * LOG2E, (CHUNK, LANES)),
         jnp.broadcast_to(sink_ref[2 * j + 1:2 * j + 2, :] * LOG2E, (CHUNK, LANES))], axis=0) for j in range(N_KV_HEADS)]
    ones_bd = _head_ones()
    norm_c = math.sqrt(HEAD_DIM)
    q_all = []
    for u in range(n_sub):
        hq, hkv, xr, gr, hqp, hkp = proj[u]
        cos_l = cos_ref[u * sub:(u + 1) * sub, :]
        sin_l = sin_ref[u * sub:(u + 1) * sub, :]
        cos_t = cos_b * cos_l - sin_b * sin_l
        sin_t = sin_b * cos_l + cos_b * sin_l
        q_cos = cos_t * (qg_ref[0:1, :] * (norm_c * LOGIT_SCALE))
        q_sin = sin_t * (qg_ref[1:2, :] * (norm_c * LOGIT_SCALE))
        q_tiles = [_norm_rope(hq[:, n * LANES:(n + 1) * LANES], hqp[:, n * LANES:(n + 1) * LANES],
                              q_cos, q_sin, ones_bd).astype(BF16) for n in range(Q_W // LANES)]
        kr = _norm_rope(hkv[:, 0:KV_W], hkp, cos_t * (kg_ref[0:1, :] * norm_c), sin_t * (kg_ref[1:2, :] * norm_c),
                        ones_bd)
        vv = hkv[:, KV_W:2 * KV_W]
        k_blocks, v_blocks = _diag_blocks(kr), _diag_blocks(vv)
        for j in range(N_KV_HEADS):
            (ktop, kbot), (vtop, vbot) = k_blocks[j], v_blocks[j]
            for i in range(sub_chunks):
                r0 = hist + (u * sub_chunks + i) * KEY_BLOCK
                rows = slice(i * CHUNK, (i + 1) * CHUNK)
                kb_scr[slot, j, r0:r0 + CHUNK, :] = ktop[rows].astype(BF16)
                kb_scr[slot, j, r0 + CHUNK:r0 + KEY_BLOCK, :] = kbot[rows].astype(BF16)
                vb_scr[slot, j, r0:r0 + CHUNK, 0:LANES] = vtop[rows].astype(BF16)
                vb_scr[slot, j, r0 + CHUNK:r0 + KEY_BLOCK, 0:LANES] = vbot[rows].astype(BF16)
        q_all.append(q_tiles)

    pairs = [(i, j) for i in range(sub_chunks) for j in range(N_KV_HEADS)]
    scores_all, gates_all = [], []
    for u in range(n_sub):
        q_tiles = q_all[u]
        scores = []
        for i, j in pairs:
            rows = slice(i * CHUNK, (i + 1) * CHUNK)
            ci = u * sub_chunks + i
            col_mask = None
            if ci < BAND_BLOCKS - 1:
                col_mask = col >= (1 - jnp.minimum(t, 1)) * ((BAND_BLOCKS - 1 - ci) * KEY_BLOCK)
            qs = jnp.concatenate([q_tiles[2 * j][rows], q_tiles[2 * j + 1][rows]], axis=0)
            scores.append(_attend_scores(qs, kb_scr[slot, j, ci * KEY_BLOCK:ci * KEY_BLOCK + band, :], col_mask))
        scores_all.append(scores)
        first = SUBLANES + u * sub
        taps = [xr_scr[first - (CONV_W - 1 - i):first - (CONV_W - 1 - i) + sub, :] for i in range(CONV_W - 1)]
        gates_all.append(_lru_gates(taps + [proj[u][2]], convw_ref, convb_ref, wg_ref))

    h_carry = h_scr[0:1, :]
    for u in range(n_sub):
        xr, gr = proj[u][2], proj[u][3]
        scores = scores_all[u]
        xc, gates = gates_all[u]
        n_pieces = D_LRU // LANES
        per_piece = len(pairs) // n_pieces
        weights, lru = [], []
        for k in range(n_pieces):
            weights += [_attend_weights(scores[p], sink_rows[pairs[p][1]])
                        for p in range(k * per_piece, (k + 1) * per_piece)]
            lru.append(_lru_piece(k, xc, gates, gr, h_carry, None, None, ba_ref, bx_ref, lam_ref))
        for (e, m), (i, j) in zip(weights, pairs):
            ci = u * sub_chunks + i
            rows = slice(u * sub + i * CHUNK, u * sub + (i + 1) * CHUNK)
            o = _attend_output(e, m, vb_scr[slot, j, ci * KEY_BLOCK:ci * KEY_BLOCK + band, :], sink_rows[j])
            c0 = 2 * j * LANES
            mix_scr[rows, c0:c0 + LANES] = o[0:CHUNK].astype(BF16)
            mix_scr[rows, c0 + LANES:c0 + 2 * LANES] = o[CHUNK:2 * CHUNK].astype(BF16)

        h_carry = jnp.concatenate([h[sub - 1:sub, :] for _, h in lru], axis=1)
        rows = slice(u * sub, (u + 1) * sub)
        for k, (rec, _) in enumerate(lru):
            mix_scr[rows, Q_W + k * LANES:Q_W + (k + 1) * LANES] = rec.astype(BF16)

        y_att = jnp.dot(mix_scr[rows, 0:Q_W], wout_ref[0:Q_W, :], preferred_element_type=F32)
        y_rec = jnp.dot(mix_scr[rows, Q_W:Q_W + D_LRU], wout_ref[Q_W:Q_W + D_LRU, :], preferred_element_type=F32)
        y_ref[0, rows, :] = xs_in[u] + y_att + y_rec

    last = n_chunks * KEY_BLOCK
    for j in range(N_KV_HEADS):
        kb_scr[1 - slot, j, 0:hist, :] = kb_scr[slot, j, last:last + hist, :]
        vb_scr[1 - slot, j, 0:hist, 0:LANES] = vb_scr[slot, j, last:last + hist, 0:LANES]
    h_scr[0:1, :] = h_carry
    xr_scr[0:SUBLANES, :] = xr[sub - SUBLANES:sub, :]

    @pl.when(t == nt - 1)
    def _():
        nk_ref[0] = kr[sub - SWA_ROWS:sub, :]
        nv_ref[0] = vv[sub - SWA_ROWS:sub, :]
        nh_ref[0] = h_carry
        nc_ref[0] = xr[sub - (CONV_W - 1):sub, :]


def _const_spec(shape):
    zeros = (0,) * len(shape)
    return pl.BlockSpec(shape, lambda *_: zeros, pipeline_mode=pl.Buffered(1))


def _even_prompt(x, p):
    bsz, s_len, _ = x.shape
    tile = PROMPT_TILE
    cos_l, sin_l = _rope_tables(jnp.arange(tile))
    inv_row, sgn_row = _rope_rows()
    nt = s_len // tile
    key_rows = (tile // CHUNK + BAND_BLOCKS - 1) * KEY_BLOCK
    consts = [p['g'], p['win'], p['qg'], p['kg'], p['sink'], p['convw'], p['convb'], p['wg'], p['ba'], p['bx'],
              p['lam'], p['wout']]
    out_shape = [
        jax.ShapeDtypeStruct((bsz, s_len, D_MODEL), F32),
        jax.ShapeDtypeStruct((bsz, SWA_ROWS, KV_W), F32),
        jax.ShapeDtypeStruct((bsz, SWA_ROWS, KV_W), F32),
        jax.ShapeDtypeStruct((bsz, 1, D_LRU), F32),
        jax.ShapeDtypeStruct((bsz, CONV_W - 1, D_LRU), F32),
    ]
    return pl.pallas_call(
        _even_prompt_kernel,
        out_shape=out_shape,
        grid=(bsz, nt),
        in_specs=[pl.BlockSpec((1, tile, D_MODEL), lambda b, t: (b, t, 0)),
                  _const_spec(cos_l.shape), _const_spec(sin_l.shape), _const_spec(inv_row.shape),
                  _const_spec(sgn_row.shape)] + [_const_spec(c.shape) for c in consts],
        out_specs=[pl.BlockSpec((1, tile, D_MODEL), lambda b, t: (b, t, 0)),
                   pl.BlockSpec((1, SWA_ROWS, KV_W), lambda b, t: (b, 0, 0)),
                   pl.BlockSpec((1, SWA_ROWS, KV_W), lambda b, t: (b, 0, 0)),
                   pl.BlockSpec((1, 1, D_LRU), lambda b, t: (b, 0, 0)),
                   pl.BlockSpec((1, CONV_W - 1, D_LRU), lambda b, t: (b, 0, 0))],
        scratch_shapes=[pltpu.VMEM((2, N_KV_HEADS, key_rows, LANES), BF16),
                        pltpu.VMEM((2, N_KV_HEADS, key_rows, 2 * LANES), BF16),
                        pltpu.VMEM((tile + SUBLANES, D_LRU), F32),
                        pltpu.VMEM((SUBLANES, D_LRU), F32),
                        pltpu.VMEM((tile, Q_W + D_LRU), BF16)],
        compiler_params=pltpu.CompilerParams(dimension_semantics=("arbitrary", "arbitrary"),
                                             vmem_limit_bytes=VMEM_LIMIT),
        name="even_mixer_prompt",
    )(x, cos_l, sin_l, inv_row, sgn_row, *consts)


def _even_sample_kernel(x_ref, cos_ref, sin_ref, kc_ref, vc_ref, h0_ref, xprev_ref,
                        g_ref, win_ref, qg_ref, kg_ref, sink_ref, convw_ref, convb_ref,
                        wg_ref, ba_ref, bx_ref, lam_ref, wout_ref,
                        y_ref, nk_ref, nv_ref, h_ref, xr_ref,
                        q_scr, k_scr, v_scr, mix_scr):
    rows_all = x_ref.shape[0]
    nseq = kc_ref.shape[0]
    t_len = rows_all // nseq
    x = x_ref[...]
    xn = _rms_rows(x, g_ref[...]).astype(BF16)
    cos_t = cos_ref[...]
    sin_t = sin_ref[...]

    o3 = Q_W + 2 * KV_W
    o5 = o3 + 2 * D_LRU
    hq = jnp.dot(xn, win_ref[:, 0:Q_W], preferred_element_type=F32)
    hkv = jnp.dot(xn, win_ref[:, Q_W:o3], preferred_element_type=F32)
    xr = jnp.dot(xn, win_ref[:, o3:o3 + D_LRU], preferred_element_type=F32)
    gr = jnp.dot(xn, win_ref[:, o3 + D_LRU:o5], preferred_element_type=F32)
    hqp = jnp.dot(xn, win_ref[:, o5:o5 + Q_W], preferred_element_type=F32)
    hkp = jnp.dot(xn, win_ref[:, o5 + Q_W:o5 + Q_W + KV_W], preferred_element_type=F32)
    ones_bd = _head_ones()
    norm_c = math.sqrt(HEAD_DIM)
    q_cos = cos_t * (qg_ref[0:1, :] * (norm_c * LOGIT_SCALE))
    q_sin = sin_t * (qg_ref[1:2, :] * (norm_c * LOGIT_SCALE))
    for n in range(Q_W // LANES):
        lanes = slice(n * LANES, (n + 1) * LANES)
        q_scr[:, lanes] = _norm_rope(hq[:, lanes], hqp[:, lanes], q_cos, q_sin, ones_bd)
    k_scr[...] = _norm_rope(hkv[:, 0:KV_W], hkp, cos_t * (kg_ref[0:1, :] * norm_c),
                            sin_t * (kg_ref[1:2, :] * norm_c), ones_bd)
    v_scr[...] = hkv[:, KV_W:2 * KV_W]

    band = BAND_BLOCKS * KEY_BLOCK
    col = _lane_iota((2 * t_len, band))
    col_mask = jnp.logical_or(col < 2 * KEY_BLOCK, (col % CHUNK) < t_len)
    ones_band = _ones_diag(band)
    pad = jnp.zeros((CHUNK - t_len, LANES), F32)

    sink_rows = [jnp.concatenate(
        [jnp.broadcast_to(sink_ref[2 * j:2 * j + 1, :] * LOG2E, (t_len, LANES)),
         jnp.broadcast_to(sink_ref[2 * j + 1:2 * j + 2, :] * LOG2E, (t_len, LANES))], axis=0) for j in range(N_KV_HEADS)]

    def group_body(gi, carry):
        work = []
        for s in range(SAMPLE_GROUP):
            b = gi * SAMPLE_GROUP + s
            r0 = pl.multiple_of(b * t_len, t_len)
            kc = kc_ref[b]
            vc = vc_ref[b]
            knew = k_scr[pl.ds(r0, t_len), :]
            vnew = v_scr[pl.ds(r0, t_len), :]
            nk_ref[b, 0:SWA_ROWS - t_len, :] = kc[t_len:SWA_ROWS]
            nk_ref[b, SWA_ROWS - t_len:SWA_ROWS, :] = knew
            nv_ref[b, 0:SWA_ROWS - t_len, :] = vc[t_len:SWA_ROWS]
            nv_ref[b, SWA_ROWS - t_len:SWA_ROWS, :] = vnew
            kfull = jnp.concatenate([kc, knew, pad], axis=0)
            vfull = jnp.concatenate([vc, vnew, pad], axis=0)
            k_blocks, v_blocks = _diag_blocks(kfull), _diag_blocks(vfull)
            for j in range(N_KV_HEADS):
                (ktop, kbot), (vtop, vbot) = k_blocks[j], v_blocks[j]
                kparts, vparts = [], []
                for c in range(BAND_BLOCKS):
                    rs = slice(c * CHUNK, (c + 1) * CHUNK)
                    kparts += [ktop[rs], kbot[rs]]
                    vparts += [vtop[rs], vbot[rs]]
                kb = jnp.concatenate(kparts, axis=0).astype(BF16)
                vb = jnp.concatenate([jnp.concatenate(vparts, axis=0).astype(BF16), ones_band], axis=1)
                c0 = 2 * j * LANES
                qs = jnp.concatenate([q_scr[pl.ds(r0, t_len), c0:c0 + LANES],
                                      q_scr[pl.ds(r0, t_len), c0 + LANES:c0 + 2 * LANES]], axis=0).astype(BF16)
                work.append((r0, j, vb, _attend_scores(qs, kb, col_mask)))
        weights = [_attend_weights(s, sink_rows[j]) for _, j, _, s in work]
        for (r0, j, vb, _), (e, m) in zip(work, weights):
            o = _attend_output(e, m, vb, sink_rows[j])
            c0 = 2 * j * LANES
            mix_scr[pl.ds(r0, t_len), c0:c0 + LANES] = o[0:t_len]
            mix_scr[pl.ds(r0, t_len), c0 + LANES:c0 + 2 * LANES] = o[t_len:2 * t_len]
        return carry

    lax.fori_loop(0, nseq // SAMPLE_GROUP, group_body, 0)

    tmod =_row_iota((rows_all, LANES)) % t_len
    tmod_w = _row_iota((rows_all, D_LRU)) % t_len
    xprev = xprev_ref[...]
    xs = []
    for i in range(CONV_W - 1):
        k = CONV_W - 1 - i
        own = pltpu.roll(xr, k, 0)
        cached = xprev if k == CONV_W - 1 else pltpu.roll(xprev, rows_all - (CONV_W - 1 - k), 0)
        xs.append(jnp.where(tmod_w >= k, own, cached))
    xs.append(xr)
    rec, h = _lru_branch(xs, gr, h0_ref[...], t_len, tmod, convw_ref, convb_ref, wg_ref, ba_ref, bx_ref, lam_ref)
    mix_scr[:, Q_W:Q_W + D_LRU] = rec
    xr_ref[...] = xr
    h_ref[...] = h
    y_ref[...] = x + jnp.dot(mix_scr[...].astype(BF16), wout_ref[...], preferred_element_type=F32)


def _even_sample(x2, cos_t, sin_t, kc, vc, h0_rows, xprev, p):
    rows_all = x2.shape[0]
    nseq = kc.shape[0]
    consts = [p['g'], p['win'], p['qg'], p['kg'], p['sink'], p['convw'], p['convb'], p['wg'], p['ba'], p['bx'],
              p['lam'], p['wout']]
    ins = [x2, cos_t, sin_t, kc, vc, h0_rows, xprev] + consts
    out_shape = [
        jax.ShapeDtypeStruct((rows_all, D_MODEL), F32),
        jax.ShapeDtypeStruct((nseq, SWA_ROWS, KV_W), F32),
        jax.ShapeDtypeStruct((nseq, SWA_ROWS, KV_W), F32),
        jax.ShapeDtypeStruct((rows_all, D_LRU), F32),
        jax.ShapeDtypeStruct((rows_all, D_LRU), F32),
    ]
    return pl.pallas_call(
        _even_sample_kernel,
        out_shape=out_shape,
        grid=(1,),
        in_specs=[_const_spec(a.shape) for a in ins],
        out_specs=[pl.BlockSpec(s.shape, lambda i, n=len(s.shape): (0,) * n) for s in out_shape],
        scratch_shapes=[pltpu.VMEM((rows_all, Q_W), F32),
                        pltpu.VMEM((rows_all, KV_W), F32),
                        pltpu.VMEM((rows_all, KV_W), F32),
                        pltpu.VMEM((rows_all, Q_W + D_LRU), F32)],
        compiler_params=pltpu.CompilerParams(dimension_semantics=("arbitrary",), vmem_limit_bytes=VMEM_LIMIT),
        name="even_mixer_sample",
    )(*ins)


def _ffn_rows(x, g_ref, w1_ref, w2_ref):
    xg = (x * g_ref[...]).astype(BF16)
    r2 = 1.0 / (jnp.mean(x * x, axis=-1, keepdims=True) + EPS)
    out = None
    step = FFN_CHUNK
    for c in range(D_FF // step):
        hcol = jnp.dot(xg, w1_ref[:, c * step:(c + 1) * step].astype(BF16), preferred_element_type=F32)
        hcol = jnp.square(jnp.maximum(hcol, 0.0)).astype(BF16)
        part = jnp.dot(hcol, w2_ref[c * step:(c + 1) * step, :].astype(BF16), preferred_element_type=F32)
        out = part if out is None else out + part
    return x + out * r2


def _ffn_kernel(xp_ref, xs_ref, g_ref, w1_ref, w2_ref, yp_ref, ys_ref):
    i = pl.program_id(0)
    yp_ref[...] = _ffn_rows(jnp.where(i == 0, xs_ref[...], xp_ref[...]), g_ref, w1_ref, w2_ref)

    @pl.when(i == 0)
    def _():
        ys_ref[...] = yp_ref[...]


def _sample_then_tiles_specs(tile, width, sample_rows=None):
    prompt = pl.BlockSpec((tile, width), lambda i: (jnp.maximum(i - 1, 0), 0))
    sample = pl.BlockSpec((sample_rows or tile, width), lambda i: (0, 0))
    return prompt, sample


def _ffn(xp, xs, g, w1, w2, layer):
    tile = xs.shape[0]
    n_tiles = xp.shape[0] // tile
    xp_spec, xs_spec = _sample_then_tiles_specs(tile, D_MODEL)
    layer_spec = lambda shape: pl.BlockSpec((None,) + shape[1:], lambda i: (layer, 0, 0),
                                            pipeline_mode=pl.Buffered(1))
    return pl.pallas_call(
        _ffn_kernel,
        out_shape=[jax.ShapeDtypeStruct(xp.shape, F32), jax.ShapeDtypeStruct(xs.shape, F32)],
        grid=(n_tiles + 1,),
        in_specs=[xp_spec, xs_spec, layer_spec(g.shape), layer_spec(w1.shape), layer_spec(w2.shape)],
        out_specs=list(_sample_then_tiles_specs(tile, D_MODEL)),
        compiler_params=pltpu.CompilerParams(dimension_semantics=("arbitrary",), vmem_limit_bytes=VMEM_LIMIT),
        name="channel_mlp",
    )(xp, xs, g, w1, w2)


def _gmlp_kernel(seg_s, xp_ref, xs_ref, g_ref, win_ref, vg_ref, wsp_ref, wss_ref, biasp_ref, biass_ref, wout_ref,
                 yp_ref, ys_ref, v_ref, v_scr):
    i = pl.program_id(0)
    is_s = i == 0
    tile = xp_ref.shape[0]
    sub = xs_ref.shape[0]
    n_sub = tile // sub
    xs_in, proj = [], []
    for u in range(n_sub):
        rows = slice(u * sub, (u + 1) * sub)
        x = jnp.where(is_s, xs_ref[...], xp_ref[rows, :]) if u == 0 else xp_ref[rows, :]
        xn = _rms_rows(x, g_ref[...]).astype(BF16)
        xs_in.append(x)
        proj.append((jnp.dot(xn, win_ref[:, D_C:2 * D_C].astype(BF16), preferred_element_type=F32),
                     jnp.dot(xn, win_ref[:, 0:D_C].astype(BF16), preferred_element_type=F32)))
    r = _row_iota((CHUNK_MLP, CHUNK_MLP))
    col = _lane_iota((CHUNK_MLP, CHUNK_MLP))
    keep_p = col <= r
    keep_s = jnp.logical_and((r // seg_s) == (col // seg_s), keep_p)
    ws = [jnp.where(is_s, jnp.where(keep_s, wss_ref[gi], 0.0), jnp.where(keep_p, wsp_ref[gi], 0.0)).astype(BF16)
          for gi in range(C_GROUPS)]
    step = MXU_WIDTH
    per_step = step // LANES
    for u in range(n_sub):
        rows = slice(u * sub, (u + 1) * sub)
        zv = _gelu(proj[u][0])
        zc = zv - jnp.mean(zv, axis=-1, keepdims=True)
        v = zc * lax.rsqrt(jnp.mean(zc * zc, axis=-1, keepdims=True) + EPS) * vg_ref[...]
        if u == 0:
            v_scr[...] = v
        vb = v.astype(BF16)
        sv = []
        for gi in range(C_GROUPS):
            lanes = slice(gi * LANES, (gi + 1) * LANES)
            bias = jnp.where(is_s, biass_ref[:, lanes], biasp_ref[:, lanes])
            blocks = [jnp.dot(ws[gi], vb[ch * CHUNK_MLP:(ch + 1) * CHUNK_MLP, lanes],
                              preferred_element_type=F32) + bias for ch in range(sub // CHUNK_MLP)]
            sv.append(jnp.concatenate(blocks, axis=0))
        acc = xs_in[u]
        for c in range(D_C // step):
            cols = slice(c * step, (c + 1) * step)
            gate = (_gelu(proj[u][1][:, cols]) *
                    jnp.concatenate(sv[c * per_step:(c + 1) * per_step], axis=1)).astype(BF16)
            acc = acc + jnp.dot(gate, wout_ref[cols, :].astype(BF16), preferred_element_type=F32)
        yp_ref[rows, :] = acc

    @pl.when(is_s)
    def _():
        ys_ref[...] = yp_ref[0:sub, :]
        v_ref[...] = v_scr[...]


def _gmlp(xp, xs, seg_s, g, win, vg, ws_p, ws_s, bias_p, bias_s, wout):
    rows_s = xs.shape[0]
    tile = GMLP_TILE
    n_tiles = xp.shape[0] // tile
    xp_spec, xs_spec = _sample_then_tiles_specs(tile, D_MODEL, rows_s)
    consts = (g, win, vg, ws_p, ws_s, bias_p, bias_s, wout)
    return pl.pallas_call(
        functools.partial(_gmlp_kernel, seg_s),
        out_shape=[jax.ShapeDtypeStruct(xp.shape, F32), jax.ShapeDtypeStruct(xs.shape, F32),
                   jax.ShapeDtypeStruct((rows_s, D_C), F32)],
        grid=(n_tiles + 1,),
        in_specs=[xp_spec, xs_spec] + [_const_spec(a.shape) for a in consts],
        out_specs=list(_sample_then_tiles_specs(tile, D_MODEL, rows_s)) +
                  [pl.BlockSpec((rows_s, D_C), lambda i: (0, 0))],
        scratch_shapes=[pltpu.VMEM((rows_s, D_C), F32)],
        compiler_params=pltpu.CompilerParams(dimension_semantics=("arbitrary",), vmem_limit_bytes=VMEM_LIMIT),
        name="gmlp_mixer",
    )(xp, xs, *consts)


def _rope_inv():
    half = HEAD_DIM // 2
    return ROPE_THETA ** (-jnp.arange(half, dtype=F32) / half)


def _rope_rows():
    inv = _rope_inv()
    sgn = jnp.ones((HEAD_DIM // 2,), F32)
    return jnp.concatenate([inv] * 4)[None, :], jnp.concatenate([-sgn, sgn, -sgn, sgn])[None, :]


def _rope_tables(pos):
    inv_row, sgn_row = _rope_rows()
    ang = pos.astype(F32)[:, None] * inv_row
    return jnp.cos(ang), jnp.sin(ang) * sgn_row


def _swap_head_halves(a):
    half = HEAD_DIM // 2
    return a.reshape(a.shape[:-1] + (-1, 2, half))[..., ::-1, :].reshape(a.shape)


def _gain_rows(g):
    return jnp.tile(jnp.concatenate([g, _swap_head_halves(g)], axis=0), (1, LANES // HEAD_DIM))


def _gate_weights(wa, wx):
    nb = LRU_BLOCKS // 2
    eye = jnp.eye(nb, dtype=bool)

    def half_diag(w, hh):
        blocks = w[hh * nb:(hh + 1) * nb]
        return jnp.where(eye[:, None, :, None], blocks[:, :, None, :], 0.0).reshape(nb * LRU_BW, nb * LRU_BW)
    return jnp.stack([jnp.concatenate([half_diag(wa, hh), half_diag(wx, hh)], axis=1)
                      for hh in range(2)]).astype(BF16)


def kernel(x_prompt, x_sample, cache_swa_k, cache_swa_v, state_lru_h, state_lru_conv, e_norm_g, e_w_in, e_q_norm_g, e_k_norm_g, e_sinks, e_conv_w, e_conv_b, e_gate_a_w, e_gate_a_b, e_gate_x_w, e_gate_x_b, e_lru_lambda, e_w_out, o_norm_g, o_w_in, o_v_norm_g, o_spatial_w, o_spatial_b, o_w_out, ffn_norm_g, ffn_w1, ffn_w2):
    bsz, s_len, _ = x_prompt.shape
    nseq, t_len, _ = x_sample.shape
    past_len = PAST_LEN
    row = lambda a: a.reshape(1, -1)

    e = 0
    p = {
        'g': row(e_norm_g[e]),
        'win': jnp.concatenate([e_w_in[e], _swap_head_halves(e_w_in[e][:, :Q_W + KV_W])], axis=1).astype(BF16),
        'qg': _gain_rows(e_q_norm_g[e:e + 1]),
        'kg': _gain_rows(e_k_norm_g[e:e + 1]),
        'sink': jnp.repeat(e_sinks[e:e + 1], HEAD_DIM, axis=1).reshape(N_Q_HEADS // 2, LANES),
        'convw': e_conv_w[e],
        'convb': row(e_conv_b[e]),
        'wg': _gate_weights(e_gate_a_w[e], e_gate_x_w[e]),
        'ba': row(e_gate_a_b[e]),
        'bx': row(e_gate_x_b[e]),
        'lam': row(e_lru_lambda[e]),
        'wout': e_w_out[e].astype(BF16),
    }
    yp, nk_p, nv_p, nh_p, nc_p = _even_prompt(x_prompt, p)

    cos_s, sin_s = _rope_tables(past_len + jnp.arange(t_len))
    cos_s = jnp.tile(cos_s, (nseq, 1))
    sin_s = jnp.tile(sin_s, (nseq, 1))
    kc = cache_swa_k[e].reshape(nseq, SWA_ROWS, KV_W)
    vc = cache_swa_v[e].reshape(nseq, SWA_ROWS, KV_W)
    h0_rows = jnp.repeat(state_lru_h[e], t_len, axis=0)
    xprev = jnp.pad(state_lru_conv[e], ((0, 0), (0, t_len - (CONV_W - 1)), (0, 0))).reshape(nseq * t_len, D_LRU)
    ys, nk_s, nv_s, h_s, xr_s = _even_sample(x_sample.reshape(nseq * t_len, D_MODEL), cos_s, sin_s, kc, vc,
                                             h0_rows, xprev, p)
    nh_s = h_s.reshape(nseq, t_len, D_LRU)[:, -1]
    nc_s = xr_s.reshape(nseq, t_len, D_LRU)[:, t_len - (CONV_W - 1):]

    ffn_g = ffn_norm_g.reshape(-1, 1, D_MODEL)
    yp, ys = _ffn(yp.reshape(bsz * s_len, D_MODEL), ys, ffn_g, ffn_w1, ffn_w2, 0)

    o = 0
    bias_p = jnp.repeat(o_spatial_b[o].T, LANES, axis=1)
    reps = CHUNK_MLP // t_len
    ws_s = jnp.tile(o_spatial_w[o][:, :t_len, :t_len], (1, reps, reps))
    bias_s = jnp.tile(jnp.repeat(o_spatial_b[o][:, :t_len].T, LANES, axis=1), (reps, 1))
    yp, ys, gv_s = _gmlp(yp, ys, t_len, row(o_norm_g[o]), o_w_in[o], row(o_v_norm_g[o]),
                         o_spatial_w[o], ws_s, bias_p, bias_s, o_w_out[o])

    yp, ys = _ffn(yp, ys, ffn_g, ffn_w1, ffn_w2, 1)

    kv_shape = (1, -1, SWA_ROWS, N_KV_HEADS, HEAD_DIM)
    return (yp.reshape(bsz, s_len, D_MODEL), ys.reshape(nseq, t_len, D_MODEL),
            nk_p.reshape(kv_shape), nv_p.reshape(kv_shape), nh_p.reshape(1, bsz, D_LRU), nc_p[None],
            nk_s.reshape(kv_shape), nv_s.reshape(kv_shape), nh_s[None], nc_s[None],
            gv_s.reshape(1, nseq, t_len, D_C))
```

```python
import functools
import math

import jax
import jax.numpy as jnp
from jax import lax
from jax.experimental import pallas as pl
from jax.experimental.pallas import tpu as pltpu

F32 = jnp.float32
BF16 = jnp.bfloat16

D_MODEL = 1024
CHUNK = 64
HEAD_DIM = 64
N_Q_HEADS = 8
N_KV_HEADS = 2
SWA_ROWS = 128
PAST_LEN = 4096
ROPE_THETA = 10000.0
NEG = -1e30
D_LRU = 512
LRU_BLOCKS = 8
LRU_BW = D_LRU // LRU_BLOCKS
CONV_W = 4
LRU_C = 8.0
Q_W = N_Q_HEADS * HEAD_DIM
KV_W = N_KV_HEADS * HEAD_DIM
CHUNK_MLP = 128
D_C = D_MODEL
C_GROUPS = 8
D_FF = 4 * D_MODEL
EPS = 1e-6
LOG2E = math.log2(math.e)
LOGIT_SCALE = HEAD_DIM ** -0.5 * LOG2E

LANES = 128
SUBLANES = 8
MXU_WIDTH = 256
KEY_BLOCK = 2 * CHUNK
BAND_BLOCKS = 3
VMEM_LIMIT = 56 * 1024 * 1024

PROMPT_TILE = 1024
PROMPT_SUBTILE = 256
GMLP_TILE = 1024
FFN_CHUNK = 1024
SAMPLE_GROUP = 8


def _gelu(x):
    k1 = -2.0 * math.sqrt(2.0 / math.pi) * math.log2(math.e)
    return x / (1.0 + jnp.exp2(x * (k1 + (k1 * 0.044715) * (x * x))))


def _rms_rows(x, g):
    return x * lax.rsqrt(jnp.mean(x * x, axis=-1, keepdims=True) + EPS) * g


def _lane_iota(shape):
    return lax.broadcasted_iota(jnp.int32, shape, 1)


def _row_iota(shape):
    return lax.broadcasted_iota(jnp.int32, shape, 0)


def _head_ones():
    return ((_row_iota((LANES, LANES)) // HEAD_DIM) == (_lane_iota((LANES, LANES)) // HEAD_DIM)).astype(BF16)


def _norm_rope(x, xp, g_cos, g_sin, ones_bd):
    ss = jnp.dot((x * x).astype(BF16), ones_bd, preferred_element_type=F32)
    return (x * g_cos + xp * g_sin) * lax.rsqrt(ss + HEAD_DIM * EPS)


def _diag_blocks(t):
    lo = _lane_iota(t.shape) < HEAD_DIM
    swapped = pltpu.roll(t, HEAD_DIM, 1)
    return [(jnp.where(lo, t, 0.0), jnp.where(lo, 0.0, swapped)),
            (jnp.where(lo, swapped, 0.0), jnp.where(lo, 0.0, t))]


def _ones_diag(rows):
    r = _row_iota((rows, LANES))
    return (((r % KEY_BLOCK) // CHUNK) == (_lane_iota((rows, LANES)) // HEAD_DIM)).astype(BF16)


def _attend_scores(qs, kb, col_mask):
    s = lax.dot_general(qs, kb, (((1,), (1,)), ((), ())), preferred_element_type=F32)
    return s if col_mask is None else jnp.where(col_mask, s, NEG)


def _attend_weights(s, sink_rows):
    mx = jnp.maximum(jnp.maximum(s[:, :LANES], s[:, LANES:2 * LANES]), s[:, 2 * LANES:])
    lo = _lane_iota(mx.shape) < HEAD_DIM
    m0 = jnp.max(jnp.where(lo, mx, -jnp.inf), axis=1, keepdims=True)
    m1 = jnp.max(jnp.where(lo, -jnp.inf, mx), axis=1, keepdims=True)
    m = jnp.maximum(jnp.where(lo, m0, m1), sink_rows)
    e = jnp.exp2(s - jnp.concatenate([m, m, m], axis=1))
    return e.astype(BF16), m


def _attend_output(e, m, vb, sink_rows):
    od = jnp.dot(e, vb, preferred_element_type=F32)
    den = od[:, LANES:] + jnp.exp2(sink_rows - m)
    return od[:, :LANES] / den


def _lru_scan(a, b, seg, tmod):
    d = 1
    while d < seg:
        ok = tmod >= d
        b = jnp.where(ok, a * pltpu.roll(b, d, 0), 0.0) + b
        if 2 * d < seg:
            a = jnp.where(ok, a * pltpu.roll(a, d, 0), a)
        d *= 2
    return b


def _lru_scan_tile(a, b, h_prev):
    rows = a.shape[0]
    groups = rows // SUBLANES
    a3 = a.reshape(groups, SUBLANES, LANES)
    b3 = b.reshape(groups, SUBLANES, LANES)
    sub = lax.broadcasted_iota(jnp.int32, a3.shape, 1)
    d = 1
    while d < SUBLANES:
        ok = sub >= d
        b3 = jnp.where(ok, a3 * pltpu.roll(b3, d, 1), 0.0) + b3
        a3 = jnp.where(ok, a3 * pltpu.roll(a3, d, 1), a3)
        d *= 2
    carry = jnp.broadcast_to(h_prev, (SUBLANES, LANES))
    out = []
    for g in range(groups):
        hg = a3[g] * carry + b3[g]
        out.append(hg)
        carry = jnp.broadcast_to(hg[SUBLANES - 1:SUBLANES, :], (SUBLANES, LANES))
    return jnp.concatenate(out, axis=0)


def _lru_branch(xs, gr, h_in, seg, tmod, convw_ref, convb_ref, wg_ref, ba_ref, bx_ref, lam_ref):
    xc, gates = _lru_gates(xs, convw_ref, convb_ref, wg_ref)
    parts = [_lru_piece(k, xc, gates, gr, h_in, seg, tmod, ba_ref, bx_ref, lam_ref) for k in range(D_LRU // LANES)]
    return jnp.concatenate([p[0] for p in parts], axis=1), jnp.concatenate([p[1] for p in parts], axis=1)


def _lru_gates(xs, convw_ref, convb_ref, wg_ref):
    xc = xs[0] * convw_ref[0:1, :] + convb_ref[...]
    for i in range(1, CONV_W):
        xc = xc + xs[i] * convw_ref[i:i + 1, :]
    xcb = xc.astype(BF16)
    half = D_LRU // 2
    return xc, [jnp.dot(xcb[:, hh * half:(hh + 1) * half], wg_ref[hh], preferred_element_type=F32)
                for hh in range(2)]


def _lru_piece(k, xc, gates, gr, h_in, seg, tmod, ba_ref, bx_ref, lam_ref):
    half = D_LRU // 2
    hh, tt = divmod(k, half // LANES)
    sl = slice(k * LANES, (k + 1) * LANES)
    g = gates[hh]
    r = jax.nn.sigmoid(g[:, tt * LANES:(tt + 1) * LANES] + ba_ref[:, sl])
    gi = jax.nn.sigmoid(g[:, half + tt * LANES:half + (tt + 1) * LANES] + bx_ref[:, sl])
    log_a = r * (-LRU_C * jax.nn.softplus(-lam_ref[:, sl]))
    a = jnp.exp(log_a)
    y = -jnp.tanh(log_a) * (a * a + 1.0)
    b = jnp.where(y > 0.0, y * lax.rsqrt(y), 0.0) * (gi * xc[:, sl])
    if seg is None:
        h = _lru_scan_tile(a, b, h_in[:, sl])
    else:
        h = _lru_scan(a, b + jnp.where(tmod == 0, a * h_in[:, sl], 0.0), seg, tmod)
    return h * _gelu(gr[:, sl]), h


def _even_prompt_kernel(x_ref, cos_ref, sin_ref, inv_ref, sgn_ref, g_ref, win_ref, qg_ref, kg_ref, sink_ref,
                        convw_ref, convb_ref, wg_ref, ba_ref, bx_ref, lam_ref, wout_ref,
                        y_ref, nk_ref, nv_ref, nh_ref, nc_ref,
                        kb_scr, vb_scr, xr_scr, h_scr, mix_scr):
    t = pl.program_id(1)
    nt = pl.num_programs(1)
    tile = x_ref.shape[1]
    n_chunks = tile // CHUNK
    hist = (BAND_BLOCKS - 1) * KEY_BLOCK
    slot = t % 2

    @pl.when(t == 0)
    def _():
        ones = _ones_diag(vb_scr.shape[2])
        for j in range(N_KV_HEADS):
            kb_scr[0, j, 0:hist, :] = jnp.zeros((hist, LANES), BF16)
            vb_scr[0, j, 0:hist, 0:LANES] = jnp.zeros((hist, LANES), BF16)
            for s in range(2):
                vb_scr[s, j, :, LANES:2 * LANES] = ones
        xr_scr[0:SUBLANES, :] = jnp.zeros((SUBLANES, D_LRU), F32)
        h_scr[...] = jnp.zeros(h_scr.shape, F32)

    sub = PROMPT_SUBTILE
    n_sub = tile // sub
    sub_chunks = sub // CHUNK
    o3 = Q_W + 2 * KV_W
    xs_in, proj = [], []
    for u in range(n_sub):
        x = x_ref[0, u * sub:(u + 1) * sub, :]
        xn = _rms_rows(x, g_ref[...]).astype(BF16)
        xs_in.append(x)
        o5 = o3 + 2 * D_LRU
        proj.append((jnp.dot(xn, win_ref[:, 0:Q_W], preferred_element_type=F32),
                     jnp.dot(xn, win_ref[:, Q_W:o3], preferred_element_type=F32),
                     jnp.dot(xn, win_ref[:, o3:o3 + D_LRU], preferred_element_type=F32),
                     jnp.dot(xn, win_ref[:, o3 + D_LRU:o5], preferred_element_type=F32),
                     jnp.dot(xn, win_ref[:, o5:o5 + Q_W], preferred_element_type=F32),
                     jnp.dot(xn, win_ref[:, o5 + Q_W:o5 + Q_W + KV_W], preferred_element_type=F32)))
    for u in range(n_sub):
        xr_scr[SUBLANES + u * sub:SUBLANES + (u + 1) * sub, :] = proj[u][2]

    base = (t * tile).astype(F32) * inv_ref[...]
    cos_b = jnp.cos(base)
    sin_b = jnp.sin(base) * sgn_ref[...]

    band = BAND_BLOCKS * KEY_BLOCK
    col = _lane_iota((2 * CHUNK, band))
    sink_rows = [jnp.concatenate(
        [jnp.broadcast_to(sink_ref[2 * j:2 * j + 1, :] * LOG2E, (CHUNK, LANES)),
         jnp.broadcast_to(sink_ref[2 * j + 1:2 * j + 2, :] * LOG2E, (CHUNK, LANES))], axis=0) for j in range(N_KV_HEADS)]
    ones_bd = _head_ones()
    norm_c = math.sqrt(HEAD_DIM)
    q_all = []
    for u in range(n_sub):
        hq, hkv, xr, gr, hqp, hkp = proj[u]
        cos_l = cos_ref[u * sub:(u + 1) * sub, :]
        sin_l = sin_ref[u * sub:(u + 1) * sub, :]
        cos_t = cos_b * cos_l - sin_b * sin_l
        sin_t = sin_b * cos_l + cos_b * sin_l
        q_cos = cos_t * (qg_ref[0:1, :] * (norm_c * LOGIT_SCALE))
        q_sin = sin_t * (qg_ref[1:2, :] * (norm_c * LOGIT_SCALE))
        q_tiles = [_norm_rope(hq[:, n * LANES:(n + 1) * LANES], hqp[:, n * LANES:(n + 1) * LANES],
                              q_cos, q_sin, ones_bd).astype(BF16) for n in range(Q_W // LANES)]
        kr = _norm_rope(hkv[:, 0:KV_W], hkp, cos_t * (kg_ref[0:1, :] * norm_c), sin_t * (kg_ref[1:2, :] * norm_c),
                        ones_bd)
        vv = hkv[:, KV_W:2 * KV_W]
        k_blocks, v_blocks = _diag_blocks(kr), _diag_blocks(vv)
        for j in range(N_KV_HEADS):
            (ktop, kbot), (vtop, vbot) = k_blocks[j], v_blocks[j]
            for i in range(sub_chunks):
                r0 = hist + (u * sub_chunks + i) * KEY_BLOCK
                rows = slice(i * CHUNK, (i + 1) * CHUNK)
                kb_scr[slot, j, r0:r0 + CHUNK, :] = ktop[rows].astype(BF16)
                kb_scr[slot, j, r0 + CHUNK:r0 + KEY_BLOCK, :] = kbot[rows].astype(BF16)
                vb_scr[slot, j, r0:r0 + CHUNK, 0:LANES] = vtop[rows].astype(BF16)
                vb_scr[slot, j, r0 + CHUNK:r0 + KEY_BLOCK, 0:LANES] = vbot[rows].astype(BF16)
        q_all.append(q_tiles)

    pairs = [(i, j) for i in range(sub_chunks) for j in range(N_KV_HEADS)]
    scores_all, gates_all = [], []
    for u in range(n_sub):
        q_tiles = q_all[u]
        scores = []
        for i, j in pairs:
            rows = slice(i * CHUNK, (i + 1) * CHUNK)
            ci = u * sub_chunks + i
            col_mask = None
            if ci < BAND_BLOCKS - 1:
                col_mask = col >= (1 - jnp.minimum(t, 1)) * ((BAND_BLOCKS - 1 - ci) * KEY_BLOCK)
            qs = jnp.concatenate([q_tiles[2 * j][rows], q_tiles[2 * j + 1][rows]], axis=0)
            scores.append(_attend_scores(qs, kb_scr[slot, j, ci * KEY_BLOCK:ci * KEY_BLOCK + band, :], col_mask))
        scores_all.append(scores)
        first = SUBLANES + u * sub
        taps = [xr_scr[first - (CONV_W - 1 - i):first - (CONV_W - 1 - i) + sub, :] for i in range(CONV_W - 1)]
        gates_all.append(_lru_gates(taps + [proj[u][2]], convw_ref, convb_ref, wg_ref))

    h_carry = h_scr[0:1, :]
    wout = wout_ref[...].astype(BF16)
    for u in range(n_sub):
        xr, gr = proj[u][2], proj[u][3]
        scores = scores_all[u]
        xc, gates = gates_all[u]
        n_pieces = D_LRU // LANES
        per_piece = len(pairs) // n_pieces
        weights, lru = [], []
        for k in range(n_pieces):
            weights += [_attend_weights(scores[p], sink_rows[pairs[p][1]])
                        for p in range(k * per_piece, (k + 1) * per_piece)]
            lru.append(_lru_piece(k, xc, gates, gr, h_carry, None, None, ba_ref, bx_ref, lam_ref))
        for (e, m), (i, j) in zip(weights, pairs):
            ci = u * sub_chunks + i
            rows = slice(u * sub + i * CHUNK, u * sub + (i + 1) * CHUNK)
            o = _attend_output(e, m, vb_scr[slot, j, ci * KEY_BLOCK:ci * KEY_BLOCK + band, :], sink_rows[j])
            c0 = 2 * j * LANES
            mix_scr[rows, c0:c0 + LANES] = o[0:CHUNK].astype(BF16)
            mix_scr[rows, c0 + LANES:c0 + 2 * LANES] = o[CHUNK:2 * CHUNK].astype(BF16)

        h_carry = jnp.concatenate([h[sub - 1:sub, :] for _, h in lru], axis=1)
        rows = slice(u * sub, (u + 1) * sub)
        for k, (rec, _) in enumerate(lru):
            mix_scr[rows, Q_W + k * LANES:Q_W + (k + 1) * LANES] = rec.astype(BF16)

        y_att = jnp.dot(mix_scr[rows, 0:Q_W], wout[0:Q_W, :], preferred_element_type=F32)
        y_rec = jnp.dot(mix_scr[rows, Q_W:Q_W + D_LRU], wout[Q_W:Q_W + D_LRU, :], preferred_element_type=F32)
        y_ref[0, rows, :] = xs_in[u] + y_att + y_rec

    last = n_chunks * KEY_BLOCK
    for j in range(N_KV_HEADS):
        kb_scr[1 - slot, j, 0:hist, :] = kb_scr[slot, j, last:last + hist, :]
        vb_scr[1 - slot, j, 0:hist, 0:LANES] = vb_scr[slot, j, last:last + hist, 0:LANES]
    h_scr[0:1, :] = h_carry
    xr_scr[0:SUBLANES, :] = xr[sub - SUBLANES:sub, :]

    @pl.when(t == nt - 1)
    def _():
        nk_ref[0] = kr[sub - SWA_ROWS:sub, :]
        nv_ref[0] = vv[sub - SWA_ROWS:sub, :]
        nh_ref[0] = h_carry
        nc_ref[0] = xr[sub - (CONV_W - 1):sub, :]


def _const_spec(shape):
    zeros = (0,) * len(shape)
    return pl.BlockSpec(shape, lambda *_: zeros, pipeline_mode=pl.Buffered(1))


def _even_prompt(x, p):
    bsz, s_len, _ = x.shape
    tile = PROMPT_TILE
    cos_l, sin_l = _rope_tables(jnp.arange(tile))
    inv_row, sgn_row = _rope_rows()
    nt = s_len // tile
    key_rows = (tile // CHUNK + BAND_BLOCKS - 1) * KEY_BLOCK
    consts = [p['g'], p['win'], p['qg'], p['kg'], p['sink'], p['convw'], p['convb'], p['wg'], p['ba'], p['bx'],
              p['lam'], p['wout']]
    out_shape = [
        jax.ShapeDtypeStruct((bsz, s_len, D_MODEL), F32),
        jax.ShapeDtypeStruct((bsz, SWA_ROWS, KV_W), F32),
        jax.ShapeDtypeStruct((bsz, SWA_ROWS, KV_W), F32),
        jax.ShapeDtypeStruct((bsz, 1, D_LRU), F32),
        jax.ShapeDtypeStruct((bsz, CONV_W - 1, D_LRU), F32),
    ]
    return pl.pallas_call(
        _even_prompt_kernel,
        out_shape=out_shape,
        grid=(bsz, nt),
        in_specs=[pl.BlockSpec((1, tile, D_MODEL), lambda b, t: (b, t, 0)),
                  _const_spec(cos_l.shape), _const_spec(sin_l.shape), _const_spec(inv_row.shape),
                  _const_spec(sgn_row.shape)] + [_const_spec(c.shape) for c in consts],
        out_specs=[pl.BlockSpec((1, tile, D_MODEL), lambda b, t: (b, t, 0)),
                   pl.BlockSpec((1, SWA_ROWS, KV_W), lambda b, t: (b, 0, 0)),
                   pl.BlockSpec((1, SWA_ROWS, KV_W), lambda b, t: (b, 0, 0)),
                   pl.BlockSpec((1, 1, D_LRU), lambda b, t: (b, 0, 0)),
                   pl.BlockSpec((1, CONV_W - 1, D_LRU), lambda b, t: (b, 0, 0))],
        scratch_shapes=[pltpu.VMEM((2, N_KV_HEADS, key_rows, LANES), BF16),
                        pltpu.VMEM((2, N_KV_HEADS, key_rows, 2 * LANES), BF16),
                        pltpu.VMEM((tile + SUBLANES, D_LRU), F32),
                        pltpu.VMEM((SUBLANES, D_LRU), F32),
                        pltpu.VMEM((tile, Q_W + D_LRU), BF16)],
        compiler_params=pltpu.CompilerParams(dimension_semantics=("arbitrary", "arbitrary"),
                                             vmem_limit_bytes=VMEM_LIMIT),
        name="even_mixer_prompt",
    )(x, cos_l, sin_l, inv_row, sgn_row, *consts)


def _even_sample_kernel(x_ref, cos_ref, sin_ref, kc_ref, vc_ref, h0_ref, xprev_ref,
                        g_ref, win_ref, qg_ref, kg_ref, sink_ref, convw_ref, convb_ref,
                        wg_ref, ba_ref, bx_ref, lam_ref, wout_ref,
                        y_ref, nk_ref, nv_ref, h_ref, xr_ref,
                        q_scr, k_scr, v_scr, mix_scr):
    rows_all = x_ref.shape[0]
    nseq = kc_ref.shape[0]
    t_len = rows_all // nseq
    x = x_ref[...]
    xn = _rms_rows(x, g_ref[...]).astype(BF16)
    cos_t = cos_ref[...]
    sin_t = sin_ref[...]

    o3 = Q_W + 2 * KV_W
    o5 = o3 + 2 * D_LRU
    hq = jnp.dot(xn, win_ref[:, 0:Q_W], preferred_element_type=F32)
    hkv = jnp.dot(xn, win_ref[:, Q_W:o3], preferred_element_type=F32)
    xr = jnp.dot(xn, win_ref[:, o3:o3 + D_LRU], preferred_element_type=F32)
    gr = jnp.dot(xn, win_ref[:, o3 + D_LRU:o5], preferred_element_type=F32)
    hqp = jnp.dot(xn, win_ref[:, o5:o5 + Q_W], preferred_element_type=F32)
    hkp = jnp.dot(xn, win_ref[:, o5 + Q_W:o5 + Q_W + KV_W], preferred_element_type=F32)
    ones_bd = _head_ones()
    norm_c = math.sqrt(HEAD_DIM)
    q_cos = cos_t * (qg_ref[0:1, :] * (norm_c * LOGIT_SCALE))
    q_sin = sin_t * (qg_ref[1:2, :] * (norm_c * LOGIT_SCALE))
    for n in range(Q_W // LANES):
        lanes = slice(n * LANES, (n + 1) * LANES)
        q_scr[:, lanes] = _norm_rope(hq[:, lanes], hqp[:, lanes], q_cos, q_sin, ones_bd)
    k_scr[...] = _norm_rope(hkv[:, 0:KV_W], hkp, cos_t * (kg_ref[0:1, :] * norm_c),
                            sin_t * (kg_ref[1:2, :] * norm_c), ones_bd)
    v_scr[...] = hkv[:, KV_W:2 * KV_W]

    band = BAND_BLOCKS * KEY_BLOCK
    col = _lane_iota((2 * t_len, band))
    col_mask = jnp.logical_or(col < 2 * KEY_BLOCK, (col % CHUNK) < t_len)
    ones_band = _ones_diag(band)
    pad = jnp.zeros((CHUNK - t_len, LANES), F32)

    sink_rows = [jnp.concatenate(
        [jnp.broadcast_to(sink_ref[2 * j:2 * j + 1, :] * LOG2E, (t_len, LANES)),
         jnp.broadcast_to(sink_ref[2 * j + 1:2 * j + 2, :] * LOG2E, (t_len, LANES))], axis=0) for j in range(N_KV_HEADS)]

    def group_body(gi, carry):
        work = []
        for s in range(SAMPLE_GROUP):
            b = gi * SAMPLE_GROUP + s
            r0 = pl.multiple_of(b * t_len, t_len)
            kc = kc_ref[b]
            vc = vc_ref[b]
            knew = k_scr[pl.ds(r0, t_len), :]
            vnew = v_scr[pl.ds(r0, t_len), :]
            nk_ref[b, 0:SWA_ROWS - t_len, :] = kc[t_len:SWA_ROWS]
            nk_ref[b, SWA_ROWS - t_len:SWA_ROWS, :] = knew
            nv_ref[b, 0:SWA_ROWS - t_len, :] = vc[t_len:SWA_ROWS]
            nv_ref[b, SWA_ROWS - t_len:SWA_ROWS, :] = vnew
            kfull = jnp.concatenate([kc, knew, pad], axis=0)
            vfull = jnp.concatenate([vc, vnew, pad], axis=0)
            k_blocks, v_blocks = _diag_blocks(kfull), _diag_blocks(vfull)
            for j in range(N_KV_HEADS):
                (ktop, kbot), (vtop, vbot) = k_blocks[j], v_blocks[j]
                kparts, vparts = [], []
                for c in range(BAND_BLOCKS):
                    rs = slice(c * CHUNK, (c + 1) * CHUNK)
                    kparts += [ktop[rs], kbot[rs]]
                    vparts += [vtop[rs], vbot[rs]]
                kb = jnp.concatenate(kparts, axis=0).astype(BF16)
                vb = jnp.concatenate([jnp.concatenate(vparts, axis=0).astype(BF16), ones_band], axis=1)
                c0 = 2 * j * LANES
                qs = jnp.concatenate([q_scr[pl.ds(r0, t_len), c0:c0 + LANES],
                                      q_scr[pl.ds(r0, t_len), c0 + LANES:c0 + 2 * LANES]], axis=0).astype(BF16)
                work.append((r0, j, vb, _attend_scores(qs, kb, col_mask)))
        weights = [_attend_weights(s, sink_rows[j]) for _, j, _, s in work]
        for (r0, j, vb, _), (e, m) in zip(work, weights):
            o = _attend_output(e, m, vb, sink_rows[j])
            c0 = 2 * j * LANES
            mix_scr[pl.ds(r0, t_len), c0:c0 + LANES] = o[0:t_len]
            mix_scr[pl.ds(r0, t_len), c0 + LANES:c0 + 2 * LANES] = o[t_len:2 * t_len]
        return carry

    lax.fori_loop(0, nseq // SAMPLE_GROUP, group_body, 0)

    tmod =_row_iota((rows_all, LANES)) % t_len
    tmod_w = _row_iota((rows_all, D_LRU)) % t_len
    xprev = xprev_ref[...]
    xs = []
    for i in range(CONV_W - 1):
        k = CONV_W - 1 - i
        own = pltpu.roll(xr, k, 0)
        cached = xprev if k == CONV_W - 1 else pltpu.roll(xprev, rows_all - (CONV_W - 1 - k), 0)
        xs.append(jnp.where(tmod_w >= k, own, cached))
    xs.append(xr)
    rec, h = _lru_branch(xs, gr, h0_ref[...], t_len, tmod, convw_ref, convb_ref, wg_ref, ba_ref, bx_ref, lam_ref)
    mix_scr[:, Q_W:Q_W + D_LRU] = rec
    xr_ref[...] = xr
    h_ref[...] = h
    y_ref[...] = x + jnp.dot(mix_scr[...].astype(BF16), wout_ref[...].astype(BF16), preferred_element_type=F32)


def _even_sample(x2, cos_t, sin_t, kc, vc, h0_rows, xprev, p):
    rows_all = x2.shape[0]
    nseq = kc.shape[0]
    consts = [p['g'], p['win'], p['qg'], p['kg'], p['sink'], p['convw'], p['convb'], p['wg'], p['ba'], p['bx'],
              p['lam'], p['wout']]
    ins = [x2, cos_t, sin_t, kc, vc, h0_rows, xprev] + consts
    out_shape = [
        jax.ShapeDtypeStruct((rows_all, D_MODEL), F32),
        jax.ShapeDtypeStruct((nseq, SWA_ROWS, KV_W), F32),
        jax.ShapeDtypeStruct((nseq, SWA_ROWS, KV_W), F32),
        jax.ShapeDtypeStruct((rows_all, D_LRU), F32),
        jax.ShapeDtypeStruct((rows_all, D_LRU), F32),
    ]
    return pl.pallas_call(
        _even_sample_kernel,
        out_shape=out_shape,
        grid=(1,),
        in_specs=[_const_spec(a.shape) for a in ins],
        out_specs=[pl.BlockSpec(s.shape, lambda i, n=len(s.shape): (0,) * n) for s in out_shape],
        scratch_shapes=[pltpu.VMEM((rows_all, Q_W), F32),
                        pltpu.VMEM((rows_all, KV_W), F32),
                        pltpu.VMEM((rows_all, KV_W), F32),
                        pltpu.VMEM((rows_all, Q_W + D_LRU), F32)],
        compiler_params=pltpu.CompilerParams(dimension_semantics=("arbitrary",), vmem_limit_bytes=VMEM_LIMIT),
        name="even_mixer_sample",
    )(*ins)


def _ffn_rows(x, g_ref, w1_ref, w2_ref):
    xg = (x * g_ref[...]).astype(BF16)
    r2 = 1.0 / (jnp.mean(x * x, axis=-1, keepdims=True) + EPS)
    out = None
    step = FFN_CHUNK
    for c in range(D_FF // step):
        hcol = jnp.dot(xg, w1_ref[:, c * step:(c + 1) * step].astype(BF16), preferred_element_type=F32)
        hcol = jnp.square(jnp.maximum(hcol, 0.0)).astype(BF16)
        part = jnp.dot(hcol, w2_ref[c * step:(c + 1) * step, :].astype(BF16), preferred_element_type=F32)
        out = part if out is None else out + part
    return x + out * r2


def _ffn_kernel(xp_ref, xs_ref, g_ref, w1_ref, w2_ref, yp_ref, ys_ref):
    i = pl.program_id(0)
    yp_ref[...] = _ffn_rows(jnp.where(i == 0, xs_ref[...], xp_ref[...]), g_ref, w1_ref, w2_ref)

    @pl.when(i == 0)
    def _():
        ys_ref[...] = yp_ref[...]


def _sample_then_tiles_specs(tile, width, sample_rows=None):
    prompt = pl.BlockSpec((tile, width), lambda i: (jnp.maximum(i - 1, 0), 0))
    sample = pl.BlockSpec((sample_rows or tile, width), lambda i: (0, 0))
    return prompt, sample


def _ffn(xp, xs, g, w1, w2, layer):
    tile = xs.shape[0]
    n_tiles = xp.shape[0] // tile
    xp_spec, xs_spec = _sample_then_tiles_specs(tile, D_MODEL)
    layer_spec = lambda shape: pl.BlockSpec((None,) + shape[1:], lambda i: (layer, 0, 0),
                                            pipeline_mode=pl.Buffered(1))
    return pl.pallas_call(
        _ffn_kernel,
        out_shape=[jax.ShapeDtypeStruct(xp.shape, F32), jax.ShapeDtypeStruct(xs.shape, F32)],
        grid=(n_tiles + 1,),
        in_specs=[xp_spec, xs_spec, layer_spec(g.shape), layer_spec(w1.shape), layer_spec(w2.shape)],
        out_specs=list(_sample_then_tiles_specs(tile, D_MODEL)),
        compiler_params=pltpu.CompilerParams(dimension_semantics=("arbitrary",), vmem_limit_bytes=VMEM_LIMIT),
        name="channel_mlp",
    )(xp, xs, g, w1, w2)


def _gmlp_kernel(seg_s, xp_ref, xs_ref, g_ref, win_ref, vg_ref, wsp_ref, wss_ref, biasp_ref, biass_ref, wout_ref,
                 yp_ref, ys_ref, v_ref, v_scr):
    i = pl.program_id(0)
    is_s = i == 0
    tile = xp_ref.shape[0]
    sub = xs_ref.shape[0]
    n_sub = tile // sub
    w_v = win_ref[:, D_C:2 * D_C].astype(BF16)
    w_u = win_ref[:, 0:D_C].astype(BF16)
    wout = wout_ref[...].astype(BF16)
    xs_in, proj = [], []
    for u in range(n_sub):
        rows = slice(u * sub, (u + 1) * sub)
        x = jnp.where(is_s, xs_ref[...], xp_ref[rows, :]) if u == 0 else xp_ref[rows, :]
        xn = _rms_rows(x, g_ref[...]).astype(BF16)
        xs_in.append(x)
        proj.append((jnp.dot(xn, w_v, preferred_element_type=F32), jnp.dot(xn, w_u, preferred_element_type=F32)))
    r = _row_iota((CHUNK_MLP, CHUNK_MLP))
    col = _lane_iota((CHUNK_MLP, CHUNK_MLP))
    keep_p = col <= r
    keep_s = jnp.logical_and((r // seg_s) == (col // seg_s), keep_p)
    ws = [jnp.where(is_s, jnp.where(keep_s, wss_ref[gi], 0.0), jnp.where(keep_p, wsp_ref[gi], 0.0)).astype(BF16)
          for gi in range(C_GROUPS)]
    step = MXU_WIDTH
    per_step = step // LANES
    for u in range(n_sub):
        rows = slice(u * sub, (u + 1) * sub)
        zv = _gelu(proj[u][0])
        zc = zv - jnp.mean(zv, axis=-1, keepdims=True)
        v = zc * lax.rsqrt(jnp.mean(zc * zc, axis=-1, keepdims=True) + EPS) * vg_ref[...]
        if u == 0:
            v_scr[...] = v
        vb = v.astype(BF16)
        sv = []
        for gi in range(C_GROUPS):
            lanes = slice(gi * LANES, (gi + 1) * LANES)
            bias = jnp.where(is_s, biass_ref[:, lanes], biasp_ref[:, lanes])
            blocks = [jnp.dot(ws[gi], vb[ch * CHUNK_MLP:(ch + 1) * CHUNK_MLP, lanes],
                              preferred_element_type=F32) + bias for ch in range(sub // CHUNK_MLP)]
            sv.append(jnp.concatenate(blocks, axis=0))
        acc = xs_in[u]
        for c in range(D_C // step):
            cols = slice(c * step, (c + 1) * step)
            gate = (_gelu(proj[u][1][:, cols]) *
                    jnp.concatenate(sv[c * per_step:(c + 1) * per_step], axis=1)).astype(BF16)
            acc = acc + jnp.dot(gate, wout[cols, :], preferred_element_type=F32)
        yp_ref[rows, :] = acc

    @pl.when(is_s)
    def _():
        ys_ref[...] = yp_ref[0:sub, :]
        v_ref[...] = v_scr[...]


def _gmlp(xp, xs, seg_s, g, win, vg, ws_p, ws_s, bias_p, bias_s, wout):
    rows_s = xs.shape[0]
    tile = GMLP_TILE
    n_tiles = xp.shape[0] // tile
    xp_spec, xs_spec = _sample_then_tiles_specs(tile, D_MODEL, rows_s)
    consts = (g, win, vg, ws_p, ws_s, bias_p, bias_s, wout)
    return pl.pallas_call(
        functools.partial(_gmlp_kernel, seg_s),
        out_shape=[jax.ShapeDtypeStruct(xp.shape, F32), jax.ShapeDtypeStruct(xs.shape, F32),
                   jax.ShapeDtypeStruct((rows_s, D_C), F32)],
        grid=(n_tiles + 1,),
        in_specs=[xp_spec, xs_spec] + [_const_spec(a.shape) for a in consts],
        out_specs=list(_sample_then_tiles_specs(tile, D_MODEL, rows_s)) +
                  [pl.BlockSpec((rows_s, D_C), lambda i: (0, 0))],
        scratch_shapes=[pltpu.VMEM((rows_s, D_C), F32)],
        compiler_params=pltpu.CompilerParams(dimension_semantics=("arbitrary",), vmem_limit_bytes=VMEM_LIMIT),
        name="gmlp_mixer",
    )(xp, xs, *consts)


def _rope_inv():
    half = HEAD_DIM // 2
    return ROPE_THETA ** (-jnp.arange(half, dtype=F32) / half)


def _rope_rows():
    inv = _rope_inv()
    sgn = jnp.ones((HEAD_DIM // 2,), F32)
    return jnp.concatenate([inv] * 4)[None, :], jnp.concatenate([-sgn, sgn, -sgn, sgn])[None, :]


def _rope_tables(pos):
    inv_row, sgn_row = _rope_rows()
    ang = pos.astype(F32)[:, None] * inv_row
    return jnp.cos(ang), jnp.sin(ang) * sgn_row


def _swap_head_halves(a):
    half = HEAD_DIM // 2
    return a.reshape(a.shape[:-1] + (-1, 2, half))[..., ::-1, :].reshape(a.shape)


def _gain_rows(g):
    return jnp.tile(jnp.concatenate([g, _swap_head_halves(g)], axis=0), (1, LANES // HEAD_DIM))


def _gate_weights(wa, wx):
    nb = LRU_BLOCKS // 2
    eye = jnp.eye(nb, dtype=bool)

    def half_diag(w, hh):
        blocks = w[hh * nb:(hh + 1) * nb]
        return jnp.where(eye[:, None, :, None], blocks[:, :, None, :], 0.0).reshape(nb * LRU_BW, nb * LRU_BW)
    return jnp.stack([jnp.concatenate([half_diag(wa, hh), half_diag(wx, hh)], axis=1)
                      for hh in range(2)]).astype(BF16)


def kernel(x_prompt, x_sample, cache_swa_k, cache_swa_v, state_lru_h, state_lru_conv, e_norm_g, e_w_in, e_q_norm_g, e_k_norm_g, e_sinks, e_conv_w, e_conv_b, e_gate_a_w, e_gate_a_b, e_gate_x_w, e_gate_x_b, e_lru_lambda, e_w_out, o_norm_g, o_w_in, o_v_norm_g, o_spatial_w, o_spatial_b, o_w_out, ffn_norm_g, ffn_w1, ffn_w2):
    bsz, s_len, _ = x_prompt.shape
    nseq, t_len, _ = x_sample.shape
    past_len = PAST_LEN
    row = lambda a: a.reshape(1, -1)

    e = 0
    p = {
        'g': row(e_norm_g[e]),
        'win': jnp.concatenate([e_w_in[e], _swap_head_halves(e_w_in[e][:, :Q_W + KV_W])], axis=1).astype(BF16),
        'qg': _gain_rows(e_q_norm_g[e:e + 1]),
        'kg': _gain_rows(e_k_norm_g[e:e + 1]),
        'sink': jnp.repeat(e_sinks[e:e + 1], HEAD_DIM, axis=1).reshape(N_Q_HEADS // 2, LANES),
        'convw': e_conv_w[e],
        'convb': row(e_conv_b[e]),
        'wg': _gate_weights(e_gate_a_w[e], e_gate_x_w[e]),
        'ba': row(e_gate_a_b[e]),
        'bx': row(e_gate_x_b[e]),
        'lam': row(e_lru_lambda[e]),
        'wout': e_w_out[e],
    }
    yp, nk_p, nv_p, nh_p, nc_p = _even_prompt(x_prompt, p)

    cos_s, sin_s = _rope_tables(past_len + jnp.arange(t_len))
    cos_s = jnp.tile(cos_s, (nseq, 1))
    sin_s = jnp.tile(sin_s, (nseq, 1))
    kc = cache_swa_k[e].reshape(nseq, SWA_ROWS, KV_W)
    vc = cache_swa_v[e].reshape(nseq, SWA_ROWS, KV_W)
    h0_rows = jnp.repeat(state_lru_h[e], t_len, axis=0)
    xprev = jnp.pad(state_lru_conv[e], ((0, 0), (0, t_len - (CONV_W - 1)), (0, 0))).reshape(nseq * t_len, D_LRU)
    ys, nk_s, nv_s, h_s, xr_s = _even_sample(x_sample.reshape(nseq * t_len, D_MODEL), cos_s, sin_s, kc, vc,
                                             h0_rows, xprev, p)
    nh_s = h_s.reshape(nseq, t_len, D_LRU)[:, -1]
    nc_s = xr_s.reshape(nseq, t_len, D_LRU)[:, t_len - (CONV_W - 1):]

    ffn_g = ffn_norm_g.reshape(-1, 1, D_MODEL)
    yp, ys = _ffn(yp.reshape(bsz * s_len, D_MODEL), ys, ffn_g, ffn_w1, ffn_w2, 0)

    o = 0
    bias_p = jnp.repeat(o_spatial_b[o].T, LANES, axis=1)
    reps = CHUNK_MLP // t_len
    ws_s = jnp.tile(o_spatial_w[o][:, :t_len, :t_len], (1, reps, reps))
    bias_s = jnp.tile(jnp.repeat(o_spatial_b[o][:, :t_len].T, LANES, axis=1), (reps, 1))
    yp, ys, gv_s = _gmlp(yp, ys, t_len, row(o_norm_g[o]), o_w_in[o], row(o_v_norm_g[o]),
                         o_spatial_w[o], ws_s, bias_p, bias_s, o_w_out[o])

    yp, ys = _ffn(yp, ys, ffn_g, ffn_w1, ffn_w2, 1)

    kv_shape = (1, -1, SWA_ROWS, N_KV_HEADS, HEAD_DIM)
    return (yp.reshape(bsz, s_len, D_MODEL), ys.reshape(nseq, t_len, D_MODEL),
            nk_p.reshape(kv_shape), nv_p.reshape(kv_shape), nh_p.reshape(1, bsz, D_LRU), nc_p[None],
            nk_s.reshape(kv_shape), nv_s.reshape(kv_shape), nh_s[None], nc_s[None],
            gv_s.reshape(1, nseq, t_len, D_C))
```
